```python
import jax, jax.numpy as jnp
from jax import lax
import numpy as np

D_MODEL = 1024
BATCH = 4
SEQ = 4096
DEPTH = 2

N_EVEN = (DEPTH + 1) // 2
N_ODD = DEPTH // 2
NORM_EPS = 1e-5

GLA_HEADS = 4
GLA_DK = 64
GLA_DV = 128
GLA_KEY = GLA_HEADS * GLA_DK
GLA_VAL = GLA_HEADS * GLA_DV
GLA_GATE_RANK = 16
GLA_GATE_NORMALIZER = 16.0
GLA_CHUNK = 64

RWKV_HEADS = 8
RWKV_HEAD = 64
RWKV_W = RWKV_HEADS * RWKV_HEAD
RWKV_DECAY_RANK = 64
RWKV_A_RANK = 64
RWKV_LN_EPS = 64e-5
RWKV_SHIFT = 3 * RWKV_W + RWKV_DECAY_RANK + RWKV_A_RANK

MIX0 = GLA_VAL + RWKV_W
SPLIT0 = [GLA_KEY, 2 * GLA_KEY, 2 * GLA_KEY + GLA_VAL,
          2 * GLA_KEY + GLA_VAL + GLA_GATE_RANK,
          2 * GLA_KEY + GLA_VAL + GLA_GATE_RANK + RWKV_SHIFT]
IN0 = 2 * GLA_KEY + GLA_VAL + GLA_GATE_RANK + RWKV_SHIFT + MIX0
SPLIT_RWKV = [RWKV_W, 2 * RWKV_W, 3 * RWKV_W, 3 * RWKV_W + RWKV_DECAY_RANK]

SWA_Q_HEADS = 16
SWA_KV_HEADS = 4
SWA_GROUP = SWA_Q_HEADS // SWA_KV_HEADS
SWA_HEAD = 64
WINDOW = 128
ROPE_DIMS = SWA_HEAD // 4
ROPE_THETA = 500000.0
MIX1 = SWA_Q_HEADS * SWA_HEAD
SWA_KV = SWA_KV_HEADS * SWA_HEAD
SWA_QKV = MIX1 + 2 * SWA_KV
IN1 = SWA_QKV + MIX1

kernel_name = "hybrid_gla_rwkv7_swa_sink_gated"


def rmsnorm(x, w, eps=NORM_EPS):
    xf = x.astype(jnp.float32)
    y = xf * lax.rsqrt(jnp.mean(xf * xf, axis=-1, keepdims=True) + eps)
    return (y * w.astype(jnp.float32)).astype(x.dtype)


def token_shift(t):
    return jnp.concatenate([jnp.zeros_like(t[:, :1]), t[:, :-1]], axis=1)


def gla_chunked(q, k, v, g):
    f32 = jnp.float32
    B, T, H, dk = q.shape
    dv = v.shape[-1]
    C = GLA_CHUNK
    NC = T // C
    q = (q.astype(f32) * (dk ** -0.5)).reshape(B, NC, C, H, dk)
    k = k.astype(f32).reshape(B, NC, C, H, dk)
    v = v.astype(f32).reshape(B, NC, C, H, dv)
    b = jnp.cumsum(g.astype(f32).reshape(B, NC, C, H, dk), axis=2)
    ref = b[:, :, C // 2:C // 2 + 1]
    att = jnp.einsum('bnihd,bnjhd->bnhij', q * jnp.exp(b - ref), k * jnp.exp(ref - b))
    causal = jnp.tril(jnp.ones((C, C), dtype=bool))
    att = jnp.where(causal, att, 0.0)
    o_intra = jnp.einsum('bnhij,bnjhv->bnihv', att, v)
    b_last = b[:, :, -1:]
    chunk_kv = jnp.einsum('bnjhd,bnjhv->nbhdv', k * jnp.exp(b_last - b), v)
    chunk_decay = jnp.exp(b_last[:, :, 0]).transpose(1, 0, 2, 3)

    def step(S, inp):
        kv, dec = inp
        return S * dec[..., None] + kv, S

    _, S_prev = lax.scan(step, jnp.zeros((B, H, dk, dv), f32), (chunk_kv, chunk_decay))
    o_inter = jnp.einsum('bnihd,nbhdv->bnihv', q * jnp.exp(b), S_prev)
    return (o_intra + o_inter).reshape(B, T, H, dv)


def rwkv7_scan(r, decay, k, v, a_vec, b_vec):
    B, T, H, N = r.shape

    def step(S, inp):
        r_t, w_t, k_t, v_t, a_t, b_t = inp
        sa = jnp.einsum('bhvk,bhk->bhv', S, a_t)
        S = S * w_t[:, :, None, :] + sa[..., None] * b_t[:, :, None, :] + v_t[..., None] * k_t[:, :, None, :]
        return S, jnp.einsum('bhvk,bhk->bhv', S, r_t)

    xs = (jnp.moveaxis(r, 1, 0), jnp.moveaxis(decay, 1, 0), jnp.moveaxis(k, 1, 0),
          jnp.moveaxis(v, 1, 0), jnp.moveaxis(a_vec, 1, 0), jnp.moveaxis(b_vec, 1, 0))
    _, out = lax.scan(step, jnp.zeros((B, H, N, N), jnp.float32), xs)
    return jnp.moveaxis(out, 0, 1)


def gla_rwkv_mixer(h, w_in, gk_up, gk_bias, gla_norm_w, mu, w0, w_up, a0, a_up,
                   k_k, k_a, r_k, ln_w, ln_b, w_out):
    f32 = jnp.float32
    B, T, _ = h.shape
    proj = h @ w_in
    gq, gk, gv, g_low, rw, gate = jnp.split(proj, SPLIT0, axis=-1)

    log_decay = jax.nn.log_sigmoid((g_low @ gk_up + gk_bias).astype(f32)) / GLA_GATE_NORMALIZER
    o_a = gla_chunked(gq.reshape(B, T, GLA_HEADS, GLA_DK), gk.reshape(B, T, GLA_HEADS, GLA_DK),
                      gv.reshape(B, T, GLA_HEADS, GLA_DV), log_decay.reshape(B, T, GLA_HEADS, GLA_DK))
    o_a = o_a * lax.rsqrt(jnp.mean(o_a * o_a, axis=-1, keepdims=True) + NORM_EPS) * gla_norm_w.astype(f32)
    o_a = o_a.reshape(B, T, GLA_VAL)

    rw = rw.astype(f32)
    rw = rw + (token_shift(rw) - rw) * mu.astype(f32)
    r, k, v, xw, xa = jnp.split(rw, SPLIT_RWKV, axis=-1)
    w = -jax.nn.softplus(-(w0.astype(f32) + jnp.tanh(xw) @ w_up.astype(f32))) - 0.5
    decay = jnp.exp(-jnp.exp(w))
    a = jax.nn.sigmoid(a0.astype(f32) + xa @ a_up.astype(f32))
    hs = (B, T, RWKV_HEADS, RWKV_HEAD)
    kk = (k * k_k.astype(f32)).reshape(hs)
    kk = kk / jnp.maximum(jnp.sqrt(jnp.sum(kk * kk, axis=-1, keepdims=True)), 1e-12)
    k = k * (1.0 + (a - 1.0) * k_a.astype(f32))
    r, k, v, decay, a = r.reshape(hs), k.reshape(hs), v.reshape(hs), decay.reshape(hs), a.reshape(hs)
    o_b = rwkv7_scan(r, decay, k, v, -kk, kk * a)
    mean = jnp.mean(o_b, axis=-1, keepdims=True)
    var = jnp.mean((o_b - mean) ** 2, axis=-1, keepdims=True)
    o_b = (o_b - mean) * lax.rsqrt(var + RWKV_LN_EPS) * ln_w.astype(f32).reshape(RWKV_HEADS, RWKV_HEAD) \
        + ln_b.astype(f32).reshape(RWKV_HEADS, RWKV_HEAD)
    o_b = o_b + jnp.sum(r * k * r_k.astype(f32), axis=-1, keepdims=True) * v
    o_b = o_b.reshape(B, T, RWKV_W)

    o = jnp.concatenate([o_a, o_b], axis=-1) * jax.nn.silu(gate.astype(f32))
    return (o.astype(h.dtype) @ w_out).astype(h.dtype)


def partial_rope(x, pos):
    f32 = jnp.float32
    half = ROPE_DIMS // 2
    inv_freq = ROPE_THETA ** (-jnp.arange(half, dtype=f32) / half)
    ang = pos.astype(f32)[:, None] * inv_freq
    cos = jnp.cos(ang)[None, :, None, :]
    sin = jnp.sin(ang)[None, :, None, :]
    xr = x[..., :ROPE_DIMS].astype(f32)
    x1, x2 = xr[..., :half], xr[..., half:]
    rot = jnp.concatenate([x1 * cos - x2 * sin, x2 * cos + x1 * sin], axis=-1)
    return jnp.concatenate([rot.astype(x.dtype), x[..., ROPE_DIMS:]], axis=-1)


def banded_sink_attention(q, k, v, sinks):
    f32 = jnp.float32
    B, T, _, hd = q.shape
    NB = T // WINDOW
    qb = q.astype(f32).reshape(B, NB, WINDOW, SWA_KV_HEADS, SWA_GROUP, hd) * (hd ** -0.5)

    def two_blocks(t):
        tb = t.astype(f32).reshape(B, NB, WINDOW, SWA_KV_HEADS, hd)
        prev = jnp.pad(tb[:, :-1], ((0, 0), (1, 0), (0, 0), (0, 0), (0, 0)))
        return jnp.concatenate([prev, tb], axis=2)

    kb, vb = two_blocks(k), two_blocks(v)
    s = jnp.einsum('bnqhgd,bnkhd->bnhgqk', qb, kb)
    qi = jnp.arange(WINDOW)[:, None]
    kj = jnp.arange(2 * WINDOW)[None, :]
    diff = qi + WINDOW - kj
    band = (diff >= 0) & (diff < WINDOW)
    valid = band[None] & ((jnp.arange(NB)[:, None, None] > 0) | (kj[None] >= WINDOW))
    s = jnp.where(valid[None, :, None, None], s, -jnp.inf)
    sink = sinks.astype(f32).reshape(SWA_KV_HEADS, SWA_GROUP)[None, None, :, :, None, None]
    m = jnp.maximum(jnp.max(s, axis=-1, keepdims=True), sink)
    p = jnp.exp(s - m)
    denom = jnp.sum(p, axis=-1, keepdims=True) + jnp.exp(sink - m)
    o = jnp.einsum('bnhgqk,bnkhd->bnqhgd', p / denom, vb)
    return o.reshape(B, T, SWA_Q_HEADS, hd)


def swa_mixer(h, w_in, b_in, sinks, w_out, b_out):
    f32 = jnp.float32
    B, T, _ = h.shape
    proj = h @ w_in
    qkv = proj[..., :SWA_QKV] + b_in
    gate = proj[..., SWA_QKV:]
    q, k, v = jnp.split(qkv, [MIX1, MIX1 + SWA_KV], axis=-1)
    pos = jnp.arange(T)
    q = partial_rope(q.reshape(B, T, SWA_Q_HEADS, SWA_HEAD), pos)
    k = partial_rope(k.reshape(B, T, SWA_KV_HEADS, SWA_HEAD), pos)
    v = v.reshape(B, T, SWA_KV_HEADS, SWA_HEAD)
    o = banded_sink_attention(q, k, v, sinks).reshape(B, T, MIX1)
    o = o * jax.nn.silu(gate.astype(f32))
    return (o.astype(h.dtype) @ w_out + b_out).astype(h.dtype)


def setup_inputs(seed: int = 0) -> dict:
    key = jax.random.key(seed)
    ks = jax.random.split(key, 24)
    f32 = jnp.float32
    nrm = lambda k, s: jax.random.normal(k, s, f32)
    return {
        "x": nrm(ks[0], (BATCH, SEQ, D_MODEL)),
        "norm_w": 1.0 + 0.02 * nrm(ks[1], (DEPTH, D_MODEL)),
        "w_in0": nrm(ks[2], (N_EVEN, D_MODEL, IN0)) * D_MODEL ** -0.5,
        "gla_gk_up": nrm(ks[3], (N_EVEN, GLA_GATE_RANK, GLA_KEY)) * GLA_GATE_RANK ** -0.5,
        "gla_gk_bias": 0.1 * nrm(ks[4], (N_EVEN, GLA_KEY)),
        "gla_norm_w": 1.0 + 0.02 * nrm(ks[5], (N_EVEN, GLA_DV)),
        "rwkv_mu": jax.random.uniform(ks[6], (N_EVEN, RWKV_SHIFT), f32),
        "rwkv_w0": jax.random.uniform(ks[7], (N_EVEN, RWKV_W), f32, -4.0, 1.0),
        "rwkv_w_up": nrm(ks[8], (N_EVEN, RWKV_DECAY_RANK, RWKV_W)) * 0.5 * RWKV_DECAY_RANK ** -0.5,
        "rwkv_a0": 0.2 * nrm(ks[9], (N_EVEN, RWKV_W)),
        "rwkv_a_up": nrm(ks[10], (N_EVEN, RWKV_A_RANK, RWKV_W)) * 0.5 * RWKV_A_RANK ** -0.5,
        "rwkv_k_k": 0.85 + 0.05 * nrm(ks[11], (N_EVEN, RWKV_W)),
        "rwkv_k_a": 1.0 + 0.05 * nrm(ks[12], (N_EVEN, RWKV_W)),
        "rwkv_r_k": 0.1 * nrm(ks[13], (N_EVEN, RWKV_HEADS, RWKV_HEAD)),
        "rwkv_ln_w": 1.0 + 0.02 * nrm(ks[14], (N_EVEN, RWKV_W)),
        "rwkv_ln_b": 0.02 * nrm(ks[15], (N_EVEN, RWKV_W)),
        "w_out0": nrm(ks[16], (N_EVEN, MIX0, D_MODEL)) * MIX0 ** -0.5,
        "w_in1": nrm(ks[17], (N_ODD, D_MODEL, IN1)) * D_MODEL ** -0.5,
        "b_in1": 0.02 * nrm(ks[18], (N_ODD, SWA_QKV)),
        "attn_sinks": 0.5 * nrm(ks[19], (N_ODD, SWA_Q_HEADS)),
        "w_out1": nrm(ks[20], (N_ODD, MIX1, D_MODEL)) * MIX1 ** -0.5,
        "b_out1": 0.02 * nrm(ks[21], (N_ODD, D_MODEL)),
        "final_norm_w": 1.0 + 0.02 * nrm(ks[22], (D_MODEL,)),
    }


def reference(x, norm_w, w_in0, gla_gk_up, gla_gk_bias, gla_norm_w, rwkv_mu, rwkv_w0, rwkv_w_up,
              rwkv_a0, rwkv_a_up, rwkv_k_k, rwkv_k_a, rwkv_r_k, rwkv_ln_w, rwkv_ln_b, w_out0,
              w_in1, b_in1, attn_sinks, w_out1, b_out1, final_norm_w):
    h = x
    for layer in range(DEPTH):
        i = layer // 2
        hn = rmsnorm(h, norm_w[layer])
        if layer % 2 == 0:
            h = h + gla_rwkv_mixer(hn, w_in0[i], gla_gk_up[i], gla_gk_bias[i], gla_norm_w[i],
                                   rwkv_mu[i], rwkv_w0[i], rwkv_w_up[i], rwkv_a0[i], rwkv_a_up[i],
                                   rwkv_k_k[i], rwkv_k_a[i], rwkv_r_k[i], rwkv_ln_w[i], rwkv_ln_b[i],
                                   w_out0[i])
        else:
            h = h + swa_mixer(hn, w_in1[i], b_in1[i], attn_sinks[i], w_out1[i], b_out1[i])
    return rmsnorm(h, final_norm_w)
```

```python
import functools

import jax
import jax.numpy as jnp
from jax import lax
from jax.experimental import pallas as pl
from jax.experimental.pallas import tpu as pltpu

F32 = jnp.float32
BF16 = jnp.bfloat16

D_MODEL = 1024
NORM_EPS = 1e-5

GLA_HEADS = 4
GLA_DK = 64
GLA_DV = 128
GLA_KEY = GLA_HEADS * GLA_DK
GLA_VAL = GLA_HEADS * GLA_DV
GLA_GATE_RANK = 16
GLA_GATE_NORMALIZER = 16.0
GLA_CHUNK = 64

RWKV_HEADS = 8
RWKV_HEAD = 64
RWKV_W = RWKV_HEADS * RWKV_HEAD
RWKV_DECAY_RANK = 64
RWKV_A_RANK = 64
RWKV_LN_EPS = 64e-5
RWKV_SHIFT = 3 * RWKV_W + RWKV_DECAY_RANK + RWKV_A_RANK
RWKV_CHUNK = 64

MIX0 = GLA_VAL + RWKV_W

SWA_Q_HEADS = 16
SWA_KV_HEADS = 4
SWA_GROUP = SWA_Q_HEADS // SWA_KV_HEADS
SWA_HEAD = 64
WINDOW = 128
ROPE_DIMS = SWA_HEAD // 4
ROPE_THETA = 500000.0
MIX1 = SWA_Q_HEADS * SWA_HEAD
SWA_KV = SWA_KV_HEADS * SWA_HEAD
SWA_QKV = MIX1 + 2 * SWA_KV

LANES = 128
HEAD = 64
GLOW_PAD = LANES
IN0_PAD = 2 * GLA_KEY + GLA_VAL + GLOW_PAD + RWKV_SHIFT + MIX0
VMEM_LIMIT = 56 * 1024 * 1024


def _cparams(sem):
    return pltpu.CompilerParams(dimension_semantics=sem, vmem_limit_bytes=VMEM_LIMIT)


def _dot(a, b):
    return jnp.dot(a.astype(BF16), b.astype(BF16), preferred_element_type=F32)


def _dot_nt(a, b):
    return lax.dot_general(a.astype(BF16), b.astype(BF16), (((1,), (1,)), ((), ())),
                           preferred_element_type=F32)


def _dot_tn(a, b):
    return lax.dot_general(a.astype(BF16), b.astype(BF16), (((0,), (0,)), ((), ())),
                           preferred_element_type=F32)


def _split3(x):
    hi = x.astype(BF16)
    r1 = x - hi.astype(F32)
    mid = r1.astype(BF16)
    lo = (r1 - mid.astype(F32)).astype(BF16)
    return hi, mid, lo


def _dot_exact_rhs(a_bf16, x):
    hi, mid, lo = _split3(x)
    d = functools.partial(jnp.dot, preferred_element_type=F32)
    return d(a_bf16, hi) + (d(a_bf16, mid) + d(a_bf16, lo))


def _dot_exact_lhs(x, b_bf16):
    hi, mid, lo = _split3(x)
    d = functools.partial(jnp.dot, preferred_element_type=F32)
    return d(hi, b_bf16) + (d(mid, b_bf16) + d(lo, b_bf16))


def _dot_f32(a, b):
    ah = a.astype(BF16)
    al = (a - ah.astype(F32)).astype(BF16)
    bh = b.astype(BF16)
    bl = (b - bh.astype(F32)).astype(BF16)
    d = functools.partial(jnp.dot, preferred_element_type=F32)
    return d(ah, bh) + (d(ah, bl) + d(al, bh))


def _iota(shape, dim):
    return lax.broadcasted_iota(jnp.int32, shape, dim)


def _tril_ones(n, dtype=BF16):
    return (_iota((n, n), 0) >= _iota((n, n), 1)).astype(dtype)


def _head_block_ones(n=LANES, dtype=BF16):
    return ((_iota((n, n), 0) // HEAD) == (_iota((n, n), 1) // HEAD)).astype(dtype)


def _softplus(z):
    return jnp.maximum(z, 0.0) + jnp.log(1.0 + jnp.exp(-jnp.abs(z)))


def _sigmoid(z):
    return 1.0 / (1.0 + jnp.exp(-z))


def _rmsnorm_rows(x, w):
    return x * lax.rsqrt(jnp.mean(x * x, axis=-1, keepdims=True) + NORM_EPS) * w


def _in0_kernel(x_ref, nw_ref, w_ref, mu_ref,
                gq_ref, gk_ref, gv_ref, glow_ref, rwm_ref, gate_ref, carry_ref, *, tiles_per_seq):
    i = pl.program_id(0)

    @pl.when(i == 0)
    def _():
        carry_ref[...] = jnp.zeros_like(carry_ref)

    xn = _rmsnorm_rows(x_ref[...], nw_ref[...]).astype(BF16)
    c0 = 0
    for ref in (gq_ref, gk_ref, gv_ref, glow_ref):
        n = ref.shape[1]
        ref[...] = jnp.dot(xn, w_ref[:, c0:c0 + n], preferred_element_type=F32)
        c0 += n
    rw = jnp.dot(xn, w_ref[:, c0:c0 + RWKV_SHIFT], preferred_element_type=F32)
    c0 += RWKV_SHIFT
    gate_ref[...] = jnp.dot(xn, w_ref[:, c0:c0 + MIX0], preferred_element_type=F32)

    tm = rw.shape[0]
    first = (i % tiles_per_seq) == 0
    prev_last = jnp.where(first, 0.0, carry_ref[7:8, :])
    rolled = pltpu.roll(rw, 1, 0)
    prev = jnp.where(_iota(rw.shape, 0) == 0, prev_last, rolled)
    rwm_ref[...] = rw + (prev - rw) * mu_ref[...]
    carry_ref[...] = rw[tm - 8:tm, :]


def _in0_call(x2, norm_w, w_pad, mu, seq_len, tm):
    n_tok = x2.shape[0]
    grid = (n_tok // tm,)
    row = lambda i: (i, 0)
    const = lambda i: (0, 0)
    outs = [GLA_KEY, GLA_KEY, GLA_VAL, GLOW_PAD, RWKV_SHIFT, MIX0]
    return pl.pallas_call(
        functools.partial(_in0_kernel, tiles_per_seq=seq_len // tm),
        grid=grid,
        in_specs=[pl.BlockSpec((tm, D_MODEL), row),
                  pl.BlockSpec((1, D_MODEL), const),
                  pl.BlockSpec((D_MODEL, IN0_PAD), const),
                  pl.BlockSpec((1, RWKV_SHIFT), const)],
        out_specs=[pl.BlockSpec((tm, n), row) for n in outs],
        out_shape=[jax.ShapeDtypeStruct((n_tok, n), F32) for n in outs],
        scratch_shapes=[pltpu.VMEM((8, RWKV_SHIFT), F32)],
        compiler_params=_cparams(("arbitrary",)),
        name="l0_norm_proj",
    )(x2, norm_w, w_pad, mu)


def _gla_kernel(q_ref, k_ref, glow_ref, v_ref, up_ref, bias_ref, nw_ref, o_ref, st_ref, *, chunks):
    c = pl.program_id(2)

    @pl.when(c == 0)
    def _():
        st_ref[...] = jnp.zeros_like(st_ref)

    C = GLA_CHUNK
    tril = _tril_ones(C)
    causal = _iota((C, C), 0) >= _iota((C, C), 1)
    lane = _iota((C, LANES), 1)
    scale = GLA_DK ** -0.5
    for j in range(chunks):
        rows = slice(j * C, (j + 1) * C)
        z = _dot(glow_ref[rows, :], up_ref[...]) + bias_ref[...]
        g = -_softplus(-z) / GLA_GATE_NORMALIZER
        b = _dot_exact_rhs(tril, g)
        ref = b[C // 2:C // 2 + 1, :]
        b_last = b[C - 1:C, :]
        q = q_ref[rows, :] * scale
        k = k_ref[rows, :]
        qe = q * jnp.exp(b - ref)
        ke = k * jnp.exp(ref - b)
        qb = q * jnp.exp(b)
        kl = k * jnp.exp(b_last - b)
        dec = jnp.exp(b_last)
        for h in range(2):
            mh = (lane // HEAD) == h
            v = v_ref[rows, h * GLA_DV:(h + 1) * GLA_DV]
            att = jnp.where(causal, _dot_nt(jnp.where(mh, qe, 0.0), ke), 0.0)
            st = st_ref[h]
            o = _dot(att, v) + _dot_nt(jnp.where(mh, qb, 0.0), st)
            st_ref[h] = st * dec + _dot_tn(v, jnp.where(mh, kl, 0.0))
            o = o * lax.rsqrt(jnp.mean(o * o, axis=-1, keepdims=True) + NORM_EPS) * nw_ref[...]
            o_ref[rows, h * GLA_DV:(h + 1) * GLA_DV] = o


def _gla_call(gq, gk, glow, gv, up_pad, bias, norm_w, batch, seq_len, chunks):
    n_tok = gq.shape[0]
    tcb = chunks * GLA_CHUNK
    steps = seq_len // tcb
    tok = lambda b, p, c: (b * steps + c, p)
    tok0 = lambda b, p, c: (b * steps + c, 0)
    return pl.pallas_call(
        functools.partial(_gla_kernel, chunks=chunks),
        grid=(batch, GLA_KEY // LANES, steps),
        in_specs=[pl.BlockSpec((tcb, LANES), tok),
                  pl.BlockSpec((tcb, LANES), tok),
                  pl.BlockSpec((tcb, GLOW_PAD), tok0),
                  pl.BlockSpec((tcb, 2 * GLA_DV), tok),
                  pl.BlockSpec((GLOW_PAD, LANES), lambda b, p, c: (0, p)),
                  pl.BlockSpec((1, LANES), lambda b, p, c: (0, p)),
                  pl.BlockSpec((1, GLA_DV), lambda b, p, c: (0, 0))],
        out_specs=pl.BlockSpec((tcb, 2 * GLA_DV), tok),
        out_shape=jax.ShapeDtypeStruct((n_tok, GLA_VAL), F32),
        scratch_shapes=[pltpu.VMEM((2, GLA_DV, LANES), F32)],
        compiler_params=_cparams(("parallel", "parallel", "arbitrary")),
        name="l0_gla",
    )(gq, gk, glow, gv, up_pad, bias, norm_w)


def _unit_lower_inverse(low, n):
    r = _iota((n, n), 0)
    c = _iota((n, n), 1)
    eye = (r == c).astype(F32)
    t = eye + jnp.where((r % 2 == 1) & (c == r - 1), low, 0.0)
    s = 2
    while s < n:
        sub = ((r // s) % 2 == 1) & ((c // s) == (r // s) - 1)
        t = t + _dot_f32(t, _dot_f32(jnp.where(sub, low, 0.0), t))
        s *= 2
    return t


def _rwkv_chunk_kernel(r_ref, k_ref, v_ref, xwa_ref, w0_ref, wup_ref, a0_ref, aup_ref,
                       kk_ref, ka_ref, rk_ref,
                       rp_ref, op_ref, bonus_ref, m_ref, n_ref, *, chunks):
    C = RWKV_CHUNK
    tril = _tril_ones(C)
    rr = _iota((C, C), 0)
    cc = _iota((C, C), 1)
    strict = rr > cc
    incl = rr >= cc
    lane = _iota((C, LANES), 1)
    hb = _head_block_ones()
    sq_r = _iota((LANES, LANES), 0)
    sq_c = _iota((LANES, LANES), 1)
    same_head = (sq_r // HEAD) == (sq_c // HEAD)
    eye128 = sq_r == sq_c
    for j in range(chunks):
        rows = slice(j * C, (j + 1) * C)
        r = r_ref[rows, :]
        k = k_ref[rows, :]
        v = v_ref[rows, :]
        xwa = xwa_ref[rows, :]
        w = -_softplus(-(w0_ref[...] + _dot(jnp.tanh(xwa), wup_ref[...]))) - 0.5
        lw = -jnp.exp(w)
        a_sig = _sigmoid(a0_ref[...] + _dot(xwa, aup_ref[...]))
        kk = k * kk_ref[...]
        kk = kk / jnp.maximum(jnp.sqrt(_dot_exact_lhs(kk * kk, hb)), 1e-12)
        k = k * (1.0 + (a_sig - 1.0) * ka_ref[...])
        bonus_ref[rows, :] = _dot_exact_lhs(r * k * rk_ref[...], hb) * v
        a = -kk
        b = kk * a_sig

        cum = _dot_exact_rhs(tril, lw)
        cum_last = cum[C - 1:C, :]
        e_pos = jnp.exp(cum)
        e_neg = jnp.exp(-cum)
        e_end = jnp.exp(cum_last - cum)
        rt = r * e_pos
        at = a * jnp.exp(cum - lw)
        bt = b * e_neg
        kt = k * e_neg
        bh = b * e_end
        kh = k * e_end
        rhs = jnp.concatenate([bt, kt], axis=0)

        akv = []
        arkv = []
        x_w = []
        x_z = []
        a_rb = []
        for h in range(2):
            mh = (lane // HEAD) == h
            lhs = jnp.concatenate([jnp.where(mh, at, 0.0), jnp.where(mh, rt, 0.0)], axis=0)
            p = _dot_nt(lhs, rhs)
            a_ab = jnp.where(strict, p[:C, :C], 0.0)
            a_ak = jnp.where(strict, p[:C, C:], 0.0)
            a_rb.append(jnp.where(incl, p[C:, :C], 0.0))
            a_rk = jnp.where(incl, p[C:, C:], 0.0)
            tinv = _unit_lower_inverse(a_ab, C)
            akv_h = _dot(a_ak, v)
            arkv.append(_dot(a_rk, v))
            x_w.append(_dot(tinv, at))
            x_z.append(_dot(tinv, akv_h))
        m0 = (lane // HEAD) == 0
        w_mat = jnp.where(m0, x_w[0], x_w[1])
        z_mat = jnp.where(m0, x_z[0], x_z[1])
        rp_ref[rows, :] = rt + jnp.where(m0, _dot(a_rb[0], w_mat), _dot(a_rb[1], w_mat))
        op_ref[rows, :] = (jnp.where(m0, _dot(a_rb[0], z_mat), _dot(a_rb[1], z_mat))
                           + jnp.where(m0, arkv[0], arkv[1]))
        m_full = _dot_tn(bh, w_mat)
        n_full = _dot_tn(bh, z_mat) + _dot_tn(kh, v)
        m_ref[0, 0, j] = jnp.where(eye128, jnp.exp(cum_last), 0.0) + jnp.where(same_head, m_full, 0.0)
        n_ref[0, 0, j] = jnp.where(same_head, n_full, 0.0)


def _rwkv_chunk_call(rwm, w0, wup_pad, a0, aup_pad, k_k, k_a, r_k, batch, seq_len, chunks):
    n_tok = rwm.shape[0]
    tcb = chunks * RWKV_CHUNK
    steps = seq_len // tcb
    pairs = RWKV_W // LANES
    nc = seq_len // RWKV_CHUNK
    col = lambda off: (lambda b, p, c: (b * steps + c, off + p))
    par = lambda b, p, c: (0, p)
    tok = lambda b, p, c: (b * steps + c, p)
    mat = lambda b, p, c: (b, p, c, 0, 0)
    return pl.pallas_call(
        functools.partial(_rwkv_chunk_kernel, chunks=chunks),
        grid=(batch, pairs, steps),
        in_specs=[pl.BlockSpec((tcb, LANES), col(0)),
                  pl.BlockSpec((tcb, LANES), col(pairs)),
                  pl.BlockSpec((tcb, LANES), col(2 * pairs)),
                  pl.BlockSpec((tcb, LANES), lambda b, p, c: (b * steps + c, 3 * pairs)),
                  pl.BlockSpec((1, LANES), par),
                  pl.BlockSpec((LANES, LANES), par),
                  pl.BlockSpec((1, LANES), par),
                  pl.BlockSpec((LANES, LANES), par),
                  pl.BlockSpec((1, LANES), par),
                  pl.BlockSpec((1, LANES), par),
                  pl.BlockSpec((1, LANES), par)],
        out_specs=[pl.BlockSpec((tcb, LANES), tok),
                   pl.BlockSpec((tcb, LANES), tok),
                   pl.BlockSpec((tcb, LANES), tok),
                   pl.BlockSpec((1, 1, chunks, LANES, LANES), mat),
                   pl.BlockSpec((1, 1, chunks, LANES, LANES), mat)],
        out_shape=[jax.ShapeDtypeStruct((n_tok, RWKV_W), F32),
                   jax.ShapeDtypeStruct((n_tok, RWKV_W), F32),
                   jax.ShapeDtypeStruct((n_tok, RWKV_W), F32),
                   jax.ShapeDtypeStruct((batch, pairs, nc, LANES, LANES), F32),
                   jax.ShapeDtypeStruct((batch, pairs, nc, LANES, LANES), F32)],
        compiler_params=_cparams(("parallel", "parallel", "parallel")),
        name="l0_rwkv_chunks",
    )(rwm, rwm, rwm, rwm, w0, wup_pad, a0, aup_pad, k_k, k_a, r_k)


def _rwkv_scan_kernel(rp_ref, op_ref, bonus_ref, m_ref, n_ref, lnw_ref, lnb_ref, o_ref, st_ref, *, chunks):
    c = pl.program_id(2)

    @pl.when(c == 0)
    def _():
        st_ref[...] = jnp.zeros_like(st_ref)

    C = RWKV_CHUNK
    hb = _head_block_ones()
    for j in range(chunks):
        rows = slice(j * C, (j + 1) * C)
        st = st_ref[...]
        o = _dot_f32(rp_ref[rows, :], st) + op_ref[rows, :]
        st_ref[...] = _dot_f32(m_ref[0, 0, j], st) + n_ref[0, 0, j]
        mean = _dot_exact_lhs(o, hb) * (1.0 / RWKV_HEAD)
        d = o - mean
        var = _dot_exact_lhs(d * d, hb) * (1.0 / RWKV_HEAD)
        o_ref[rows, :] = d * lax.rsqrt(var + RWKV_LN_EPS) * lnw_ref[...] + lnb_ref[...] + bonus_ref[rows, :]


def _rwkv_scan_call(rp, op, bonus, m, n, ln_w, ln_b, batch, seq_len, chunks):
    n_tok = rp.shape[0]
    tcb = chunks * RWKV_CHUNK
    steps = seq_len // tcb
    pairs = RWKV_W // LANES
    tok = lambda b, p, c: (b * steps + c, p)
    par = lambda b, p, c: (0, p)
    mat = lambda b, p, c: (b, p, c, 0, 0)
    return pl.pallas_call(
        functools.partial(_rwkv_scan_kernel, chunks=chunks),
        grid=(batch, pairs, steps),
        in_specs=[pl.BlockSpec((tcb, LANES), tok),
                  pl.BlockSpec((tcb, LANES), tok),
                  pl.BlockSpec((tcb, LANES), tok),
                  pl.BlockSpec((1, 1, chunks, LANES, LANES), mat),
                  pl.BlockSpec((1, 1, chunks, LANES, LANES), mat),
                  pl.BlockSpec((1, LANES), par),
                  pl.BlockSpec((1, LANES), par)],
        out_specs=pl.BlockSpec((tcb, LANES), tok),
        out_shape=jax.ShapeDtypeStruct((n_tok, RWKV_W), F32),
        scratch_shapes=[pltpu.VMEM((LANES, LANES), F32)],
        compiler_params=_cparams(("parallel", "parallel", "arbitrary")),
        name="l0_rwkv_scan",
    )(rp, op, bonus, m, n, ln_w, ln_b)


def _out0_kernel(oa_ref, ob_ref, gate_ref, x_ref, w_ref, h_ref):
    g = gate_ref[...]
    g = g * _sigmoid(g)
    ya = (oa_ref[...] * g[:, :GLA_VAL]).astype(BF16)
    yb = (ob_ref[...] * g[:, GLA_VAL:]).astype(BF16)
    h_ref[...] = (x_ref[...]
                  + jnp.dot(ya, w_ref[:GLA_VAL, :], preferred_element_type=F32)
                  + jnp.dot(yb, w_ref[GLA_VAL:, :], preferred_element_type=F32))


def _out0_call(oa, ob, gate, x2, w_out, tm):
    n_tok = x2.shape[0]
    row = lambda i: (i, 0)
    const = lambda i: (0, 0)
    return pl.pallas_call(
        _out0_kernel,
        grid=(n_tok // tm,),
        in_specs=[pl.BlockSpec((tm, GLA_VAL), row),
                  pl.BlockSpec((tm, RWKV_W), row),
                  pl.BlockSpec((tm, MIX0), row),
                  pl.BlockSpec((tm, D_MODEL), row),
                  pl.BlockSpec((MIX0, D_MODEL), const)],
        out_specs=pl.BlockSpec((tm, D_MODEL), row),
        out_shape=jax.ShapeDtypeStruct((n_tok, D_MODEL), F32),
        compiler_params=_cparams(("parallel",)),
        name="l0_gate_out",
    )(oa, ob, gate, x2, w_out)


def _rope_group(x, cos, sin_lo, sin_hi):
    half = ROPE_DIMS // 2
    return x * cos + pltpu.roll(x, LANES - half, 1) * sin_lo + pltpu.roll(x, half, 1) * sin_hi


def _in1_kernel(h_ref, nw_ref, w_ref, b_ref, cos_ref, slo_ref, shi_ref,
                q_ref, k_ref, v_ref, gate_ref):
    hn = _rmsnorm_rows(h_ref[...], nw_ref[...]).astype(BF16)
    cos = cos_ref[...]
    slo = slo_ref[...]
    shi = shi_ref[...]
    scale = SWA_HEAD ** -0.5
    for g in range(MIX1 // LANES):
        cols = slice(g * LANES, (g + 1) * LANES)
        y = jnp.dot(hn, w_ref[:, cols], preferred_element_type=F32) + b_ref[:, cols]
        q_ref[:, cols] = (_rope_group(y, cos, slo, shi) * scale).astype(q_ref.dtype)
    for g in range(SWA_KV // LANES):
        cols = slice(MIX1 + g * LANES, MIX1 + (g + 1) * LANES)
        y = jnp.dot(hn, w_ref[:, cols], preferred_element_type=F32) + b_ref[:, cols]
        k_ref[:, g * LANES:(g + 1) * LANES] = _rope_group(y, cos, slo, shi).astype(k_ref.dtype)
    cols = slice(MIX1 + SWA_KV, SWA_QKV)
    v_ref[...] = (jnp.dot(hn, w_ref[:, cols], preferred_element_type=F32) + b_ref[:, cols]).astype(v_ref.dtype)
    gate_ref[...] = jnp.dot(hn, w_ref[:, SWA_QKV:], preferred_element_type=F32)


def _in1_call(h1, norm_w, w_in, b_in, cos, slo, shi, seq_len, tm):
    n_tok = h1.shape[0]
    tps = seq_len // tm
    row = lambda i: (i, 0)
    const = lambda i: (0, 0)
    pos = lambda i: (i % tps, 0)
    return pl.pallas_call(
        _in1_kernel,
        grid=(n_tok // tm,),
        in_specs=[pl.BlockSpec((tm, D_MODEL), row),
                  pl.BlockSpec((1, D_MODEL), const),
                  pl.BlockSpec((D_MODEL, SWA_QKV + MIX1), const),
                  pl.BlockSpec((1, SWA_QKV), const),
                  pl.BlockSpec((tm, LANES), pos),
                  pl.BlockSpec((tm, LANES), pos),
                  pl.BlockSpec((tm, LANES), pos)],
        out_specs=[pl.BlockSpec((tm, MIX1), row),
                   pl.BlockSpec((tm, SWA_KV), row),
                   pl.BlockSpec((tm, SWA_KV), row),
                   pl.BlockSpec((tm, MIX1), row)],
        out_shape=[jax.ShapeDtypeStruct((n_tok, MIX1), BF16),
                   jax.ShapeDtypeStruct((n_tok, SWA_KV), BF16),
                   jax.ShapeDtypeStruct((n_tok, SWA_KV), BF16),
                   jax.ShapeDtypeStruct((n_tok, MIX1), F32)],
        compiler_params=_cparams(("parallel",)),
        name="l1_norm_proj_rope",
    )(h1, norm_w, w_in, b_in, cos, slo, shi)


def _swa_kernel(sink_ref, q_ref, kc_ref, kp_ref, vc_ref, vp_ref, o_ref):
    n = pl.program_id(1)
    W = WINDOW
    qi = _iota((W, 2 * W), 0)
    kj = _iota((W, 2 * W), 1)
    diff = qi + W - kj
    first_key = jnp.where(n > 0, 0, W)
    valid = (diff >= 0) & (diff < W) & (kj >= first_key)
    lane = _iota((W, LANES), 1)
    groups = MIX1 // LANES // (SWA_KV // LANES)
    for pp in range(SWA_KV // LANES):
        cols = slice(pp * LANES, (pp + 1) * LANES)
        kk = jnp.concatenate([kp_ref[:, cols], kc_ref[:, cols]], axis=0)
        vv = jnp.concatenate([vp_ref[:, cols], vc_ref[:, cols]], axis=0)
        for g in range(groups):
            blk = pp * groups + g
            q = q_ref[:, blk * LANES:(blk + 1) * LANES]
            outs = []
            for hh in range(2):
                sink = sink_ref[2 * blk + hh]
                qm = jnp.where((lane // HEAD) == hh, q, jnp.zeros_like(q))
                s = lax.dot_general(qm, kk, (((1,), (1,)), ((), ())), preferred_element_type=F32)
                s = jnp.where(valid, s, -jnp.inf)
                m = jnp.maximum(jnp.max(s, axis=-1, keepdims=True), sink)
                p = jnp.exp(s - m)
                denom = jnp.sum(p, axis=-1, keepdims=True) + jnp.exp(sink - m)
                outs.append(jnp.dot(p.astype(BF16), vv, preferred_element_type=F32) / denom)
            o_ref[:, blk * LANES:(blk + 1) * LANES] = jnp.where((lane // HEAD) == 0, outs[0], outs[1])


def _swa_call(sinks, q, k, v, batch, seq_len):
    n_tok = q.shape[0]
    nb = seq_len // WINDOW
    cur = lambda b, n: (b * nb + n, 0)
    prev = lambda b, n: (jnp.maximum(b * nb + n - 1, 0), 0)
    return pl.pallas_call(
        _swa_kernel,
        grid=(batch, nb),
        in_specs=[pl.BlockSpec(memory_space=pltpu.SMEM),
                  pl.BlockSpec((WINDOW, MIX1), cur),
                  pl.BlockSpec((WINDOW, SWA_KV), cur),
                  pl.BlockSpec((WINDOW, SWA_KV), prev),
                  pl.BlockSpec((WINDOW, SWA_KV), cur),
                  pl.BlockSpec((WINDOW, SWA_KV), prev)],
        out_specs=pl.BlockSpec((WINDOW, MIX1), cur),
        out_shape=jax.ShapeDtypeStruct((n_tok, MIX1), F32),
        compiler_params=_cparams(("parallel", "parallel")),
        name="l1_swa",
    )(sinks, q, k, k, v, v)


def _out1_kernel(o_ref, gate_ref, h_ref, w_ref, b_ref, nw_ref, y_ref):
    g = gate_ref[...]
    y = (o_ref[...] * (g * _sigmoid(g))).astype(BF16)
    h = h_ref[...] + jnp.dot(y, w_ref[...], preferred_element_type=F32) + b_ref[...]
    y_ref[...] = _rmsnorm_rows(h, nw_ref[...])


def _out1_call(o, gate, h1, w_out, b_out, norm_w, tm):
    n_tok = h1.shape[0]
    row = lambda i: (i, 0)
    const = lambda i: (0, 0)
    return pl.pallas_call(
        _out1_kernel,
        grid=(n_tok // tm,),
        in_specs=[pl.BlockSpec((tm, MIX1), row),
                  pl.BlockSpec((tm, MIX1), row),
                  pl.BlockSpec((tm, D_MODEL), row),
                  pl.BlockSpec((MIX1, D_MODEL), const),
                  pl.BlockSpec((1, D_MODEL), const),
                  pl.BlockSpec((1, D_MODEL), const)],
        out_specs=pl.BlockSpec((tm, D_MODEL), row),
        out_shape=jax.ShapeDtypeStruct((n_tok, D_MODEL), F32),
        compiler_params=_cparams(("parallel",)),
        name="l1_gate_out_norm",
    )(o, gate, h1, w_out, b_out, norm_w)


def _pad_rows(w, rows):
    return jnp.concatenate([w, jnp.zeros((rows - w.shape[0], w.shape[1]), w.dtype)], axis=0)


def _swa_head_order():
    order = []
    for pp in range(SWA_KV_HEADS // 2):
        for g in range(SWA_GROUP):
            order += [(2 * pp) * SWA_GROUP + g, (2 * pp + 1) * SWA_GROUP + g]
    return order


def _head_cols(order):
    return jnp.concatenate([jnp.arange(h * SWA_HEAD, (h + 1) * SWA_HEAD) for h in order])


def _rope_tables(seq_len):
    half = ROPE_DIMS // 2
    inv_freq = ROPE_THETA ** (-jnp.arange(half, dtype=F32) / half)
    ang = jnp.arange(seq_len).astype(F32)[:, None] * inv_freq
    cos = jnp.cos(ang)
    sin = jnp.sin(ang)
    ones = jnp.ones((seq_len, SWA_HEAD - ROPE_DIMS), F32)
    zeros = jnp.zeros((seq_len, SWA_HEAD - ROPE_DIMS), F32)
    z8 = jnp.zeros_like(sin)
    cos_h = jnp.concatenate([cos, cos, ones], axis=1)
    slo_h = jnp.concatenate([-sin, z8, zeros], axis=1)
    shi_h = jnp.concatenate([z8, sin, zeros], axis=1)
    two = lambda t: jnp.concatenate([t, t], axis=1)
    return two(cos_h), two(slo_h), two(shi_h)


def _forward(x, norm_w, w_in0, gla_gk_up, gla_gk_bias, gla_norm_w, rwkv_mu, rwkv_w0, rwkv_w_up,
             rwkv_a0, rwkv_a_up, rwkv_k_k, rwkv_k_a, rwkv_r_k, rwkv_ln_w, rwkv_ln_b, w_out0,
             w_in1, b_in1, attn_sinks, w_out1, b_out1, final_norm_w, *, tm, gla_chunks, rwkv_chunks):
    batch, seq_len, _ = x.shape
    x2 = x.reshape(batch * seq_len, D_MODEL)
    row = lambda t: t.reshape(1, -1)

    c_glow = 2 * GLA_KEY + GLA_VAL
    w0 = w_in0[0]
    w_pad = jnp.concatenate(
        [w0[:, :c_glow + GLA_GATE_RANK],
         jnp.zeros((D_MODEL, GLOW_PAD - GLA_GATE_RANK), w0.dtype),
         w0[:, c_glow + GLA_GATE_RANK:]], axis=1).astype(BF16)
    gq, gk, gv, glow, rwm, gate0 = _in0_call(x2, row(norm_w[0]), w_pad, row(rwkv_mu[0]), seq_len, tm)

    up_pad = _pad_rows(gla_gk_up[0], GLOW_PAD).astype(BF16)
    o_a = _gla_call(gq, gk, glow, gv, up_pad, row(gla_gk_bias[0]), row(gla_norm_w[0]),
                    batch, seq_len, gla_chunks)

    zeros_r = jnp.zeros((RWKV_DECAY_RANK, RWKV_W), F32)
    wup_pad = jnp.concatenate([rwkv_w_up[0], zeros_r], axis=0).astype(BF16)
    aup_pad = jnp.concatenate([zeros_r, rwkv_a_up[0]], axis=0).astype(BF16)
    rp, op, bonus, m, n = _rwkv_chunk_call(
        rwm, row(rwkv_w0[0]), wup_pad, row(rwkv_a0[0]), aup_pad,
        row(rwkv_k_k[0]), row(rwkv_k_a[0]), row(rwkv_r_k[0]), batch, seq_len, rwkv_chunks)
    o_b = _rwkv_scan_call(rp, op, bonus, m, n, row(rwkv_ln_w[0]), row(rwkv_ln_b[0]),
                          batch, seq_len, rwkv_chunks)

    h1 = _out0_call(o_a, o_b, gate0, x2, w_out0[0].astype(BF16), tm)

    qcols = _head_cols(_swa_head_order())
    w1 = w_in1[0]
    w1p = jnp.concatenate([w1[:, :MIX1][:, qcols], w1[:, MIX1:SWA_QKV], w1[:, SWA_QKV:][:, qcols]],
                          axis=1).astype(BF16)
    b1 = b_in1[0]
    b1p = row(jnp.concatenate([b1[:MIX1][qcols], b1[MIX1:]]))
    sinks_p = attn_sinks[0][jnp.asarray(_swa_head_order())]
    cos, slo, shi = _rope_tables(seq_len)
    q, k, v, gate1 = _in1_call(h1, row(norm_w[1]), w1p, b1p, cos, slo, shi, seq_len, tm)
    o1 = _swa_call(sinks_p, q, k, v, batch, seq_len)
    w_out1p = w_out1[0][qcols, :].astype(BF16)
    y = _out1_call(o1, gate1, h1, w_out1p, row(b_out1[0]), row(final_norm_w), tm)
    return y.reshape(batch, seq_len, D_MODEL)


def kernel(x, norm_w, w_in0, gla_gk_up, gla_gk_bias, gla_norm_w, rwkv_mu, rwkv_w0, rwkv_w_up, rwkv_a0,
           rwkv_a_up, rwkv_k_k, rwkv_k_a, rwkv_r_k, rwkv_ln_w, rwkv_ln_b, w_out0, w_in1, b_in1,
           attn_sinks, w_out1, b_out1, final_norm_w):
    return _forward(x, norm_w, w_in0, gla_gk_up, gla_gk_bias, gla_norm_w, rwkv_mu, rwkv_w0, rwkv_w_up,
                    rwkv_a0, rwkv_a_up, rwkv_k_k, rwkv_k_a, rwkv_r_k, rwkv_ln_w, rwkv_ln_b, w_out0,
                    w_in1, b_in1, attn_sinks, w_out1, b_out1, final_norm_w,
                    tm=256, gla_chunks=4, rwkv_chunks=2)
```

```python
import functools

import jax
import jax.numpy as jnp
from jax import lax
from jax.experimental import pallas as pl
from jax.experimental.pallas import tpu as pltpu

F32 = jnp.float32
BF16 = jnp.bfloat16

D_MODEL = 1024
NORM_EPS = 1e-5

GLA_HEADS = 4
GLA_DK = 64
GLA_DV = 128
GLA_KEY = GLA_HEADS * GLA_DK
GLA_VAL = GLA_HEADS * GLA_DV
GLA_GATE_RANK = 16
GLA_GATE_NORMALIZER = 16.0
GLA_CHUNK = 64

RWKV_HEADS = 8
RWKV_HEAD = 64
RWKV_W = RWKV_HEADS * RWKV_HEAD
RWKV_DECAY_RANK = 64
RWKV_A_RANK = 64
RWKV_LN_EPS = 64e-5
RWKV_SHIFT = 3 * RWKV_W + RWKV_DECAY_RANK + RWKV_A_RANK
RWKV_CHUNK = 64

MIX0 = GLA_VAL + RWKV_W

SWA_Q_HEADS = 16
SWA_KV_HEADS = 4
SWA_GROUP = SWA_Q_HEADS // SWA_KV_HEADS
SWA_HEAD = 64
WINDOW = 128
ROPE_DIMS = SWA_HEAD // 4
ROPE_THETA = 500000.0
MIX1 = SWA_Q_HEADS * SWA_HEAD
SWA_KV = SWA_KV_HEADS * SWA_HEAD
SWA_QKV = MIX1 + 2 * SWA_KV

LANES = 128
HEAD = 64
GLOW_PAD = LANES
IN0_PAD = 2 * GLA_KEY + GLA_VAL + GLOW_PAD + RWKV_SHIFT + MIX0
VMEM_LIMIT = 56 * 1024 * 1024


def _cparams(sem):
    return pltpu.CompilerParams(dimension_semantics=sem, vmem_limit_bytes=VMEM_LIMIT)


def _dot(a, b):
    return jnp.dot(a.astype(BF16), b.astype(BF16), preferred_element_type=F32)


def _dot_nt(a, b):
    return lax.dot_general(a.astype(BF16), b.astype(BF16), (((1,), (1,)), ((), ())),
                           preferred_element_type=F32)


def _dot_tn(a, b):
    return lax.dot_general(a.astype(BF16), b.astype(BF16), (((0,), (0,)), ((), ())),
                           preferred_element_type=F32)


def _split3(x):
    hi = x.astype(BF16)
    r1 = x - hi.astype(F32)
    mid = r1.astype(BF16)
    lo = (r1 - mid.astype(F32)).astype(BF16)
    return hi, mid, lo


def _dot_exact_rhs(a_bf16, x):
    hi, mid, lo = _split3(x)
    d = functools.partial(jnp.dot, preferred_element_type=F32)
    return d(a_bf16, hi) + (d(a_bf16, mid) + d(a_bf16, lo))


def _dot_exact_lhs(x, b_bf16):
    hi, mid, lo = _split3(x)
    d = functools.partial(jnp.dot, preferred_element_type=F32)
    return d(hi, b_bf16) + (d(mid, b_bf16) + d(lo, b_bf16))


def _dot_f32(a, b):
    ah = a.astype(BF16)
    al = (a - ah.astype(F32)).astype(BF16)
    bh = b.astype(BF16)
    bl = (b - bh.astype(F32)).astype(BF16)
    d = functools.partial(jnp.dot, preferred_element_type=F32)
    return d(ah, bh) + (d(ah, bl) + d(al, bh))


def _iota(shape, dim):
    return lax.broadcasted_iota(jnp.int32, shape, dim)


def _tril_ones(n, dtype=BF16):
    return (_iota((n, n), 0) >= _iota((n, n), 1)).astype(dtype)


def _head_block_ones(n=LANES, dtype=BF16):
    return ((_iota((n, n), 0) // HEAD) == (_iota((n, n), 1) // HEAD)).astype(dtype)


def _softplus(z):
    return jnp.maximum(z, 0.0) + jnp.log(1.0 + jnp.exp(-jnp.abs(z)))


def _sigmoid(z):
    return 1.0 / (1.0 + jnp.exp(-z))


def _rmsnorm_rows(x, w):
    return x * lax.rsqrt(jnp.mean(x * x, axis=-1, keepdims=True) + NORM_EPS) * w


def _in0_kernel(x_ref, nw_ref, w_ref, mu_ref,
                gq_ref, gk_ref, gv_ref, glow_ref, rwm_ref, gate_ref, carry_ref, *, tiles_per_seq):
    i = pl.program_id(0)

    @pl.when(i == 0)
    def _():
        carry_ref[...] = jnp.zeros_like(carry_ref)

    xn = _rmsnorm_rows(x_ref[...], nw_ref[...]).astype(BF16)
    c0 = 0
    for ref in (gq_ref, gk_ref, gv_ref, glow_ref):
        n = ref.shape[1]
        ref[...] = jnp.dot(xn, w_ref[:, c0:c0 + n], preferred_element_type=F32)
        c0 += n
    rw = jnp.dot(xn, w_ref[:, c0:c0 + RWKV_SHIFT], preferred_element_type=F32)
    c0 += RWKV_SHIFT
    gate_ref[...] = jnp.dot(xn, w_ref[:, c0:c0 + MIX0], preferred_element_type=F32)

    tm = rw.shape[0]
    first = (i % tiles_per_seq) == 0
    prev_last = jnp.where(first, 0.0, carry_ref[7:8, :])
    rolled = pltpu.roll(rw, 1, 0)
    prev = jnp.where(_iota(rw.shape, 0) == 0, prev_last, rolled)
    rwm_ref[...] = rw + (prev - rw) * mu_ref[...]
    carry_ref[...] = rw[tm - 8:tm, :]


def _in0_call(x2, norm_w, w_pad, mu, seq_len, tm):
    n_tok = x2.shape[0]
    grid = (n_tok // tm,)
    row = lambda i: (i, 0)
    const = lambda i: (0, 0)
    outs = [GLA_KEY, GLA_KEY, GLA_VAL, GLOW_PAD, RWKV_SHIFT, MIX0]
    return pl.pallas_call(
        functools.partial(_in0_kernel, tiles_per_seq=seq_len // tm),
        grid=grid,
        in_specs=[pl.BlockSpec((tm, D_MODEL), row),
                  pl.BlockSpec((1, D_MODEL), const),
                  pl.BlockSpec((D_MODEL, IN0_PAD), const),
                  pl.BlockSpec((1, RWKV_SHIFT), const)],
        out_specs=[pl.BlockSpec((tm, n), row) for n in outs],
        out_shape=[jax.ShapeDtypeStruct((n_tok, n), F32) for n in outs],
        scratch_shapes=[pltpu.VMEM((8, RWKV_SHIFT), F32)],
        compiler_params=_cparams(("arbitrary",)),
        name="l0_norm_proj",
    )(x2, norm_w, w_pad, mu)


def _gla_kernel(q_ref, k_ref, glow_ref, v_ref, up_ref, bias_ref, nw_ref, o_ref, st_ref, *, chunks):
    c = pl.program_id(2)

    @pl.when(c == 0)
    def _():
        st_ref[...] = jnp.zeros_like(st_ref)

    C = GLA_CHUNK
    tril = _tril_ones(C)
    causal = _iota((C, C), 0) >= _iota((C, C), 1)
    lane = _iota((C, LANES), 1)
    scale = GLA_DK ** -0.5
    head_masks = [(lane // HEAD) == h for h in range(2)]
    z = _dot(glow_ref[...], up_ref[...]) + bias_ref[...]
    g_all = -_softplus(-z) / GLA_GATE_NORMALIZER
    q_all = q_ref[...] * scale
    k_all = k_ref[...]
    rows = [slice(j * C, (j + 1) * C) for j in range(chunks)]
    bs = [_dot_exact_rhs(tril, g_all[rw]) for rw in rows]
    qe, ke, qb, kl, dec = [], [], [], [], []
    for rw, b in zip(rows, bs):
        ref = b[C // 2:C // 2 + 1, :]
        b_last = b[C - 1:C, :]
        qe.append(q_all[rw] * jnp.exp(b - ref))
        ke.append(k_all[rw] * jnp.exp(ref - b))
        qb.append(q_all[rw] * jnp.exp(b))
        kl.append(k_all[rw] * jnp.exp(b_last - b))
        dec.append(jnp.exp(b_last))
    chains = [(j, h) for j in range(chunks) for h in range(2)]
    vs = [v_ref[rows[j], h * GLA_DV:(h + 1) * GLA_DV] for j, h in chains]
    att = [jnp.where(causal, _dot_nt(jnp.where(head_masks[h], qe[j], 0.0), ke[j]), 0.0) for j, h in chains]
    kv = [_dot_tn(v, jnp.where(head_masks[h], kl[j], 0.0)) for v, (j, h) in zip(vs, chains)]
    intra = [_dot(a, v) for a, v in zip(att, vs)]
    states = [[st_ref[h]] for h in range(2)]
    for j in range(chunks):
        for h in range(2):
            states[h].append(states[h][j] * dec[j] + kv[2 * j + h])
    for h in range(2):
        st_ref[h] = states[h][chunks]
    for i, (j, h) in enumerate(chains):
        o = intra[i] + _dot_nt(jnp.where(head_masks[h], qb[j], 0.0), states[h][j])
        o = o * lax.rsqrt(jnp.mean(o * o, axis=-1, keepdims=True) + NORM_EPS) * nw_ref[...]
        o_ref[rows[j], h * GLA_DV:(h + 1) * GLA_DV] = o


def _gla_call(gq, gk, glow, gv, up_pad, bias, norm_w, batch, seq_len, chunks):
    n_tok = gq.shape[0]
    tcb = chunks * GLA_CHUNK
    steps = seq_len // tcb
    tok = lambda b, p, c: (b * steps + c, p)
    tok0 = lambda b, p, c: (b * steps + c, 0)
    return pl.pallas_call(
        functools.partial(_gla_kernel, chunks=chunks),
        grid=(batch, GLA_KEY // LANES, steps),
        in_specs=[pl.BlockSpec((tcb, LANES), tok),
                  pl.BlockSpec((tcb, LANES), tok),
                  pl.BlockSpec((tcb, GLOW_PAD), tok0),
                  pl.BlockSpec((tcb, 2 * GLA_DV), tok),
                  pl.BlockSpec((GLOW_PAD, LANES), lambda b, p, c: (0, p)),
                  pl.BlockSpec((1, LANES), lambda b, p, c: (0, p)),
                  pl.BlockSpec((1, GLA_DV), lambda b, p, c: (0, 0))],
        out_specs=pl.BlockSpec((tcb, 2 * GLA_DV), tok),
        out_shape=jax.ShapeDtypeStruct((n_tok, GLA_VAL), F32),
        scratch_shapes=[pltpu.VMEM((2, GLA_DV, LANES), F32)],
        compiler_params=_cparams(("parallel", "parallel", "arbitrary")),
        name="l0_gla",
    )(gq, gk, glow, gv, up_pad, bias, norm_w)


def _merge_masks(n):
    r = _iota((n, n), 0)
    c = _iota((n, n), 1)
    masks = []
    s = 1
    while s < n:
        masks.append(((r // s) % 2 == 1) & ((c // s) == (r // s) - 1))
        s *= 2
    return (r == c).astype(F32), masks


def _unit_lower_inverses(lows, n):
    eye, masks = _merge_masks(n)
    ts = [eye + jnp.where(masks[0], low, 0.0) for low in lows]
    for sub in masks[1:]:
        ys = [_dot(jnp.where(sub, low, 0.0), t) for low, t in zip(lows, ts)]
        ts = [t + _dot(t, y) for t, y in zip(ts, ys)]
    return ts


def _rwkv_chunk_kernel(r_ref, k_ref, v_ref, xwa_ref, w0_ref, wup_ref, a0_ref, aup_ref,
                       kk_ref, ka_ref, rk_ref,
                       rp_ref, op_ref, bonus_ref, m_ref, n_ref, *, chunks):
    C = RWKV_CHUNK
    tril = _tril_ones(C)
    rr = _iota((2 * C, C), 0)
    cc = _iota((2 * C, C), 1)
    tri2 = ((rr < C) & (rr > cc)) | (rr - C >= cc)
    lane = _iota((C, LANES), 1)
    head_masks = [(lane // HEAD) == h for h in range(2)]
    hb = _head_block_ones()
    sq_r = _iota((LANES, LANES), 0)
    sq_c = _iota((LANES, LANES), 1)
    same_head = (sq_r // HEAD) == (sq_c // HEAD)
    eye128 = sq_r == sq_c

    r_all = r_ref[...]
    k_all = k_ref[...]
    v_all = v_ref[...]
    xwa = xwa_ref[...]
    w = -_softplus(-(w0_ref[...] + _dot(jnp.tanh(xwa), wup_ref[...]))) - 0.5
    lw_all = -jnp.exp(w)
    a_sig = _sigmoid(a0_ref[...] + _dot(xwa, aup_ref[...]))
    kk = k_all * kk_ref[...]
    kk = kk / jnp.maximum(jnp.sqrt(_dot_exact_lhs(kk * kk, hb)), 1e-12)
    k_all = k_all * (1.0 + (a_sig - 1.0) * ka_ref[...])
    bonus_ref[...] = _dot_exact_lhs(r_all * k_all * rk_ref[...], hb) * v_all
    a_all = -kk
    b_all = kk * a_sig

    rows = [slice(j * C, (j + 1) * C) for j in range(chunks)]
    cums = [_dot_exact_rhs(tril, lw_all[rw]) for rw in rows]
    rt, at, bh, kh, v, rhs, dec = [], [], [], [], [], [], []
    for rw, cum in zip(rows, cums):
        cum_last = cum[C - 1:C, :]
        e_pos = jnp.exp(cum)
        e_neg = jnp.exp(-cum)
        e_end = jnp.exp(cum_last - cum)
        rt.append(r_all[rw] * e_pos)
        at.append(a_all[rw] * jnp.exp(cum - lw_all[rw]))
        bh.append(b_all[rw] * e_end)
        kh.append(k_all[rw] * e_end)
        v.append(v_all[rw])
        rhs.append(jnp.concatenate([b_all[rw] * e_neg, k_all[rw] * e_neg], axis=0))
        dec.append(jnp.exp(cum_last))

    chains = [(j, h) for j in range(chunks) for h in range(2)]
    ps = [_dot_nt(jnp.concatenate([jnp.where(head_masks[h], at[j], 0.0),
                                   jnp.where(head_masks[h], rt[j], 0.0)], axis=0), rhs[j])
          for j, h in chains]
    left = [jnp.where(tri2, p[:, :C], 0.0) for p in ps]
    right = [jnp.where(tri2, p[:, C:], 0.0) for p in ps]
    tinvs = _unit_lower_inverses([lf[:C] for lf in left], C)
    kv = [_dot(rg, v[j]) for rg, (j, h) in zip(right, chains)]
    xs = [_dot(t, jnp.concatenate([at[j], q[:C]], axis=1))
          for t, q, (j, h) in zip(tinvs, kv, chains)]
    m0 = head_masks[0]
    wz = [jnp.concatenate([jnp.where(m0, xs[2 * j][:, :LANES], xs[2 * j + 1][:, :LANES]),
                           jnp.where(m0, xs[2 * j][:, LANES:], xs[2 * j + 1][:, LANES:])], axis=1)
          for j in range(chunks)]
    ro = [_dot(lf[C:], wz[j]) for lf, (j, h) in zip(left, chains)]
    mn = [_dot_tn(bh[j], wz[j]) for j in range(chunks)]
    kv_end = [_dot_tn(kh[j], v[j]) for j in range(chunks)]
    for j in range(chunks):
        r0, r1 = ro[2 * j], ro[2 * j + 1]
        rp_ref[rows[j], :] = rt[j] + jnp.where(m0, r0[:, :LANES], r1[:, :LANES])
        op_ref[rows[j], :] = (jnp.where(m0, r0[:, LANES:], r1[:, LANES:])
                              + jnp.where(m0, kv[2 * j][C:], kv[2 * j + 1][C:]))
        m_ref[0, 0, j] = jnp.where(eye128, dec[j], 0.0) + jnp.where(same_head, mn[j][:, :LANES], 0.0)
        n_ref[0, 0, j] = jnp.where(same_head, mn[j][:, LANES:] + kv_end[j], 0.0)


def _rwkv_chunk_call(rwm, w0, wup_pad, a0, aup_pad, k_k, k_a, r_k, batch, seq_len, chunks):
    n_tok = rwm.shape[0]
    tcb = chunks * RWKV_CHUNK
    steps = seq_len // tcb
    pairs = RWKV_W // LANES
    nc = seq_len // RWKV_CHUNK
    col = lambda off: (lambda b, p, c: (b * steps + c, off + p))
    par = lambda b, p, c: (0, p)
    tok = lambda b, p, c: (b * steps + c, p)
    mat = lambda b, p, c: (b, p, c, 0, 0)
    return pl.pallas_call(
        functools.partial(_rwkv_chunk_kernel, chunks=chunks),
        grid=(batch, pairs, steps),
        in_specs=[pl.BlockSpec((tcb, LANES), col(0)),
                  pl.BlockSpec((tcb, LANES), col(pairs)),
                  pl.BlockSpec((tcb, LANES), col(2 * pairs)),
                  pl.BlockSpec((tcb, LANES), lambda b, p, c: (b * steps + c, 3 * pairs)),
                  pl.BlockSpec((1, LANES), par),
                  pl.BlockSpec((LANES, LANES), par),
                  pl.BlockSpec((1, LANES), par),
                  pl.BlockSpec((LANES, LANES), par),
                  pl.BlockSpec((1, LANES), par),
                  pl.BlockSpec((1, LANES), par),
                  pl.BlockSpec((1, LANES), par)],
        out_specs=[pl.BlockSpec((tcb, LANES), tok),
                   pl.BlockSpec((tcb, LANES), tok),
                   pl.BlockSpec((tcb, LANES), tok),
                   pl.BlockSpec((1, 1, chunks, LANES, LANES), mat),
                   pl.BlockSpec((1, 1, chunks, LANES, LANES), mat)],
        out_shape=[jax.ShapeDtypeStruct((n_tok, RWKV_W), F32),
                   jax.ShapeDtypeStruct((n_tok, RWKV_W), F32),
                   jax.ShapeDtypeStruct((n_tok, RWKV_W), F32),
                   jax.ShapeDtypeStruct((batch, pairs, nc, LANES, LANES), F32),
                   jax.ShapeDtypeStruct((batch, pairs, nc, LANES, LANES), F32)],
        compiler_params=_cparams(("parallel", "parallel", "parallel")),
        name="l0_rwkv_chunks",
    )(rwm, rwm, rwm, rwm, w0, wup_pad, a0, aup_pad, k_k, k_a, r_k)


def _rwkv_scan_kernel(rp_ref, op_ref, bonus_ref, m_ref, n_ref, lnw_ref, lnb_ref, o_ref, st_ref, *, chunks):
    c = pl.program_id(1)

    @pl.when(c == 0)
    def _():
        st_ref[...] = jnp.zeros_like(st_ref)

    C = RWKV_CHUNK
    pairs = RWKV_W // LANES
    hb = _head_block_ones()
    states = [[st_ref[p]] for p in range(pairs)]
    for j in range(chunks):
        for p in range(pairs):
            states[p].append(_dot_f32(m_ref[0, p, j], states[p][j]) + n_ref[0, p, j])
    for p in range(pairs):
        st_ref[p] = states[p][chunks]
    for p in range(pairs):
        cols = slice(p * LANES, (p + 1) * LANES)
        o = jnp.concatenate([_dot(rp_ref[j * C:(j + 1) * C, cols], states[p][j]) for j in range(chunks)], axis=0)
        o = o + op_ref[:, cols]
        mean = _dot_exact_lhs(o, hb) * (1.0 / RWKV_HEAD)
        d = o - mean
        var = _dot_exact_lhs(d * d, hb) * (1.0 / RWKV_HEAD)
        o_ref[:, cols] = (d * lax.rsqrt(var + RWKV_LN_EPS) * lnw_ref[:, cols] + lnb_ref[:, cols]
                          + bonus_ref[:, cols])


def _rwkv_scan_call(rp, op, bonus, m, n, ln_w, ln_b, batch, seq_len, chunks):
    n_tok = rp.shape[0]
    tcb = chunks * RWKV_CHUNK
    steps = seq_len // tcb
    pairs = RWKV_W // LANES
    tok = lambda b, c: (b * steps + c, 0)
    const = lambda b, c: (0, 0)
    mat = lambda b, c: (b, 0, c, 0, 0)
    return pl.pallas_call(
        functools.partial(_rwkv_scan_kernel, chunks=chunks),
        grid=(batch, steps),
        in_specs=[pl.BlockSpec((tcb, RWKV_W), tok),
                  pl.BlockSpec((tcb, RWKV_W), tok),
                  pl.BlockSpec((tcb, RWKV_W), tok),
                  pl.BlockSpec((1, pairs, chunks, LANES, LANES), mat),
                  pl.BlockSpec((1, pairs, chunks, LANES, LANES), mat),
                  pl.BlockSpec((1, RWKV_W), const),
                  pl.BlockSpec((1, RWKV_W), const)],
        out_specs=pl.BlockSpec((tcb, RWKV_W), tok),
        out_shape=jax.ShapeDtypeStruct((n_tok, RWKV_W), F32),
        scratch_shapes=[pltpu.VMEM((pairs, LANES, LANES), F32)],
        compiler_params=_cparams(("parallel", "arbitrary")),
        name="l0_rwkv_scan",
    )(rp, op, bonus, m, n, ln_w, ln_b)


def _out0_kernel(oa_ref, ob_ref, gate_ref, x_ref, w_ref, h_ref):
    g = gate_ref[...]
    g = g * _sigmoid(g)
    ya = (oa_ref[...] * g[:, :GLA_VAL]).astype(BF16)
    yb = (ob_ref[...] * g[:, GLA_VAL:]).astype(BF16)
    h_ref[...] = (x_ref[...]
                  + jnp.dot(ya, w_ref[:GLA_VAL, :], preferred_element_type=F32)
                  + jnp.dot(yb, w_ref[GLA_VAL:, :], preferred_element_type=F32))


def _out0_call(oa, ob, gate, x2, w_out, tm):
    n_tok = x2.shape[0]
    row = lambda i: (i, 0)
    const = lambda i: (0, 0)
    return pl.pallas_call(
        _out0_kernel,
        grid=(n_tok // tm,),
        in_specs=[pl.BlockSpec((tm, GLA_VAL), row),
                  pl.BlockSpec((tm, RWKV_W), row),
                  pl.BlockSpec((tm, MIX0), row),
                  pl.BlockSpec((tm, D_MODEL), row),
                  pl.BlockSpec((MIX0, D_MODEL), const)],
        out_specs=pl.BlockSpec((tm, D_MODEL), row),
        out_shape=jax.ShapeDtypeStruct((n_tok, D_MODEL), F32),
        compiler_params=_cparams(("parallel",)),
        name="l0_gate_out",
    )(oa, ob, gate, x2, w_out)


def _rope_group(x, cos, sin_lo, sin_hi):
    half = ROPE_DIMS // 2
    return x * cos + pltpu.roll(x, LANES - half, 1) * sin_lo + pltpu.roll(x, half, 1) * sin_hi


def _in1_kernel(h_ref, nw_ref, w_ref, b_ref, cos_ref, slo_ref, shi_ref,
                q_ref, k_ref, v_ref, gate_ref):
    hn = _rmsnorm_rows(h_ref[...], nw_ref[...]).astype(BF16)
    cos = cos_ref[...]
    slo = slo_ref[...]
    shi = shi_ref[...]
    scale = SWA_HEAD ** -0.5
    for g in range(MIX1 // LANES):
        cols = slice(g * LANES, (g + 1) * LANES)
        y = jnp.dot(hn, w_ref[:, cols], preferred_element_type=F32) + b_ref[:, cols]
        q_ref[:, cols] = (_rope_group(y, cos, slo, shi) * scale).astype(q_ref.dtype)
    for g in range(SWA_KV // LANES):
        cols = slice(MIX1 + g * LANES, MIX1 + (g + 1) * LANES)
        y = jnp.dot(hn, w_ref[:, cols], preferred_element_type=F32) + b_ref[:, cols]
        k_ref[:, g * LANES:(g + 1) * LANES] = _rope_group(y, cos, slo, shi).astype(k_ref.dtype)
    cols = slice(MIX1 + SWA_KV, SWA_QKV)
    v_ref[...] = (jnp.dot(hn, w_ref[:, cols], preferred_element_type=F32) + b_ref[:, cols]).astype(v_ref.dtype)
    gate_ref[...] = jnp.dot(hn, w_ref[:, SWA_QKV:], preferred_element_type=F32)


def _in1_call(h1, norm_w, w_in, b_in, cos, slo, shi, seq_len, tm):
    n_tok = h1.shape[0]
    tps = seq_len // tm
    row = lambda i: (i, 0)
    const = lambda i: (0, 0)
    pos = lambda i: (i % tps, 0)
    return pl.pallas_call(
        _in1_kernel,
        grid=(n_tok // tm,),
        in_specs=[pl.BlockSpec((tm, D_MODEL), row),
                  pl.BlockSpec((1, D_MODEL), const),
                  pl.BlockSpec((D_MODEL, SWA_QKV + MIX1), const),
                  pl.BlockSpec((1, SWA_QKV), const),
                  pl.BlockSpec((tm, LANES), pos),
                  pl.BlockSpec((tm, LANES), pos),
                  pl.BlockSpec((tm, LANES), pos)],
        out_specs=[pl.BlockSpec((tm, MIX1), row),
                   pl.BlockSpec((tm, SWA_KV), row),
                   pl.BlockSpec((tm, SWA_KV), row),
                   pl.BlockSpec((tm, MIX1), row)],
        out_shape=[jax.ShapeDtypeStruct((n_tok, MIX1), BF16),
                   jax.ShapeDtypeStruct((n_tok, SWA_KV), BF16),
                   jax.ShapeDtypeStruct((n_tok, SWA_KV), BF16),
                   jax.ShapeDtypeStruct((n_tok, MIX1), F32)],
        compiler_params=_cparams(("parallel",)),
        name="l1_norm_proj_rope",
    )(h1, norm_w, w_in, b_in, cos, slo, shi)


def _swa_kernel(sink_ref, q_ref, kc_ref, kp_ref, vc_ref, vp_ref, o_ref):
    n = pl.program_id(1)
    W = WINDOW
    qi = _iota((W, 2 * W), 0)
    kj = _iota((W, 2 * W), 1)
    diff = qi + W - kj
    first_key = jnp.where(n > 0, 0, W)
    valid = (diff >= 0) & (diff < W) & (kj >= first_key)
    lane = _iota((W, LANES), 1)
    groups = MIX1 // LANES // (SWA_KV // LANES)
    for pp in range(SWA_KV // LANES):
        cols = slice(pp * LANES, (pp + 1) * LANES)
        kk = jnp.concatenate([kp_ref[:, cols], kc_ref[:, cols]], axis=0)
        vv = jnp.concatenate([vp_ref[:, cols], vc_ref[:, cols]], axis=0)
        for g in range(groups):
            blk = pp * groups + g
            q = q_ref[:, blk * LANES:(blk + 1) * LANES]
            outs = []
            for hh in range(2):
                sink = sink_ref[2 * blk + hh]
                qm = jnp.where((lane // HEAD) == hh, q, jnp.zeros_like(q))
                s = lax.dot_general(qm, kk, (((1,), (1,)), ((), ())), preferred_element_type=F32)
                s = jnp.where(valid, s, -jnp.inf)
                m = jnp.maximum(jnp.max(s, axis=-1, keepdims=True), sink)
                p = jnp.exp(s - m)
                denom = jnp.sum(p, axis=-1, keepdims=True) + jnp.exp(sink - m)
                outs.append(jnp.dot(p.astype(BF16), vv, preferred_element_type=F32) / denom)
            o_ref[:, blk * LANES:(blk + 1) * LANES] = jnp.where((lane // HEAD) == 0, outs[0], outs[1])


def _swa_call(sinks, q, k, v, batch, seq_len):
    n_tok = q.shape[0]
    nb = seq_len // WINDOW
    cur = lambda b, n: (b * nb + n, 0)
    prev = lambda b, n: (jnp.maximum(b * nb + n - 1, 0), 0)
    return pl.pallas_call(
        _swa_kernel,
        grid=(batch, nb),
        in_specs=[pl.BlockSpec(memory_space=pltpu.SMEM),
                  pl.BlockSpec((WINDOW, MIX1), cur),
                  pl.BlockSpec((WINDOW, SWA_KV), cur),
                  pl.BlockSpec((WINDOW, SWA_KV), prev),
                  pl.BlockSpec((WINDOW, SWA_KV), cur),
                  pl.BlockSpec((WINDOW, SWA_KV), prev)],
        out_specs=pl.BlockSpec((WINDOW, MIX1), cur),
        out_shape=jax.ShapeDtypeStruct((n_tok, MIX1), F32),
        compiler_params=_cparams(("parallel", "parallel")),
        name="l1_swa",
    )(sinks, q, k, k, v, v)


def _out1_kernel(o_ref, gate_ref, h_ref, w_ref, b_ref, nw_ref, y_ref):
    g = gate_ref[...]
    y = (o_ref[...] * (g * _sigmoid(g))).astype(BF16)
    h = h_ref[...] + jnp.dot(y, w_ref[...], preferred_element_type=F32) + b_ref[...]
    y_ref[...] = _rmsnorm_rows(h, nw_ref[...])


def _out1_call(o, gate, h1, w_out, b_out, norm_w, tm):
    n_tok = h1.shape[0]
    row = lambda i: (i, 0)
    const = lambda i: (0, 0)
    return pl.pallas_call(
        _out1_kernel,
        grid=(n_tok // tm,),
        in_specs=[pl.BlockSpec((tm, MIX1), row),
                  pl.BlockSpec((tm, MIX1), row),
                  pl.BlockSpec((tm, D_MODEL), row),
                  pl.BlockSpec((MIX1, D_MODEL), const),
                  pl.BlockSpec((1, D_MODEL), const),
                  pl.BlockSpec((1, D_MODEL), const)],
        out_specs=pl.BlockSpec((tm, D_MODEL), row),
        out_shape=jax.ShapeDtypeStruct((n_tok, D_MODEL), F32),
        compiler_params=_cparams(("parallel",)),
        name="l1_gate_out_norm",
    )(o, gate, h1, w_out, b_out, norm_w)


def _pad_rows(w, rows):
    return jnp.concatenate([w, jnp.zeros((rows - w.shape[0], w.shape[1]), w.dtype)], axis=0)


def _swa_head_order():
    order = []
    for pp in range(SWA_KV_HEADS // 2):
        for g in range(SWA_GROUP):
            order += [(2 * pp) * SWA_GROUP + g, (2 * pp + 1) * SWA_GROUP + g]
    return order


def _head_cols(order):
    return jnp.concatenate([jnp.arange(h * SWA_HEAD, (h + 1) * SWA_HEAD) for h in order])


def _rope_tables(seq_len):
    half = ROPE_DIMS // 2
    inv_freq = ROPE_THETA ** (-jnp.arange(half, dtype=F32) / half)
    ang = jnp.arange(seq_len).astype(F32)[:, None] * inv_freq
    cos = jnp.cos(ang)
    sin = jnp.sin(ang)
    ones = jnp.ones((seq_len, SWA_HEAD - ROPE_DIMS), F32)
    zeros = jnp.zeros((seq_len, SWA_HEAD - ROPE_DIMS), F32)
    z8 = jnp.zeros_like(sin)
    cos_h = jnp.concatenate([cos, cos, ones], axis=1)
    slo_h = jnp.concatenate([-sin, z8, zeros], axis=1)
    shi_h = jnp.concatenate([z8, sin, zeros], axis=1)
    two = lambda t: jnp.concatenate([t, t], axis=1)
    return two(cos_h), two(slo_h), two(shi_h)


def _forward(x, norm_w, w_in0, gla_gk_up, gla_gk_bias, gla_norm_w, rwkv_mu, rwkv_w0, rwkv_w_up,
             rwkv_a0, rwkv_a_up, rwkv_k_k, rwkv_k_a, rwkv_r_k, rwkv_ln_w, rwkv_ln_b, w_out0,
             w_in1, b_in1, attn_sinks, w_out1, b_out1, final_norm_w, *, tm, gla_chunks, rwkv_chunks):
    batch, seq_len, _ = x.shape
    x2 = x.reshape(batch * seq_len, D_MODEL)
    row = lambda t: t.reshape(1, -1)

    c_glow = 2 * GLA_KEY + GLA_VAL
    w0 = w_in0[0]
    w_pad = jnp.concatenate(
        [w0[:, :c_glow + GLA_GATE_RANK],
         jnp.zeros((D_MODEL, GLOW_PAD - GLA_GATE_RANK), w0.dtype),
         w0[:, c_glow + GLA_GATE_RANK:]], axis=1).astype(BF16)
    gq, gk, gv, glow, rwm, gate0 = _in0_call(x2, row(norm_w[0]), w_pad, row(rwkv_mu[0]), seq_len, tm)

    up_pad = _pad_rows(gla_gk_up[0], GLOW_PAD).astype(BF16)
    o_a = _gla_call(gq, gk, glow, gv, up_pad, row(gla_gk_bias[0]), row(gla_norm_w[0]),
                    batch, seq_len, gla_chunks)

    zeros_r = jnp.zeros((RWKV_DECAY_RANK, RWKV_W), F32)
    wup_pad = jnp.concatenate([rwkv_w_up[0], zeros_r], axis=0).astype(BF16)
    aup_pad = jnp.concatenate([zeros_r, rwkv_a_up[0]], axis=0).astype(BF16)
    rp, op, bonus, m, n = _rwkv_chunk_call(
        rwm, row(rwkv_w0[0]), wup_pad, row(rwkv_a0[0]), aup_pad,
        row(rwkv_k_k[0]), row(rwkv_k_a[0]), row(rwkv_r_k[0]), batch, seq_len, rwkv_chunks)
    o_b = _rwkv_scan_call(rp, op, bonus, m, n, row(rwkv_ln_w[0]), row(rwkv_ln_b[0]),
                          batch, seq_len, rwkv_chunks)

    h1 = _out0_call(o_a, o_b, gate0, x2, w_out0[0].astype(BF16), tm)

    qcols = _head_cols(_swa_head_order())
    w1 = w_in1[0]
    w1p = jnp.concatenate([w1[:, :MIX1][:, qcols], w1[:, MIX1:SWA_QKV], w1[:, SWA_QKV:][:, qcols]],
                          axis=1).astype(BF16)
    b1 = b_in1[0]
    b1p = row(jnp.concatenate([b1[:MIX1][qcols], b1[MIX1:]]))
    sinks_p = attn_sinks[0][jnp.asarray(_swa_head_order())]
    cos, slo, shi = _rope_tables(seq_len)
    q, k, v, gate1 = _in1_call(h1, row(norm_w[1]), w1p, b1p, cos, slo, shi, seq_len, tm)
    o1 = _swa_call(sinks_p, q, k, v, batch, seq_len)
    w_out1p = w_out1[0][qcols, :].astype(BF16)
    y = _out1_call(o1, gate1, h1, w_out1p, row(b_out1[0]), row(final_norm_w), tm)
    return y.reshape(batch, seq_len, D_MODEL)


def kernel(x, norm_w, w_in0, gla_gk_up, gla_gk_bias, gla_norm_w, rwkv_mu, rwkv_w0, rwkv_w_up, rwkv_a0,
           rwkv_a_up, rwkv_k_k, rwkv_k_a, rwkv_r_k, rwkv_ln_w, rwkv_ln_b, w_out0, w_in1, b_in1,
           attn_sinks, w_out1, b_out1, final_norm_w):
    return _forward(x, norm_w, w_in0, gla_gk_up, gla_gk_bias, gla_norm_w, rwkv_mu, rwkv_w0, rwkv_w_up,
                    rwkv_a0, rwkv_a_up, rwkv_k_k, rwkv_k_a, rwkv_r_k, rwkv_ln_w, rwkv_ln_b, w_out0,
                    w_in1, b_in1, attn_sinks, w_out1, b_out1, final_norm_w,
                    tm=256, gla_chunks=4, rwkv_chunks=8)
```

```python
import functools

import jax
import jax.numpy as jnp
from jax import lax
from jax.experimental import pallas as pl
from jax.experimental.pallas import tpu as pltpu

F32 = jnp.float32
BF16 = jnp.bfloat16

D_MODEL = 1024
NORM_EPS = 1e-5

GLA_HEADS = 4
GLA_DK = 64
GLA_DV = 128
GLA_KEY = GLA_HEADS * GLA_DK
GLA_VAL = GLA_HEADS * GLA_DV
GLA_GATE_RANK = 16
GLA_GATE_NORMALIZER = 16.0
GLA_CHUNK = 64

RWKV_HEADS = 8
RWKV_HEAD = 64
RWKV_W = RWKV_HEADS * RWKV_HEAD
RWKV_DECAY_RANK = 64
RWKV_A_RANK = 64
RWKV_LN_EPS = 64e-5
RWKV_RKV = 3 * RWKV_W
RWKV_LORA = RWKV_DECAY_RANK + RWKV_A_RANK
RWKV_SHIFT = RWKV_RKV + RWKV_LORA
RWKV_CHUNK = 64

MIX0 = GLA_VAL + RWKV_W
GLA_QKV = 2 * GLA_KEY + GLA_VAL

SWA_Q_HEADS = 16
SWA_KV_HEADS = 4
SWA_GROUP = SWA_Q_HEADS // SWA_KV_HEADS
SWA_HEAD = 64
WINDOW = 128
ROPE_DIMS = SWA_HEAD // 4
ROPE_THETA = 500000.0
MIX1 = SWA_Q_HEADS * SWA_HEAD
SWA_KV = SWA_KV_HEADS * SWA_HEAD
SWA_QKV = MIX1 + 2 * SWA_KV

LANES = 128
HEAD = 64
GLOW_PAD = LANES
VMEM_LIMIT = 56 * 1024 * 1024


def _cparams(sem):
    return pltpu.CompilerParams(dimension_semantics=sem, vmem_limit_bytes=VMEM_LIMIT)


def _dot(a, b):
    return jnp.dot(a.astype(BF16), b.astype(BF16), preferred_element_type=F32)


def _dot_nt(a, b):
    return lax.dot_general(a.astype(BF16), b.astype(BF16), (((1,), (1,)), ((), ())),
                           preferred_element_type=F32)


def _dot_tn(a, b):
    return lax.dot_general(a.astype(BF16), b.astype(BF16), (((0,), (0,)), ((), ())),
                           preferred_element_type=F32)


def _split2(x):
    hi = x.astype(BF16)
    lo = (x - hi.astype(F32)).astype(BF16)
    return hi, lo


def _dot_exact_rhs(a_bf16, x):
    hi, lo = _split2(x)
    return (jnp.dot(a_bf16, hi, preferred_element_type=F32)
            + jnp.dot(a_bf16, lo, preferred_element_type=F32))


def _dot_exact_lhs(x, b_bf16):
    hi, lo = _split2(x)
    return (jnp.dot(hi, b_bf16, preferred_element_type=F32)
            + jnp.dot(lo, b_bf16, preferred_element_type=F32))


def _iota(shape, dim):
    return lax.broadcasted_iota(jnp.int32, shape, dim)


def _tril_ones(n, dtype=BF16):
    return (_iota((n, n), 0) >= _iota((n, n), 1)).astype(dtype)


def _head_block_ones(n=LANES, dtype=BF16):
    return ((_iota((n, n), 0) // HEAD) == (_iota((n, n), 1) // HEAD)).astype(dtype)


def _head_stack(x):
    head = (_iota(x.shape, 1) % LANES) // HEAD
    return jnp.concatenate([jnp.where(head == 0, x, 0.0), jnp.where(head == 1, x, 0.0)], axis=0)


def _softplus(z):
    return jnp.maximum(z, 0.0) + jnp.log(1.0 + jnp.exp(-jnp.abs(z)))


def _sigmoid(z):
    return 1.0 / (1.0 + jnp.exp(-z))


def _rmsnorm_rows(x, w):
    return x * lax.rsqrt(jnp.mean(x * x, axis=-1, keepdims=True) + NORM_EPS) * w


def _in0_kernel(x_ref, nw_ref, wg_ref, wl_ref, wr_ref, mu_ref,
                gqkv_ref, glow_ref, rkv_ref, lora_ref, gate_ref, carry_ref, *, tiles_per_seq):
    i = pl.program_id(0)

    @pl.when(i == 0)
    def _():
        carry_ref[...] = jnp.zeros_like(carry_ref)

    xn = _rmsnorm_rows(x_ref[...], nw_ref[...]).astype(BF16)
    gqkv_ref[...] = jnp.dot(xn, wg_ref[...], preferred_element_type=F32).astype(gqkv_ref.dtype)
    glow_ref[...] = jnp.dot(xn, wl_ref[...], preferred_element_type=F32)
    rw = jnp.dot(xn, wr_ref[:, :RWKV_SHIFT], preferred_element_type=F32)
    gate_ref[...] = jnp.dot(xn, wr_ref[:, RWKV_SHIFT:], preferred_element_type=F32).astype(gate_ref.dtype)

    tm = rw.shape[0]
    first = (i % tiles_per_seq) == 0
    prev_last = jnp.where(first, 0.0, carry_ref[7:8, :])
    rolled = pltpu.roll(rw, 1, 0)
    prev = jnp.where(_iota(rw.shape, 0) == 0, prev_last, rolled)
    mixed = rw + (prev - rw) * mu_ref[...]
    rkv_ref[...] = mixed[:, :RWKV_RKV].astype(rkv_ref.dtype)
    lora_ref[...] = mixed[:, RWKV_RKV:]
    carry_ref[...] = rw[tm - 8:tm, :]


def _in0_call(x2, norm_w, w_gla, w_glow, w_rest, mu, seq_len, tm):
    n_tok = x2.shape[0]
    row = lambda i: (i, 0)
    const = lambda i: (0, 0)
    outs = [(GLA_QKV, BF16), (GLOW_PAD, F32), (RWKV_RKV, BF16), (RWKV_LORA, F32), (MIX0, BF16)]
    return pl.pallas_call(
        functools.partial(_in0_kernel, tiles_per_seq=seq_len // tm),
        grid=(n_tok // tm,),
        in_specs=[pl.BlockSpec((tm, D_MODEL), row),
                  pl.BlockSpec((1, D_MODEL), const),
                  pl.BlockSpec((D_MODEL, GLA_QKV), const),
                  pl.BlockSpec((D_MODEL, GLOW_PAD), const),
                  pl.BlockSpec((D_MODEL, RWKV_SHIFT + MIX0), const),
                  pl.BlockSpec((1, RWKV_SHIFT), const)],
        out_specs=[pl.BlockSpec((tm, n), row) for n, _ in outs],
        out_shape=[jax.ShapeDtypeStruct((n_tok, n), dt) for n, dt in outs],
        scratch_shapes=[pltpu.VMEM((8, RWKV_SHIFT), F32)],
        compiler_params=_cparams(("arbitrary",)),
        name="l0_norm_proj",
    )(x2, norm_w, w_gla, w_glow, w_rest, mu)


def _gla_kernel(q_ref, k_ref, glow_ref, v_ref, up_ref, bias_ref, nw_ref, o_ref, st_ref, *, chunks):
    c = pl.program_id(2)

    @pl.when(c == 0)
    def _():
        st_ref[...] = jnp.zeros_like(st_ref)

    C = GLA_CHUNK
    tril = _tril_ones(C)
    causal = _iota((C, LANES), 0) >= (_iota((C, LANES), 1) % HEAD)
    sr = _iota((2 * GLA_DV, LANES), 0)
    sl = _iota((2 * GLA_DV, LANES), 1)
    st_mask = (sr // GLA_DV) == (sl // HEAD)
    vl = _iota((C, 2 * GLA_DV), 1)
    scale = GLA_DK ** -0.5
    z = _dot(glow_ref[...], up_ref[...]) + bias_ref[...]
    g_all = -_softplus(-z) / GLA_GATE_NORMALIZER
    q_all = q_ref[...].astype(F32) * scale
    k_all = k_ref[...].astype(F32)
    rows = [slice(j * C, (j + 1) * C) for j in range(chunks)]
    bs = [_dot_exact_rhs(tril, g_all[rw]) for rw in rows]
    qe, ke, qb, kl, dec, vs = [], [], [], [], [], []
    for rw, b in zip(rows, bs):
        ref = b[C // 2:C // 2 + 1, :]
        b_last = b[C - 1:C, :]
        qe.append(q_all[rw] * jnp.exp(b - ref))
        ke.append(k_all[rw] * jnp.exp(ref - b))
        qb.append(q_all[rw] * jnp.exp(b))
        kl.append(k_all[rw] * jnp.exp(b_last - b))
        dec.append(jnp.exp(b_last))
        vs.append(v_ref[rw, :])
    att = [jnp.where(causal, _dot_nt(qe[j], _head_stack(ke[j])), 0.0) for j in range(chunks)]
    kv = [jnp.where(st_mask, _dot_tn(vs[j], kl[j]), 0.0) for j in range(chunks)]
    v_diag = [jnp.concatenate([jnp.where(vl < GLA_DV, vs[j], jnp.zeros_like(vs[j])),
                               jnp.where(vl >= GLA_DV, vs[j], jnp.zeros_like(vs[j]))], axis=0)
              for j in range(chunks)]
    intra = [jnp.dot(att[j].astype(BF16), v_diag[j], preferred_element_type=F32) for j in range(chunks)]
    states = [st_ref[...]]
    for j in range(chunks):
        states.append(states[j] * dec[j] + kv[j])
    st_ref[...] = states[chunks]
    for j in range(chunks):
        o = intra[j] + _dot_nt(qb[j], states[j])
        for h in range(2):
            oh = o[:, h * GLA_DV:(h + 1) * GLA_DV]
            oh = oh * lax.rsqrt(jnp.mean(oh * oh, axis=-1, keepdims=True) + NORM_EPS) * nw_ref[...]
            o_ref[rows[j], h * GLA_DV:(h + 1) * GLA_DV] = oh.astype(o_ref.dtype)


def _gla_call(gqkv, glow, up_pad, bias, norm_w, batch, seq_len, chunks):
    n_tok = gqkv.shape[0]
    tcb = chunks * GLA_CHUNK
    steps = seq_len // tcb
    pairs = GLA_KEY // LANES
    return pl.pallas_call(
        functools.partial(_gla_kernel, chunks=chunks),
        grid=(batch, pairs, steps),
        in_specs=[pl.BlockSpec((tcb, LANES), lambda b, p, c: (b * steps + c, p)),
                  pl.BlockSpec((tcb, LANES), lambda b, p, c: (b * steps + c, pairs + p)),
                  pl.BlockSpec((tcb, GLOW_PAD), lambda b, p, c: (b * steps + c, 0)),
                  pl.BlockSpec((tcb, 2 * GLA_DV), lambda b, p, c: (b * steps + c, pairs + p)),
                  pl.BlockSpec((GLOW_PAD, LANES), lambda b, p, c: (0, p)),
                  pl.BlockSpec((1, LANES), lambda b, p, c: (0, p)),
                  pl.BlockSpec((1, GLA_DV), lambda b, p, c: (0, 0))],
        out_specs=pl.BlockSpec((tcb, 2 * GLA_DV), lambda b, p, c: (b * steps + c, p)),
        out_shape=jax.ShapeDtypeStruct((n_tok, GLA_VAL), BF16),
        scratch_shapes=[pltpu.VMEM((2 * GLA_DV, LANES), F32)],
        compiler_params=_cparams(("parallel", "parallel", "arbitrary")),
        name="l0_gla",
    )(gqkv, gqkv, glow, gqkv, up_pad, bias, norm_w)


def _merge_masks(n):
    r = _iota((n, LANES), 0)
    c = _iota((n, LANES), 1) % HEAD
    masks = []
    s = 1
    while s < n:
        masks.append(((r // s) % 2 == 1) & ((c // s) == (r // s) - 1))
        s *= 2
    return (r == c).astype(F32), masks


def _unit_lower_inverses(lows, n):
    eye, masks = _merge_masks(n)
    ts = [eye + jnp.where(masks[0], low, 0.0) for low in lows]
    for sub in masks[1:]:
        ys = [_dot(jnp.where(sub, low, 0.0), _head_stack(t)) for low, t in zip(lows, ts)]
        ts = [t + _dot(t, _head_stack(y)) for t, y in zip(ts, ys)]
    return ts


def _rwkv_chunk_kernel(r_ref, k_ref, v_ref, xwa_ref, w0_ref, wup_ref, a0_ref, aup_ref,
                       kk_ref, ka_ref, rk_ref,
                       rp_ref, op_ref, bonus_ref, m_ref, n_ref, *, chunks):
    C = RWKV_CHUNK
    tril = _tril_ones(C)
    rr = _iota((2 * C, LANES), 0)
    cc = _iota((2 * C, LANES), 1) % HEAD
    tri2 = ((rr < C) & (rr > cc)) | (rr - C >= cc)
    hb = _head_block_ones()
    sq_r = _iota((LANES, LANES), 0)
    sq_c = _iota((LANES, LANES), 1)
    same_head = (sq_r // HEAD) == (sq_c // HEAD)
    eye128 = sq_r == sq_c

    r_all = r_ref[...].astype(F32)
    k_all = k_ref[...].astype(F32)
    v_all = v_ref[...].astype(F32)
    xwa = xwa_ref[...]
    w = -_softplus(-(w0_ref[...] + _dot(jnp.tanh(xwa), wup_ref[...]))) - 0.5
    lw_all = -jnp.exp(w)
    a_sig = _sigmoid(a0_ref[...] + _dot(xwa, aup_ref[...]))
    kk = k_all * kk_ref[...]
    kk = kk / jnp.maximum(jnp.sqrt(_dot_exact_lhs(kk * kk, hb)), 1e-12)
    k_all = k_all * (1.0 + (a_sig - 1.0) * ka_ref[...])
    bonus_ref[...] = _dot_exact_lhs(r_all * k_all * rk_ref[...], hb) * v_all
    a_all = -kk
    b_all = kk * a_sig

    rows = [slice(j * C, (j + 1) * C) for j in range(chunks)]
    cums = [_dot_exact_rhs(tril, lw_all[rw]) for rw in rows]
    rt, at, bt, kt, ends, v, dec = [], [], [], [], [], [], []
    for rw, cum in zip(rows, cums):
        cum_last = cum[C - 1:C, :]
        e_neg = jnp.exp(-cum)
        e_end = jnp.exp(cum_last - cum)
        rt.append(r_all[rw] * jnp.exp(cum))
        at.append(a_all[rw] * jnp.exp(cum - lw_all[rw]))
        bt.append(b_all[rw] * e_neg)
        kt.append(k_all[rw] * e_neg)
        ends.append(jnp.concatenate([b_all[rw] * e_end, k_all[rw] * e_end], axis=0))
        v.append(v_all[rw])
        dec.append(jnp.exp(cum_last))

    n = range(chunks)
    lhs = [jnp.concatenate([at[j], rt[j]], axis=0) for j in n]
    left = [jnp.where(tri2, _dot_nt(lhs[j], _head_stack(bt[j])), 0.0) for j in n]
    right = [jnp.where(tri2, _dot_nt(lhs[j], _head_stack(kt[j])), 0.0) for j in n]
    tinv = _unit_lower_inverses([lf[:C] for lf in left], C)
    kv = [_dot(right[j], _head_stack(v[j])) for j in n]
    wz = [_dot(tinv[j], _head_stack(jnp.concatenate([at[j], kv[j][:C]], axis=1))) for j in n]
    ro = [_dot(left[j][C:], _head_stack(wz[j])) for j in n]
    zero = jnp.zeros((C, LANES), F32)
    mn = [_dot_tn(ends[j], jnp.concatenate([wz[j], jnp.concatenate([zero, v[j]], axis=1)], axis=0))
          for j in n]
    for j in n:
        rp_ref[rows[j], :] = (rt[j] + ro[j][:, :LANES]).astype(rp_ref.dtype)
        op_ref[rows[j], :] = ro[j][:, LANES:] + kv[j][C:]
        m_ref[0, 0, j] = (jnp.where(eye128, dec[j], 0.0)
                          + jnp.where(same_head, mn[j][:, :LANES], 0.0)).astype(m_ref.dtype)
        n_ref[0, 0, j] = jnp.where(same_head, mn[j][:, LANES:], 0.0)


def _rwkv_chunk_call(rkv, lora, w0, wup_pad, a0, aup_pad, k_k, k_a, r_k, batch, seq_len, chunks):
    n_tok = rkv.shape[0]
    tcb = chunks * RWKV_CHUNK
    steps = seq_len // tcb
    pairs = RWKV_W // LANES
    nc = seq_len // RWKV_CHUNK
    col = lambda off: (lambda b, p, c: (b * steps + c, off + p))
    par = lambda b, p, c: (0, p)
    tok = lambda b, p, c: (b * steps + c, p)
    mat = lambda b, p, c: (b, p, c, 0, 0)
    return pl.pallas_call(
        functools.partial(_rwkv_chunk_kernel, chunks=chunks),
        grid=(batch, pairs, steps),
        in_specs=[pl.BlockSpec((tcb, LANES), col(0)),
                  pl.BlockSpec((tcb, LANES), col(pairs)),
                  pl.BlockSpec((tcb, LANES), col(2 * pairs)),
                  pl.BlockSpec((tcb, RWKV_LORA), lambda b, p, c: (b * steps + c, 0)),
                  pl.BlockSpec((1, LANES), par),
                  pl.BlockSpec((RWKV_LORA, LANES), par),
                  pl.BlockSpec((1, LANES), par),
                  pl.BlockSpec((RWKV_LORA, LANES), par),
                  pl.BlockSpec((1, LANES), par),
                  pl.BlockSpec((1, LANES), par),
                  pl.BlockSpec((1, LANES), par)],
        out_specs=[pl.BlockSpec((tcb, LANES), tok),
                   pl.BlockSpec((tcb, LANES), tok),
                   pl.BlockSpec((tcb, LANES), tok),
                   pl.BlockSpec((1, 1, chunks, LANES, LANES), mat),
                   pl.BlockSpec((1, 1, chunks, LANES, LANES), mat)],
        out_shape=[jax.ShapeDtypeStruct((n_tok, RWKV_W), BF16),
                   jax.ShapeDtypeStruct((n_tok, RWKV_W), F32),
                   jax.ShapeDtypeStruct((n_tok, RWKV_W), F32),
                   jax.ShapeDtypeStruct((batch, pairs, nc, LANES, LANES), BF16),
                   jax.ShapeDtypeStruct((batch, pairs, nc, LANES, LANES), F32)],
        compiler_params=_cparams(("parallel", "parallel", "parallel")),
        name="l0_rwkv_chunks",
    )(rkv, rkv, rkv, lora, w0, wup_pad, a0, aup_pad, k_k, k_a, r_k)


def _rwkv_scan_kernel(rp_ref, op_ref, bonus_ref, m_ref, n_ref, lnw_ref, lnb_ref, o_ref, st_ref, *, chunks):
    c = pl.program_id(0)

    @pl.when(c == 0)
    def _():
        st_ref[...] = jnp.zeros_like(st_ref)

    C = RWKV_CHUNK
    batch = rp_ref.shape[0]
    pairs = RWKV_W // LANES
    hb = _head_block_ones()
    seqs = [(b, p) for b in range(batch) for p in range(pairs)]
    states = {bp: [st_ref[bp[0], bp[1]]] for bp in seqs}
    for j in range(chunks):
        for b, p in seqs:
            states[b, p].append(_dot(m_ref[b, p, j], states[b, p][j]) + n_ref[b, p, j])
    for b, p in seqs:
        st_ref[b, p] = states[b, p][chunks]
    for b, p in seqs:
        cols = slice(p * LANES, (p + 1) * LANES)
        o = jnp.concatenate([_dot(rp_ref[b, j * C:(j + 1) * C, cols], states[b, p][j]) for j in range(chunks)],
                            axis=0)
        o = o + op_ref[b, :, cols]
        mean = _dot_exact_lhs(o, hb) * (1.0 / RWKV_HEAD)
        d = o - mean
        var = _dot_exact_lhs(d * d, hb) * (1.0 / RWKV_HEAD)
        o_ref[b, :, cols] = (d * lax.rsqrt(var + RWKV_LN_EPS) * lnw_ref[:, cols] + lnb_ref[:, cols]
                             + bonus_ref[b, :, cols]).astype(o_ref.dtype)


def _rwkv_scan_call(rp, op, bonus, m, n, ln_w, ln_b, batch, seq_len, chunks):
    tcb = chunks * RWKV_CHUNK
    pairs = RWKV_W // LANES
    seq3 = lambda t: t.reshape(batch, seq_len, RWKV_W)
    tok = lambda c: (0, c, 0)
    const = lambda c: (0, 0)
    mat = lambda c: (0, 0, c, 0, 0)
    out = pl.pallas_call(
        functools.partial(_rwkv_scan_kernel, chunks=chunks),
        grid=(seq_len // tcb,),
        in_specs=[pl.BlockSpec((batch, tcb, RWKV_W), tok),
                  pl.BlockSpec((batch, tcb, RWKV_W), tok),
                  pl.BlockSpec((batch, tcb, RWKV_W), tok),
                  pl.BlockSpec((batch, pairs, chunks, LANES, LANES), mat),
                  pl.BlockSpec((batch, pairs, chunks, LANES, LANES), mat),
                  pl.BlockSpec((1, RWKV_W), const),
                  pl.BlockSpec((1, RWKV_W), const)],
        out_specs=pl.BlockSpec((batch, tcb, RWKV_W), tok),
        out_shape=jax.ShapeDtypeStruct((batch, seq_len, RWKV_W), BF16),
        scratch_shapes=[pltpu.VMEM((batch, pairs, LANES, LANES), F32)],
        compiler_params=_cparams(("arbitrary",)),
        name="l0_rwkv_scan",
    )(seq3(rp), seq3(op), seq3(bonus), m, n, ln_w, ln_b)
    return out.reshape(batch * seq_len, RWKV_W)


def _out0_kernel(oa_ref, ob_ref, gate_ref, x_ref, w_ref, h_ref):
    g = gate_ref[...].astype(F32)
    g = g * _sigmoid(g)
    ya = (oa_ref[...].astype(F32) * g[:, :GLA_VAL]).astype(BF16)
    yb = (ob_ref[...].astype(F32) * g[:, GLA_VAL:]).astype(BF16)
    h_ref[...] = (x_ref[...]
                  + jnp.dot(ya, w_ref[:GLA_VAL, :], preferred_element_type=F32)
                  + jnp.dot(yb, w_ref[GLA_VAL:, :], preferred_element_type=F32))


def _out0_call(oa, ob, gate, x2, w_out, tm):
    n_tok = x2.shape[0]
    row = lambda i: (i, 0)
    const = lambda i: (0, 0)
    return pl.pallas_call(
        _out0_kernel,
        grid=(n_tok // tm,),
        in_specs=[pl.BlockSpec((tm, GLA_VAL), row),
                  pl.BlockSpec((tm, RWKV_W), row),
                  pl.BlockSpec((tm, MIX0), row),
                  pl.BlockSpec((tm, D_MODEL), row),
                  pl.BlockSpec((MIX0, D_MODEL), const)],
        out_specs=pl.BlockSpec((tm, D_MODEL), row),
        out_shape=jax.ShapeDtypeStruct((n_tok, D_MODEL), F32),
        compiler_params=_cparams(("parallel",)),
        name="l0_gate_out",
    )(oa, ob, gate, x2, w_out)


def _rope_group(x, cos, sin_lo, sin_hi):
    half = ROPE_DIMS // 2
    return x * cos + pltpu.roll(x, LANES - half, 1) * sin_lo + pltpu.roll(x, half, 1) * sin_hi


def _in1_kernel(h_ref, nw_ref, w_ref, b_ref, cos_ref, slo_ref, shi_ref,
                q_ref, k_ref, v_ref, gate_ref):
    hn = _rmsnorm_rows(h_ref[...], nw_ref[...]).astype(BF16)
    cos = cos_ref[...]
    slo = slo_ref[...]
    shi = shi_ref[...]
    scale = SWA_HEAD ** -0.5
    for g in range(MIX1 // LANES):
        cols = slice(g * LANES, (g + 1) * LANES)
        y = jnp.dot(hn, w_ref[:, cols], preferred_element_type=F32) + b_ref[:, cols]
        q_ref[:, cols] = (_rope_group(y, cos, slo, shi) * scale).astype(q_ref.dtype)
    for g in range(SWA_KV // LANES):
        cols = slice(MIX1 + g * LANES, MIX1 + (g + 1) * LANES)
        y = jnp.dot(hn, w_ref[:, cols], preferred_element_type=F32) + b_ref[:, cols]
        k_ref[:, g * LANES:(g + 1) * LANES] = _rope_group(y, cos, slo, shi).astype(k_ref.dtype)
    cols = slice(MIX1 + SWA_KV, SWA_QKV)
    v_ref[...] = (jnp.dot(hn, w_ref[:, cols], preferred_element_type=F32) + b_ref[:, cols]).astype(v_ref.dtype)
    gate_ref[...] = jnp.dot(hn, w_ref[:, SWA_QKV:], preferred_element_type=F32).astype(gate_ref.dtype)


def _in1_call(h1, norm_w, w_in, b_in, cos, slo, shi, seq_len, tm):
    n_tok = h1.shape[0]
    tps = seq_len // tm
    row = lambda i: (i, 0)
    const = lambda i: (0, 0)
    pos = lambda i: (i % tps, 0)
    return pl.pallas_call(
        _in1_kernel,
        grid=(n_tok // tm,),
        in_specs=[pl.BlockSpec((tm, D_MODEL), row),
                  pl.BlockSpec((1, D_MODEL), const),
                  pl.BlockSpec((D_MODEL, SWA_QKV + MIX1), const),
                  pl.BlockSpec((1, SWA_QKV), const),
                  pl.BlockSpec((tm, LANES), pos),
                  pl.BlockSpec((tm, LANES), pos),
                  pl.BlockSpec((tm, LANES), pos)],
        out_specs=[pl.BlockSpec((tm, MIX1), row),
                   pl.BlockSpec((tm, SWA_KV), row),
                   pl.BlockSpec((tm, SWA_KV), row),
                   pl.BlockSpec((tm, MIX1), row)],
        out_shape=[jax.ShapeDtypeStruct((n_tok, MIX1), BF16),
                   jax.ShapeDtypeStruct((n_tok, SWA_KV), BF16),
                   jax.ShapeDtypeStruct((n_tok, SWA_KV), BF16),
                   jax.ShapeDtypeStruct((n_tok, MIX1), BF16)],
        compiler_params=_cparams(("parallel",)),
        name="l1_norm_proj_rope",
    )(h1, norm_w, w_in, b_in, cos, slo, shi)


def _swa_kernel(sink_ref, q_ref, kc_ref, kp_ref, vc_ref, vp_ref, o_ref):
    n = pl.program_id(1)
    W = WINDOW
    qi = _iota((W, 2 * W), 0)
    kj = _iota((W, 2 * W), 1)
    diff = qi + W - kj
    first_key = jnp.where(n > 0, 0, W)
    valid = (diff >= 0) & (diff < W) & (kj >= first_key)
    lane = _iota((W, LANES), 1)
    groups = MIX1 // LANES // (SWA_KV // LANES)
    for pp in range(SWA_KV // LANES):
        cols = slice(pp * LANES, (pp + 1) * LANES)
        kk = jnp.concatenate([kp_ref[:, cols], kc_ref[:, cols]], axis=0)
        vv = jnp.concatenate([vp_ref[:, cols], vc_ref[:, cols]], axis=0)
        for g in range(groups):
            blk = pp * groups + g
            q = q_ref[:, blk * LANES:(blk + 1) * LANES]
            outs = []
            for hh in range(2):
                sink = sink_ref[2 * blk + hh]
                qm = jnp.where((lane // HEAD) == hh, q, jnp.zeros_like(q))
                s = lax.dot_general(qm, kk, (((1,), (1,)), ((), ())), preferred_element_type=F32)
                s = jnp.where(valid, s, -jnp.inf)
                m = jnp.maximum(jnp.max(s, axis=-1, keepdims=True), sink)
                p = jnp.exp(s - m)
                denom = jnp.sum(p, axis=-1, keepdims=True) + jnp.exp(sink - m)
                outs.append(jnp.dot(p.astype(BF16), vv, preferred_element_type=F32) / denom)
            o_ref[:, blk * LANES:(blk + 1) * LANES] = jnp.where((lane // HEAD) == 0, outs[0], outs[1]).astype(o_ref.dtype)


def _swa_call(sinks, q, k, v, batch, seq_len):
    n_tok = q.shape[0]
    nb = seq_len // WINDOW
    cur = lambda b, n: (b * nb + n, 0)
    prev = lambda b, n: (jnp.maximum(b * nb + n - 1, 0), 0)
    return pl.pallas_call(
        _swa_kernel,
        grid=(batch, nb),
        in_specs=[pl.BlockSpec(memory_space=pltpu.SMEM),
                  pl.BlockSpec((WINDOW, MIX1), cur),
                  pl.BlockSpec((WINDOW, SWA_KV), cur),
                  pl.BlockSpec((WINDOW, SWA_KV), prev),
                  pl.BlockSpec((WINDOW, SWA_KV), cur),
                  pl.BlockSpec((WINDOW, SWA_KV), prev)],
        out_specs=pl.BlockSpec((WINDOW, MIX1), cur),
        out_shape=jax.ShapeDtypeStruct((n_tok, MIX1), BF16),
        compiler_params=_cparams(("parallel", "parallel")),
        name="l1_swa",
    )(sinks, q, k, k, v, v)


def _out1_kernel(o_ref, gate_ref, h_ref, w_ref, b_ref, nw_ref, y_ref):
    g = gate_ref[...].astype(F32)
    y = (o_ref[...].astype(F32) * (g * _sigmoid(g))).astype(BF16)
    h = h_ref[...] + jnp.dot(y, w_ref[...], preferred_element_type=F32) + b_ref[...]
    y_ref[...] = _rmsnorm_rows(h, nw_ref[...])


def _out1_call(o, gate, h1, w_out, b_out, norm_w, tm):
    n_tok = h1.shape[0]
    row = lambda i: (i, 0)
    const = lambda i: (0, 0)
    return pl.pallas_call(
        _out1_kernel,
        grid=(n_tok // tm,),
        in_specs=[pl.BlockSpec((tm, MIX1), row),
                  pl.BlockSpec((tm, MIX1), row),
                  pl.BlockSpec((tm, D_MODEL), row),
                  pl.BlockSpec((MIX1, D_MODEL), const),
                  pl.BlockSpec((1, D_MODEL), const),
                  pl.BlockSpec((1, D_MODEL), const)],
        out_specs=pl.BlockSpec((tm, D_MODEL), row),
        out_shape=jax.ShapeDtypeStruct((n_tok, D_MODEL), F32),
        compiler_params=_cparams(("parallel",)),
        name="l1_gate_out_norm",
    )(o, gate, h1, w_out, b_out, norm_w)


def _pad_rows(w, rows):
    return jnp.concatenate([w, jnp.zeros((rows - w.shape[0], w.shape[1]), w.dtype)], axis=0)


def _pair_heads(t, axis):
    shape = t.shape
    split = shape[:axis] + (SWA_KV_HEADS // 2, 2, SWA_GROUP, SWA_HEAD) + shape[axis + 1:]
    return jnp.swapaxes(t.reshape(split), axis + 1, axis + 2).reshape(shape)


def _rope_tables(seq_len):
    half = ROPE_DIMS // 2
    inv_freq = ROPE_THETA ** (-jnp.arange(half, dtype=F32) / half)
    d = jnp.arange(LANES) % SWA_HEAD
    ang = jnp.arange(seq_len).astype(F32)[:, None] * inv_freq[d % half][None, :]
    cos = jnp.where(d < ROPE_DIMS, jnp.cos(ang), 1.0)
    sin = jnp.sin(ang)
    sin_lo = jnp.where(d < half, -sin, 0.0)
    sin_hi = jnp.where((d >= half) & (d < ROPE_DIMS), sin, 0.0)
    return cos, sin_lo, sin_hi


def _forward(x, norm_w, w_in0, gla_gk_up, gla_gk_bias, gla_norm_w, rwkv_mu, rwkv_w0, rwkv_w_up,
             rwkv_a0, rwkv_a_up, rwkv_k_k, rwkv_k_a, rwkv_r_k, rwkv_ln_w, rwkv_ln_b, w_out0,
             w_in1, b_in1, attn_sinks, w_out1, b_out1, final_norm_w, *, tm, gla_chunks, rwkv_chunks, scan_chunks):
    batch, seq_len, _ = x.shape
    x2 = x.reshape(batch * seq_len, D_MODEL)
    row = lambda t: t.reshape(1, -1)

    w0 = w_in0[0]
    w_gla = w0[:, :GLA_QKV].astype(BF16)
    w_glow = jnp.pad(w0[:, GLA_QKV:GLA_QKV + GLA_GATE_RANK].astype(BF16),
                     ((0, 0), (0, GLOW_PAD - GLA_GATE_RANK)))
    w_rest = w0[:, GLA_QKV + GLA_GATE_RANK:].astype(BF16)
    gqkv, glow, rkv, lora, gate0 = _in0_call(x2, row(norm_w[0]), w_gla, w_glow, w_rest, row(rwkv_mu[0]),
                                             seq_len, tm)

    up_pad = _pad_rows(gla_gk_up[0], GLOW_PAD).astype(BF16)
    o_a = _gla_call(gqkv, glow, up_pad, row(gla_gk_bias[0]), row(gla_norm_w[0]), batch, seq_len, gla_chunks)

    zeros_r = jnp.zeros((RWKV_DECAY_RANK, RWKV_W), F32)
    wup_pad = jnp.concatenate([rwkv_w_up[0], zeros_r], axis=0).astype(BF16)
    aup_pad = jnp.concatenate([zeros_r, rwkv_a_up[0]], axis=0).astype(BF16)
    rp, op, bonus, m, n = _rwkv_chunk_call(
        rkv, lora, row(rwkv_w0[0]), wup_pad, row(rwkv_a0[0]), aup_pad,
        row(rwkv_k_k[0]), row(rwkv_k_a[0]), row(rwkv_r_k[0]), batch, seq_len, rwkv_chunks)
    o_b = _rwkv_scan_call(rp, op, bonus, m, n, row(rwkv_ln_w[0]), row(rwkv_ln_b[0]),
                          batch, seq_len, scan_chunks)

    h1 = _out0_call(o_a, o_b, gate0, x2, w_out0[0].astype(BF16), tm)

    w1 = w_in1[0]
    w1p = jnp.concatenate([_pair_heads(w1[:, :MIX1], 1), w1[:, MIX1:SWA_QKV], _pair_heads(w1[:, SWA_QKV:], 1)],
                          axis=1).astype(BF16)
    b1 = b_in1[0]
    b1p = row(jnp.concatenate([_pair_heads(b1[:MIX1], 0), b1[MIX1:]]))
    sinks_p = jnp.swapaxes(attn_sinks[0].reshape(SWA_KV_HEADS // 2, 2, SWA_GROUP), 1, 2).reshape(SWA_Q_HEADS)
    cos, slo, shi = _rope_tables(seq_len)
    q, k, v, gate1 = _in1_call(h1, row(norm_w[1]), w1p, b1p, cos, slo, shi, seq_len, tm)
    o1 = _swa_call(sinks_p, q, k, v, batch, seq_len)
    w_out1p = _pair_heads(w_out1[0], 0).astype(BF16)
    y = _out1_call(o1, gate1, h1, w_out1p, row(b_out1[0]), row(final_norm_w), tm)
    return y.reshape(batch, seq_len, D_MODEL)


def kernel(x, norm_w, w_in0, gla_gk_up, gla_gk_bias, gla_norm_w, rwkv_mu, rwkv_w0, rwkv_w_up, rwkv_a0,
           rwkv_a_up, rwkv_k_k, rwkv_k_a, rwkv_r_k, rwkv_ln_w, rwkv_ln_b, w_out0, w_in1, b_in1,
           attn_sinks, w_out1, b_out1, final_norm_w):
    return _forward(x, norm_w, w_in0, gla_gk_up, gla_gk_bias, gla_norm_w, rwkv_mu, rwkv_w0, rwkv_w_up,
                    rwkv_a0, rwkv_a_up, rwkv_k_k, rwkv_k_a, rwkv_r_k, rwkv_ln_w, rwkv_ln_b, w_out0,
                    w_in1, b_in1, attn_sinks, w_out1, b_out1, final_norm_w,
                    tm=256, gla_chunks=16, rwkv_chunks=32, scan_chunks=4)
```

```python
import functools

import jax
import jax.numpy as jnp
from jax import lax
from jax.experimental import pallas as pl
from jax.experimental.pallas import tpu as pltpu

F32 = jnp.float32
BF16 = jnp.bfloat16

D_MODEL = 1024
NORM_EPS = 1e-5

GLA_HEADS = 4
GLA_DK = 64
GLA_DV = 128
GLA_KEY = GLA_HEADS * GLA_DK
GLA_VAL = GLA_HEADS * GLA_DV
GLA_GATE_RANK = 16
GLA_GATE_NORMALIZER = 16.0
GLA_CHUNK = 64

RWKV_HEADS = 8
RWKV_HEAD = 64
RWKV_W = RWKV_HEADS * RWKV_HEAD
RWKV_DECAY_RANK = 64
RWKV_A_RANK = 64
RWKV_LN_EPS = 64e-5
RWKV_RKV = 3 * RWKV_W
RWKV_LORA = RWKV_DECAY_RANK + RWKV_A_RANK
RWKV_SHIFT = RWKV_RKV + RWKV_LORA
RWKV_CHUNK = 64

MIX0 = GLA_VAL + RWKV_W
GLA_QKV = 2 * GLA_KEY + GLA_VAL

SWA_Q_HEADS = 16
SWA_KV_HEADS = 4
SWA_GROUP = SWA_Q_HEADS // SWA_KV_HEADS
SWA_HEAD = 64
WINDOW = 128
ROPE_DIMS = SWA_HEAD // 4
ROPE_THETA = 500000.0
MIX1 = SWA_Q_HEADS * SWA_HEAD
SWA_KV = SWA_KV_HEADS * SWA_HEAD
SWA_QKV = MIX1 + 2 * SWA_KV

LANES = 128
HEAD = 64
GLOW_PAD = LANES
VMEM_LIMIT = 56 * 1024 * 1024


def _cparams(sem):
    return pltpu.CompilerParams(dimension_semantics=sem, vmem_limit_bytes=VMEM_LIMIT)


def _dot(a, b):
    return jnp.dot(a.astype(BF16), b.astype(BF16), preferred_element_type=F32)


def _dot_nt(a, b):
    return lax.dot_general(a.astype(BF16), b.astype(BF16), (((1,), (1,)), ((), ())),
                           preferred_element_type=F32)


def _dot_tn(a, b):
    return lax.dot_general(a.astype(BF16), b.astype(BF16), (((0,), (0,)), ((), ())),
                           preferred_element_type=F32)


def _split2(x):
    hi = x.astype(BF16)
    lo = (x - hi.astype(F32)).astype(BF16)
    return hi, lo


def _dot_exact_rhs(a_bf16, x):
    hi, lo = _split2(x)
    return (jnp.dot(a_bf16, hi, preferred_element_type=F32)
            + jnp.dot(a_bf16, lo, preferred_element_type=F32))


def _dot_exact_lhs(x, b_bf16):
    hi, lo = _split2(x)
    return (jnp.dot(hi, b_bf16, preferred_element_type=F32)
            + jnp.dot(lo, b_bf16, preferred_element_type=F32))


def _iota(shape, dim):
    return lax.broadcasted_iota(jnp.int32, shape, dim)


def _tril_ones(n, dtype=BF16):
    return (_iota((n, n), 0) >= _iota((n, n), 1)).astype(dtype)


def _head_block_ones(n=LANES, dtype=BF16):
    return ((_iota((n, n), 0) // HEAD) == (_iota((n, n), 1) // HEAD)).astype(dtype)


def _head_stack(x):
    head = (_iota(x.shape, 1) % LANES) // HEAD
    return jnp.concatenate([jnp.where(head == 0, x, 0.0), jnp.where(head == 1, x, 0.0)], axis=0)


def _softplus(z):
    return jnp.maximum(z, 0.0) + jnp.log(1.0 + jnp.exp(-jnp.abs(z)))


def _sigmoid(z):
    return 1.0 / (1.0 + jnp.exp(-z))


def _rmsnorm_rows(x, w):
    return x * lax.rsqrt(jnp.mean(x * x, axis=-1, keepdims=True) + NORM_EPS) * w


def _in0_kernel(x_ref, nw_ref, wg_ref, wl_ref, wr_ref, mu_ref,
                gqkv_ref, glow_ref, rkv_ref, lora_ref, gate_ref, carry_ref, *, tiles_per_seq):
    i = pl.program_id(0)

    @pl.when(i == 0)
    def _():
        carry_ref[...] = jnp.zeros_like(carry_ref)

    xn = _rmsnorm_rows(x_ref[...], nw_ref[...]).astype(BF16)
    gqkv_ref[...] = jnp.dot(xn, wg_ref[...], preferred_element_type=F32).astype(gqkv_ref.dtype)
    glow_ref[...] = jnp.dot(xn, wl_ref[...], preferred_element_type=F32)
    rw = jnp.dot(xn, wr_ref[:, :RWKV_SHIFT], preferred_element_type=F32)
    gate_ref[...] = jnp.dot(xn, wr_ref[:, RWKV_SHIFT:], preferred_element_type=F32).astype(gate_ref.dtype)

    tm = rw.shape[0]
    first = (i % tiles_per_seq) == 0
    prev_last = jnp.where(first, 0.0, carry_ref[7:8, :])
    rolled = pltpu.roll(rw, 1, 0)
    prev = jnp.where(_iota(rw.shape, 0) == 0, prev_last, rolled)
    mixed = rw + (prev - rw) * mu_ref[...]
    rkv_ref[...] = mixed[:, :RWKV_RKV].astype(rkv_ref.dtype)
    lora_ref[...] = mixed[:, RWKV_RKV:]
    carry_ref[...] = rw[tm - 8:tm, :]


def _in0_call(x2, norm_w, w_gla, w_glow, w_rest, mu, seq_len, tm):
    n_tok = x2.shape[0]
    row = lambda i: (i, 0)
    const = lambda i: (0, 0)
    outs = [(GLA_QKV, BF16), (GLOW_PAD, F32), (RWKV_RKV, BF16), (RWKV_LORA, F32), (MIX0, BF16)]
    return pl.pallas_call(
        functools.partial(_in0_kernel, tiles_per_seq=seq_len // tm),
        grid=(n_tok // tm,),
        in_specs=[pl.BlockSpec((tm, D_MODEL), row),
                  pl.BlockSpec((1, D_MODEL), const),
                  pl.BlockSpec((D_MODEL, GLA_QKV), const),
                  pl.BlockSpec((D_MODEL, GLOW_PAD), const),
                  pl.BlockSpec((D_MODEL, RWKV_SHIFT + MIX0), const),
                  pl.BlockSpec((1, RWKV_SHIFT), const)],
        out_specs=[pl.BlockSpec((tm, n), row) for n, _ in outs],
        out_shape=[jax.ShapeDtypeStruct((n_tok, n), dt) for n, dt in outs],
        scratch_shapes=[pltpu.VMEM((8, RWKV_SHIFT), F32)],
        compiler_params=_cparams(("arbitrary",)),
        name="l0_norm_proj",
    )(x2, norm_w, w_gla, w_glow, w_rest, mu)


def _gla_kernel(q_ref, k_ref, glow_ref, v_ref, up_ref, bias_ref, nw_ref, o_ref, st_ref, *, chunks):
    c = pl.program_id(2)

    @pl.when(c == 0)
    def _():
        st_ref[...] = jnp.zeros_like(st_ref)

    C = GLA_CHUNK
    tril = _tril_ones(C)
    causal = _iota((C, LANES), 0) >= (_iota((C, LANES), 1) % HEAD)
    sr = _iota((2 * GLA_DV, LANES), 0)
    sl = _iota((2 * GLA_DV, LANES), 1)
    st_mask = (sr // GLA_DV) == (sl // HEAD)
    vl = _iota((C, 2 * GLA_DV), 1)
    scale = GLA_DK ** -0.5
    z = _dot(glow_ref[...], up_ref[...]) + bias_ref[...]
    g_all = -_softplus(-z) / GLA_GATE_NORMALIZER
    q_all = q_ref[...].astype(F32) * scale
    k_all = k_ref[...].astype(F32)
    rows = [slice(j * C, (j + 1) * C) for j in range(chunks)]
    bs = [_dot_exact_rhs(tril, g_all[rw]) for rw in rows]
    qe, ke, qb, kl, dec, vs = [], [], [], [], [], []
    for rw, b in zip(rows, bs):
        ref = b[C // 2:C // 2 + 1, :]
        b_last = b[C - 1:C, :]
        qe.append(q_all[rw] * jnp.exp(b - ref))
        ke.append(k_all[rw] * jnp.exp(ref - b))
        qb.append(q_all[rw] * jnp.exp(b))
        kl.append(k_all[rw] * jnp.exp(b_last - b))
        dec.append(jnp.exp(b_last))
        vs.append(v_ref[rw, :])
    att = [jnp.where(causal, _dot_nt(qe[j], _head_stack(ke[j])), 0.0) for j in range(chunks)]
    kv = [jnp.where(st_mask, _dot_tn(vs[j], kl[j]), 0.0) for j in range(chunks)]
    v_diag = [jnp.concatenate([jnp.where(vl < GLA_DV, vs[j], jnp.zeros_like(vs[j])),
                               jnp.where(vl >= GLA_DV, vs[j], jnp.zeros_like(vs[j]))], axis=0)
              for j in range(chunks)]
    intra = [jnp.dot(att[j].astype(BF16), v_diag[j], preferred_element_type=F32) for j in range(chunks)]
    states = [st_ref[...]]
    for j in range(chunks):
        states.append(states[j] * dec[j] + kv[j])
    st_ref[...] = states[chunks]
    for j in range(chunks):
        o = intra[j] + _dot_nt(qb[j], states[j])
        for h in range(2):
            oh = o[:, h * GLA_DV:(h + 1) * GLA_DV]
            oh = oh * lax.rsqrt(jnp.mean(oh * oh, axis=-1, keepdims=True) + NORM_EPS) * nw_ref[...]
            o_ref[rows[j], h * GLA_DV:(h + 1) * GLA_DV] = oh.astype(o_ref.dtype)


def _gla_call(gqkv, glow, up_pad, bias, norm_w, batch, seq_len, chunks):
    n_tok = gqkv.shape[0]
    tcb = chunks * GLA_CHUNK
    steps = seq_len // tcb
    pairs = GLA_KEY // LANES
    return pl.pallas_call(
        functools.partial(_gla_kernel, chunks=chunks),
        grid=(batch, pairs, steps),
        in_specs=[pl.BlockSpec((tcb, LANES), lambda b, p, c: (b * steps + c, p)),
                  pl.BlockSpec((tcb, LANES), lambda b, p, c: (b * steps + c, pairs + p)),
                  pl.BlockSpec((tcb, GLOW_PAD), lambda b, p, c: (b * steps + c, 0)),
                  pl.BlockSpec((tcb, 2 * GLA_DV), lambda b, p, c: (b * steps + c, pairs + p)),
                  pl.BlockSpec((GLOW_PAD, LANES), lambda b, p, c: (0, p)),
                  pl.BlockSpec((1, LANES), lambda b, p, c: (0, p)),
                  pl.BlockSpec((1, GLA_DV), lambda b, p, c: (0, 0))],
        out_specs=pl.BlockSpec((tcb, 2 * GLA_DV), lambda b, p, c: (b * steps + c, p)),
        out_shape=jax.ShapeDtypeStruct((n_tok, GLA_VAL), BF16),
        scratch_shapes=[pltpu.VMEM((2 * GLA_DV, LANES), F32)],
        compiler_params=_cparams(("parallel", "parallel", "arbitrary")),
        name="l0_gla",
    )(gqkv, gqkv, glow, gqkv, up_pad, bias, norm_w)


def _merge_masks(n):
    r = _iota((n, LANES), 0)
    c = _iota((n, LANES), 1) % HEAD
    masks = []
    s = 1
    while s < n:
        masks.append(((r // s) % 2 == 1) & ((c // s) == (r // s) - 1))
        s *= 2
    return (r == c).astype(F32), masks


def _unit_lower_inverses(lows, n):
    eye, masks = _merge_masks(n)
    ts = [eye + jnp.where(masks[0], low, 0.0) for low in lows]
    for sub in masks[1:]:
        ys = [_dot(jnp.where(sub, low, 0.0), _head_stack(t)) for low, t in zip(lows, ts)]
        ts = [t + _dot(t, _head_stack(y)) for t, y in zip(ts, ys)]
    return ts


def _rwkv_chunk_kernel(r_ref, k_ref, v_ref, xwa_ref, w0_ref, wup_ref, a0_ref, aup_ref,
                       kk_ref, ka_ref, rk_ref,
                       rp_ref, op_ref, bonus_ref, m_ref, n_ref, *, chunks):
    C = RWKV_CHUNK
    tril = _tril_ones(C)
    rr = _iota((2 * C, LANES), 0)
    cc = _iota((2 * C, LANES), 1) % HEAD
    tri2 = ((rr < C) & (rr > cc)) | (rr - C >= cc)
    hb = _head_block_ones()
    sq_r = _iota((LANES, LANES), 0)
    sq_c = _iota((LANES, LANES), 1)
    same_head = (sq_r // HEAD) == (sq_c // HEAD)
    eye128 = sq_r == sq_c

    r_all = r_ref[...].astype(F32)
    k_all = k_ref[...].astype(F32)
    v_all = v_ref[...].astype(F32)
    xwa = xwa_ref[...]
    w = -_softplus(-(w0_ref[...] + _dot(jnp.tanh(xwa), wup_ref[...]))) - 0.5
    lw_all = -jnp.exp(w)
    a_sig = _sigmoid(a0_ref[...] + _dot(xwa, aup_ref[...]))
    kk = k_all * kk_ref[...]
    kk = kk / jnp.maximum(jnp.sqrt(_dot_exact_lhs(kk * kk, hb)), 1e-12)
    k_all = k_all * (1.0 + (a_sig - 1.0) * ka_ref[...])
    bonus_ref[...] = _dot_exact_lhs(r_all * k_all * rk_ref[...], hb) * v_all
    a_all = -kk
    b_all = kk * a_sig

    rows = [slice(j * C, (j + 1) * C) for j in range(chunks)]
    cums = [_dot_exact_rhs(tril, lw_all[rw]) for rw in rows]
    rt, at, bt, kt, ends, v, dec = [], [], [], [], [], [], []
    for rw, cum in zip(rows, cums):
        cum_last = cum[C - 1:C, :]
        e_neg = jnp.exp(-cum)
        e_end = jnp.exp(cum_last - cum)
        rt.append(r_all[rw] * jnp.exp(cum))
        at.append(a_all[rw] * jnp.exp(cum - lw_all[rw]))
        bt.append(b_all[rw] * e_neg)
        kt.append(k_all[rw] * e_neg)
        ends.append(jnp.concatenate([b_all[rw] * e_end, k_all[rw] * e_end], axis=0))
        v.append(v_all[rw])
        dec.append(jnp.exp(cum_last))

    n = range(chunks)
    lhs = [jnp.concatenate([at[j], rt[j]], axis=0) for j in n]
    left = [jnp.where(tri2, _dot_nt(lhs[j], _head_stack(bt[j])), 0.0) for j in n]
    right = [jnp.where(tri2, _dot_nt(lhs[j], _head_stack(kt[j])), 0.0) for j in n]
    tinv = _unit_lower_inverses([lf[:C] for lf in left], C)
    kv = [_dot(right[j], _head_stack(v[j])) for j in n]
    wz = [_dot(tinv[j], _head_stack(jnp.concatenate([at[j], kv[j][:C]], axis=1))) for j in n]
    ro = [_dot(left[j][C:], _head_stack(wz[j])) for j in n]
    zero = jnp.zeros((C, LANES), F32)
    mn = [_dot_tn(ends[j], jnp.concatenate([wz[j], jnp.concatenate([zero, v[j]], axis=1)], axis=0))
          for j in n]
    for j in n:
        rp_ref[rows[j], :] = (rt[j] + ro[j][:, :LANES]).astype(rp_ref.dtype)
        op_ref[rows[j], :] = ro[j][:, LANES:] + kv[j][C:]
        m_ref[0, 0, j] = (jnp.where(eye128, dec[j], 0.0)
                          + jnp.where(same_head, mn[j][:, :LANES], 0.0)).astype(m_ref.dtype)
        n_ref[0, 0, j] = jnp.where(same_head, mn[j][:, LANES:], 0.0)


def _rwkv_chunk_call(rkv, lora, w0, wup_pad, a0, aup_pad, k_k, k_a, r_k, batch, seq_len, chunks):
    n_tok = rkv.shape[0]
    tcb = chunks * RWKV_CHUNK
    steps = seq_len // tcb
    pairs = RWKV_W // LANES
    nc = seq_len // RWKV_CHUNK
    col = lambda off: (lambda b, p, c: (b * steps + c, off + p))
    par = lambda b, p, c: (0, p)
    tok = lambda b, p, c: (b * steps + c, p)
    mat = lambda b, p, c: (b, p, c, 0, 0)
    return pl.pallas_call(
        functools.partial(_rwkv_chunk_kernel, chunks=chunks),
        grid=(batch, pairs, steps),
        in_specs=[pl.BlockSpec((tcb, LANES), col(0)),
                  pl.BlockSpec((tcb, LANES), col(pairs)),
                  pl.BlockSpec((tcb, LANES), col(2 * pairs)),
                  pl.BlockSpec((tcb, RWKV_LORA), lambda b, p, c: (b * steps + c, 0)),
                  pl.BlockSpec((1, LANES), par),
                  pl.BlockSpec((RWKV_LORA, LANES), par),
                  pl.BlockSpec((1, LANES), par),
                  pl.BlockSpec((RWKV_LORA, LANES), par),
                  pl.BlockSpec((1, LANES), par),
                  pl.BlockSpec((1, LANES), par),
                  pl.BlockSpec((1, LANES), par)],
        out_specs=[pl.BlockSpec((tcb, LANES), tok),
                   pl.BlockSpec((tcb, LANES), tok),
                   pl.BlockSpec((tcb, LANES), tok),
                   pl.BlockSpec((1, 1, chunks, LANES, LANES), mat),
                   pl.BlockSpec((1, 1, chunks, LANES, LANES), mat)],
        out_shape=[jax.ShapeDtypeStruct((n_tok, RWKV_W), BF16),
                   jax.ShapeDtypeStruct((n_tok, RWKV_W), F32),
                   jax.ShapeDtypeStruct((n_tok, RWKV_W), F32),
                   jax.ShapeDtypeStruct((batch, pairs, nc, LANES, LANES), BF16),
                   jax.ShapeDtypeStruct((batch, pairs, nc, LANES, LANES), F32)],
        compiler_params=_cparams(("parallel", "parallel", "parallel")),
        name="l0_rwkv_chunks",
    )(rkv, rkv, rkv, lora, w0, wup_pad, a0, aup_pad, k_k, k_a, r_k)


def _rwkv_scan_kernel(rp_ref, op_ref, bonus_ref, m_ref, n_ref, lnw_ref, lnb_ref, o_ref, st_ref, *, chunks):
    c = pl.program_id(0)

    @pl.when(c == 0)
    def _():
        st_ref[...] = jnp.zeros_like(st_ref)

    C = RWKV_CHUNK
    batch = rp_ref.shape[0]
    pairs = RWKV_W // LANES
    hb = _head_block_ones()
    seqs = [(b, p) for b in range(batch) for p in range(pairs)]
    states = {bp: [st_ref[bp[0], bp[1]]] for bp in seqs}
    for j in range(chunks):
        for b, p in seqs:
            states[b, p].append(_dot(m_ref[b, p, j], states[b, p][j]) + n_ref[b, p, j])
    for b, p in seqs:
        st_ref[b, p] = states[b, p][chunks]
    cols = {bp: slice(bp[1] * LANES, (bp[1] + 1) * LANES) for bp in seqs}
    os = [jnp.concatenate([_dot(rp_ref[b, j * C:(j + 1) * C, cols[b, p]], states[b, p][j])
                           for j in range(chunks)], axis=0) + op_ref[b, :, cols[b, p]] for b, p in seqs]
    means = [_dot_exact_lhs(o, hb) * (1.0 / RWKV_HEAD) for o in os]
    ds = [o - mean for o, mean in zip(os, means)]
    variances = [_dot_exact_lhs(d * d, hb) * (1.0 / RWKV_HEAD) for d in ds]
    for (b, p), d, var in zip(seqs, ds, variances):
        c_ = cols[b, p]
        o_ref[b, :, c_] = (d * lax.rsqrt(var + RWKV_LN_EPS) * lnw_ref[:, c_] + lnb_ref[:, c_]
                           + bonus_ref[b, :, c_]).astype(o_ref.dtype)


def _rwkv_scan_call(rp, op, bonus, m, n, ln_w, ln_b, batch, seq_len, chunks):
    tcb = chunks * RWKV_CHUNK
    pairs = RWKV_W // LANES
    seq3 = lambda t: t.reshape(batch, seq_len, RWKV_W)
    tok = lambda c: (0, c, 0)
    const = lambda c: (0, 0)
    mat = lambda c: (0, 0, c, 0, 0)
    out = pl.pallas_call(
        functools.partial(_rwkv_scan_kernel, chunks=chunks),
        grid=(seq_len // tcb,),
        in_specs=[pl.BlockSpec((batch, tcb, RWKV_W), tok),
                  pl.BlockSpec((batch, tcb, RWKV_W), tok),
                  pl.BlockSpec((batch, tcb, RWKV_W), tok),
                  pl.BlockSpec((batch, pairs, chunks, LANES, LANES), mat),
                  pl.BlockSpec((batch, pairs, chunks, LANES, LANES), mat),
                  pl.BlockSpec((1, RWKV_W), const),
                  pl.BlockSpec((1, RWKV_W), const)],
        out_specs=pl.BlockSpec((batch, tcb, RWKV_W), tok),
        out_shape=jax.ShapeDtypeStruct((batch, seq_len, RWKV_W), BF16),
        scratch_shapes=[pltpu.VMEM((batch, pairs, LANES, LANES), F32)],
        compiler_params=_cparams(("arbitrary",)),
        name="l0_rwkv_scan",
    )(seq3(rp), seq3(op), seq3(bonus), m, n, ln_w, ln_b)
    return out.reshape(batch * seq_len, RWKV_W)


def _out0_kernel(oa_ref, ob_ref, gate_ref, x_ref, w_ref, h_ref):
    g = gate_ref[...].astype(F32)
    g = g * _sigmoid(g)
    ya = (oa_ref[...].astype(F32) * g[:, :GLA_VAL]).astype(BF16)
    yb = (ob_ref[...].astype(F32) * g[:, GLA_VAL:]).astype(BF16)
    h_ref[...] = (x_ref[...]
                  + jnp.dot(ya, w_ref[:GLA_VAL, :], preferred_element_type=F32)
                  + jnp.dot(yb, w_ref[GLA_VAL:, :], preferred_element_type=F32))


def _out0_call(oa, ob, gate, x2, w_out, tm):
    n_tok = x2.shape[0]
    row = lambda i: (i, 0)
    const = lambda i: (0, 0)
    return pl.pallas_call(
        _out0_kernel,
        grid=(n_tok // tm,),
        in_specs=[pl.BlockSpec((tm, GLA_VAL), row),
                  pl.BlockSpec((tm, RWKV_W), row),
                  pl.BlockSpec((tm, MIX0), row),
                  pl.BlockSpec((tm, D_MODEL), row),
                  pl.BlockSpec((MIX0, D_MODEL), const)],
        out_specs=pl.BlockSpec((tm, D_MODEL), row),
        out_shape=jax.ShapeDtypeStruct((n_tok, D_MODEL), F32),
        compiler_params=_cparams(("parallel",)),
        name="l0_gate_out",
    )(oa, ob, gate, x2, w_out)


def _rope_group(x, cos, sin_lo, sin_hi):
    half = ROPE_DIMS // 2
    return x * cos + pltpu.roll(x, LANES - half, 1) * sin_lo + pltpu.roll(x, half, 1) * sin_hi


def _in1_kernel(h_ref, nw_ref, w_ref, b_ref, cos_ref, slo_ref, shi_ref,
                q_ref, k_ref, v_ref, gate_ref):
    hn = _rmsnorm_rows(h_ref[...], nw_ref[...]).astype(BF16)
    cos = cos_ref[...]
    slo = slo_ref[...]
    shi = shi_ref[...]
    scale = SWA_HEAD ** -0.5
    for g in range(MIX1 // LANES):
        cols = slice(g * LANES, (g + 1) * LANES)
        y = jnp.dot(hn, w_ref[:, cols], preferred_element_type=F32) + b_ref[:, cols]
        q_ref[:, cols] = (_rope_group(y, cos, slo, shi) * scale).astype(q_ref.dtype)
    for g in range(SWA_KV // LANES):
        cols = slice(MIX1 + g * LANES, MIX1 + (g + 1) * LANES)
        y = jnp.dot(hn, w_ref[:, cols], preferred_element_type=F32) + b_ref[:, cols]
        k_ref[:, g * LANES:(g + 1) * LANES] = _rope_group(y, cos, slo, shi).astype(k_ref.dtype)
    cols = slice(MIX1 + SWA_KV, SWA_QKV)
    v_ref[...] = (jnp.dot(hn, w_ref[:, cols], preferred_element_type=F32) + b_ref[:, cols]).astype(v_ref.dtype)
    gate_ref[...] = jnp.dot(hn, w_ref[:, SWA_QKV:], preferred_element_type=F32).astype(gate_ref.dtype)


def _in1_call(h1, norm_w, w_in, b_in, cos, slo, shi, seq_len, tm):
    n_tok = h1.shape[0]
    tps = seq_len // tm
    row = lambda i: (i, 0)
    const = lambda i: (0, 0)
    pos = lambda i: (i % tps, 0)
    return pl.pallas_call(
        _in1_kernel,
        grid=(n_tok // tm,),
        in_specs=[pl.BlockSpec((tm, D_MODEL), row),
                  pl.BlockSpec((1, D_MODEL), const),
                  pl.BlockSpec((D_MODEL, SWA_QKV + MIX1), const),
                  pl.BlockSpec((1, SWA_QKV), const),
                  pl.BlockSpec((tm, LANES), pos),
                  pl.BlockSpec((tm, LANES), pos),
                  pl.BlockSpec((tm, LANES), pos)],
        out_specs=[pl.BlockSpec((tm, MIX1), row),
                   pl.BlockSpec((tm, SWA_KV), row),
                   pl.BlockSpec((tm, SWA_KV), row),
                   pl.BlockSpec((tm, MIX1), row)],
        out_shape=[jax.ShapeDtypeStruct((n_tok, MIX1), BF16),
                   jax.ShapeDtypeStruct((n_tok, SWA_KV), BF16),
                   jax.ShapeDtypeStruct((n_tok, SWA_KV), BF16),
                   jax.ShapeDtypeStruct((n_tok, MIX1), BF16)],
        compiler_params=_cparams(("parallel",)),
        name="l1_norm_proj_rope",
    )(h1, norm_w, w_in, b_in, cos, slo, shi)


def _swa_kernel(sink_ref, q_ref, kc_ref, kp_ref, vc_ref, vp_ref, o_ref):
    n = pl.program_id(1)
    W = WINDOW
    qi = _iota((W, 2 * W), 0)
    kj = _iota((W, 2 * W), 1)
    diff = qi + W - kj
    first_key = jnp.where(n > 0, 0, W)
    valid = (diff >= 0) & (diff < W) & (kj >= first_key)
    lane = _iota((W, LANES), 1)
    groups = MIX1 // LANES // (SWA_KV // LANES)
    for pp in range(SWA_KV // LANES):
        cols = slice(pp * LANES, (pp + 1) * LANES)
        kk = jnp.concatenate([kp_ref[:, cols], kc_ref[:, cols]], axis=0)
        vv = jnp.concatenate([vp_ref[:, cols], vc_ref[:, cols]], axis=0)
        for g in range(groups):
            blk = pp * groups + g
            q = q_ref[:, blk * LANES:(blk + 1) * LANES]
            outs = []
            for hh in range(2):
                sink = sink_ref[2 * blk + hh]
                qm = jnp.where((lane // HEAD) == hh, q, jnp.zeros_like(q))
                s = lax.dot_general(qm, kk, (((1,), (1,)), ((), ())), preferred_element_type=F32)
                s = jnp.where(valid, s, -jnp.inf)
                m = jnp.maximum(jnp.max(s, axis=-1, keepdims=True), sink)
                p = jnp.exp(s - m)
                denom = jnp.sum(p, axis=-1, keepdims=True) + jnp.exp(sink - m)
                outs.append(jnp.dot(p.astype(BF16), vv, preferred_element_type=F32) / denom)
            o_ref[:, blk * LANES:(blk + 1) * LANES] = jnp.where((lane // HEAD) == 0, outs[0], outs[1]).astype(o_ref.dtype)


def _swa_call(sinks, q, k, v, batch, seq_len):
    n_tok = q.shape[0]
    nb = seq_len // WINDOW
    cur = lambda b, n: (b * nb + n, 0)
    prev = lambda b, n: (jnp.maximum(b * nb + n - 1, 0), 0)
    return pl.pallas_call(
        _swa_kernel,
        grid=(batch, nb),
        in_specs=[pl.BlockSpec(memory_space=pltpu.SMEM),
                  pl.BlockSpec((WINDOW, MIX1), cur),
                  pl.BlockSpec((WINDOW, SWA_KV), cur),
                  pl.BlockSpec((WINDOW, SWA_KV), prev),
                  pl.BlockSpec((WINDOW, SWA_KV), cur),
                  pl.BlockSpec((WINDOW, SWA_KV), prev)],
        out_specs=pl.BlockSpec((WINDOW, MIX1), cur),
        out_shape=jax.ShapeDtypeStruct((n_tok, MIX1), BF16),
        compiler_params=_cparams(("parallel", "parallel")),
        name="l1_swa",
    )(sinks, q, k, k, v, v)


def _out1_kernel(o_ref, gate_ref, h_ref, w_ref, b_ref, nw_ref, y_ref):
    g = gate_ref[...].astype(F32)
    y = (o_ref[...].astype(F32) * (g * _sigmoid(g))).astype(BF16)
    h = h_ref[...] + jnp.dot(y, w_ref[...], preferred_element_type=F32) + b_ref[...]
    y_ref[...] = _rmsnorm_rows(h, nw_ref[...])


def _out1_call(o, gate, h1, w_out, b_out, norm_w, tm):
    n_tok = h1.shape[0]
    row = lambda i: (i, 0)
    const = lambda i: (0, 0)
    return pl.pallas_call(
        _out1_kernel,
        grid=(n_tok // tm,),
        in_specs=[pl.BlockSpec((tm, MIX1), row),
                  pl.BlockSpec((tm, MIX1), row),
                  pl.BlockSpec((tm, D_MODEL), row),
                  pl.BlockSpec((MIX1, D_MODEL), const),
                  pl.BlockSpec((1, D_MODEL), const),
                  pl.BlockSpec((1, D_MODEL), const)],
        out_specs=pl.BlockSpec((tm, D_MODEL), row),
        out_shape=jax.ShapeDtypeStruct((n_tok, D_MODEL), F32),
        compiler_params=_cparams(("parallel",)),
        name="l1_gate_out_norm",
    )(o, gate, h1, w_out, b_out, norm_w)


def _pad_rows(w, rows):
    return jnp.concatenate([w, jnp.zeros((rows - w.shape[0], w.shape[1]), w.dtype)], axis=0)


def _pair_heads(t, axis):
    shape = t.shape
    split = shape[:axis] + (SWA_KV_HEADS // 2, 2, SWA_GROUP, SWA_HEAD) + shape[axis + 1:]
    return jnp.swapaxes(t.reshape(split), axis + 1, axis + 2).reshape(shape)


def _rope_tables(seq_len):
    half = ROPE_DIMS // 2
    inv_freq = ROPE_THETA ** (-jnp.arange(half, dtype=F32) / half)
    d = jnp.arange(LANES) % SWA_HEAD
    ang = jnp.arange(seq_len).astype(F32)[:, None] * inv_freq[d % half][None, :]
    cos = jnp.where(d < ROPE_DIMS, jnp.cos(ang), 1.0)
    sin = jnp.sin(ang)
    sin_lo = jnp.where(d < half, -sin, 0.0)
    sin_hi = jnp.where((d >= half) & (d < ROPE_DIMS), sin, 0.0)
    return cos, sin_lo, sin_hi


def _forward(x, norm_w, w_in0, gla_gk_up, gla_gk_bias, gla_norm_w, rwkv_mu, rwkv_w0, rwkv_w_up,
             rwkv_a0, rwkv_a_up, rwkv_k_k, rwkv_k_a, rwkv_r_k, rwkv_ln_w, rwkv_ln_b, w_out0,
             w_in1, b_in1, attn_sinks, w_out1, b_out1, final_norm_w, *, tm, gla_chunks, rwkv_chunks, scan_chunks):
    batch, seq_len, _ = x.shape
    x2 = x.reshape(batch * seq_len, D_MODEL)
    row = lambda t: t.reshape(1, -1)

    w0 = w_in0[0]
    w_gla = w0[:, :GLA_QKV].astype(BF16)
    w_glow = jnp.pad(w0[:, GLA_QKV:GLA_QKV + GLA_GATE_RANK].astype(BF16),
                     ((0, 0), (0, GLOW_PAD - GLA_GATE_RANK)))
    w_rest = w0[:, GLA_QKV + GLA_GATE_RANK:].astype(BF16)
    gqkv, glow, rkv, lora, gate0 = _in0_call(x2, row(norm_w[0]), w_gla, w_glow, w_rest, row(rwkv_mu[0]),
                                             seq_len, tm)

    up_pad = _pad_rows(gla_gk_up[0], GLOW_PAD).astype(BF16)
    o_a = _gla_call(gqkv, glow, up_pad, row(gla_gk_bias[0]), row(gla_norm_w[0]), batch, seq_len, gla_chunks)

    zeros_r = jnp.zeros((RWKV_DECAY_RANK, RWKV_W), F32)
    wup_pad = jnp.concatenate([rwkv_w_up[0], zeros_r], axis=0).astype(BF16)
    aup_pad = jnp.concatenate([zeros_r, rwkv_a_up[0]], axis=0).astype(BF16)
    rp, op, bonus, m, n = _rwkv_chunk_call(
        rkv, lora, row(rwkv_w0[0]), wup_pad, row(rwkv_a0[0]), aup_pad,
        row(rwkv_k_k[0]), row(rwkv_k_a[0]), row(rwkv_r_k[0]), batch, seq_len, rwkv_chunks)
    o_b = _rwkv_scan_call(rp, op, bonus, m, n, row(rwkv_ln_w[0]), row(rwkv_ln_b[0]),
                          batch, seq_len, scan_chunks)

    h1 = _out0_call(o_a, o_b, gate0, x2, w_out0[0].astype(BF16), tm)

    w1 = w_in1[0]
    w1p = jnp.concatenate([_pair_heads(w1[:, :MIX1], 1), w1[:, MIX1:SWA_QKV], _pair_heads(w1[:, SWA_QKV:], 1)],
                          axis=1).astype(BF16)
    b1 = b_in1[0]
    b1p = row(jnp.concatenate([_pair_heads(b1[:MIX1], 0), b1[MIX1:]]))
    sinks_p = jnp.swapaxes(attn_sinks[0].reshape(SWA_KV_HEADS // 2, 2, SWA_GROUP), 1, 2).reshape(SWA_Q_HEADS)
    cos, slo, shi = _rope_tables(seq_len)
    q, k, v, gate1 = _in1_call(h1, row(norm_w[1]), w1p, b1p, cos, slo, shi, seq_len, tm)
    o1 = _swa_call(sinks_p, q, k, v, batch, seq_len)
    w_out1p = _pair_heads(w_out1[0], 0).astype(BF16)
    y = _out1_call(o1, gate1, h1, w_out1p, row(b_out1[0]), row(final_norm_w), tm)
    return y.reshape(batch, seq_len, D_MODEL)


def kernel(x, norm_w, w_in0, gla_gk_up, gla_gk_bias, gla_norm_w, rwkv_mu, rwkv_w0, rwkv_w_up, rwkv_a0,
           rwkv_a_up, rwkv_k_k, rwkv_k_a, rwkv_r_k, rwkv_ln_w, rwkv_ln_b, w_out0, w_in1, b_in1,
           attn_sinks, w_out1, b_out1, final_norm_w):
    return _forward(x, norm_w, w_in0, gla_gk_up, gla_gk_bias, gla_norm_w, rwkv_mu, rwkv_w0, rwkv_w_up,
                    rwkv_a0, rwkv_a_up, rwkv_k_k, rwkv_k_a, rwkv_r_k, rwkv_ln_w, rwkv_ln_b, w_out0,
                    w_in1, b_in1, attn_sinks, w_out1, b_out1, final_norm_w,
                    tm=512, gla_chunks=16, rwkv_chunks=32, scan_chunks=4)
```

```python
import functools

import jax
import jax.numpy as jnp
from jax import lax
from jax.experimental import pallas as pl
from jax.experimental.pallas import tpu as pltpu

F32 = jnp.float32
BF16 = jnp.bfloat16

D_MODEL = 1024
NORM_EPS = 1e-5

GLA_HEADS = 4
GLA_DK = 64
GLA_DV = 128
GLA_KEY = GLA_HEADS * GLA_DK
GLA_VAL = GLA_HEADS * GLA_DV
GLA_GATE_RANK = 16
GLA_GATE_NORMALIZER = 16.0
GLA_CHUNK = 64

RWKV_HEADS = 8
RWKV_HEAD = 64
RWKV_W = RWKV_HEADS * RWKV_HEAD
RWKV_DECAY_RANK = 64
RWKV_A_RANK = 64
RWKV_LN_EPS = 64e-5
RWKV_RKV = 3 * RWKV_W
RWKV_LORA = RWKV_DECAY_RANK + RWKV_A_RANK
RWKV_SHIFT = RWKV_RKV + RWKV_LORA
RWKV_CHUNK = 64

MIX0 = GLA_VAL + RWKV_W
GLA_QKV = 2 * GLA_KEY + GLA_VAL

SWA_Q_HEADS = 16
SWA_KV_HEADS = 4
SWA_GROUP = SWA_Q_HEADS // SWA_KV_HEADS
SWA_HEAD = 64
WINDOW = 128
ROPE_DIMS = SWA_HEAD // 4
ROPE_THETA = 500000.0
MIX1 = SWA_Q_HEADS * SWA_HEAD
SWA_KV = SWA_KV_HEADS * SWA_HEAD
SWA_QKV = MIX1 + 2 * SWA_KV

LANES = 128
HEAD = 64
GLOW_PAD = LANES
VMEM_LIMIT = 56 * 1024 * 1024


def _cparams(sem):
    return pltpu.CompilerParams(dimension_semantics=sem, vmem_limit_bytes=VMEM_LIMIT)


def _dot(a, b):
    return jnp.dot(a.astype(BF16), b.astype(BF16), preferred_element_type=F32)


def _dot_nt(a, b):
    return lax.dot_general(a.astype(BF16), b.astype(BF16), (((1,), (1,)), ((), ())),
                           preferred_element_type=F32)


def _dot_tn(a, b):
    return lax.dot_general(a.astype(BF16), b.astype(BF16), (((0,), (0,)), ((), ())),
                           preferred_element_type=F32)


def _split2(x):
    hi = x.astype(BF16)
    lo = (x - hi.astype(F32)).astype(BF16)
    return hi, lo


def _dot_exact_rhs(a_bf16, x):
    hi, lo = _split2(x)
    return (jnp.dot(a_bf16, hi, preferred_element_type=F32)
            + jnp.dot(a_bf16, lo, preferred_element_type=F32))


def _dot_exact_lhs(x, b_bf16):
    hi, lo = _split2(x)
    return (jnp.dot(hi, b_bf16, preferred_element_type=F32)
            + jnp.dot(lo, b_bf16, preferred_element_type=F32))


def _iota(shape, dim):
    return lax.broadcasted_iota(jnp.int32, shape, dim)


def _tril_ones(n, dtype=BF16):
    return (_iota((n, n), 0) >= _iota((n, n), 1)).astype(dtype)


def _head_block_ones(n=LANES, dtype=BF16):
    return ((_iota((n, n), 0) // HEAD) == (_iota((n, n), 1) // HEAD)).astype(dtype)


def _head_stack(x):
    head = (_iota(x.shape, 1) % LANES) // HEAD
    return jnp.concatenate([jnp.where(head == 0, x, 0.0), jnp.where(head == 1, x, 0.0)], axis=0)


def _softplus(z):
    return jnp.maximum(z, 0.0) + jnp.log(1.0 + jnp.exp(-jnp.abs(z)))


def _sigmoid(z):
    return 1.0 / (1.0 + jnp.exp(-z))


def _rmsnorm_rows(x, w):
    return x * lax.rsqrt(jnp.mean(x * x, axis=-1, keepdims=True) + NORM_EPS) * w


def _in0_kernel(x_ref, nw_ref, wg_ref, wl_ref, wr_ref, mu_ref,
                gqkv_ref, glow_ref, rkv_ref, lora_ref, gate_ref, carry_ref, *, tiles_per_seq):
    i = pl.program_id(0)

    @pl.when(i == 0)
    def _():
        carry_ref[...] = jnp.zeros_like(carry_ref)

    xn = _rmsnorm_rows(x_ref[...], nw_ref[...]).astype(BF16)
    gqkv_ref[...] = jnp.dot(xn, wg_ref[...], preferred_element_type=F32).astype(gqkv_ref.dtype)
    glow_ref[...] = jnp.dot(xn, wl_ref[...], preferred_element_type=F32)
    rw = jnp.dot(xn, wr_ref[:, :RWKV_SHIFT], preferred_element_type=F32)
    gate_ref[...] = jnp.dot(xn, wr_ref[:, RWKV_SHIFT:], preferred_element_type=F32).astype(gate_ref.dtype)

    tm = rw.shape[0]
    first = (i % tiles_per_seq) == 0
    prev_last = jnp.where(first, 0.0, carry_ref[7:8, :])
    rolled = pltpu.roll(rw, 1, 0)
    prev = jnp.where(_iota(rw.shape, 0) == 0, prev_last, rolled)
    mixed = rw + (prev - rw) * mu_ref[...]
    rkv_ref[...] = mixed[:, :RWKV_RKV].astype(rkv_ref.dtype)
    lora_ref[...] = mixed[:, RWKV_RKV:]
    carry_ref[...] = rw[tm - 8:tm, :]


def _in0_call(x2, norm_w, w_gla, w_glow, w_rest, mu, seq_len, tm):
    n_tok = x2.shape[0]
    row = lambda i: (i, 0)
    const = lambda i: (0, 0)
    outs = [(GLA_QKV, BF16), (GLOW_PAD, F32), (RWKV_RKV, BF16), (RWKV_LORA, F32), (MIX0, BF16)]
    return pl.pallas_call(
        functools.partial(_in0_kernel, tiles_per_seq=seq_len // tm),
        grid=(n_tok // tm,),
        in_specs=[pl.BlockSpec((tm, D_MODEL), row),
                  pl.BlockSpec((1, D_MODEL), const),
                  pl.BlockSpec((D_MODEL, GLA_QKV), const),
                  pl.BlockSpec((D_MODEL, GLOW_PAD), const),
                  pl.BlockSpec((D_MODEL, RWKV_SHIFT + MIX0), const),
                  pl.BlockSpec((1, RWKV_SHIFT), const)],
        out_specs=[pl.BlockSpec((tm, n), row) for n, _ in outs],
        out_shape=[jax.ShapeDtypeStruct((n_tok, n), dt) for n, dt in outs],
        scratch_shapes=[pltpu.VMEM((8, RWKV_SHIFT), F32)],
        compiler_params=_cparams(("arbitrary",)),
        name="l0_norm_proj",
    )(x2, norm_w, w_gla, w_glow, w_rest, mu)


def _gla_kernel(q_ref, k_ref, glow_ref, v_ref, up_ref, bias_ref, nw_ref, o_ref, st_ref, *, chunks):
    c = pl.program_id(2)

    @pl.when(c == 0)
    def _():
        st_ref[...] = jnp.zeros_like(st_ref)

    C = GLA_CHUNK
    tril = _tril_ones(C)
    causal = _iota((C, LANES), 0) >= (_iota((C, LANES), 1) % HEAD)
    sr = _iota((2 * GLA_DV, LANES), 0)
    sl = _iota((2 * GLA_DV, LANES), 1)
    st_mask = (sr // GLA_DV) == (sl // HEAD)
    vl = _iota((C, 2 * GLA_DV), 1)
    scale = GLA_DK ** -0.5
    z = _dot(glow_ref[...], up_ref[...]) + bias_ref[...]
    g_all = -_softplus(-z) / GLA_GATE_NORMALIZER
    q_all = q_ref[...].astype(F32) * scale
    k_all = k_ref[...].astype(F32)
    rows = [slice(j * C, (j + 1) * C) for j in range(chunks)]
    bs = [_dot_exact_rhs(tril, g_all[rw]) for rw in rows]
    qe, ke, qb, kl, dec, vs = [], [], [], [], [], []
    for rw, b in zip(rows, bs):
        ref = b[C // 2:C // 2 + 1, :]
        b_last = b[C - 1:C, :]
        qe.append(q_all[rw] * jnp.exp(b - ref))
        ke.append(k_all[rw] * jnp.exp(ref - b))
        qb.append(q_all[rw] * jnp.exp(b))
        kl.append(k_all[rw] * jnp.exp(b_last - b))
        dec.append(jnp.exp(b_last))
        vs.append(v_ref[rw, :])
    att = [jnp.where(causal, _dot_nt(qe[j], _head_stack(ke[j])), 0.0) for j in range(chunks)]
    kv = [jnp.where(st_mask, _dot_tn(vs[j], kl[j]), 0.0) for j in range(chunks)]
    v_diag = [jnp.concatenate([jnp.where(vl < GLA_DV, vs[j], jnp.zeros_like(vs[j])),
                               jnp.where(vl >= GLA_DV, vs[j], jnp.zeros_like(vs[j]))], axis=0)
              for j in range(chunks)]
    intra = [jnp.dot(att[j].astype(BF16), v_diag[j], preferred_element_type=F32) for j in range(chunks)]
    states = [st_ref[...]]
    for j in range(chunks):
        states.append(states[j] * dec[j] + kv[j])
    st_ref[...] = states[chunks]
    for j in range(chunks):
        o = intra[j] + _dot_nt(qb[j], states[j])
        for h in range(2):
            oh = o[:, h * GLA_DV:(h + 1) * GLA_DV]
            oh = oh * lax.rsqrt(jnp.mean(oh * oh, axis=-1, keepdims=True) + NORM_EPS) * nw_ref[...]
            o_ref[rows[j], h * GLA_DV:(h + 1) * GLA_DV] = oh.astype(o_ref.dtype)


def _gla_call(gqkv, glow, up_pad, bias, norm_w, batch, seq_len, chunks):
    n_tok = gqkv.shape[0]
    tcb = chunks * GLA_CHUNK
    steps = seq_len // tcb
    pairs = GLA_KEY // LANES
    return pl.pallas_call(
        functools.partial(_gla_kernel, chunks=chunks),
        grid=(batch, pairs, steps),
        in_specs=[pl.BlockSpec((tcb, LANES), lambda b, p, c: (b * steps + c, p)),
                  pl.BlockSpec((tcb, LANES), lambda b, p, c: (b * steps + c, pairs + p)),
                  pl.BlockSpec((tcb, GLOW_PAD), lambda b, p, c: (b * steps + c, 0)),
                  pl.BlockSpec((tcb, 2 * GLA_DV), lambda b, p, c: (b * steps + c, pairs + p)),
                  pl.BlockSpec((GLOW_PAD, LANES), lambda b, p, c: (0, p)),
                  pl.BlockSpec((1, LANES), lambda b, p, c: (0, p)),
                  pl.BlockSpec((1, GLA_DV), lambda b, p, c: (0, 0))],
        out_specs=pl.BlockSpec((tcb, 2 * GLA_DV), lambda b, p, c: (b * steps + c, p)),
        out_shape=jax.ShapeDtypeStruct((n_tok, GLA_VAL), BF16),
        scratch_shapes=[pltpu.VMEM((2 * GLA_DV, LANES), F32)],
        compiler_params=_cparams(("parallel", "parallel", "arbitrary")),
        name="l0_gla",
    )(gqkv, gqkv, glow, gqkv, up_pad, bias, norm_w)


def _merge_masks(n):
    r = _iota((n, LANES), 0)
    c = _iota((n, LANES), 1) % HEAD
    masks = []
    s = 1
    while s < n:
        masks.append(((r // s) % 2 == 1) & ((c // s) == (r // s) - 1))
        s *= 2
    return (r == c).astype(F32), masks


def _unit_lower_inverses(lows, n):
    eye, masks = _merge_masks(n)
    ts = [eye + jnp.where(masks[0], low, 0.0) for low in lows]
    for sub in masks[1:]:
        ys = [_dot(jnp.where(sub, low, 0.0), _head_stack(t)) for low, t in zip(lows, ts)]
        ts = [t + _dot(t, _head_stack(y)) for t, y in zip(ts, ys)]
    return ts


def _rwkv_chunk_kernel(r_ref, k_ref, v_ref, xwa_ref, w0_ref, wup_ref, a0_ref, aup_ref,
                       kk_ref, ka_ref, rk_ref,
                       rp_ref, op_ref, bonus_ref, m_ref, n_ref, *, chunks):
    C = RWKV_CHUNK
    tril = _tril_ones(C)
    rr = _iota((2 * C, LANES), 0)
    cc = _iota((2 * C, LANES), 1) % HEAD
    tri2 = ((rr < C) & (rr > cc)) | (rr - C >= cc)
    hb = _head_block_ones()
    sq_r = _iota((LANES, LANES), 0)
    sq_c = _iota((LANES, LANES), 1)
    same_head = (sq_r // HEAD) == (sq_c // HEAD)
    eye128 = sq_r == sq_c

    r_all = r_ref[...].astype(F32)
    k_all = k_ref[...].astype(F32)
    v_all = v_ref[...].astype(F32)
    xwa = xwa_ref[...]
    w = -_softplus(-(w0_ref[...] + _dot(jnp.tanh(xwa), wup_ref[...]))) - 0.5
    lw_all = -jnp.exp(w)
    a_sig = _sigmoid(a0_ref[...] + _dot(xwa, aup_ref[...]))
    kk = k_all * kk_ref[...]
    kk = kk / jnp.maximum(jnp.sqrt(_dot_exact_lhs(kk * kk, hb)), 1e-12)
    k_all = k_all * (1.0 + (a_sig - 1.0) * ka_ref[...])
    bonus_ref[...] = _dot_exact_lhs(r_all * k_all * rk_ref[...], hb) * v_all
    a_all = -kk
    b_all = kk * a_sig

    rows = [slice(j * C, (j + 1) * C) for j in range(chunks)]
    cums = [_dot_exact_rhs(tril, lw_all[rw]) for rw in rows]
    rt, at, bt, kt, ends, v, dec = [], [], [], [], [], [], []
    for rw, cum in zip(rows, cums):
        cum_last = cum[C - 1:C, :]
        e_neg = jnp.exp(-cum)
        e_end = jnp.exp(cum_last - cum)
        rt.append(r_all[rw] * jnp.exp(cum))
        at.append(a_all[rw] * jnp.exp(cum - lw_all[rw]))
        bt.append(b_all[rw] * e_neg)
        kt.append(k_all[rw] * e_neg)
        ends.append(jnp.concatenate([b_all[rw] * e_end, k_all[rw] * e_end], axis=0))
        v.append(v_all[rw])
        dec.append(jnp.exp(cum_last))

    n = range(chunks)
    lhs = [jnp.concatenate([at[j], rt[j]], axis=0) for j in n]
    left = [jnp.where(tri2, _dot_nt(lhs[j], _head_stack(bt[j])), 0.0) for j in n]
    right = [jnp.where(tri2, _dot_nt(lhs[j], _head_stack(kt[j])), 0.0) for j in n]
    tinv = _unit_lower_inverses([lf[:C] for lf in left], C)
    kv = [_dot(right[j], _head_stack(v[j])) for j in n]
    wz = [_dot(tinv[j], _head_stack(jnp.concatenate([at[j], kv[j][:C]], axis=1))) for j in n]
    ro = [_dot(left[j][C:], _head_stack(wz[j])) for j in n]
    zero = jnp.zeros((C, LANES), F32)
    mn = [_dot_tn(ends[j], jnp.concatenate([wz[j], jnp.concatenate([zero, v[j]], axis=1)], axis=0))
          for j in n]
    for j in n:
        rp_ref[rows[j], :] = (rt[j] + ro[j][:, :LANES]).astype(rp_ref.dtype)
        op_ref[rows[j], :] = ro[j][:, LANES:] + kv[j][C:]
        m_ref[0, 0, j] = (jnp.where(eye128, dec[j], 0.0)
                          + jnp.where(same_head, mn[j][:, :LANES], 0.0)).astype(m_ref.dtype)
        n_ref[0, 0, j] = jnp.where(same_head, mn[j][:, LANES:], 0.0)


def _rwkv_chunk_call(rkv, lora, w0, wup_pad, a0, aup_pad, k_k, k_a, r_k, batch, seq_len, chunks):
    n_tok = rkv.shape[0]
    tcb = chunks * RWKV_CHUNK
    steps = seq_len // tcb
    pairs = RWKV_W // LANES
    nc = seq_len // RWKV_CHUNK
    col = lambda off: (lambda b, p, c: (b * steps + c, off + p))
    par = lambda b, p, c: (0, p)
    tok = lambda b, p, c: (b * steps + c, p)
    mat = lambda b, p, c: (b, p, c, 0, 0)
    return pl.pallas_call(
        functools.partial(_rwkv_chunk_kernel, chunks=chunks),
        grid=(batch, pairs, steps),
        in_specs=[pl.BlockSpec((tcb, LANES), col(0)),
                  pl.BlockSpec((tcb, LANES), col(pairs)),
                  pl.BlockSpec((tcb, LANES), col(2 * pairs)),
                  pl.BlockSpec((tcb, RWKV_LORA), lambda b, p, c: (b * steps + c, 0)),
                  pl.BlockSpec((1, LANES), par),
                  pl.BlockSpec((RWKV_LORA, LANES), par),
                  pl.BlockSpec((1, LANES), par),
                  pl.BlockSpec((RWKV_LORA, LANES), par),
                  pl.BlockSpec((1, LANES), par),
                  pl.BlockSpec((1, LANES), par),
                  pl.BlockSpec((1, LANES), par)],
        out_specs=[pl.BlockSpec((tcb, LANES), tok),
                   pl.BlockSpec((tcb, LANES), tok),
                   pl.BlockSpec((tcb, LANES), tok),
                   pl.BlockSpec((1, 1, chunks, LANES, LANES), mat),
                   pl.BlockSpec((1, 1, chunks, LANES, LANES), mat)],
        out_shape=[jax.ShapeDtypeStruct((n_tok, RWKV_W), BF16),
                   jax.ShapeDtypeStruct((n_tok, RWKV_W), F32),
                   jax.ShapeDtypeStruct((n_tok, RWKV_W), F32),
                   jax.ShapeDtypeStruct((batch, pairs, nc, LANES, LANES), BF16),
                   jax.ShapeDtypeStruct((batch, pairs, nc, LANES, LANES), F32)],
        compiler_params=_cparams(("parallel", "parallel", "parallel")),
        name="l0_rwkv_chunks",
    )(rkv, rkv, rkv, lora, w0, wup_pad, a0, aup_pad, k_k, k_a, r_k)


def _rwkv_scan_kernel(rp_ref, op_ref, bonus_ref, m_ref, n_ref, lnw_ref, lnb_ref, o_ref, st_ref, *, chunks):
    c = pl.program_id(0)

    @pl.when(c == 0)
    def _():
        st_ref[...] = jnp.zeros_like(st_ref)

    C = RWKV_CHUNK
    batch = rp_ref.shape[0]
    pairs = RWKV_W // LANES
    hb = _head_block_ones()
    seqs = [(b, p) for b in range(batch) for p in range(pairs)]
    states = {bp: [st_ref[bp[0], bp[1]]] for bp in seqs}
    for j in range(chunks):
        for b, p in seqs:
            states[b, p].append(_dot(m_ref[b, p, j], states[b, p][j]) + n_ref[b, p, j])
    for b, p in seqs:
        st_ref[b, p] = states[b, p][chunks]
    cols = {bp: slice(bp[1] * LANES, (bp[1] + 1) * LANES) for bp in seqs}
    os = [jnp.concatenate([_dot(rp_ref[b, j * C:(j + 1) * C, cols[b, p]], states[b, p][j])
                           for j in range(chunks)], axis=0) + op_ref[b, :, cols[b, p]] for b, p in seqs]
    means = [_dot_exact_lhs(o, hb) * (1.0 / RWKV_HEAD) for o in os]
    ds = [o - mean for o, mean in zip(os, means)]
    variances = [_dot_exact_lhs(d * d, hb) * (1.0 / RWKV_HEAD) for d in ds]
    for (b, p), d, var in zip(seqs, ds, variances):
        c_ = cols[b, p]
        o_ref[b, :, c_] = (d * lax.rsqrt(var + RWKV_LN_EPS) * lnw_ref[:, c_] + lnb_ref[:, c_]
                           + bonus_ref[b, :, c_]).astype(o_ref.dtype)


def _rwkv_scan_call(rp, op, bonus, m, n, ln_w, ln_b, batch, seq_len, chunks):
    tcb = chunks * RWKV_CHUNK
    pairs = RWKV_W // LANES
    seq3 = lambda t: t.reshape(batch, seq_len, RWKV_W)
    tok = lambda c: (0, c, 0)
    const = lambda c: (0, 0)
    mat = lambda c: (0, 0, c, 0, 0)
    out = pl.pallas_call(
        functools.partial(_rwkv_scan_kernel, chunks=chunks),
        grid=(seq_len // tcb,),
        in_specs=[pl.BlockSpec((batch, tcb, RWKV_W), tok),
                  pl.BlockSpec((batch, tcb, RWKV_W), tok),
                  pl.BlockSpec((batch, tcb, RWKV_W), tok),
                  pl.BlockSpec((batch, pairs, chunks, LANES, LANES), mat),
                  pl.BlockSpec((batch, pairs, chunks, LANES, LANES), mat),
                  pl.BlockSpec((1, RWKV_W), const),
                  pl.BlockSpec((1, RWKV_W), const)],
        out_specs=pl.BlockSpec((batch, tcb, RWKV_W), tok),
        out_shape=jax.ShapeDtypeStruct((batch, seq_len, RWKV_W), BF16),
        scratch_shapes=[pltpu.VMEM((batch, pairs, LANES, LANES), F32)],
        compiler_params=_cparams(("arbitrary",)),
        name="l0_rwkv_scan",
    )(seq3(rp), seq3(op), seq3(bonus), m, n, ln_w, ln_b)
    return out.reshape(batch * seq_len, RWKV_W)


def _out0_kernel(oa_ref, ob_ref, gate_ref, x_ref, w_ref, h_ref):
    g = gate_ref[...].astype(F32)
    g = g * _sigmoid(g)
    ya = (oa_ref[...].astype(F32) * g[:, :GLA_VAL]).astype(BF16)
    yb = (ob_ref[...].astype(F32) * g[:, GLA_VAL:]).astype(BF16)
    h_ref[...] = (x_ref[...]
                  + jnp.dot(ya, w_ref[:GLA_VAL, :], preferred_element_type=F32)
                  + jnp.dot(yb, w_ref[GLA_VAL:, :], preferred_element_type=F32))


def _out0_call(oa, ob, gate, x2, w_out, tm):
    n_tok = x2.shape[0]
    row = lambda i: (i, 0)
    const = lambda i: (0, 0)
    return pl.pallas_call(
        _out0_kernel,
        grid=(n_tok // tm,),
        in_specs=[pl.BlockSpec((tm, GLA_VAL), row),
                  pl.BlockSpec((tm, RWKV_W), row),
                  pl.BlockSpec((tm, MIX0), row),
                  pl.BlockSpec((tm, D_MODEL), row),
                  pl.BlockSpec((MIX0, D_MODEL), const)],
        out_specs=pl.BlockSpec((tm, D_MODEL), row),
        out_shape=jax.ShapeDtypeStruct((n_tok, D_MODEL), F32),
        compiler_params=_cparams(("parallel",)),
        name="l0_gate_out",
    )(oa, ob, gate, x2, w_out)


def _rope_group(x, cos, sin_lo, sin_hi):
    half = ROPE_DIMS // 2
    return x * cos + pltpu.roll(x, LANES - half, 1) * sin_lo + pltpu.roll(x, half, 1) * sin_hi


def _in1_kernel(h_ref, nw_ref, w_ref, b_ref, cos_ref, slo_ref, shi_ref,
                q_ref, k_ref, v_ref, gate_ref):
    hn = _rmsnorm_rows(h_ref[...], nw_ref[...]).astype(BF16)
    cos = cos_ref[...]
    slo = slo_ref[...]
    shi = shi_ref[...]
    scale = SWA_HEAD ** -0.5
    for g in range(MIX1 // LANES):
        cols = slice(g * LANES, (g + 1) * LANES)
        y = jnp.dot(hn, w_ref[:, cols], preferred_element_type=F32) + b_ref[:, cols]
        q_ref[:, cols] = (_rope_group(y, cos, slo, shi) * scale).astype(q_ref.dtype)
    for g in range(SWA_KV // LANES):
        cols = slice(MIX1 + g * LANES, MIX1 + (g + 1) * LANES)
        y = jnp.dot(hn, w_ref[:, cols], preferred_element_type=F32) + b_ref[:, cols]
        k_ref[:, g * LANES:(g + 1) * LANES] = _rope_group(y, cos, slo, shi).astype(k_ref.dtype)
    cols = slice(MIX1 + SWA_KV, SWA_QKV)
    v_ref[...] = (jnp.dot(hn, w_ref[:, cols], preferred_element_type=F32) + b_ref[:, cols]).astype(v_ref.dtype)
    gate_ref[...] = jnp.dot(hn, w_ref[:, SWA_QKV:], preferred_element_type=F32).astype(gate_ref.dtype)


def _in1_call(h1, norm_w, w_in, b_in, cos, slo, shi, seq_len, tm):
    n_tok = h1.shape[0]
    tps = seq_len // tm
    row = lambda i: (i, 0)
    const = lambda i: (0, 0)
    pos = lambda i: (i % tps, 0)
    return pl.pallas_call(
        _in1_kernel,
        grid=(n_tok // tm,),
        in_specs=[pl.BlockSpec((tm, D_MODEL), row),
                  pl.BlockSpec((1, D_MODEL), const),
                  pl.BlockSpec((D_MODEL, SWA_QKV + MIX1), const),
                  pl.BlockSpec((1, SWA_QKV), const),
                  pl.BlockSpec((tm, LANES), pos),
                  pl.BlockSpec((tm, LANES), pos),
                  pl.BlockSpec((tm, LANES), pos)],
        out_specs=[pl.BlockSpec((tm, MIX1), row),
                   pl.BlockSpec((tm, SWA_KV), row),
                   pl.BlockSpec((tm, SWA_KV), row),
                   pl.BlockSpec((tm, MIX1), row)],
        out_shape=[jax.ShapeDtypeStruct((n_tok, MIX1), BF16),
                   jax.ShapeDtypeStruct((n_tok, SWA_KV), BF16),
                   jax.ShapeDtypeStruct((n_tok, SWA_KV), BF16),
                   jax.ShapeDtypeStruct((n_tok, MIX1), BF16)],
        compiler_params=_cparams(("parallel",)),
        name="l1_norm_proj_rope",
    )(h1, norm_w, w_in, b_in, cos, slo, shi)


def _swa_kernel(sink_ref, q_ref, kc_ref, kp_ref, vc_ref, vp_ref, o_ref, *, q_blocks):
    n = pl.program_id(1)
    W = WINDOW
    from_prev = _iota((W, 2 * W), 0) > (_iota((W, 2 * W), 1) % W)
    no_prev = jnp.where(n > 0, 0.0, -jnp.inf)
    col_row = _iota((1, 2 * W), 1)
    out_row = _iota((LANES, W), 0)
    kv_groups = SWA_KV // LANES
    groups = MIX1 // LANES // kv_groups
    tasks = [(j, pp, pp * groups + g) for j in range(q_blocks) for pp in range(kv_groups) for g in range(groups)]
    kk, vt = {}, {}
    for j in range(q_blocks):
        for pp in range(kv_groups):
            cols = slice(pp * LANES, (pp + 1) * LANES)
            if j == 0:
                kk[j, pp] = jnp.concatenate([kp_ref[:, cols], kc_ref[:W, cols]], axis=0)
                vv = jnp.concatenate([vp_ref[:, cols], vc_ref[:W, cols]], axis=0)
            else:
                kk[j, pp] = kc_ref[(j - 1) * W:(j + 1) * W, cols]
                vv = vc_ref[(j - 1) * W:(j + 1) * W, cols]
            vt[j, pp] = vv.astype(F32).T.astype(BF16)

    def scores(j, pp, blk):
        q = q_ref[j * W:(j + 1) * W, blk * LANES:(blk + 1) * LANES]
        return lax.dot_general(kk[j, pp], _head_stack(q), (((1,), (1,)), ((), ())), preferred_element_type=F32)

    ahead = 8
    pending = [scores(*t) for t in tasks[:ahead]]
    for i, (j, pp, blk) in enumerate(tasks):
        st = pending.pop(0)
        if i + ahead < len(tasks):
            pending.append(scores(*tasks[i + ahead]))
        s_prev = st[:W] + no_prev if j == 0 else st[:W]
        s = jnp.where(from_prev, s_prev, st[W:])
        sink = jnp.where(col_row < W, sink_ref[2 * blk], sink_ref[2 * blk + 1])
        m = jnp.maximum(jnp.max(s, axis=0, keepdims=True), sink)
        p = jnp.exp(s - m)
        denom = jnp.sum(p, axis=0, keepdims=True) + jnp.exp(sink - m)
        pb = p.astype(BF16)
        zero = jnp.zeros_like(pb)
        p2 = jnp.concatenate([jnp.where(from_prev, pb, zero), jnp.where(from_prev, zero, pb)], axis=0)
        ot = jnp.dot(vt[j, pp], p2, preferred_element_type=F32) * (1.0 / denom)
        ot = jnp.where(out_row < HEAD, ot[:, :W], ot[:, W:])
        o_ref[j * W:(j + 1) * W, blk * LANES:(blk + 1) * LANES] = ot.T.astype(o_ref.dtype)


def _swa_call(sinks, q, k, v, batch, seq_len, q_blocks):
    n_tok = q.shape[0]
    rows = q_blocks * WINDOW
    steps = seq_len // rows
    cur = lambda b, n: (b * steps + n, 0)
    prev = lambda b, n: (jnp.maximum((b * steps + n) * q_blocks - 1, 0), 0)
    return pl.pallas_call(
        functools.partial(_swa_kernel, q_blocks=q_blocks),
        grid=(batch, steps),
        in_specs=[pl.BlockSpec(memory_space=pltpu.SMEM),
                  pl.BlockSpec((rows, MIX1), cur),
                  pl.BlockSpec((rows, SWA_KV), cur),
                  pl.BlockSpec((WINDOW, SWA_KV), prev),
                  pl.BlockSpec((rows, SWA_KV), cur),
                  pl.BlockSpec((WINDOW, SWA_KV), prev)],
        out_specs=pl.BlockSpec((rows, MIX1), cur),
        out_shape=jax.ShapeDtypeStruct((n_tok, MIX1), BF16),
        compiler_params=_cparams(("parallel", "parallel")),
        name="l1_swa",
    )(sinks, q, k, k, v, v)


def _out1_kernel(o_ref, gate_ref, h_ref, w_ref, b_ref, nw_ref, y_ref):
    g = gate_ref[...].astype(F32)
    y = (o_ref[...].astype(F32) * (g * _sigmoid(g))).astype(BF16)
    h = h_ref[...] + jnp.dot(y, w_ref[...], preferred_element_type=F32) + b_ref[...]
    y_ref[...] = _rmsnorm_rows(h, nw_ref[...])


def _out1_call(o, gate, h1, w_out, b_out, norm_w, tm):
    n_tok = h1.shape[0]
    row = lambda i: (i, 0)
    const = lambda i: (0, 0)
    return pl.pallas_call(
        _out1_kernel,
        grid=(n_tok // tm,),
        in_specs=[pl.BlockSpec((tm, MIX1), row),
                  pl.BlockSpec((tm, MIX1), row),
                  pl.BlockSpec((tm, D_MODEL), row),
                  pl.BlockSpec((MIX1, D_MODEL), const),
                  pl.BlockSpec((1, D_MODEL), const),
                  pl.BlockSpec((1, D_MODEL), const)],
        out_specs=pl.BlockSpec((tm, D_MODEL), row),
        out_shape=jax.ShapeDtypeStruct((n_tok, D_MODEL), F32),
        compiler_params=_cparams(("parallel",)),
        name="l1_gate_out_norm",
    )(o, gate, h1, w_out, b_out, norm_w)


def _pad_rows(w, rows):
    return jnp.concatenate([w, jnp.zeros((rows - w.shape[0], w.shape[1]), w.dtype)], axis=0)


def _pair_heads(t, axis):
    shape = t.shape
    split = shape[:axis] + (SWA_KV_HEADS // 2, 2, SWA_GROUP, SWA_HEAD) + shape[axis + 1:]
    return jnp.swapaxes(t.reshape(split), axis + 1, axis + 2).reshape(shape)


def _rope_tables(seq_len):
    half = ROPE_DIMS // 2
    inv_freq = ROPE_THETA ** (-jnp.arange(half, dtype=F32) / half)
    d = jnp.arange(LANES) % SWA_HEAD
    ang = jnp.arange(seq_len).astype(F32)[:, None] * inv_freq[d % half][None, :]
    cos = jnp.where(d < ROPE_DIMS, jnp.cos(ang), 1.0)
    sin = jnp.sin(ang)
    sin_lo = jnp.where(d < half, -sin, 0.0)
    sin_hi = jnp.where((d >= half) & (d < ROPE_DIMS), sin, 0.0)
    return cos, sin_lo, sin_hi


def _forward(x, norm_w, w_in0, gla_gk_up, gla_gk_bias, gla_norm_w, rwkv_mu, rwkv_w0, rwkv_w_up,
             rwkv_a0, rwkv_a_up, rwkv_k_k, rwkv_k_a, rwkv_r_k, rwkv_ln_w, rwkv_ln_b, w_out0,
             w_in1, b_in1, attn_sinks, w_out1, b_out1, final_norm_w, *, tm, gla_chunks, rwkv_chunks, scan_chunks,
             swa_blocks):
    batch, seq_len, _ = x.shape
    x2 = x.reshape(batch * seq_len, D_MODEL)
    row = lambda t: t.reshape(1, -1)

    w0 = w_in0[0]
    w_gla = w0[:, :GLA_QKV].astype(BF16)
    w_glow = jnp.pad(w0[:, GLA_QKV:GLA_QKV + GLA_GATE_RANK].astype(BF16),
                     ((0, 0), (0, GLOW_PAD - GLA_GATE_RANK)))
    w_rest = w0[:, GLA_QKV + GLA_GATE_RANK:].astype(BF16)
    gqkv, glow, rkv, lora, gate0 = _in0_call(x2, row(norm_w[0]), w_gla, w_glow, w_rest, row(rwkv_mu[0]),
                                             seq_len, tm)

    up_pad = _pad_rows(gla_gk_up[0], GLOW_PAD).astype(BF16)
    o_a = _gla_call(gqkv, glow, up_pad, row(gla_gk_bias[0]), row(gla_norm_w[0]), batch, seq_len, gla_chunks)

    zeros_r = jnp.zeros((RWKV_DECAY_RANK, RWKV_W), F32)
    wup_pad = jnp.concatenate([rwkv_w_up[0], zeros_r], axis=0).astype(BF16)
    aup_pad = jnp.concatenate([zeros_r, rwkv_a_up[0]], axis=0).astype(BF16)
    rp, op, bonus, m, n = _rwkv_chunk_call(
        rkv, lora, row(rwkv_w0[0]), wup_pad, row(rwkv_a0[0]), aup_pad,
        row(rwkv_k_k[0]), row(rwkv_k_a[0]), row(rwkv_r_k[0]), batch, seq_len, rwkv_chunks)
    o_b = _rwkv_scan_call(rp, op, bonus, m, n, row(rwkv_ln_w[0]), row(rwkv_ln_b[0]),
                          batch, seq_len, scan_chunks)

    h1 = _out0_call(o_a, o_b, gate0, x2, w_out0[0].astype(BF16), tm)

    w1 = w_in1[0]
    w1p = jnp.concatenate([_pair_heads(w1[:, :MIX1], 1), w1[:, MIX1:SWA_QKV], _pair_heads(w1[:, SWA_QKV:], 1)],
                          axis=1).astype(BF16)
    b1 = b_in1[0]
    b1p = row(jnp.concatenate([_pair_heads(b1[:MIX1], 0), b1[MIX1:]]))
    sinks_p = jnp.swapaxes(attn_sinks[0].reshape(SWA_KV_HEADS // 2, 2, SWA_GROUP), 1, 2).reshape(SWA_Q_HEADS)
    cos, slo, shi = _rope_tables(seq_len)
    q, k, v, gate1 = _in1_call(h1, row(norm_w[1]), w1p, b1p, cos, slo, shi, seq_len, tm)
    o1 = _swa_call(sinks_p, q, k, v, batch, seq_len, swa_blocks)
    w_out1p = _pair_heads(w_out1[0], 0).astype(BF16)
    y = _out1_call(o1, gate1, h1, w_out1p, row(b_out1[0]), row(final_norm_w), tm)
    return y.reshape(batch, seq_len, D_MODEL)


def kernel(x, norm_w, w_in0, gla_gk_up, gla_gk_bias, gla_norm_w, rwkv_mu, rwkv_w0, rwkv_w_up, rwkv_a0,
           rwkv_a_up, rwkv_k_k, rwkv_k_a, rwkv_r_k, rwkv_ln_w, rwkv_ln_b, w_out0, w_in1, b_in1,
           attn_sinks, w_out1, b_out1, final_norm_w):
    return _forward(x, norm_w, w_in0, gla_gk_up, gla_gk_bias, gla_norm_w, rwkv_mu, rwkv_w0, rwkv_w_up,
                    rwkv_a0, rwkv_a_up, rwkv_k_k, rwkv_k_a, rwkv_r_k, rwkv_ln_w, rwkv_ln_b, w_out0,
                    w_in1, b_in1, attn_sinks, w_out1, b_out1, final_norm_w,
                    tm=512, gla_chunks=16, rwkv_chunks=32, scan_chunks=4, swa_blocks=4)
```

```python
import functools

import jax
import jax.numpy as jnp
from jax import lax
from jax.experimental import pallas as pl
from jax.experimental.pallas import tpu as pltpu

F32 = jnp.float32
BF16 = jnp.bfloat16

D_MODEL = 1024
NORM_EPS = 1e-5

GLA_HEADS = 4
GLA_DK = 64
GLA_DV = 128
GLA_KEY = GLA_HEADS * GLA_DK
GLA_VAL = GLA_HEADS * GLA_DV
GLA_GATE_RANK = 16
GLA_GATE_NORMALIZER = 16.0
GLA_CHUNK = 64

RWKV_HEADS = 8
RWKV_HEAD = 64
RWKV_W = RWKV_HEADS * RWKV_HEAD
RWKV_DECAY_RANK = 64
RWKV_A_RANK = 64
RWKV_LN_EPS = 64e-5
RWKV_RKV = 3 * RWKV_W
RWKV_LORA = RWKV_DECAY_RANK + RWKV_A_RANK
RWKV_SHIFT = RWKV_RKV + RWKV_LORA
RWKV_CHUNK = 64

MIX0 = GLA_VAL + RWKV_W
GLA_QKV = 2 * GLA_KEY + GLA_VAL

SWA_Q_HEADS = 16
SWA_KV_HEADS = 4
SWA_GROUP = SWA_Q_HEADS // SWA_KV_HEADS
SWA_HEAD = 64
WINDOW = 128
ROPE_DIMS = SWA_HEAD // 4
ROPE_THETA = 500000.0
MIX1 = SWA_Q_HEADS * SWA_HEAD
SWA_KV = SWA_KV_HEADS * SWA_HEAD
SWA_QKV = MIX1 + 2 * SWA_KV

LANES = 128
HEAD = 64
GLOW_PAD = LANES
VMEM_LIMIT = 56 * 1024 * 1024


def _cparams(sem):
    return pltpu.CompilerParams(dimension_semantics=sem, vmem_limit_bytes=VMEM_LIMIT)


def _dot(a, b):
    return jnp.dot(a.astype(BF16), b.astype(BF16), preferred_element_type=F32)


def _dot_nt(a, b):
    return lax.dot_general(a.astype(BF16), b.astype(BF16), (((1,), (1,)), ((), ())),
                           preferred_element_type=F32)


def _dot_tn(a, b):
    return lax.dot_general(a.astype(BF16), b.astype(BF16), (((0,), (0,)), ((), ())),
                           preferred_element_type=F32)


def _split2(x):
    hi = x.astype(BF16)
    lo = (x - hi.astype(F32)).astype(BF16)
    return hi, lo


def _dot_exact_rhs(a_bf16, x):
    hi, lo = _split2(x)
    return (jnp.dot(a_bf16, hi, preferred_element_type=F32)
            + jnp.dot(a_bf16, lo, preferred_element_type=F32))


def _dot_exact_lhs(x, b_bf16):
    hi, lo = _split2(x)
    return (jnp.dot(hi, b_bf16, preferred_element_type=F32)
            + jnp.dot(lo, b_bf16, preferred_element_type=F32))


def _iota(shape, dim):
    return lax.broadcasted_iota(jnp.int32, shape, dim)


def _tril_ones(n, dtype=BF16):
    return (_iota((n, n), 0) >= _iota((n, n), 1)).astype(dtype)


def _head_block_ones(n=LANES, dtype=BF16):
    return ((_iota((n, n), 0) // HEAD) == (_iota((n, n), 1) // HEAD)).astype(dtype)


def _head_stack(x):
    head = (_iota(x.shape, 1) % LANES) // HEAD
    return jnp.concatenate([jnp.where(head == 0, x, 0.0), jnp.where(head == 1, x, 0.0)], axis=0)


def _softplus(z):
    return jnp.maximum(z, 0.0) + jnp.log(1.0 + jnp.exp(-jnp.abs(z)))


def _sigmoid(z):
    return 1.0 / (1.0 + jnp.exp(-z))


def _rmsnorm_rows(x, w):
    return x * lax.rsqrt(jnp.mean(x * x, axis=-1, keepdims=True) + NORM_EPS) * w


def _in0_kernel(x_ref, nw_ref, wg_ref, wl_ref, wr_ref, mu_ref,
                gqkv_ref, glow_ref, rkv_ref, lora_ref, gate_ref, carry_ref, *, tiles_per_seq):
    i = pl.program_id(0)

    @pl.when(i == 0)
    def _():
        carry_ref[...] = jnp.zeros_like(carry_ref)

    xn = _rmsnorm_rows(x_ref[...], nw_ref[...]).astype(BF16)
    gqkv_ref[...] = jnp.dot(xn, wg_ref[...], preferred_element_type=F32).astype(gqkv_ref.dtype)
    glow_ref[...] = jnp.dot(xn, wl_ref[...], preferred_element_type=F32)
    rw = jnp.dot(xn, wr_ref[:, :RWKV_SHIFT], preferred_element_type=F32)
    gate_ref[...] = jnp.dot(xn, wr_ref[:, RWKV_SHIFT:], preferred_element_type=F32).astype(gate_ref.dtype)

    tm = rw.shape[0]
    first = (i % tiles_per_seq) == 0
    prev_last = jnp.where(first, 0.0, carry_ref[7:8, :])
    rolled = pltpu.roll(rw, 1, 0)
    prev = jnp.where(_iota(rw.shape, 0) == 0, prev_last, rolled)
    mixed = rw + (prev - rw) * mu_ref[...]
    rkv_ref[...] = mixed[:, :RWKV_RKV].astype(rkv_ref.dtype)
    lora_ref[...] = mixed[:, RWKV_RKV:]
    carry_ref[...] = rw[tm - 8:tm, :]


def _in0_call(x2, norm_w, w_gla, w_glow, w_rest, mu, seq_len, tm):
    n_tok = x2.shape[0]
    row = lambda i: (i, 0)
    const = lambda i: (0, 0)
    outs = [(GLA_QKV, BF16), (GLOW_PAD, F32), (RWKV_RKV, BF16), (RWKV_LORA, F32), (MIX0, BF16)]
    return pl.pallas_call(
        functools.partial(_in0_kernel, tiles_per_seq=seq_len // tm),
        grid=(n_tok // tm,),
        in_specs=[pl.BlockSpec((tm, D_MODEL), row),
                  pl.BlockSpec((1, D_MODEL), const),
                  pl.BlockSpec((D_MODEL, GLA_QKV), const),
                  pl.BlockSpec((D_MODEL, GLOW_PAD), const),
                  pl.BlockSpec((D_MODEL, RWKV_SHIFT + MIX0), const),
                  pl.BlockSpec((1, RWKV_SHIFT), const)],
        out_specs=[pl.BlockSpec((tm, n), row) for n, _ in outs],
        out_shape=[jax.ShapeDtypeStruct((n_tok, n), dt) for n, dt in outs],
        scratch_shapes=[pltpu.VMEM((8, RWKV_SHIFT), F32)],
        compiler_params=_cparams(("arbitrary",)),
        name="l0_norm_proj",
    )(x2, norm_w, w_gla, w_glow, w_rest, mu)


def _gla_kernel(q_ref, k_ref, glow_ref, v_ref, up_ref, bias_ref, nw_ref, o_ref, st_ref, *, chunks):
    c = pl.program_id(2)

    @pl.when(c == 0)
    def _():
        st_ref[...] = jnp.zeros_like(st_ref)

    C = GLA_CHUNK
    tril = _tril_ones(C)
    causal = _iota((C, LANES), 0) >= (_iota((C, LANES), 1) % HEAD)
    sr = _iota((2 * GLA_DV, LANES), 0)
    sl = _iota((2 * GLA_DV, LANES), 1)
    st_mask = (sr // GLA_DV) == (sl // HEAD)
    vl = _iota((C, 2 * GLA_DV), 1)
    scale = GLA_DK ** -0.5
    z = _dot(glow_ref[...], up_ref[...]) + bias_ref[...]
    g_all = -_softplus(-z) / GLA_GATE_NORMALIZER
    q_all = q_ref[...].astype(F32) * scale
    k_all = k_ref[...].astype(F32)
    rows = [slice(j * C, (j + 1) * C) for j in range(chunks)]
    bs = [_dot_exact_rhs(tril, g_all[rw]) for rw in rows]
    qe, ke, qb, kl, dec, vs = [], [], [], [], [], []
    for rw, b in zip(rows, bs):
        ref = b[C // 2:C // 2 + 1, :]
        b_last = b[C - 1:C, :]
        qe.append(q_all[rw] * jnp.exp(b - ref))
        ke.append(k_all[rw] * jnp.exp(ref - b))
        qb.append(q_all[rw] * jnp.exp(b))
        kl.append(k_all[rw] * jnp.exp(b_last - b))
        dec.append(jnp.exp(b_last))
        vs.append(v_ref[rw, :])
    att = [jnp.where(causal, _dot_nt(qe[j], _head_stack(ke[j])), 0.0) for j in range(chunks)]
    kv = [jnp.where(st_mask, _dot_tn(vs[j], kl[j]), 0.0) for j in range(chunks)]
    v_diag = [jnp.concatenate([jnp.where(vl < GLA_DV, vs[j], jnp.zeros_like(vs[j])),
                               jnp.where(vl >= GLA_DV, vs[j], jnp.zeros_like(vs[j]))], axis=0)
              for j in range(chunks)]
    intra = [jnp.dot(att[j].astype(BF16), v_diag[j], preferred_element_type=F32) for j in range(chunks)]
    states = [st_ref[...]]
    for j in range(chunks):
        states.append(states[j] * dec[j] + kv[j])
    st_ref[...] = states[chunks]
    for j in range(chunks):
        o = intra[j] + _dot_nt(qb[j], states[j])
        for h in range(2):
            oh = o[:, h * GLA_DV:(h + 1) * GLA_DV]
            oh = oh * lax.rsqrt(jnp.mean(oh * oh, axis=-1, keepdims=True) + NORM_EPS) * nw_ref[...]
            o_ref[rows[j], h * GLA_DV:(h + 1) * GLA_DV] = oh.astype(o_ref.dtype)


def _gla_call(gqkv, glow, up_pad, bias, norm_w, batch, seq_len, chunks):
    n_tok = gqkv.shape[0]
    tcb = chunks * GLA_CHUNK
    steps = seq_len // tcb
    pairs = GLA_KEY // LANES
    return pl.pallas_call(
        functools.partial(_gla_kernel, chunks=chunks),
        grid=(batch, pairs, steps),
        in_specs=[pl.BlockSpec((tcb, LANES), lambda b, p, c: (b * steps + c, p)),
                  pl.BlockSpec((tcb, LANES), lambda b, p, c: (b * steps + c, pairs + p)),
                  pl.BlockSpec((tcb, GLOW_PAD), lambda b, p, c: (b * steps + c, 0)),
                  pl.BlockSpec((tcb, 2 * GLA_DV), lambda b, p, c: (b * steps + c, pairs + p)),
                  pl.BlockSpec((GLOW_PAD, LANES), lambda b, p, c: (0, p)),
                  pl.BlockSpec((1, LANES), lambda b, p, c: (0, p)),
                  pl.BlockSpec((1, GLA_DV), lambda b, p, c: (0, 0))],
        out_specs=pl.BlockSpec((tcb, 2 * GLA_DV), lambda b, p, c: (b * steps + c, p)),
        out_shape=jax.ShapeDtypeStruct((n_tok, GLA_VAL), BF16),
        scratch_shapes=[pltpu.VMEM((2 * GLA_DV, LANES), F32)],
        compiler_params=_cparams(("parallel", "parallel", "arbitrary")),
        name="l0_gla",
    )(gqkv, gqkv, glow, gqkv, up_pad, bias, norm_w)


def _merge_masks(n):
    r = _iota((n, LANES), 0)
    c = _iota((n, LANES), 1) % HEAD
    masks = []
    s = 1
    while s < n:
        masks.append(((r // s) % 2 == 1) & ((c // s) == (r // s) - 1))
        s *= 2
    return (r == c).astype(F32), masks


def _unit_lower_inverses(lows, n):
    eye, masks = _merge_masks(n)
    ts = [eye + jnp.where(masks[0], low, 0.0) for low in lows]
    for sub in masks[1:]:
        ys = [_dot(jnp.where(sub, low, 0.0), _head_stack(t)) for low, t in zip(lows, ts)]
        ts = [t + _dot(t, _head_stack(y)) for t, y in zip(ts, ys)]
    return ts


def _rwkv_chunk_kernel(r_ref, k_ref, v_ref, xwa_ref, w0_ref, wup_ref, a0_ref, aup_ref,
                       kk_ref, ka_ref, rk_ref,
                       rp_ref, op_ref, bonus_ref, m_ref, n_ref, *, chunks):
    C = RWKV_CHUNK
    tril = _tril_ones(C)
    rr = _iota((2 * C, LANES), 0)
    cc = _iota((2 * C, LANES), 1) % HEAD
    tri2 = ((rr < C) & (rr > cc)) | (rr - C >= cc)
    hb = _head_block_ones()
    sq_r = _iota((LANES, LANES), 0)
    sq_c = _iota((LANES, LANES), 1)
    same_head = (sq_r // HEAD) == (sq_c // HEAD)
    eye128 = sq_r == sq_c

    r_all = r_ref[...].astype(F32)
    k_all = k_ref[...].astype(F32)
    v_all = v_ref[...].astype(F32)
    xwa = xwa_ref[...]
    w = -_softplus(-(w0_ref[...] + _dot(jnp.tanh(xwa), wup_ref[...]))) - 0.5
    lw_all = -jnp.exp(w)
    a_sig = _sigmoid(a0_ref[...] + _dot(xwa, aup_ref[...]))
    kk = k_all * kk_ref[...]
    kk = kk / jnp.maximum(jnp.sqrt(_dot_exact_lhs(kk * kk, hb)), 1e-12)
    k_all = k_all * (1.0 + (a_sig - 1.0) * ka_ref[...])
    bonus_ref[...] = _dot_exact_lhs(r_all * k_all * rk_ref[...], hb) * v_all
    a_all = -kk
    b_all = kk * a_sig

    rows = [slice(j * C, (j + 1) * C) for j in range(chunks)]
    cums = [_dot_exact_rhs(tril, lw_all[rw]) for rw in rows]
    rt, at, bt, kt, ends, v, dec = [], [], [], [], [], [], []
    for rw, cum in zip(rows, cums):
        cum_last = cum[C - 1:C, :]
        e_neg = jnp.exp(-cum)
        e_end = jnp.exp(cum_last - cum)
        rt.append(r_all[rw] * jnp.exp(cum))
        at.append(a_all[rw] * jnp.exp(cum - lw_all[rw]))
        bt.append(b_all[rw] * e_neg)
        kt.append(k_all[rw] * e_neg)
        ends.append(jnp.concatenate([b_all[rw] * e_end, k_all[rw] * e_end], axis=0))
        v.append(v_all[rw])
        dec.append(jnp.exp(cum_last))

    n = range(chunks)
    lhs = [jnp.concatenate([at[j], rt[j]], axis=0) for j in n]
    left = [jnp.where(tri2, _dot_nt(lhs[j], _head_stack(bt[j])), 0.0) for j in n]
    right = [jnp.where(tri2, _dot_nt(lhs[j], _head_stack(kt[j])), 0.0) for j in n]
    tinv = _unit_lower_inverses([lf[:C] for lf in left], C)
    kv = [_dot(right[j], _head_stack(v[j])) for j in n]
    wz = [_dot(tinv[j], _head_stack(jnp.concatenate([at[j], kv[j][:C]], axis=1))) for j in n]
    ro = [_dot(left[j][C:], _head_stack(wz[j])) for j in n]
    zero = jnp.zeros((C, LANES), F32)
    mn = [_dot_tn(ends[j], jnp.concatenate([wz[j], jnp.concatenate([zero, v[j]], axis=1)], axis=0))
          for j in n]
    for j in n:
        rp_ref[rows[j], :] = (rt[j] + ro[j][:, :LANES]).astype(rp_ref.dtype)
        op_ref[rows[j], :] = ro[j][:, LANES:] + kv[j][C:]
        m_ref[0, 0, j] = (jnp.where(eye128, dec[j], 0.0)
                          + jnp.where(same_head, mn[j][:, :LANES], 0.0)).astype(m_ref.dtype)
        n_ref[0, 0, j] = jnp.where(same_head, mn[j][:, LANES:], 0.0)


def _rwkv_chunk_call(rkv, lora, w0, wup_pad, a0, aup_pad, k_k, k_a, r_k, batch, seq_len, chunks):
    n_tok = rkv.shape[0]
    tcb = chunks * RWKV_CHUNK
    steps = seq_len // tcb
    pairs = RWKV_W // LANES
    nc = seq_len // RWKV_CHUNK
    col = lambda off: (lambda b, p, c: (b * steps + c, off + p))
    par = lambda b, p, c: (0, p)
    tok = lambda b, p, c: (b * steps + c, p)
    mat = lambda b, p, c: (b, p, c, 0, 0)
    return pl.pallas_call(
        functools.partial(_rwkv_chunk_kernel, chunks=chunks),
        grid=(batch, pairs, steps),
        in_specs=[pl.BlockSpec((tcb, LANES), col(0)),
                  pl.BlockSpec((tcb, LANES), col(pairs)),
                  pl.BlockSpec((tcb, LANES), col(2 * pairs)),
                  pl.BlockSpec((tcb, RWKV_LORA), lambda b, p, c: (b * steps + c, 0)),
                  pl.BlockSpec((1, LANES), par),
                  pl.BlockSpec((RWKV_LORA, LANES), par),
                  pl.BlockSpec((1, LANES), par),
                  pl.BlockSpec((RWKV_LORA, LANES), par),
                  pl.BlockSpec((1, LANES), par),
                  pl.BlockSpec((1, LANES), par),
                  pl.BlockSpec((1, LANES), par)],
        out_specs=[pl.BlockSpec((tcb, LANES), tok),
                   pl.BlockSpec((tcb, LANES), tok),
                   pl.BlockSpec((tcb, LANES), tok),
                   pl.BlockSpec((1, 1, chunks, LANES, LANES), mat),
                   pl.BlockSpec((1, 1, chunks, LANES, LANES), mat)],
        out_shape=[jax.ShapeDtypeStruct((n_tok, RWKV_W), BF16),
                   jax.ShapeDtypeStruct((n_tok, RWKV_W), F32),
                   jax.ShapeDtypeStruct((n_tok, RWKV_W), F32),
                   jax.ShapeDtypeStruct((batch, pairs, nc, LANES, LANES), BF16),
                   jax.ShapeDtypeStruct((batch, pairs, nc, LANES, LANES), F32)],
        compiler_params=_cparams(("parallel", "parallel", "parallel")),
        name="l0_rwkv_chunks",
    )(rkv, rkv, rkv, lora, w0, wup_pad, a0, aup_pad, k_k, k_a, r_k)


def _rwkv_scan_kernel(rp_ref, op_ref, bonus_ref, m_ref, n_ref, lnw_ref, lnb_ref, o_ref, st_ref, *, chunks):
    c = pl.program_id(0)

    @pl.when(c == 0)
    def _():
        st_ref[...] = jnp.zeros_like(st_ref)

    C = RWKV_CHUNK
    batch = rp_ref.shape[0]
    pairs = RWKV_W // LANES
    hb = _head_block_ones()
    seqs = [(b, p) for b in range(batch) for p in range(pairs)]
    states = {bp: [st_ref[bp[0], bp[1]]] for bp in seqs}
    for j in range(chunks):
        for b, p in seqs:
            states[b, p].append(_dot(m_ref[b, p, j], states[b, p][j]) + n_ref[b, p, j])
    for b, p in seqs:
        st_ref[b, p] = states[b, p][chunks]
    cols = {bp: slice(bp[1] * LANES, (bp[1] + 1) * LANES) for bp in seqs}
    os = [jnp.concatenate([_dot(rp_ref[b, j * C:(j + 1) * C, cols[b, p]], states[b, p][j])
                           for j in range(chunks)], axis=0) + op_ref[b, :, cols[b, p]] for b, p in seqs]
    means = [_dot_exact_lhs(o, hb) * (1.0 / RWKV_HEAD) for o in os]
    ds = [o - mean for o, mean in zip(os, means)]
    variances = [_dot_exact_lhs(d * d, hb) * (1.0 / RWKV_HEAD) for d in ds]
    for (b, p), d, var in zip(seqs, ds, variances):
        c_ = cols[b, p]
        o_ref[b, :, c_] = (d * lax.rsqrt(var + RWKV_LN_EPS) * lnw_ref[:, c_] + lnb_ref[:, c_]
                           + bonus_ref[b, :, c_]).astype(o_ref.dtype)


def _rwkv_scan_call(rp, op, bonus, m, n, ln_w, ln_b, batch, seq_len, chunks):
    tcb = chunks * RWKV_CHUNK
    pairs = RWKV_W // LANES
    seq3 = lambda t: t.reshape(batch, seq_len, RWKV_W)
    tok = lambda c: (0, c, 0)
    const = lambda c: (0, 0)
    mat = lambda c: (0, 0, c, 0, 0)
    out = pl.pallas_call(
        functools.partial(_rwkv_scan_kernel, chunks=chunks),
        grid=(seq_len // tcb,),
        in_specs=[pl.BlockSpec((batch, tcb, RWKV_W), tok),
                  pl.BlockSpec((batch, tcb, RWKV_W), tok),
                  pl.BlockSpec((batch, tcb, RWKV_W), tok),
                  pl.BlockSpec((batch, pairs, chunks, LANES, LANES), mat),
                  pl.BlockSpec((batch, pairs, chunks, LANES, LANES), mat),
                  pl.BlockSpec((1, RWKV_W), const),
                  pl.BlockSpec((1, RWKV_W), const)],
        out_specs=pl.BlockSpec((batch, tcb, RWKV_W), tok),
        out_shape=jax.ShapeDtypeStruct((batch, seq_len, RWKV_W), BF16),
        scratch_shapes=[pltpu.VMEM((batch, pairs, LANES, LANES), F32)],
        compiler_params=_cparams(("arbitrary",)),
        name="l0_rwkv_scan",
    )(seq3(rp), seq3(op), seq3(bonus), m, n, ln_w, ln_b)
    return out.reshape(batch * seq_len, RWKV_W)


def _out0_kernel(oa_ref, ob_ref, gate_ref, x_ref, w_ref, h_ref):
    g = gate_ref[...].astype(F32)
    g = g * _sigmoid(g)
    ya = (oa_ref[...].astype(F32) * g[:, :GLA_VAL]).astype(BF16)
    yb = (ob_ref[...].astype(F32) * g[:, GLA_VAL:]).astype(BF16)
    h_ref[...] = (x_ref[...]
                  + jnp.dot(ya, w_ref[:GLA_VAL, :], preferred_element_type=F32)
                  + jnp.dot(yb, w_ref[GLA_VAL:, :], preferred_element_type=F32))


def _out0_call(oa, ob, gate, x2, w_out, tm):
    n_tok = x2.shape[0]
    row = lambda i: (i, 0)
    const = lambda i: (0, 0)
    return pl.pallas_call(
        _out0_kernel,
        grid=(n_tok // tm,),
        in_specs=[pl.BlockSpec((tm, GLA_VAL), row),
                  pl.BlockSpec((tm, RWKV_W), row),
                  pl.BlockSpec((tm, MIX0), row),
                  pl.BlockSpec((tm, D_MODEL), row),
                  pl.BlockSpec((MIX0, D_MODEL), const)],
        out_specs=pl.BlockSpec((tm, D_MODEL), row),
        out_shape=jax.ShapeDtypeStruct((n_tok, D_MODEL), F32),
        compiler_params=_cparams(("parallel",)),
        name="l0_gate_out",
    )(oa, ob, gate, x2, w_out)


def _rope_group(x, cos, sin_lo, sin_hi):
    half = ROPE_DIMS // 2
    return x * cos + pltpu.roll(x, LANES - half, 1) * sin_lo + pltpu.roll(x, half, 1) * sin_hi


def _in1_kernel(h_ref, nw_ref, wq_ref, wkv_ref, wg_ref, b_ref, cos_ref, slo_ref, shi_ref,
                q_ref, k_ref, v_ref, gate_ref):
    hn = _rmsnorm_rows(h_ref[...], nw_ref[...]).astype(BF16)
    cos = cos_ref[...]
    slo = slo_ref[...]
    shi = shi_ref[...]
    scale = SWA_HEAD ** -0.5
    q = jnp.dot(hn, wq_ref[...], preferred_element_type=F32) + b_ref[:, :MIX1]
    kv = jnp.dot(hn, wkv_ref[...], preferred_element_type=F32) + b_ref[:, MIX1:]
    gate_ref[...] = jnp.dot(hn, wg_ref[...], preferred_element_type=F32).astype(gate_ref.dtype)
    for g in range(MIX1 // LANES):
        cols = slice(g * LANES, (g + 1) * LANES)
        q_ref[:, cols] = (_rope_group(q[:, cols], cos, slo, shi) * scale).astype(q_ref.dtype)
    for g in range(SWA_KV // LANES):
        cols = slice(g * LANES, (g + 1) * LANES)
        k_ref[:, cols] = _rope_group(kv[:, cols], cos, slo, shi).astype(k_ref.dtype)
    v_ref[...] = kv[:, SWA_KV:].astype(v_ref.dtype)


def _in1_call(h1, norm_w, w_q, w_kv, w_gate, b_in, cos, slo, shi, seq_len, tm):
    n_tok = h1.shape[0]
    tps = seq_len // tm
    row = lambda i: (i, 0)
    const = lambda i: (0, 0)
    pos = lambda i: (i % tps, 0)
    return pl.pallas_call(
        _in1_kernel,
        grid=(n_tok // tm,),
        in_specs=[pl.BlockSpec((tm, D_MODEL), row),
                  pl.BlockSpec((1, D_MODEL), const),
                  pl.BlockSpec((D_MODEL, MIX1), const),
                  pl.BlockSpec((D_MODEL, 2 * SWA_KV), const),
                  pl.BlockSpec((D_MODEL, MIX1), const),
                  pl.BlockSpec((1, SWA_QKV), const),
                  pl.BlockSpec((tm, LANES), pos),
                  pl.BlockSpec((tm, LANES), pos),
                  pl.BlockSpec((tm, LANES), pos)],
        out_specs=[pl.BlockSpec((tm, MIX1), row),
                   pl.BlockSpec((tm, SWA_KV), row),
                   pl.BlockSpec((tm, SWA_KV), row),
                   pl.BlockSpec((tm, MIX1), row)],
        out_shape=[jax.ShapeDtypeStruct((n_tok, MIX1), BF16),
                   jax.ShapeDtypeStruct((n_tok, SWA_KV), BF16),
                   jax.ShapeDtypeStruct((n_tok, SWA_KV), BF16),
                   jax.ShapeDtypeStruct((n_tok, MIX1), BF16)],
        compiler_params=_cparams(("parallel",)),
        name="l1_norm_proj_rope",
    )(h1, norm_w, w_q, w_kv, w_gate, b_in, cos, slo, shi)


def _swa_kernel(sink_ref, q_ref, kc_ref, kp_ref, vc_ref, vp_ref, o_ref, *, q_blocks):
    n = pl.program_id(1)
    W = WINDOW
    from_prev = _iota((W, 2 * W), 0) > (_iota((W, 2 * W), 1) % W)
    no_prev = jnp.where(n > 0, 0.0, -jnp.inf)
    col_row = _iota((1, 2 * W), 1)
    out_row = _iota((LANES, W), 0)
    kv_groups = SWA_KV // LANES
    groups = MIX1 // LANES // kv_groups
    tasks = [(j, pp, pp * groups + g) for j in range(q_blocks) for pp in range(kv_groups) for g in range(groups)]
    kk, vt = {}, {}
    for j in range(q_blocks):
        for pp in range(kv_groups):
            cols = slice(pp * LANES, (pp + 1) * LANES)
            if j == 0:
                kk[j, pp] = jnp.concatenate([kp_ref[:, cols], kc_ref[:W, cols]], axis=0)
                vv = jnp.concatenate([vp_ref[:, cols], vc_ref[:W, cols]], axis=0)
            else:
                kk[j, pp] = kc_ref[(j - 1) * W:(j + 1) * W, cols]
                vv = vc_ref[(j - 1) * W:(j + 1) * W, cols]
            vt[j, pp] = vv.astype(F32).T.astype(BF16)

    def scores(j, pp, blk):
        q = q_ref[j * W:(j + 1) * W, blk * LANES:(blk + 1) * LANES]
        return lax.dot_general(kk[j, pp], _head_stack(q), (((1,), (1,)), ((), ())), preferred_element_type=F32)

    ahead = 8
    pending = [scores(*t) for t in tasks[:ahead]]
    for i, (j, pp, blk) in enumerate(tasks):
        st = pending.pop(0)
        if i + ahead < len(tasks):
            pending.append(scores(*tasks[i + ahead]))
        s_prev = st[:W] + no_prev if j == 0 else st[:W]
        s = jnp.where(from_prev, s_prev, st[W:])
        sink = jnp.where(col_row < W, sink_ref[2 * blk], sink_ref[2 * blk + 1])
        m = jnp.maximum(jnp.max(s, axis=0, keepdims=True), sink)
        p = jnp.exp(s - m)
        denom = jnp.sum(p, axis=0, keepdims=True) + jnp.exp(sink - m)
        pb = p.astype(BF16)
        zero = jnp.zeros_like(pb)
        p2 = jnp.concatenate([jnp.where(from_prev, pb, zero), jnp.where(from_prev, zero, pb)], axis=0)
        ot = jnp.dot(vt[j, pp], p2, preferred_element_type=F32) * (1.0 / denom)
        ot = jnp.where(out_row < HEAD, ot[:, :W], ot[:, W:])
        o_ref[j * W:(j + 1) * W, blk * LANES:(blk + 1) * LANES] = ot.T.astype(o_ref.dtype)


def _swa_call(sinks, q, k, v, batch, seq_len, q_blocks):
    n_tok = q.shape[0]
    rows = q_blocks * WINDOW
    steps = seq_len // rows
    cur = lambda b, n: (b * steps + n, 0)
    prev = lambda b, n: (jnp.maximum((b * steps + n) * q_blocks - 1, 0), 0)
    return pl.pallas_call(
        functools.partial(_swa_kernel, q_blocks=q_blocks),
        grid=(batch, steps),
        in_specs=[pl.BlockSpec(memory_space=pltpu.SMEM),
                  pl.BlockSpec((rows, MIX1), cur),
                  pl.BlockSpec((rows, SWA_KV), cur),
                  pl.BlockSpec((WINDOW, SWA_KV), prev),
                  pl.BlockSpec((rows, SWA_KV), cur),
                  pl.BlockSpec((WINDOW, SWA_KV), prev)],
        out_specs=pl.BlockSpec((rows, MIX1), cur),
        out_shape=jax.ShapeDtypeStruct((n_tok, MIX1), BF16),
        compiler_params=_cparams(("parallel", "parallel")),
        name="l1_swa",
    )(sinks, q, k, k, v, v)


def _out1_kernel(o_ref, gate_ref, h_ref, w_ref, b_ref, nw_ref, y_ref):
    g = gate_ref[...].astype(F32)
    y = (o_ref[...].astype(F32) * (g * _sigmoid(g))).astype(BF16)
    h = h_ref[...] + jnp.dot(y, w_ref[...], preferred_element_type=F32) + b_ref[...]
    y_ref[...] = _rmsnorm_rows(h, nw_ref[...])


def _out1_call(o, gate, h1, w_out, b_out, norm_w, tm):
    n_tok = h1.shape[0]
    row = lambda i: (i, 0)
    const = lambda i: (0, 0)
    return pl.pallas_call(
        _out1_kernel,
        grid=(n_tok // tm,),
        in_specs=[pl.BlockSpec((tm, MIX1), row),
                  pl.BlockSpec((tm, MIX1), row),
                  pl.BlockSpec((tm, D_MODEL), row),
                  pl.BlockSpec((MIX1, D_MODEL), const),
                  pl.BlockSpec((1, D_MODEL), const),
                  pl.BlockSpec((1, D_MODEL), const)],
        out_specs=pl.BlockSpec((tm, D_MODEL), row),
        out_shape=jax.ShapeDtypeStruct((n_tok, D_MODEL), F32),
        compiler_params=_cparams(("parallel",)),
        name="l1_gate_out_norm",
    )(o, gate, h1, w_out, b_out, norm_w)


def _pad_rows(w, rows):
    return jnp.concatenate([w, jnp.zeros((rows - w.shape[0], w.shape[1]), w.dtype)], axis=0)


def _pair_heads(t, axis):
    shape = t.shape
    split = shape[:axis] + (SWA_KV_HEADS // 2, 2, SWA_GROUP, SWA_HEAD) + shape[axis + 1:]
    return jnp.swapaxes(t.reshape(split), axis + 1, axis + 2).reshape(shape)


def _rope_tables(seq_len):
    half = ROPE_DIMS // 2
    inv_freq = ROPE_THETA ** (-jnp.arange(half, dtype=F32) / half)
    ang = jnp.arange(seq_len).astype(F32)[:, None] * inv_freq
    trig = jnp.concatenate([jnp.cos(ang), jnp.sin(ang)], axis=1)
    d = jnp.arange(LANES) % SWA_HEAD
    src = jnp.arange(2 * half)[:, None]
    f = (d % half)[None, :]
    rot = (d < ROPE_DIMS)[None, :]
    lo = (d < half)[None, :]
    sel_cos = ((src == f) & rot).astype(F32)
    sel_lo = -((src == half + f) & lo).astype(F32)
    sel_hi = ((src == half + f) & rot & ~lo).astype(F32)
    sel = jnp.concatenate([sel_cos, sel_lo, sel_hi], axis=1)
    tab = jnp.dot(trig, sel, precision=lax.Precision.HIGHEST)
    cos = tab[:, :LANES] + (~rot).astype(F32)
    return cos, tab[:, LANES:2 * LANES], tab[:, 2 * LANES:]


def _forward(x, norm_w, w_in0, gla_gk_up, gla_gk_bias, gla_norm_w, rwkv_mu, rwkv_w0, rwkv_w_up,
             rwkv_a0, rwkv_a_up, rwkv_k_k, rwkv_k_a, rwkv_r_k, rwkv_ln_w, rwkv_ln_b, w_out0,
             w_in1, b_in1, attn_sinks, w_out1, b_out1, final_norm_w, *, tm, gla_chunks, rwkv_chunks, scan_chunks,
             swa_blocks):
    batch, seq_len, _ = x.shape
    x2 = x.reshape(batch * seq_len, D_MODEL)
    row = lambda t: t.reshape(1, -1)

    w0 = w_in0[0].astype(BF16)
    w_gla = w0[:, :GLA_QKV]
    w_glow = jnp.pad(w0[:, GLA_QKV:GLA_QKV + GLA_GATE_RANK], ((0, 0), (0, GLOW_PAD - GLA_GATE_RANK)))
    w_rest = w0[:, GLA_QKV + GLA_GATE_RANK:]
    gqkv, glow, rkv, lora, gate0 = _in0_call(x2, row(norm_w[0]), w_gla, w_glow, w_rest, row(rwkv_mu[0]),
                                             seq_len, tm)

    up_pad = _pad_rows(gla_gk_up[0], GLOW_PAD).astype(BF16)
    o_a = _gla_call(gqkv, glow, up_pad, row(gla_gk_bias[0]), row(gla_norm_w[0]), batch, seq_len, gla_chunks)

    zeros_r = jnp.zeros((RWKV_DECAY_RANK, RWKV_W), F32)
    wup_pad = jnp.concatenate([rwkv_w_up[0], zeros_r], axis=0).astype(BF16)
    aup_pad = jnp.concatenate([zeros_r, rwkv_a_up[0]], axis=0).astype(BF16)
    rp, op, bonus, m, n = _rwkv_chunk_call(
        rkv, lora, row(rwkv_w0[0]), wup_pad, row(rwkv_a0[0]), aup_pad,
        row(rwkv_k_k[0]), row(rwkv_k_a[0]), row(rwkv_r_k[0]), batch, seq_len, rwkv_chunks)
    o_b = _rwkv_scan_call(rp, op, bonus, m, n, row(rwkv_ln_w[0]), row(rwkv_ln_b[0]),
                          batch, seq_len, scan_chunks)

    h1 = _out0_call(o_a, o_b, gate0, x2, w_out0[0].astype(BF16), tm)

    w1 = w_in1[0].astype(BF16)
    w_q = _pair_heads(w1[:, :MIX1], 1)
    w_kv = w1[:, MIX1:SWA_QKV]
    w_gate = _pair_heads(w1[:, SWA_QKV:], 1)
    b1 = b_in1[0]
    b1p = row(jnp.concatenate([_pair_heads(b1[:MIX1], 0), b1[MIX1:]]))
    sinks_p = jnp.swapaxes(attn_sinks[0].reshape(SWA_KV_HEADS // 2, 2, SWA_GROUP), 1, 2).reshape(SWA_Q_HEADS)
    cos, slo, shi = _rope_tables(seq_len)
    q, k, v, gate1 = _in1_call(h1, row(norm_w[1]), w_q, w_kv, w_gate, b1p, cos, slo, shi, seq_len, tm)
    o1 = _swa_call(sinks_p, q, k, v, batch, seq_len, swa_blocks)
    w_out1p = _pair_heads(w_out1[0].astype(BF16), 0)
    y = _out1_call(o1, gate1, h1, w_out1p, row(b_out1[0]), row(final_norm_w), tm)
    return y.reshape(batch, seq_len, D_MODEL)


def kernel(x, norm_w, w_in0, gla_gk_up, gla_gk_bias, gla_norm_w, rwkv_mu, rwkv_w0, rwkv_w_up, rwkv_a0,
           rwkv_a_up, rwkv_k_k, rwkv_k_a, rwkv_r_k, rwkv_ln_w, rwkv_ln_b, w_out0, w_in1, b_in1,
           attn_sinks, w_out1, b_out1, final_norm_w):
    return _forward(x, norm_w, w_in0, gla_gk_up, gla_gk_bias, gla_norm_w, rwkv_mu, rwkv_w0, rwkv_w_up,
                    rwkv_a0, rwkv_a_up, rwkv_k_k, rwkv_k_a, rwkv_r_k, rwkv_ln_w, rwkv_ln_b, w_out0,
                    w_in1, b_in1, attn_sinks, w_out1, b_out1, final_norm_w,
                    tm=512, gla_chunks=16, rwkv_chunks=32, scan_chunks=4, swa_blocks=4)
```

```python
import functools

import jax
import jax.numpy as jnp
from jax import lax
from jax.experimental import pallas as pl
from jax.experimental.pallas import tpu as pltpu

F32 = jnp.float32
BF16 = jnp.bfloat16

D_MODEL = 1024
NORM_EPS = 1e-5

GLA_HEADS = 4
GLA_DK = 64
GLA_DV = 128
GLA_KEY = GLA_HEADS * GLA_DK
GLA_VAL = GLA_HEADS * GLA_DV
GLA_GATE_RANK = 16
GLA_GATE_NORMALIZER = 16.0
GLA_CHUNK = 64

RWKV_HEADS = 8
RWKV_HEAD = 64
RWKV_W = RWKV_HEADS * RWKV_HEAD
RWKV_DECAY_RANK = 64
RWKV_A_RANK = 64
RWKV_LN_EPS = 64e-5
RWKV_RKV = 3 * RWKV_W
RWKV_LORA = RWKV_DECAY_RANK + RWKV_A_RANK
RWKV_SHIFT = RWKV_RKV + RWKV_LORA
RWKV_CHUNK = 64
RWKV_GROUP = 16

MIX0 = GLA_VAL + RWKV_W
GLA_QKV = 2 * GLA_KEY + GLA_VAL

SWA_Q_HEADS = 16
SWA_KV_HEADS = 4
SWA_GROUP = SWA_Q_HEADS // SWA_KV_HEADS
SWA_HEAD = 64
WINDOW = 128
ROPE_DIMS = SWA_HEAD // 4
ROPE_THETA = 500000.0
MIX1 = SWA_Q_HEADS * SWA_HEAD
SWA_KV = SWA_KV_HEADS * SWA_HEAD
SWA_QKV = MIX1 + 2 * SWA_KV

LANES = 128
HEAD = 64
GLOW_PAD = LANES
VMEM_LIMIT = 56 * 1024 * 1024


def _cparams(sem):
    return pltpu.CompilerParams(dimension_semantics=sem, vmem_limit_bytes=VMEM_LIMIT)


def _dot(a, b):
    return jnp.dot(a.astype(BF16), b.astype(BF16), preferred_element_type=F32)


def _dot_nt(a, b):
    return lax.dot_general(a.astype(BF16), b.astype(BF16), (((1,), (1,)), ((), ())),
                           preferred_element_type=F32)


def _dot_tn(a, b):
    return lax.dot_general(a.astype(BF16), b.astype(BF16), (((0,), (0,)), ((), ())),
                           preferred_element_type=F32)


def _split2(x):
    hi = x.astype(BF16)
    lo = (x - hi.astype(F32)).astype(BF16)
    return hi, lo


def _dot_exact_rhs(a_bf16, x):
    hi, lo = _split2(x)
    return (jnp.dot(a_bf16, hi, preferred_element_type=F32)
            + jnp.dot(a_bf16, lo, preferred_element_type=F32))


def _dot_exact_lhs(x, b_bf16):
    hi, lo = _split2(x)
    return (jnp.dot(hi, b_bf16, preferred_element_type=F32)
            + jnp.dot(lo, b_bf16, preferred_element_type=F32))


def _iota(shape, dim):
    return lax.broadcasted_iota(jnp.int32, shape, dim)


def _tril_ones(n, dtype=BF16):
    return (_iota((n, n), 0) >= _iota((n, n), 1)).astype(dtype)


def _head_block_ones(n=LANES, dtype=BF16):
    return ((_iota((n, n), 0) // HEAD) == (_iota((n, n), 1) // HEAD)).astype(dtype)


def _head_stack(x):
    head = (_iota(x.shape, 1) % LANES) // HEAD
    return jnp.concatenate([jnp.where(head == 0, x, 0.0), jnp.where(head == 1, x, 0.0)], axis=0)


def _softplus(z):
    return jnp.maximum(z, 0.0) + jnp.log(1.0 + jnp.exp(-jnp.abs(z)))


def _sigmoid(z):
    return 1.0 / (1.0 + jnp.exp(-z))


def _rmsnorm_rows(x, w):
    return x * lax.rsqrt(jnp.mean(x * x, axis=-1, keepdims=True) + NORM_EPS) * w


def _in0_kernel(x_ref, nw_ref, wg_ref, wl_ref, wr_ref, mu_ref,
                gqkv_ref, glow_ref, rkv_ref, lora_ref, gate_ref, carry_ref, *, tiles_per_seq):
    i = pl.program_id(0)

    @pl.when(i == 0)
    def _():
        carry_ref[...] = jnp.zeros_like(carry_ref)

    xn = _rmsnorm_rows(x_ref[...], nw_ref[...]).astype(BF16)
    gqkv_ref[...] = jnp.dot(xn, wg_ref[...], preferred_element_type=F32).astype(gqkv_ref.dtype)
    glow_ref[...] = jnp.dot(xn, wl_ref[...], preferred_element_type=F32)
    rw = jnp.dot(xn, wr_ref[:, :RWKV_SHIFT], preferred_element_type=F32)
    gate_ref[...] = jnp.dot(xn, wr_ref[:, RWKV_SHIFT:], preferred_element_type=F32).astype(gate_ref.dtype)

    tm = rw.shape[0]
    first = (i % tiles_per_seq) == 0
    prev_last = jnp.where(first, 0.0, carry_ref[7:8, :])
    rolled = pltpu.roll(rw, 1, 0)
    prev = jnp.where(_iota(rw.shape, 0) == 0, prev_last, rolled)
    mixed = rw + (prev - rw) * mu_ref[...]
    rkv_ref[...] = mixed[:, :RWKV_RKV].astype(rkv_ref.dtype)
    lora_ref[...] = mixed[:, RWKV_RKV:]
    carry_ref[...] = rw[tm - 8:tm, :]


def _in0_call(x2, norm_w, w_gla, w_glow, w_rest, mu, seq_len, tm):
    n_tok = x2.shape[0]
    row = lambda i: (i, 0)
    const = lambda i: (0, 0)
    outs = [(GLA_QKV, BF16), (GLOW_PAD, F32), (RWKV_RKV, BF16), (RWKV_LORA, F32), (MIX0, BF16)]
    return pl.pallas_call(
        functools.partial(_in0_kernel, tiles_per_seq=seq_len // tm),
        grid=(n_tok // tm,),
        in_specs=[pl.BlockSpec((tm, D_MODEL), row),
                  pl.BlockSpec((1, D_MODEL), const),
                  pl.BlockSpec((D_MODEL, GLA_QKV), const),
                  pl.BlockSpec((D_MODEL, GLOW_PAD), const),
                  pl.BlockSpec((D_MODEL, RWKV_SHIFT + MIX0), const),
                  pl.BlockSpec((1, RWKV_SHIFT), const)],
        out_specs=[pl.BlockSpec((tm, n), row) for n, _ in outs],
        out_shape=[jax.ShapeDtypeStruct((n_tok, n), dt) for n, dt in outs],
        scratch_shapes=[pltpu.VMEM((8, RWKV_SHIFT), F32)],
        compiler_params=_cparams(("arbitrary",)),
        name="l0_norm_proj",
    )(x2, norm_w, w_gla, w_glow, w_rest, mu)


def _gla_kernel(q_ref, k_ref, glow_ref, v_ref, up_ref, bias_ref, nw_ref, o_ref, st_ref, *, chunks):
    c = pl.program_id(2)

    @pl.when(c == 0)
    def _():
        st_ref[...] = jnp.zeros_like(st_ref)

    C = GLA_CHUNK
    tril = _tril_ones(C)
    causal = _iota((C, LANES), 0) >= (_iota((C, LANES), 1) % HEAD)
    sr = _iota((2 * GLA_DV, LANES), 0)
    sl = _iota((2 * GLA_DV, LANES), 1)
    st_mask = (sr // GLA_DV) == (sl // HEAD)
    vl = _iota((C, 2 * GLA_DV), 1)
    scale = GLA_DK ** -0.5
    z = _dot(glow_ref[...], up_ref[...]) + bias_ref[...]
    g_all = -_softplus(-z) / GLA_GATE_NORMALIZER
    q_all = q_ref[...].astype(F32) * scale
    k_all = k_ref[...].astype(F32)
    rows = [slice(j * C, (j + 1) * C) for j in range(chunks)]
    bs = [_dot_exact_rhs(tril, g_all[rw]) for rw in rows]
    qe, ke, qb, kl, dec, vs = [], [], [], [], [], []
    for rw, b in zip(rows, bs):
        ref = b[C // 2:C // 2 + 1, :]
        b_last = b[C - 1:C, :]
        qe.append(q_all[rw] * jnp.exp(b - ref))
        ke.append(k_all[rw] * jnp.exp(ref - b))
        qb.append(q_all[rw] * jnp.exp(b))
        kl.append(k_all[rw] * jnp.exp(b_last - b))
        dec.append(jnp.exp(b_last))
        vs.append(v_ref[rw, :])
    att = [jnp.where(causal, _dot_nt(qe[j], _head_stack(ke[j])), 0.0) for j in range(chunks)]
    kv = [jnp.where(st_mask, _dot_tn(vs[j], kl[j]), 0.0) for j in range(chunks)]
    v_diag = [jnp.concatenate([jnp.where(vl < GLA_DV, vs[j], jnp.zeros_like(vs[j])),
                               jnp.where(vl >= GLA_DV, vs[j], jnp.zeros_like(vs[j]))], axis=0)
              for j in range(chunks)]
    intra = [jnp.dot(att[j].astype(BF16), v_diag[j], preferred_element_type=F32) for j in range(chunks)]
    states = [st_ref[...]]
    for j in range(chunks):
        states.append(states[j] * dec[j] + kv[j])
    st_ref[...] = states[chunks]
    for j in range(chunks):
        o = intra[j] + _dot_nt(qb[j], states[j])
        for h in range(2):
            oh = o[:, h * GLA_DV:(h + 1) * GLA_DV]
            oh = oh * lax.rsqrt(jnp.mean(oh * oh, axis=-1, keepdims=True) + NORM_EPS) * nw_ref[...]
            o_ref[rows[j], h * GLA_DV:(h + 1) * GLA_DV] = oh.astype(o_ref.dtype)


def _gla_call(gqkv, glow, up_pad, bias, norm_w, batch, seq_len, chunks):
    n_tok = gqkv.shape[0]
    tcb = chunks * GLA_CHUNK
    steps = seq_len // tcb
    pairs = GLA_KEY // LANES
    return pl.pallas_call(
        functools.partial(_gla_kernel, chunks=chunks),
        grid=(batch, pairs, steps),
        in_specs=[pl.BlockSpec((tcb, LANES), lambda b, p, c: (b * steps + c, p)),
                  pl.BlockSpec((tcb, LANES), lambda b, p, c: (b * steps + c, pairs + p)),
                  pl.BlockSpec((tcb, GLOW_PAD), lambda b, p, c: (b * steps + c, 0)),
                  pl.BlockSpec((tcb, 2 * GLA_DV), lambda b, p, c: (b * steps + c, pairs + p)),
                  pl.BlockSpec((GLOW_PAD, LANES), lambda b, p, c: (0, p)),
                  pl.BlockSpec((1, LANES), lambda b, p, c: (0, p)),
                  pl.BlockSpec((1, GLA_DV), lambda b, p, c: (0, 0))],
        out_specs=pl.BlockSpec((tcb, 2 * GLA_DV), lambda b, p, c: (b * steps + c, p)),
        out_shape=jax.ShapeDtypeStruct((n_tok, GLA_VAL), BF16),
        scratch_shapes=[pltpu.VMEM((2 * GLA_DV, LANES), F32)],
        compiler_params=_cparams(("parallel", "parallel", "arbitrary")),
        name="l0_gla",
    )(gqkv, gqkv, glow, gqkv, up_pad, bias, norm_w)


def _merge_masks(n):
    r = _iota((n, LANES), 0)
    c = _iota((n, LANES), 1) % HEAD
    masks = []
    s = 1
    while s < n:
        masks.append(((r // s) % 2 == 1) & ((c // s) == (r // s) - 1))
        s *= 2
    return (r == c).astype(F32), masks


def _run_interleaved(main, main_steps, side, side_steps):
    done = 0
    spread = max(1, (3 * main_steps) // 4)
    for i, _ in enumerate(main):
        target = -(-(i + 1) * side_steps // spread)
        while done < min(target, side_steps):
            next(side, None)
            done += 1
    for _ in side:
        pass


def _rwkv_chunk_kernel(r_ref, k_ref, v_ref, xwa_ref, w0_ref, wup_ref, a0_ref, aup_ref,
                       kk_ref, ka_ref, rk_ref,
                       rp_ref, op_ref, bonus_ref, m_ref, n_ref, *, chunks, group):
    C = RWKV_CHUNK
    tril = _tril_ones(C)
    rr = _iota((2 * C, LANES), 0)
    cc = _iota((2 * C, LANES), 1) % HEAD
    tri2 = ((rr < C) & (rr > cc)) | (rr - C >= cc)
    hb = _head_block_ones()
    sq_r = _iota((LANES, LANES), 0)
    sq_c = _iota((LANES, LANES), 1)
    same_head = (sq_r // HEAD) == (sq_c // HEAD)
    eye128 = sq_r == sq_c

    eye, merge = _merge_masks(C)
    zero = jnp.zeros((C, LANES), F32)
    n = range(group)

    def prepare(g, out):
        rows = slice(g * group * C, (g + 1) * group * C)
        r_all = r_ref[rows, :].astype(F32)
        k_all = k_ref[rows, :].astype(F32)
        v_all = v_ref[rows, :].astype(F32)
        xwa = xwa_ref[rows, :]
        w = -_softplus(-(w0_ref[...] + _dot(jnp.tanh(xwa), wup_ref[...]))) - 0.5
        lw_all = -jnp.exp(w)
        a_sig = _sigmoid(a0_ref[...] + _dot(xwa, aup_ref[...]))
        kk = k_all * kk_ref[...]
        kk = kk / jnp.maximum(jnp.sqrt(_dot_exact_lhs(kk * kk, hb)), 1e-12)
        k_all = k_all * (1.0 + (a_sig - 1.0) * ka_ref[...])
        bonus_ref[rows, :] = _dot_exact_lhs(r_all * k_all * rk_ref[...], hb) * v_all
        a_all = -kk
        b_all = kk * a_sig
        yield
        for j in n:
            rw = slice(j * C, (j + 1) * C)
            cum = _dot_exact_rhs(tril, lw_all[rw])
            cum_last = cum[C - 1:C, :]
            e_neg = jnp.exp(-cum)
            e_end = jnp.exp(cum_last - cum)
            out.append(dict(
                rt=r_all[rw] * jnp.exp(cum),
                at=a_all[rw] * jnp.exp(cum - lw_all[rw]),
                bt=b_all[rw] * e_neg,
                kt=k_all[rw] * e_neg,
                ends=jnp.concatenate([b_all[rw] * e_end, k_all[rw] * e_end], axis=0),
                v=v_all[rw],
                dec=jnp.exp(cum_last)))
            yield

    def solve(g, ops):
        lhs = [jnp.concatenate([o["at"], o["rt"]], axis=0) for o in ops]
        left = [jnp.where(tri2, _dot_nt(lhs[j], _head_stack(ops[j]["bt"])), 0.0) for j in n]
        yield
        right = [jnp.where(tri2, _dot_nt(lhs[j], _head_stack(ops[j]["kt"])), 0.0) for j in n]
        yield
        lows = [lf[:C] for lf in left]
        ts = [eye + jnp.where(merge[0], low, 0.0) for low in lows]
        for sub in merge[1:]:
            ys = [_dot(jnp.where(sub, low, 0.0), _head_stack(t)) for low, t in zip(lows, ts)]
            yield
            ts = [t + _dot(t, _head_stack(y)) for t, y in zip(ts, ys)]
            yield
        kv = [_dot(right[j], _head_stack(ops[j]["v"])) for j in n]
        yield
        wz = [_dot(ts[j], _head_stack(jnp.concatenate([ops[j]["at"], kv[j][:C]], axis=1))) for j in n]
        yield
        ro = [_dot(left[j][C:], _head_stack(wz[j])) for j in n]
        yield
        mn = [_dot_tn(ops[j]["ends"],
                      jnp.concatenate([wz[j], jnp.concatenate([zero, ops[j]["v"]], axis=1)], axis=0))
              for j in n]
        for j in n:
            c = g * group + j
            rows = slice(c * C, (c + 1) * C)
            rp_ref[rows, :] = (ops[j]["rt"] + ro[j][:, :LANES]).astype(rp_ref.dtype)
            op_ref[rows, :] = ro[j][:, LANES:] + kv[j][C:]
            m_ref[0, 0, c] = (jnp.where(eye128, ops[j]["dec"], 0.0)
                              + jnp.where(same_head, mn[j][:, :LANES], 0.0)).astype(m_ref.dtype)
            n_ref[0, 0, c] = jnp.where(same_head, mn[j][:, LANES:], 0.0)
        yield

    solve_stages = 2 + 2 * (len(merge) - 1) + 4
    groups = chunks // group
    ops = [[] for _ in range(groups + 1)]
    for _ in prepare(0, ops[0]):
        pass
    for g in range(groups):
        side = prepare(g + 1, ops[g + 1]) if g + 1 < groups else iter(())
        _run_interleaved(solve(g, ops[g]), solve_stages, side, group + 1)


def _rwkv_chunk_call(rkv, lora, w0, wup_pad, a0, aup_pad, k_k, k_a, r_k, batch, seq_len, chunks):
    n_tok = rkv.shape[0]
    tcb = chunks * RWKV_CHUNK
    steps = seq_len // tcb
    pairs = RWKV_W // LANES
    nc = seq_len // RWKV_CHUNK
    col = lambda off: (lambda b, p, c: (b * steps + c, off + p))
    par = lambda b, p, c: (0, p)
    tok = lambda b, p, c: (b * steps + c, p)
    mat = lambda b, p, c: (b, p, c, 0, 0)
    return pl.pallas_call(
        functools.partial(_rwkv_chunk_kernel, chunks=chunks, group=RWKV_GROUP),
        grid=(batch, pairs, steps),
        in_specs=[pl.BlockSpec((tcb, LANES), col(0)),
                  pl.BlockSpec((tcb, LANES), col(pairs)),
                  pl.BlockSpec((tcb, LANES), col(2 * pairs)),
                  pl.BlockSpec((tcb, RWKV_LORA), lambda b, p, c: (b * steps + c, 0)),
                  pl.BlockSpec((1, LANES), par),
                  pl.BlockSpec((RWKV_LORA, LANES), par),
                  pl.BlockSpec((1, LANES), par),
                  pl.BlockSpec((RWKV_LORA, LANES), par),
                  pl.BlockSpec((1, LANES), par),
                  pl.BlockSpec((1, LANES), par),
                  pl.BlockSpec((1, LANES), par)],
        out_specs=[pl.BlockSpec((tcb, LANES), tok),
                   pl.BlockSpec((tcb, LANES), tok),
                   pl.BlockSpec((tcb, LANES), tok),
                   pl.BlockSpec((1, 1, chunks, LANES, LANES), mat),
                   pl.BlockSpec((1, 1, chunks, LANES, LANES), mat)],
        out_shape=[jax.ShapeDtypeStruct((n_tok, RWKV_W), BF16),
                   jax.ShapeDtypeStruct((n_tok, RWKV_W), F32),
                   jax.ShapeDtypeStruct((n_tok, RWKV_W), F32),
                   jax.ShapeDtypeStruct((batch, pairs, nc, LANES, LANES), BF16),
                   jax.ShapeDtypeStruct((batch, pairs, nc, LANES, LANES), F32)],
        compiler_params=_cparams(("parallel", "parallel", "parallel")),
        name="l0_rwkv_chunks",
    )(rkv, rkv, rkv, lora, w0, wup_pad, a0, aup_pad, k_k, k_a, r_k)


def _rwkv_scan_kernel(rp_ref, op_ref, bonus_ref, m_ref, n_ref, lnw_ref, lnb_ref, o_ref, st_ref, *, chunks):
    c = pl.program_id(0)

    @pl.when(c == 0)
    def _():
        st_ref[...] = jnp.zeros_like(st_ref)

    C = RWKV_CHUNK
    batch = rp_ref.shape[0]
    pairs = RWKV_W // LANES
    hb = _head_block_ones()
    seqs = [(b, p) for b in range(batch) for p in range(pairs)]
    states = {bp: [st_ref[bp[0], bp[1]]] for bp in seqs}
    for j in range(chunks):
        for b, p in seqs:
            states[b, p].append(_dot(m_ref[b, p, j], states[b, p][j]) + n_ref[b, p, j])
    for b, p in seqs:
        st_ref[b, p] = states[b, p][chunks]
    cols = {bp: slice(bp[1] * LANES, (bp[1] + 1) * LANES) for bp in seqs}
    os = [jnp.concatenate([_dot(rp_ref[b, j * C:(j + 1) * C, cols[b, p]], states[b, p][j])
                           for j in range(chunks)], axis=0) + op_ref[b, :, cols[b, p]] for b, p in seqs]
    means = [_dot_exact_lhs(o, hb) * (1.0 / RWKV_HEAD) for o in os]
    ds = [o - mean for o, mean in zip(os, means)]
    variances = [_dot_exact_lhs(d * d, hb) * (1.0 / RWKV_HEAD) for d in ds]
    for (b, p), d, var in zip(seqs, ds, variances):
        c_ = cols[b, p]
        o_ref[b, :, c_] = (d * lax.rsqrt(var + RWKV_LN_EPS) * lnw_ref[:, c_] + lnb_ref[:, c_]
                           + bonus_ref[b, :, c_]).astype(o_ref.dtype)


def _rwkv_scan_call(rp, op, bonus, m, n, ln_w, ln_b, batch, seq_len, chunks):
    tcb = chunks * RWKV_CHUNK
    pairs = RWKV_W // LANES
    seq3 = lambda t: t.reshape(batch, seq_len, RWKV_W)
    tok = lambda c: (0, c, 0)
    const = lambda c: (0, 0)
    mat = lambda c: (0, 0, c, 0, 0)
    out = pl.pallas_call(
        functools.partial(_rwkv_scan_kernel, chunks=chunks),
        grid=(seq_len // tcb,),
        in_specs=[pl.BlockSpec((batch, tcb, RWKV_W), tok),
                  pl.BlockSpec((batch, tcb, RWKV_W), tok),
                  pl.BlockSpec((batch, tcb, RWKV_W), tok),
                  pl.BlockSpec((batch, pairs, chunks, LANES, LANES), mat),
                  pl.BlockSpec((batch, pairs, chunks, LANES, LANES), mat),
                  pl.BlockSpec((1, RWKV_W), const),
                  pl.BlockSpec((1, RWKV_W), const)],
        out_specs=pl.BlockSpec((batch, tcb, RWKV_W), tok),
        out_shape=jax.ShapeDtypeStruct((batch, seq_len, RWKV_W), BF16),
        scratch_shapes=[pltpu.VMEM((batch, pairs, LANES, LANES), F32)],
        compiler_params=_cparams(("arbitrary",)),
        name="l0_rwkv_scan",
    )(seq3(rp), seq3(op), seq3(bonus), m, n, ln_w, ln_b)
    return out.reshape(batch * seq_len, RWKV_W)


def _out0_kernel(oa_ref, ob_ref, gate_ref, x_ref, w_ref, h_ref):
    g = gate_ref[...].astype(F32)
    g = g * _sigmoid(g)
    ya = (oa_ref[...].astype(F32) * g[:, :GLA_VAL]).astype(BF16)
    yb = (ob_ref[...].astype(F32) * g[:, GLA_VAL:]).astype(BF16)
    h_ref[...] = (x_ref[...]
                  + jnp.dot(ya, w_ref[:GLA_VAL, :], preferred_element_type=F32)
                  + jnp.dot(yb, w_ref[GLA_VAL:, :], preferred_element_type=F32))


def _out0_call(oa, ob, gate, x2, w_out, tm):
    n_tok = x2.shape[0]
    row = lambda i: (i, 0)
    const = lambda i: (0, 0)
    return pl.pallas_call(
        _out0_kernel,
        grid=(n_tok // tm,),
        in_specs=[pl.BlockSpec((tm, GLA_VAL), row),
                  pl.BlockSpec((tm, RWKV_W), row),
                  pl.BlockSpec((tm, MIX0), row),
                  pl.BlockSpec((tm, D_MODEL), row),
                  pl.BlockSpec((MIX0, D_MODEL), const)],
        out_specs=pl.BlockSpec((tm, D_MODEL), row),
        out_shape=jax.ShapeDtypeStruct((n_tok, D_MODEL), F32),
        compiler_params=_cparams(("parallel",)),
        name="l0_gate_out",
    )(oa, ob, gate, x2, w_out)


def _rope_group(x, cos, sin_lo, sin_hi):
    half = ROPE_DIMS // 2
    return x * cos + pltpu.roll(x, LANES - half, 1) * sin_lo + pltpu.roll(x, half, 1) * sin_hi


def _in1_kernel(h_ref, nw_ref, wq_ref, wkv_ref, wg_ref, b_ref, cos_ref, slo_ref, shi_ref,
                q_ref, k_ref, v_ref, gate_ref):
    hn = _rmsnorm_rows(h_ref[...], nw_ref[...]).astype(BF16)
    cos = cos_ref[...]
    slo = slo_ref[...]
    shi = shi_ref[...]
    scale = SWA_HEAD ** -0.5
    q = jnp.dot(hn, wq_ref[...], preferred_element_type=F32) + b_ref[:, :MIX1]
    kv = jnp.dot(hn, wkv_ref[...], preferred_element_type=F32) + b_ref[:, MIX1:]
    gate_ref[...] = jnp.dot(hn, wg_ref[...], preferred_element_type=F32).astype(gate_ref.dtype)
    for g in range(MIX1 // LANES):
        cols = slice(g * LANES, (g + 1) * LANES)
        q_ref[:, cols] = (_rope_group(q[:, cols], cos, slo, shi) * scale).astype(q_ref.dtype)
    for g in range(SWA_KV // LANES):
        cols = slice(g * LANES, (g + 1) * LANES)
        k_ref[:, cols] = _rope_group(kv[:, cols], cos, slo, shi).astype(k_ref.dtype)
    v_ref[...] = kv[:, SWA_KV:].astype(v_ref.dtype)


def _in1_call(h1, norm_w, w_q, w_kv, w_gate, b_in, cos, slo, shi, seq_len, tm):
    n_tok = h1.shape[0]
    tps = seq_len // tm
    row = lambda i: (i, 0)
    const = lambda i: (0, 0)
    pos = lambda i: (i % tps, 0)
    return pl.pallas_call(
        _in1_kernel,
        grid=(n_tok // tm,),
        in_specs=[pl.BlockSpec((tm, D_MODEL), row),
                  pl.BlockSpec((1, D_MODEL), const),
                  pl.BlockSpec((D_MODEL, MIX1), const),
                  pl.BlockSpec((D_MODEL, 2 * SWA_KV), const),
                  pl.BlockSpec((D_MODEL, MIX1), const),
                  pl.BlockSpec((1, SWA_QKV), const),
                  pl.BlockSpec((tm, LANES), pos),
                  pl.BlockSpec((tm, LANES), pos),
                  pl.BlockSpec((tm, LANES), pos)],
        out_specs=[pl.BlockSpec((tm, MIX1), row),
                   pl.BlockSpec((tm, SWA_KV), row),
                   pl.BlockSpec((tm, SWA_KV), row),
                   pl.BlockSpec((tm, MIX1), row)],
        out_shape=[jax.ShapeDtypeStruct((n_tok, MIX1), BF16),
                   jax.ShapeDtypeStruct((n_tok, SWA_KV), BF16),
                   jax.ShapeDtypeStruct((n_tok, SWA_KV), BF16),
                   jax.ShapeDtypeStruct((n_tok, MIX1), BF16)],
        compiler_params=_cparams(("parallel",)),
        name="l1_norm_proj_rope",
    )(h1, norm_w, w_q, w_kv, w_gate, b_in, cos, slo, shi)


def _swa_kernel(sink_ref, q_ref, kc_ref, kp_ref, vc_ref, vp_ref, o_ref, *, q_blocks):
    n = pl.program_id(1)
    W = WINDOW
    from_prev = _iota((W, 2 * W), 0) > (_iota((W, 2 * W), 1) % W)
    no_prev = jnp.where(n > 0, 0.0, -jnp.inf)
    col_row = _iota((1, 2 * W), 1)
    out_row = _iota((LANES, W), 0)
    kv_groups = SWA_KV // LANES
    groups = MIX1 // LANES // kv_groups
    tasks = [(j, pp, pp * groups + g) for j in range(q_blocks) for pp in range(kv_groups) for g in range(groups)]
    kk, vt = {}, {}
    for j in range(q_blocks):
        for pp in range(kv_groups):
            cols = slice(pp * LANES, (pp + 1) * LANES)
            if j == 0:
                kk[j, pp] = jnp.concatenate([kp_ref[:, cols], kc_ref[:W, cols]], axis=0)
                vv = jnp.concatenate([vp_ref[:, cols], vc_ref[:W, cols]], axis=0)
            else:
                kk[j, pp] = kc_ref[(j - 1) * W:(j + 1) * W, cols]
                vv = vc_ref[(j - 1) * W:(j + 1) * W, cols]
            vt[j, pp] = vv.astype(F32).T.astype(BF16)

    def scores(j, pp, blk):
        q = q_ref[j * W:(j + 1) * W, blk * LANES:(blk + 1) * LANES]
        return lax.dot_general(kk[j, pp], _head_stack(q), (((1,), (1,)), ((), ())), preferred_element_type=F32)

    ahead = 8
    pending = [scores(*t) for t in tasks[:ahead]]
    for i, (j, pp, blk) in enumerate(tasks):
        st = pending.pop(0)
        if i + ahead < len(tasks):
            pending.append(scores(*tasks[i + ahead]))
        s_prev = st[:W] + no_prev if j == 0 else st[:W]
        s = jnp.where(from_prev, s_prev, st[W:])
        sink = jnp.where(col_row < W, sink_ref[2 * blk], sink_ref[2 * blk + 1])
        m = jnp.maximum(jnp.max(s, axis=0, keepdims=True), sink)
        p = jnp.exp(s - m)
        denom = jnp.sum(p, axis=0, keepdims=True) + jnp.exp(sink - m)
        pb = p.astype(BF16)
        zero = jnp.zeros_like(pb)
        p2 = jnp.concatenate([jnp.where(from_prev, pb, zero), jnp.where(from_prev, zero, pb)], axis=0)
        ot = jnp.dot(vt[j, pp], p2, preferred_element_type=F32) * (1.0 / denom)
        ot = jnp.where(out_row < HEAD, ot[:, :W], ot[:, W:])
        o_ref[j * W:(j + 1) * W, blk * LANES:(blk + 1) * LANES] = ot.T.astype(o_ref.dtype)


def _swa_call(sinks, q, k, v, batch, seq_len, q_blocks):
    n_tok = q.shape[0]
    rows = q_blocks * WINDOW
    steps = seq_len // rows
    cur = lambda b, n: (b * steps + n, 0)
    prev = lambda b, n: (jnp.maximum((b * steps + n) * q_blocks - 1, 0), 0)
    return pl.pallas_call(
        functools.partial(_swa_kernel, q_blocks=q_blocks),
        grid=(batch, steps),
        in_specs=[pl.BlockSpec(memory_space=pltpu.SMEM),
                  pl.BlockSpec((rows, MIX1), cur),
                  pl.BlockSpec((rows, SWA_KV), cur),
                  pl.BlockSpec((WINDOW, SWA_KV), prev),
                  pl.BlockSpec((rows, SWA_KV), cur),
                  pl.BlockSpec((WINDOW, SWA_KV), prev)],
        out_specs=pl.BlockSpec((rows, MIX1), cur),
        out_shape=jax.ShapeDtypeStruct((n_tok, MIX1), BF16),
        compiler_params=_cparams(("parallel", "parallel")),
        name="l1_swa",
    )(sinks, q, k, k, v, v)


def _out1_kernel(o_ref, gate_ref, h_ref, w_ref, b_ref, nw_ref, y_ref):
    g = gate_ref[...].astype(F32)
    y = (o_ref[...].astype(F32) * (g * _sigmoid(g))).astype(BF16)
    h = h_ref[...] + jnp.dot(y, w_ref[...], preferred_element_type=F32) + b_ref[...]
    y_ref[...] = _rmsnorm_rows(h, nw_ref[...])


def _out1_call(o, gate, h1, w_out, b_out, norm_w, tm):
    n_tok = h1.shape[0]
    row = lambda i: (i, 0)
    const = lambda i: (0, 0)
    return pl.pallas_call(
        _out1_kernel,
        grid=(n_tok // tm,),
        in_specs=[pl.BlockSpec((tm, MIX1), row),
                  pl.BlockSpec((tm, MIX1), row),
                  pl.BlockSpec((tm, D_MODEL), row),
                  pl.BlockSpec((MIX1, D_MODEL), const),
                  pl.BlockSpec((1, D_MODEL), const),
                  pl.BlockSpec((1, D_MODEL), const)],
        out_specs=pl.BlockSpec((tm, D_MODEL), row),
        out_shape=jax.ShapeDtypeStruct((n_tok, D_MODEL), F32),
        compiler_params=_cparams(("parallel",)),
        name="l1_gate_out_norm",
    )(o, gate, h1, w_out, b_out, norm_w)


def _pad_rows(w, rows):
    return jnp.concatenate([w, jnp.zeros((rows - w.shape[0], w.shape[1]), w.dtype)], axis=0)


def _pair_heads(t, axis):
    shape = t.shape
    split = shape[:axis] + (SWA_KV_HEADS // 2, 2, SWA_GROUP, SWA_HEAD) + shape[axis + 1:]
    return jnp.swapaxes(t.reshape(split), axis + 1, axis + 2).reshape(shape)


def _rope_tables(seq_len):
    half = ROPE_DIMS // 2
    inv_freq = ROPE_THETA ** (-jnp.arange(half, dtype=F32) / half)
    ang = jnp.arange(seq_len).astype(F32)[:, None] * inv_freq
    trig = jnp.concatenate([jnp.cos(ang), jnp.sin(ang)], axis=1)
    d = jnp.arange(LANES) % SWA_HEAD
    src = jnp.arange(2 * half)[:, None]
    f = (d % half)[None, :]
    rot = (d < ROPE_DIMS)[None, :]
    lo = (d < half)[None, :]
    sel_cos = ((src == f) & rot).astype(F32)
    sel_lo = -((src == half + f) & lo).astype(F32)
    sel_hi = ((src == half + f) & rot & ~lo).astype(F32)
    sel = jnp.concatenate([sel_cos, sel_lo, sel_hi], axis=1)
    tab = jnp.dot(trig, sel, precision=lax.Precision.HIGHEST)
    cos = tab[:, :LANES] + (~rot).astype(F32)
    return cos, tab[:, LANES:2 * LANES], tab[:, 2 * LANES:]


def _forward(x, norm_w, w_in0, gla_gk_up, gla_gk_bias, gla_norm_w, rwkv_mu, rwkv_w0, rwkv_w_up,
             rwkv_a0, rwkv_a_up, rwkv_k_k, rwkv_k_a, rwkv_r_k, rwkv_ln_w, rwkv_ln_b, w_out0,
             w_in1, b_in1, attn_sinks, w_out1, b_out1, final_norm_w, *, tm, gla_chunks, rwkv_chunks, scan_chunks,
             swa_blocks):
    batch, seq_len, _ = x.shape
    x2 = x.reshape(batch * seq_len, D_MODEL)
    row = lambda t: t.reshape(1, -1)

    w0 = w_in0[0].astype(BF16)
    w_gla = w0[:, :GLA_QKV]
    w_glow = jnp.pad(w0[:, GLA_QKV:GLA_QKV + GLA_GATE_RANK], ((0, 0), (0, GLOW_PAD - GLA_GATE_RANK)))
    w_rest = w0[:, GLA_QKV + GLA_GATE_RANK:]
    gqkv, glow, rkv, lora, gate0 = _in0_call(x2, row(norm_w[0]), w_gla, w_glow, w_rest, row(rwkv_mu[0]),
                                             seq_len, tm)

    up_pad = _pad_rows(gla_gk_up[0], GLOW_PAD).astype(BF16)
    o_a = _gla_call(gqkv, glow, up_pad, row(gla_gk_bias[0]), row(gla_norm_w[0]), batch, seq_len, gla_chunks)

    zeros_r = jnp.zeros((RWKV_DECAY_RANK, RWKV_W), F32)
    wup_pad = jnp.concatenate([rwkv_w_up[0], zeros_r], axis=0).astype(BF16)
    aup_pad = jnp.concatenate([zeros_r, rwkv_a_up[0]], axis=0).astype(BF16)
    rp, op, bonus, m, n = _rwkv_chunk_call(
        rkv, lora, row(rwkv_w0[0]), wup_pad, row(rwkv_a0[0]), aup_pad,
        row(rwkv_k_k[0]), row(rwkv_k_a[0]), row(rwkv_r_k[0]), batch, seq_len, rwkv_chunks)
    o_b = _rwkv_scan_call(rp, op, bonus, m, n, row(rwkv_ln_w[0]), row(rwkv_ln_b[0]),
                          batch, seq_len, scan_chunks)

    h1 = _out0_call(o_a, o_b, gate0, x2, w_out0[0].astype(BF16), tm)

    w1 = w_in1[0].astype(BF16)
    w_q = _pair_heads(w1[:, :MIX1], 1)
    w_kv = w1[:, MIX1:SWA_QKV]
    w_gate = _pair_heads(w1[:, SWA_QKV:], 1)
    b1 = b_in1[0]
    b1p = row(jnp.concatenate([_pair_heads(b1[:MIX1], 0), b1[MIX1:]]))
    sinks_p = jnp.swapaxes(attn_sinks[0].reshape(SWA_KV_HEADS // 2, 2, SWA_GROUP), 1, 2).reshape(SWA_Q_HEADS)
    cos, slo, shi = _rope_tables(seq_len)
    q, k, v, gate1 = _in1_call(h1, row(norm_w[1]), w_q, w_kv, w_gate, b1p, cos, slo, shi, seq_len, tm)
    o1 = _swa_call(sinks_p, q, k, v, batch, seq_len, swa_blocks)
    w_out1p = _pair_heads(w_out1[0].astype(BF16), 0)
    y = _out1_call(o1, gate1, h1, w_out1p, row(b_out1[0]), row(final_norm_w), tm)
    return y.reshape(batch, seq_len, D_MODEL)


def kernel(x, norm_w, w_in0, gla_gk_up, gla_gk_bias, gla_norm_w, rwkv_mu, rwkv_w0, rwkv_w_up, rwkv_a0,
           rwkv_a_up, rwkv_k_k, rwkv_k_a, rwkv_r_k, rwkv_ln_w, rwkv_ln_b, w_out0, w_in1, b_in1,
           attn_sinks, w_out1, b_out1, final_norm_w):
    return _forward(x, norm_w, w_in0, gla_gk_up, gla_gk_bias, gla_norm_w, rwkv_mu, rwkv_w0, rwkv_w_up,
                    rwkv_a0, rwkv_a_up, rwkv_k_k, rwkv_k_a, rwkv_r_k, rwkv_ln_w, rwkv_ln_b, w_out0,
                    w_in1, b_in1, attn_sinks, w_out1, b_out1, final_norm_w,
                    tm=512, gla_chunks=16, rwkv_chunks=64, scan_chunks=4, swa_blocks=4)
```

```python
import functools

import jax
import jax.numpy as jnp
from jax import lax
from jax.experimental import pallas as pl
from jax.experimental.pallas import tpu as pltpu

F32 = jnp.float32
BF16 = jnp.bfloat16

D_MODEL = 1024
NORM_EPS = 1e-5

GLA_HEADS = 4
GLA_DK = 64
GLA_DV = 128
GLA_KEY = GLA_HEADS * GLA_DK
GLA_VAL = GLA_HEADS * GLA_DV
GLA_GATE_RANK = 16
GLA_GATE_NORMALIZER = 16.0
GLA_CHUNK = 64

RWKV_HEADS = 8
RWKV_HEAD = 64
RWKV_W = RWKV_HEADS * RWKV_HEAD
RWKV_DECAY_RANK = 64
RWKV_A_RANK = 64
RWKV_LN_EPS = 64e-5
RWKV_RKV = 3 * RWKV_W
RWKV_LORA = RWKV_DECAY_RANK + RWKV_A_RANK
RWKV_SHIFT = RWKV_RKV + RWKV_LORA
RWKV_CHUNK = 64
RWKV_GROUP = 16

MIX0 = GLA_VAL + RWKV_W
GLA_QKV = 2 * GLA_KEY + GLA_VAL

SWA_Q_HEADS = 16
SWA_KV_HEADS = 4
SWA_GROUP = SWA_Q_HEADS // SWA_KV_HEADS
SWA_HEAD = 64
WINDOW = 128
ROPE_DIMS = SWA_HEAD // 4
ROPE_THETA = 500000.0
MIX1 = SWA_Q_HEADS * SWA_HEAD
SWA_KV = SWA_KV_HEADS * SWA_HEAD
SWA_QKV = MIX1 + 2 * SWA_KV

LANES = 128
HEAD = 64
GLOW_PAD = LANES
VMEM_LIMIT = 56 * 1024 * 1024


def _cparams(sem):
    return pltpu.CompilerParams(dimension_semantics=sem, vmem_limit_bytes=VMEM_LIMIT)


def _dot(a, b):
    return jnp.dot(a.astype(BF16), b.astype(BF16), preferred_element_type=F32)


def _dot_nt(a, b):
    return lax.dot_general(a.astype(BF16), b.astype(BF16), (((1,), (1,)), ((), ())),
                           preferred_element_type=F32)


def _dot_tn(a, b):
    return lax.dot_general(a.astype(BF16), b.astype(BF16), (((0,), (0,)), ((), ())),
                           preferred_element_type=F32)


def _split2(x):
    hi = x.astype(BF16)
    lo = (x - hi.astype(F32)).astype(BF16)
    return hi, lo


def _dot_exact_rhs(a_bf16, x):
    hi, lo = _split2(x)
    return (jnp.dot(a_bf16, hi, preferred_element_type=F32)
            + jnp.dot(a_bf16, lo, preferred_element_type=F32))


def _dot_exact_lhs(x, b_bf16):
    hi, lo = _split2(x)
    return (jnp.dot(hi, b_bf16, preferred_element_type=F32)
            + jnp.dot(lo, b_bf16, preferred_element_type=F32))


def _iota(shape, dim):
    return lax.broadcasted_iota(jnp.int32, shape, dim)


def _tril_ones(n, dtype=BF16):
    return (_iota((n, n), 0) >= _iota((n, n), 1)).astype(dtype)


def _head_block_ones(n=LANES, dtype=BF16):
    return ((_iota((n, n), 0) // HEAD) == (_iota((n, n), 1) // HEAD)).astype(dtype)


def _head_stack(x):
    head = (_iota(x.shape, 1) % LANES) // HEAD
    return jnp.concatenate([jnp.where(head == 0, x, 0.0), jnp.where(head == 1, x, 0.0)], axis=0)


def _softplus(z):
    return jnp.maximum(z, 0.0) + jnp.log(1.0 + jnp.exp(-jnp.abs(z)))


def _sigmoid(z):
    return 1.0 / (1.0 + jnp.exp(-z))


def _rmsnorm_rows(x, w):
    return x * lax.rsqrt(jnp.mean(x * x, axis=-1, keepdims=True) + NORM_EPS) * w


def _in0_kernel(x_ref, nw_ref, wg_ref, wl_ref, wr_ref, mu_ref,
                gqkv_ref, glow_ref, rkv_ref, lora_ref, gate_ref, carry_ref, *, tiles_per_seq):
    i = pl.program_id(0)

    @pl.when(i == 0)
    def _():
        carry_ref[...] = jnp.zeros_like(carry_ref)

    xn = _rmsnorm_rows(x_ref[...], nw_ref[...]).astype(BF16)
    gqkv_ref[...] = jnp.dot(xn, wg_ref[...], preferred_element_type=F32).astype(gqkv_ref.dtype)
    glow_ref[...] = jnp.dot(xn, wl_ref[...], preferred_element_type=F32)
    rw = jnp.dot(xn, wr_ref[:, :RWKV_SHIFT], preferred_element_type=F32)
    gate_ref[...] = jnp.dot(xn, wr_ref[:, RWKV_SHIFT:], preferred_element_type=F32).astype(gate_ref.dtype)

    tm = rw.shape[0]
    first = (i % tiles_per_seq) == 0
    prev_last = jnp.where(first, 0.0, carry_ref[7:8, :])
    rolled = pltpu.roll(rw, 1, 0)
    prev = jnp.where(_iota(rw.shape, 0) == 0, prev_last, rolled)
    mixed = rw + (prev - rw) * mu_ref[...]
    rkv_ref[...] = mixed[:, :RWKV_RKV].astype(rkv_ref.dtype)
    lora_ref[...] = mixed[:, RWKV_RKV:]
    carry_ref[...] = rw[tm - 8:tm, :]


def _in0_call(x2, norm_w, w_gla, w_glow, w_rest, mu, seq_len, tm):
    n_tok = x2.shape[0]
    row = lambda i: (i, 0)
    const = lambda i: (0, 0)
    outs = [(GLA_QKV, BF16), (GLOW_PAD, F32), (RWKV_RKV, BF16), (RWKV_LORA, F32), (MIX0, BF16)]
    return pl.pallas_call(
        functools.partial(_in0_kernel, tiles_per_seq=seq_len // tm),
        grid=(n_tok // tm,),
        in_specs=[pl.BlockSpec((tm, D_MODEL), row),
                  pl.BlockSpec((1, D_MODEL), const),
                  pl.BlockSpec((D_MODEL, GLA_QKV), const),
                  pl.BlockSpec((D_MODEL, GLOW_PAD), const),
                  pl.BlockSpec((D_MODEL, RWKV_SHIFT + MIX0), const),
                  pl.BlockSpec((1, RWKV_SHIFT), const)],
        out_specs=[pl.BlockSpec((tm, n), row) for n, _ in outs],
        out_shape=[jax.ShapeDtypeStruct((n_tok, n), dt) for n, dt in outs],
        scratch_shapes=[pltpu.VMEM((8, RWKV_SHIFT), F32)],
        compiler_params=_cparams(("arbitrary",)),
        name="l0_norm_proj",
    )(x2, norm_w, w_gla, w_glow, w_rest, mu)


def _gla_kernel(q_ref, k_ref, glow_ref, v_ref, up_ref, bias_ref, nw_ref, o_ref, st_ref, *, chunks):
    c = pl.program_id(2)

    @pl.when(c == 0)
    def _():
        st_ref[...] = jnp.zeros_like(st_ref)

    C = GLA_CHUNK
    tril = _tril_ones(C)
    causal = _iota((C, LANES), 0) >= (_iota((C, LANES), 1) % HEAD)
    sr = _iota((2 * GLA_DV, LANES), 0)
    sl = _iota((2 * GLA_DV, LANES), 1)
    st_mask = (sr // GLA_DV) == (sl // HEAD)
    vl = _iota((C, 2 * GLA_DV), 1)
    scale = GLA_DK ** -0.5
    z = _dot(glow_ref[...], up_ref[...]) + bias_ref[...]
    g_all = -_softplus(-z) / GLA_GATE_NORMALIZER
    q_all = q_ref[...].astype(F32) * scale
    k_all = k_ref[...].astype(F32)
    rows = [slice(j * C, (j + 1) * C) for j in range(chunks)]
    bs = [_dot_exact_rhs(tril, g_all[rw]) for rw in rows]
    qe, ke, qb, kl, dec, vs = [], [], [], [], [], []
    for rw, b in zip(rows, bs):
        ref = b[C // 2:C // 2 + 1, :]
        b_last = b[C - 1:C, :]
        qe.append(q_all[rw] * jnp.exp(b - ref))
        ke.append(k_all[rw] * jnp.exp(ref - b))
        qb.append(q_all[rw] * jnp.exp(b))
        kl.append(k_all[rw] * jnp.exp(b_last - b))
        dec.append(jnp.exp(b_last))
        vs.append(v_ref[rw, :])
    att = [jnp.where(causal, _dot_nt(qe[j], _head_stack(ke[j])), 0.0) for j in range(chunks)]
    kv = [jnp.where(st_mask, _dot_tn(vs[j], kl[j]), 0.0) for j in range(chunks)]
    v_diag = [jnp.concatenate([jnp.where(vl < GLA_DV, vs[j], jnp.zeros_like(vs[j])),
                               jnp.where(vl >= GLA_DV, vs[j], jnp.zeros_like(vs[j]))], axis=0)
              for j in range(chunks)]
    intra = [jnp.dot(att[j].astype(BF16), v_diag[j], preferred_element_type=F32) for j in range(chunks)]
    states = [st_ref[...]]
    for j in range(chunks):
        states.append(states[j] * dec[j] + kv[j])
    st_ref[...] = states[chunks]
    for j in range(chunks):
        o = intra[j] + _dot_nt(qb[j], states[j])
        for h in range(2):
            oh = o[:, h * GLA_DV:(h + 1) * GLA_DV]
            oh = oh * lax.rsqrt(jnp.mean(oh * oh, axis=-1, keepdims=True) + NORM_EPS) * nw_ref[...]
            o_ref[rows[j], h * GLA_DV:(h + 1) * GLA_DV] = oh.astype(o_ref.dtype)


def _gla_call(gqkv, glow, up_pad, bias, norm_w, batch, seq_len, chunks):
    n_tok = gqkv.shape[0]
    tcb = chunks * GLA_CHUNK
    steps = seq_len // tcb
    pairs = GLA_KEY // LANES
    return pl.pallas_call(
        functools.partial(_gla_kernel, chunks=chunks),
        grid=(batch, pairs, steps),
        in_specs=[pl.BlockSpec((tcb, LANES), lambda b, p, c: (b * steps + c, p)),
                  pl.BlockSpec((tcb, LANES), lambda b, p, c: (b * steps + c, pairs + p)),
                  pl.BlockSpec((tcb, GLOW_PAD), lambda b, p, c: (b * steps + c, 0)),
                  pl.BlockSpec((tcb, 2 * GLA_DV), lambda b, p, c: (b * steps + c, pairs + p)),
                  pl.BlockSpec((GLOW_PAD, LANES), lambda b, p, c: (0, p)),
                  pl.BlockSpec((1, LANES), lambda b, p, c: (0, p)),
                  pl.BlockSpec((1, GLA_DV), lambda b, p, c: (0, 0))],
        out_specs=pl.BlockSpec((tcb, 2 * GLA_DV), lambda b, p, c: (b * steps + c, p)),
        out_shape=jax.ShapeDtypeStruct((n_tok, GLA_VAL), BF16),
        scratch_shapes=[pltpu.VMEM((2 * GLA_DV, LANES), F32)],
        compiler_params=_cparams(("parallel", "parallel", "arbitrary")),
        name="l0_gla",
    )(gqkv, gqkv, glow, gqkv, up_pad, bias, norm_w)


def _merge_masks(n):
    r = _iota((n, LANES), 0)
    c = _iota((n, LANES), 1) % HEAD
    masks = []
    s = 1
    while s < n:
        masks.append(((r // s) % 2 == 1) & ((c // s) == (r // s) - 1))
        s *= 2
    return (r == c).astype(F32), masks


def _run_interleaved(main, main_steps, side, side_steps):
    done = 0
    spread = max(1, (3 * main_steps) // 4)
    for i, _ in enumerate(main):
        target = -(-(i + 1) * side_steps // spread)
        while done < min(target, side_steps):
            next(side, None)
            done += 1
    for _ in side:
        pass


def _rwkv_chunk_kernel(r_ref, k_ref, v_ref, xwa_ref, w0_ref, wup_ref, a0_ref, aup_ref,
                       kk_ref, ka_ref, rk_ref,
                       rp_ref, op_ref, bonus_ref, m_ref, n_ref, *, chunks, group):
    C = RWKV_CHUNK
    tril = _tril_ones(C)
    rr = _iota((2 * C, LANES), 0)
    cc = _iota((2 * C, LANES), 1) % HEAD
    tri2 = ((rr < C) & (rr > cc)) | (rr - C >= cc)
    hb = _head_block_ones()
    sq_r = _iota((LANES, LANES), 0)
    sq_c = _iota((LANES, LANES), 1)
    same_head = (sq_r // HEAD) == (sq_c // HEAD)
    eye128 = sq_r == sq_c

    eye, merge = _merge_masks(C)
    zero = jnp.zeros((C, LANES), F32)
    n = range(group)

    def prepare(g, out):
        rows = slice(g * group * C, (g + 1) * group * C)
        r_all = r_ref[rows, :].astype(F32)
        k_all = k_ref[rows, :].astype(F32)
        v_all = v_ref[rows, :].astype(F32)
        xwa = xwa_ref[rows, :]
        w = -_softplus(-(w0_ref[...] + _dot(jnp.tanh(xwa), wup_ref[...]))) - 0.5
        lw_all = -jnp.exp(w)
        a_sig = _sigmoid(a0_ref[...] + _dot(xwa, aup_ref[...]))
        kk = k_all * kk_ref[...]
        kk = kk / jnp.maximum(jnp.sqrt(_dot_exact_lhs(kk * kk, hb)), 1e-12)
        k_all = k_all * (1.0 + (a_sig - 1.0) * ka_ref[...])
        bonus_ref[rows, :] = _dot_exact_lhs(r_all * k_all * rk_ref[...], hb) * v_all
        a_all = -kk
        b_all = kk * a_sig
        yield
        for j in n:
            rw = slice(j * C, (j + 1) * C)
            cum = _dot_exact_rhs(tril, lw_all[rw])
            cum_last = cum[C - 1:C, :]
            e_neg = jnp.exp(-cum)
            e_end = jnp.exp(cum_last - cum)
            out.append(dict(
                rt=r_all[rw] * jnp.exp(cum),
                at=a_all[rw] * jnp.exp(cum - lw_all[rw]),
                bt=b_all[rw] * e_neg,
                kt=k_all[rw] * e_neg,
                ends=jnp.concatenate([b_all[rw] * e_end, k_all[rw] * e_end], axis=0),
                v=v_all[rw],
                dec=jnp.exp(cum_last)))
            yield

    def solve(g, ops):
        lhs = [jnp.concatenate([o["at"], o["rt"]], axis=0) for o in ops]
        left = [jnp.where(tri2, _dot_nt(lhs[j], _head_stack(ops[j]["bt"])), 0.0) for j in n]
        yield
        right = [jnp.where(tri2, _dot_nt(lhs[j], _head_stack(ops[j]["kt"])), 0.0) for j in n]
        yield
        lows = [lf[:C] for lf in left]
        ts = [eye + jnp.where(merge[0], low, 0.0) for low in lows]
        for sub in merge[1:]:
            ys = [_dot(jnp.where(sub, low, 0.0), _head_stack(t)) for low, t in zip(lows, ts)]
            yield
            ts = [t + _dot(t, _head_stack(y)) for t, y in zip(ts, ys)]
            yield
        kv = [_dot(right[j], _head_stack(ops[j]["v"])) for j in n]
        yield
        wz = [_dot(ts[j], _head_stack(jnp.concatenate([ops[j]["at"], kv[j][:C]], axis=1))) for j in n]
        yield
        ro = [_dot(left[j][C:], _head_stack(wz[j])) for j in n]
        yield
        mn = [_dot_tn(ops[j]["ends"],
                      jnp.concatenate([wz[j], jnp.concatenate([zero, ops[j]["v"]], axis=1)], axis=0))
              for j in n]
        for j in n:
            c = g * group + j
            rows = slice(c * C, (c + 1) * C)
            rp_ref[rows, :] = (ops[j]["rt"] + ro[j][:, :LANES]).astype(rp_ref.dtype)
            op_ref[rows, :] = ro[j][:, LANES:] + kv[j][C:]
            m_ref[0, 0, c] = (jnp.where(eye128, ops[j]["dec"], 0.0)
                              + jnp.where(same_head, mn[j][:, :LANES], 0.0)).astype(m_ref.dtype)
            n_ref[0, 0, c] = jnp.where(same_head, mn[j][:, LANES:], 0.0)
        yield

    solve_stages = 2 + 2 * (len(merge) - 1) + 4
    groups = chunks // group
    ops = [[] for _ in range(groups + 1)]
    for _ in prepare(0, ops[0]):
        pass
    for g in range(groups):
        side = prepare(g + 1, ops[g + 1]) if g + 1 < groups else iter(())
        _run_interleaved(solve(g, ops[g]), solve_stages, side, group + 1)


def _rwkv_chunk_call(rkv, lora, w0, wup_pad, a0, aup_pad, k_k, k_a, r_k, batch, seq_len, chunks):
    n_tok = rkv.shape[0]
    tcb = chunks * RWKV_CHUNK
    steps = seq_len // tcb
    pairs = RWKV_W // LANES
    nc = seq_len // RWKV_CHUNK
    col = lambda off: (lambda b, p, c: (b * steps + c, off + p))
    par = lambda b, p, c: (0, p)
    tok = lambda b, p, c: (b * steps + c, p)
    mat = lambda b, p, c: (b, p, c, 0, 0)
    return pl.pallas_call(
        functools.partial(_rwkv_chunk_kernel, chunks=chunks, group=RWKV_GROUP),
        grid=(batch, pairs, steps),
        in_specs=[pl.BlockSpec((tcb, LANES), col(0)),
                  pl.BlockSpec((tcb, LANES), col(pairs)),
                  pl.BlockSpec((tcb, LANES), col(2 * pairs)),
                  pl.BlockSpec((tcb, RWKV_LORA), lambda b, p, c: (b * steps + c, 0)),
                  pl.BlockSpec((1, LANES), par),
                  pl.BlockSpec((RWKV_LORA, LANES), par),
                  pl.BlockSpec((1, LANES), par),
                  pl.BlockSpec((RWKV_LORA, LANES), par),
                  pl.BlockSpec((1, LANES), par),
                  pl.BlockSpec((1, LANES), par),
                  pl.BlockSpec((1, LANES), par)],
        out_specs=[pl.BlockSpec((tcb, LANES), tok),
                   pl.BlockSpec((tcb, LANES), tok),
                   pl.BlockSpec((tcb, LANES), tok),
                   pl.BlockSpec((1, 1, chunks, LANES, LANES), mat),
                   pl.BlockSpec((1, 1, chunks, LANES, LANES), mat)],
        out_shape=[jax.ShapeDtypeStruct((n_tok, RWKV_W), BF16),
                   jax.ShapeDtypeStruct((n_tok, RWKV_W), F32),
                   jax.ShapeDtypeStruct((n_tok, RWKV_W), F32),
                   jax.ShapeDtypeStruct((batch, pairs, nc, LANES, LANES), BF16),
                   jax.ShapeDtypeStruct((batch, pairs, nc, LANES, LANES), F32)],
        compiler_params=_cparams(("parallel", "parallel", "parallel")),
        name="l0_rwkv_chunks",
    )(rkv, rkv, rkv, lora, w0, wup_pad, a0, aup_pad, k_k, k_a, r_k)


def _rwkv_scan_kernel(rp_ref, op_ref, bonus_ref, m_ref, n_ref, lnw_ref, lnb_ref, o_ref, st_ref, *, chunks):
    c = pl.program_id(0)

    @pl.when(c == 0)
    def _():
        st_ref[...] = jnp.zeros_like(st_ref)

    C = RWKV_CHUNK
    batch = rp_ref.shape[0]
    pairs = RWKV_W // LANES
    hb = _head_block_ones()
    seqs = [(b, p) for b in range(batch) for p in range(pairs)]
    states = {bp: [st_ref[bp[0], bp[1]]] for bp in seqs}
    for j in range(chunks):
        for b, p in seqs:
            states[b, p].append(_dot(m_ref[b, p, j], states[b, p][j]) + n_ref[b, p, j])
    for b, p in seqs:
        st_ref[b, p] = states[b, p][chunks]
    cols = {bp: slice(bp[1] * LANES, (bp[1] + 1) * LANES) for bp in seqs}
    os = [jnp.concatenate([_dot(rp_ref[b, j * C:(j + 1) * C, cols[b, p]], states[b, p][j])
                           for j in range(chunks)], axis=0) + op_ref[b, :, cols[b, p]] for b, p in seqs]
    means = [_dot_exact_lhs(o, hb) * (1.0 / RWKV_HEAD) for o in os]
    ds = [o - mean for o, mean in zip(os, means)]
    variances = [_dot_exact_lhs(d * d, hb) * (1.0 / RWKV_HEAD) for d in ds]
    for (b, p), d, var in zip(seqs, ds, variances):
        c_ = cols[b, p]
        o_ref[b, :, c_] = (d * lax.rsqrt(var + RWKV_LN_EPS) * lnw_ref[:, c_] + lnb_ref[:, c_]
                           + bonus_ref[b, :, c_]).astype(o_ref.dtype)


def _rwkv_scan_call(rp, op, bonus, m, n, ln_w, ln_b, batch, seq_len, chunks):
    tcb = chunks * RWKV_CHUNK
    pairs = RWKV_W // LANES
    seq3 = lambda t: t.reshape(batch, seq_len, RWKV_W)
    tok = lambda c: (0, c, 0)
    const = lambda c: (0, 0)
    mat = lambda c: (0, 0, c, 0, 0)
    out = pl.pallas_call(
        functools.partial(_rwkv_scan_kernel, chunks=chunks),
        grid=(seq_len // tcb,),
        in_specs=[pl.BlockSpec((batch, tcb, RWKV_W), tok),
                  pl.BlockSpec((batch, tcb, RWKV_W), tok),
                  pl.BlockSpec((batch, tcb, RWKV_W), tok),
                  pl.BlockSpec((batch, pairs, chunks, LANES, LANES), mat),
                  pl.BlockSpec((batch, pairs, chunks, LANES, LANES), mat),
                  pl.BlockSpec((1, RWKV_W), const),
                  pl.BlockSpec((1, RWKV_W), const)],
        out_specs=pl.BlockSpec((batch, tcb, RWKV_W), tok),
        out_shape=jax.ShapeDtypeStruct((batch, seq_len, RWKV_W), BF16),
        scratch_shapes=[pltpu.VMEM((batch, pairs, LANES, LANES), F32)],
        compiler_params=_cparams(("arbitrary",)),
        name="l0_rwkv_scan",
    )(seq3(rp), seq3(op), seq3(bonus), m, n, ln_w, ln_b)
    return out.reshape(batch * seq_len, RWKV_W)


def _gated_out0(oa_ref, ob_ref, gate_ref, x_ref, w_ref):
    g = gate_ref[...].astype(F32)
    g = g * _sigmoid(g)
    ya = (oa_ref[...].astype(F32) * g[:, :GLA_VAL]).astype(BF16)
    yb = (ob_ref[...].astype(F32) * g[:, GLA_VAL:]).astype(BF16)
    return (x_ref[...]
            + jnp.dot(ya, w_ref[:GLA_VAL, :], preferred_element_type=F32)
            + jnp.dot(yb, w_ref[GLA_VAL:, :], preferred_element_type=F32))


def _rope_group(x, cos, sin_lo, sin_hi):
    half = ROPE_DIMS // 2
    return x * cos + pltpu.roll(x, LANES - half, 1) * sin_lo + pltpu.roll(x, half, 1) * sin_hi


def _mid_kernel(oa_ref, ob_ref, gate0_ref, x_ref, wo_ref,
                nw_ref, wq_ref, wkv_ref, wg_ref, b_ref, cos_ref, slo_ref, shi_ref,
                h_ref, q_ref, k_ref, v_ref, gate_ref):
    h = _gated_out0(oa_ref, ob_ref, gate0_ref, x_ref, wo_ref)
    h_ref[...] = h
    hn = _rmsnorm_rows(h, nw_ref[...]).astype(BF16)
    cos = cos_ref[...]
    slo = slo_ref[...]
    shi = shi_ref[...]
    scale = SWA_HEAD ** -0.5
    q = jnp.dot(hn, wq_ref[...], preferred_element_type=F32) + b_ref[:, :MIX1]
    kv = jnp.dot(hn, wkv_ref[...], preferred_element_type=F32) + b_ref[:, MIX1:]
    gate_ref[...] = jnp.dot(hn, wg_ref[...], preferred_element_type=F32).astype(gate_ref.dtype)
    for g in range(MIX1 // LANES):
        cols = slice(g * LANES, (g + 1) * LANES)
        q_ref[:, cols] = (_rope_group(q[:, cols], cos, slo, shi) * scale).astype(q_ref.dtype)
    for g in range(SWA_KV // LANES):
        cols = slice(g * LANES, (g + 1) * LANES)
        k_ref[:, cols] = _rope_group(kv[:, cols], cos, slo, shi).astype(k_ref.dtype)
    v_ref[...] = kv[:, SWA_KV:].astype(v_ref.dtype)


def _mid_call(oa, ob, gate0, x2, w_out0, norm_w, w_q, w_kv, w_gate, b_in, cos, slo, shi, seq_len, tm):
    n_tok = x2.shape[0]
    tps = seq_len // tm
    row = lambda i: (i, 0)
    const = lambda i: (0, 0)
    pos = lambda i: (i % tps, 0)
    return pl.pallas_call(
        _mid_kernel,
        grid=(n_tok // tm,),
        in_specs=[pl.BlockSpec((tm, GLA_VAL), row),
                  pl.BlockSpec((tm, RWKV_W), row),
                  pl.BlockSpec((tm, MIX0), row),
                  pl.BlockSpec((tm, D_MODEL), row),
                  pl.BlockSpec((MIX0, D_MODEL), const),
                  pl.BlockSpec((1, D_MODEL), const),
                  pl.BlockSpec((D_MODEL, MIX1), const),
                  pl.BlockSpec((D_MODEL, 2 * SWA_KV), const),
                  pl.BlockSpec((D_MODEL, MIX1), const),
                  pl.BlockSpec((1, SWA_QKV), const),
                  pl.BlockSpec((tm, LANES), pos),
                  pl.BlockSpec((tm, LANES), pos),
                  pl.BlockSpec((tm, LANES), pos)],
        out_specs=[pl.BlockSpec((tm, D_MODEL), row),
                   pl.BlockSpec((tm, MIX1), row),
                   pl.BlockSpec((tm, SWA_KV), row),
                   pl.BlockSpec((tm, SWA_KV), row),
                   pl.BlockSpec((tm, MIX1), row)],
        out_shape=[jax.ShapeDtypeStruct((n_tok, D_MODEL), F32),
                   jax.ShapeDtypeStruct((n_tok, MIX1), BF16),
                   jax.ShapeDtypeStruct((n_tok, SWA_KV), BF16),
                   jax.ShapeDtypeStruct((n_tok, SWA_KV), BF16),
                   jax.ShapeDtypeStruct((n_tok, MIX1), BF16)],
        compiler_params=_cparams(("parallel",)),
        name="l0_out_l1_proj",
    )(oa, ob, gate0, x2, w_out0, norm_w, w_q, w_kv, w_gate, b_in, cos, slo, shi)


def _swa_kernel(sink_ref, q_ref, kc_ref, kp_ref, vc_ref, vp_ref, gate_ref, h_ref, w_ref, b_ref, nw_ref,
                y_ref, o_ref, *, q_blocks):
    n = pl.program_id(1)
    W = WINDOW
    from_prev = _iota((W, 2 * W), 0) > (_iota((W, 2 * W), 1) % W)
    no_prev = jnp.where(n > 0, 0.0, -jnp.inf)
    col_row = _iota((1, 2 * W), 1)
    out_row = _iota((LANES, W), 0)
    kv_groups = SWA_KV // LANES
    groups = MIX1 // LANES // kv_groups
    tasks = [(j, pp, pp * groups + g) for j in range(q_blocks) for pp in range(kv_groups) for g in range(groups)]
    kk, vt = {}, {}
    for j in range(q_blocks):
        for pp in range(kv_groups):
            cols = slice(pp * LANES, (pp + 1) * LANES)
            if j == 0:
                kk[j, pp] = jnp.concatenate([kp_ref[:, cols], kc_ref[:W, cols]], axis=0)
                vv = jnp.concatenate([vp_ref[:, cols], vc_ref[:W, cols]], axis=0)
            else:
                kk[j, pp] = kc_ref[(j - 1) * W:(j + 1) * W, cols]
                vv = vc_ref[(j - 1) * W:(j + 1) * W, cols]
            vt[j, pp] = vv.astype(F32).T.astype(BF16)

    def scores(j, pp, blk):
        q = q_ref[j * W:(j + 1) * W, blk * LANES:(blk + 1) * LANES]
        return lax.dot_general(kk[j, pp], _head_stack(q), (((1,), (1,)), ((), ())), preferred_element_type=F32)

    ahead = 8
    pending = [scores(*t) for t in tasks[:ahead]]
    for i, (j, pp, blk) in enumerate(tasks):
        st = pending.pop(0)
        if i + ahead < len(tasks):
            pending.append(scores(*tasks[i + ahead]))
        s_prev = st[:W] + no_prev if j == 0 else st[:W]
        s = jnp.where(from_prev, s_prev, st[W:])
        sink = jnp.where(col_row < W, sink_ref[2 * blk], sink_ref[2 * blk + 1])
        m = jnp.maximum(jnp.max(s, axis=0, keepdims=True), sink)
        p = jnp.exp(s - m)
        denom = jnp.sum(p, axis=0, keepdims=True) + jnp.exp(sink - m)
        pb = p.astype(BF16)
        zero = jnp.zeros_like(pb)
        p2 = jnp.concatenate([jnp.where(from_prev, pb, zero), jnp.where(from_prev, zero, pb)], axis=0)
        ot = jnp.dot(vt[j, pp], p2, preferred_element_type=F32) * (1.0 / denom)
        ot = jnp.where(out_row < HEAD, ot[:, :W], ot[:, W:])
        o_ref[j * W:(j + 1) * W, blk * LANES:(blk + 1) * LANES] = ot.T.astype(o_ref.dtype)

    g = gate_ref[...].astype(F32)
    y = (o_ref[...].astype(F32) * (g * _sigmoid(g))).astype(BF16)
    h = h_ref[...] + jnp.dot(y, w_ref[...], preferred_element_type=F32) + b_ref[...]
    y_ref[...] = _rmsnorm_rows(h, nw_ref[...])


def _swa_call(sinks, q, k, v, gate, h1, w_out, b_out, norm_w, batch, seq_len, q_blocks):
    n_tok = q.shape[0]
    rows = q_blocks * WINDOW
    steps = seq_len // rows
    cur = lambda b, n: (b * steps + n, 0)
    prev = lambda b, n: (jnp.maximum((b * steps + n) * q_blocks - 1, 0), 0)
    const = lambda b, n: (0, 0)
    return pl.pallas_call(
        functools.partial(_swa_kernel, q_blocks=q_blocks),
        grid=(batch, steps),
        in_specs=[pl.BlockSpec(memory_space=pltpu.SMEM),
                  pl.BlockSpec((rows, MIX1), cur),
                  pl.BlockSpec((rows, SWA_KV), cur),
                  pl.BlockSpec((WINDOW, SWA_KV), prev),
                  pl.BlockSpec((rows, SWA_KV), cur),
                  pl.BlockSpec((WINDOW, SWA_KV), prev),
                  pl.BlockSpec((rows, MIX1), cur),
                  pl.BlockSpec((rows, D_MODEL), cur),
                  pl.BlockSpec((MIX1, D_MODEL), const),
                  pl.BlockSpec((1, D_MODEL), const),
                  pl.BlockSpec((1, D_MODEL), const)],
        out_specs=pl.BlockSpec((rows, D_MODEL), cur),
        out_shape=jax.ShapeDtypeStruct((n_tok, D_MODEL), F32),
        scratch_shapes=[pltpu.VMEM((rows, MIX1), BF16)],
        compiler_params=_cparams(("parallel", "parallel")),
        name="l1_swa_out",
    )(sinks, q, k, k, v, v, gate, h1, w_out, b_out, norm_w)


def _pad_rows(w, rows):
    return jnp.concatenate([w, jnp.zeros((rows - w.shape[0], w.shape[1]), w.dtype)], axis=0)


def _pair_heads(t, axis):
    shape = t.shape
    split = shape[:axis] + (SWA_KV_HEADS // 2, 2, SWA_GROUP, SWA_HEAD) + shape[axis + 1:]
    return jnp.swapaxes(t.reshape(split), axis + 1, axis + 2).reshape(shape)


def _rope_tables(seq_len):
    half = ROPE_DIMS // 2
    inv_freq = ROPE_THETA ** (-jnp.arange(half, dtype=F32) / half)
    ang = jnp.arange(seq_len).astype(F32)[:, None] * inv_freq
    trig = jnp.concatenate([jnp.cos(ang), jnp.sin(ang)], axis=1)
    d = jnp.arange(LANES) % SWA_HEAD
    src = jnp.arange(2 * half)[:, None]
    f = (d % half)[None, :]
    rot = (d < ROPE_DIMS)[None, :]
    lo = (d < half)[None, :]
    sel_cos = ((src == f) & rot).astype(F32)
    sel_lo = -((src == half + f) & lo).astype(F32)
    sel_hi = ((src == half + f) & rot & ~lo).astype(F32)
    sel = jnp.concatenate([sel_cos, sel_lo, sel_hi], axis=1)
    tab = jnp.dot(trig, sel, precision=lax.Precision.HIGHEST)
    cos = tab[:, :LANES] + (~rot).astype(F32)
    return cos, tab[:, LANES:2 * LANES], tab[:, 2 * LANES:]


def _forward(x, norm_w, w_in0, gla_gk_up, gla_gk_bias, gla_norm_w, rwkv_mu, rwkv_w0, rwkv_w_up,
             rwkv_a0, rwkv_a_up, rwkv_k_k, rwkv_k_a, rwkv_r_k, rwkv_ln_w, rwkv_ln_b, w_out0,
             w_in1, b_in1, attn_sinks, w_out1, b_out1, final_norm_w, *, tm, gla_chunks, rwkv_chunks, scan_chunks,
             swa_blocks):
    batch, seq_len, _ = x.shape
    x2 = x.reshape(batch * seq_len, D_MODEL)
    row = lambda t: t.reshape(1, -1)

    w0 = w_in0[0].astype(BF16)
    w_gla = w0[:, :GLA_QKV]
    w_glow = jnp.pad(w0[:, GLA_QKV:GLA_QKV + GLA_GATE_RANK], ((0, 0), (0, GLOW_PAD - GLA_GATE_RANK)))
    w_rest = w0[:, GLA_QKV + GLA_GATE_RANK:]
    gqkv, glow, rkv, lora, gate0 = _in0_call(x2, row(norm_w[0]), w_gla, w_glow, w_rest, row(rwkv_mu[0]),
                                             seq_len, tm)

    up_pad = _pad_rows(gla_gk_up[0], GLOW_PAD).astype(BF16)
    o_a = _gla_call(gqkv, glow, up_pad, row(gla_gk_bias[0]), row(gla_norm_w[0]), batch, seq_len, gla_chunks)

    zeros_r = jnp.zeros((RWKV_DECAY_RANK, RWKV_W), F32)
    wup_pad = jnp.concatenate([rwkv_w_up[0], zeros_r], axis=0).astype(BF16)
    aup_pad = jnp.concatenate([zeros_r, rwkv_a_up[0]], axis=0).astype(BF16)
    rp, op, bonus, m, n = _rwkv_chunk_call(
        rkv, lora, row(rwkv_w0[0]), wup_pad, row(rwkv_a0[0]), aup_pad,
        row(rwkv_k_k[0]), row(rwkv_k_a[0]), row(rwkv_r_k[0]), batch, seq_len, rwkv_chunks)
    o_b = _rwkv_scan_call(rp, op, bonus, m, n, row(rwkv_ln_w[0]), row(rwkv_ln_b[0]),
                          batch, seq_len, scan_chunks)


    w1 = w_in1[0].astype(BF16)
    w_q = _pair_heads(w1[:, :MIX1], 1)
    w_kv = w1[:, MIX1:SWA_QKV]
    w_gate = _pair_heads(w1[:, SWA_QKV:], 1)
    b1 = b_in1[0]
    b1p = row(jnp.concatenate([_pair_heads(b1[:MIX1], 0), b1[MIX1:]]))
    sinks_p = jnp.swapaxes(attn_sinks[0].reshape(SWA_KV_HEADS // 2, 2, SWA_GROUP), 1, 2).reshape(SWA_Q_HEADS)
    cos, slo, shi = _rope_tables(seq_len)
    h1, q, k, v, gate1 = _mid_call(o_a, o_b, gate0, x2, w_out0[0].astype(BF16), row(norm_w[1]), w_q, w_kv, w_gate,
                                   b1p, cos, slo, shi, seq_len, tm)
    w_out1p = _pair_heads(w_out1[0].astype(BF16), 0)
    y = _swa_call(sinks_p, q, k, v, gate1, h1, w_out1p, row(b_out1[0]), row(final_norm_w),
                  batch, seq_len, swa_blocks)
    return y.reshape(batch, seq_len, D_MODEL)


def kernel(x, norm_w, w_in0, gla_gk_up, gla_gk_bias, gla_norm_w, rwkv_mu, rwkv_w0, rwkv_w_up, rwkv_a0,
           rwkv_a_up, rwkv_k_k, rwkv_k_a, rwkv_r_k, rwkv_ln_w, rwkv_ln_b, w_out0, w_in1, b_in1,
           attn_sinks, w_out1, b_out1, final_norm_w):
    return _forward(x, norm_w, w_in0, gla_gk_up, gla_gk_bias, gla_norm_w, rwkv_mu, rwkv_w0, rwkv_w_up,
                    rwkv_a0, rwkv_a_up, rwkv_k_k, rwkv_k_a, rwkv_r_k, rwkv_ln_w, rwkv_ln_b, w_out0,
                    w_in1, b_in1, attn_sinks, w_out1, b_out1, final_norm_w,
                    tm=512, gla_chunks=16, rwkv_chunks=64, scan_chunks=4, swa_blocks=4)
```

```python
import functools

import jax
import jax.numpy as jnp
from jax import lax
from jax.experimental import pallas as pl
from jax.experimental.pallas import tpu as pltpu

F32 = jnp.float32
BF16 = jnp.bfloat16

D_MODEL = 1024
NORM_EPS = 1e-5

GLA_HEADS = 4
GLA_DK = 64
GLA_DV = 128
GLA_KEY = GLA_HEADS * GLA_DK
GLA_VAL = GLA_HEADS * GLA_DV
GLA_GATE_RANK = 16
GLA_GATE_NORMALIZER = 16.0
GLA_CHUNK = 64

RWKV_HEADS = 8
RWKV_HEAD = 64
RWKV_W = RWKV_HEADS * RWKV_HEAD
RWKV_DECAY_RANK = 64
RWKV_A_RANK = 64
RWKV_LN_EPS = 64e-5
RWKV_RKV = 3 * RWKV_W
RWKV_LORA = RWKV_DECAY_RANK + RWKV_A_RANK
RWKV_SHIFT = RWKV_RKV + RWKV_LORA
RWKV_CHUNK = 64
RWKV_GROUP = 16

MIX0 = GLA_VAL + RWKV_W
GLA_QKV = 2 * GLA_KEY + GLA_VAL

SWA_Q_HEADS = 16
SWA_KV_HEADS = 4
SWA_GROUP = SWA_Q_HEADS // SWA_KV_HEADS
SWA_HEAD = 64
WINDOW = 128
ROPE_DIMS = SWA_HEAD // 4
ROPE_THETA = 500000.0
MIX1 = SWA_Q_HEADS * SWA_HEAD
SWA_KV = SWA_KV_HEADS * SWA_HEAD
SWA_QKV = MIX1 + 2 * SWA_KV

LANES = 128
HEAD = 64
GLOW_PAD = LANES
VMEM_LIMIT = 56 * 1024 * 1024


def _cparams(sem):
    return pltpu.CompilerParams(dimension_semantics=sem, vmem_limit_bytes=VMEM_LIMIT)


def _dot(a, b):
    return jnp.dot(a.astype(BF16), b.astype(BF16), preferred_element_type=F32)


def _dot_nt(a, b):
    return lax.dot_general(a.astype(BF16), b.astype(BF16), (((1,), (1,)), ((), ())),
                           preferred_element_type=F32)


def _dot_tn(a, b):
    return lax.dot_general(a.astype(BF16), b.astype(BF16), (((0,), (0,)), ((), ())),
                           preferred_element_type=F32)


def _split2(x):
    hi = x.astype(BF16)
    lo = (x - hi.astype(F32)).astype(BF16)
    return hi, lo


def _dot_exact_rhs(a_bf16, x):
    hi, lo = _split2(x)
    return (jnp.dot(a_bf16, hi, preferred_element_type=F32)
            + jnp.dot(a_bf16, lo, preferred_element_type=F32))


def _dot_exact_lhs(x, b_bf16):
    hi, lo = _split2(x)
    return (jnp.dot(hi, b_bf16, preferred_element_type=F32)
            + jnp.dot(lo, b_bf16, preferred_element_type=F32))


def _iota(shape, dim):
    return lax.broadcasted_iota(jnp.int32, shape, dim)


def _tril_ones(n, dtype=BF16):
    return (_iota((n, n), 0) >= _iota((n, n), 1)).astype(dtype)


def _head_block_ones(n=LANES, dtype=BF16):
    return ((_iota((n, n), 0) // HEAD) == (_iota((n, n), 1) // HEAD)).astype(dtype)


def _head_stack(x):
    head = (_iota(x.shape, 1) % LANES) // HEAD
    return jnp.concatenate([jnp.where(head == 0, x, 0.0), jnp.where(head == 1, x, 0.0)], axis=0)


def _softplus(z):
    return jnp.maximum(z, 0.0) + jnp.log(1.0 + jnp.exp(-jnp.abs(z)))


def _sigmoid(z):
    return 1.0 / (1.0 + jnp.exp(-z))


def _rmsnorm_rows(x, w):
    return x * lax.rsqrt(jnp.mean(x * x, axis=-1, keepdims=True) + NORM_EPS) * w


def _in0_kernel(x_ref, nw_ref, w_ref, mu_ref,
                gqkv_ref, glow_ref, rkv_ref, lora_ref, gate_ref, carry_ref, wg_ref, wl_ref, wr_ref,
                *, tiles_per_seq):
    i = pl.program_id(0)

    @pl.when(i == 0)
    def _():
        carry_ref[...] = jnp.zeros_like(carry_ref)
        wg_ref[...] = w_ref[0, :, :GLA_QKV].astype(BF16)
        wl_ref[...] = w_ref[0, :, GLA_QKV:GLA_QKV + GLOW_PAD].astype(BF16)
        wr_ref[...] = w_ref[0, :, GLA_QKV + GLA_GATE_RANK:].astype(BF16)

    xn = _rmsnorm_rows(x_ref[...], nw_ref[...]).astype(BF16)
    gqkv_ref[...] = jnp.dot(xn, wg_ref[...], preferred_element_type=F32).astype(gqkv_ref.dtype)
    glow_ref[...] = jnp.dot(xn, wl_ref[...], preferred_element_type=F32)
    rw = jnp.dot(xn, wr_ref[:, :RWKV_SHIFT], preferred_element_type=F32)
    gate_ref[...] = jnp.dot(xn, wr_ref[:, RWKV_SHIFT:], preferred_element_type=F32).astype(gate_ref.dtype)

    tm = rw.shape[0]
    first = (i % tiles_per_seq) == 0
    prev_last = jnp.where(first, 0.0, carry_ref[7:8, :])
    rolled = pltpu.roll(rw, 1, 0)
    prev = jnp.where(_iota(rw.shape, 0) == 0, prev_last, rolled)
    mixed = rw + (prev - rw) * mu_ref[...]
    rkv_ref[...] = mixed[:, :RWKV_RKV].astype(rkv_ref.dtype)
    lora_ref[...] = mixed[:, RWKV_RKV:]
    carry_ref[...] = rw[tm - 8:tm, :]


def _in0_call(x2, norm_w, w_in, mu, seq_len, tm):
    n_tok = x2.shape[0]
    row = lambda i: (i, 0)
    const = lambda i: (0, 0)
    outs = [(GLA_QKV, BF16), (GLOW_PAD, F32), (RWKV_RKV, BF16), (RWKV_LORA, F32), (MIX0, BF16)]
    return pl.pallas_call(
        functools.partial(_in0_kernel, tiles_per_seq=seq_len // tm),
        grid=(n_tok // tm,),
        in_specs=[pl.BlockSpec((tm, D_MODEL), row),
                  pl.BlockSpec((1, D_MODEL), const),
                  pl.BlockSpec((1,) + w_in.shape[1:], lambda i: (0, 0, 0), pipeline_mode=pl.Buffered(1)),
                  pl.BlockSpec((1, RWKV_SHIFT), const)],
        out_specs=[pl.BlockSpec((tm, n), row) for n, _ in outs],
        out_shape=[jax.ShapeDtypeStruct((n_tok, n), dt) for n, dt in outs],
        scratch_shapes=[pltpu.VMEM((8, RWKV_SHIFT), F32),
                        pltpu.VMEM((D_MODEL, GLA_QKV), BF16),
                        pltpu.VMEM((D_MODEL, GLOW_PAD), BF16),
                        pltpu.VMEM((D_MODEL, RWKV_SHIFT + MIX0), BF16)],
        compiler_params=_cparams(("arbitrary",)),
        name="l0_norm_proj",
    )(x2, norm_w, w_in, mu)


def _gla_kernel(q_ref, k_ref, glow_ref, v_ref, up_ref, bias_ref, nw_ref, o_ref, st_ref, *, chunks):
    c = pl.program_id(2)

    @pl.when(c == 0)
    def _():
        st_ref[...] = jnp.zeros_like(st_ref)

    C = GLA_CHUNK
    tril = _tril_ones(C)
    causal = _iota((C, LANES), 0) >= (_iota((C, LANES), 1) % HEAD)
    sr = _iota((2 * GLA_DV, LANES), 0)
    sl = _iota((2 * GLA_DV, LANES), 1)
    st_mask = (sr // GLA_DV) == (sl // HEAD)
    vl = _iota((C, 2 * GLA_DV), 1)
    scale = GLA_DK ** -0.5
    z = _dot(glow_ref[...], up_ref[...]) + bias_ref[...]
    g_all = -_softplus(-z) / GLA_GATE_NORMALIZER
    q_all = q_ref[...].astype(F32) * scale
    k_all = k_ref[...].astype(F32)
    rows = [slice(j * C, (j + 1) * C) for j in range(chunks)]
    bs = [_dot_exact_rhs(tril, g_all[rw]) for rw in rows]
    qe, ke, qb, kl, dec, vs = [], [], [], [], [], []
    for rw, b in zip(rows, bs):
        ref = b[C // 2:C // 2 + 1, :]
        b_last = b[C - 1:C, :]
        qe.append(q_all[rw] * jnp.exp(b - ref))
        ke.append(k_all[rw] * jnp.exp(ref - b))
        qb.append(q_all[rw] * jnp.exp(b))
        kl.append(k_all[rw] * jnp.exp(b_last - b))
        dec.append(jnp.exp(b_last))
        vs.append(v_ref[rw, :])
    att = [jnp.where(causal, _dot_nt(qe[j], _head_stack(ke[j])), 0.0) for j in range(chunks)]
    kv = [jnp.where(st_mask, _dot_tn(vs[j], kl[j]), 0.0) for j in range(chunks)]
    v_diag = [jnp.concatenate([jnp.where(vl < GLA_DV, vs[j], jnp.zeros_like(vs[j])),
                               jnp.where(vl >= GLA_DV, vs[j], jnp.zeros_like(vs[j]))], axis=0)
              for j in range(chunks)]
    intra = [jnp.dot(att[j].astype(BF16), v_diag[j], preferred_element_type=F32) for j in range(chunks)]
    states = [st_ref[...]]
    for j in range(chunks):
        states.append(states[j] * dec[j] + kv[j])
    st_ref[...] = states[chunks]
    for j in range(chunks):
        o = intra[j] + _dot_nt(qb[j], states[j])
        for h in range(2):
            oh = o[:, h * GLA_DV:(h + 1) * GLA_DV]
            oh = oh * lax.rsqrt(jnp.mean(oh * oh, axis=-1, keepdims=True) + NORM_EPS) * nw_ref[...]
            o_ref[rows[j], h * GLA_DV:(h + 1) * GLA_DV] = oh.astype(o_ref.dtype)


def _gla_call(gqkv, glow, up_pad, bias, norm_w, batch, seq_len, chunks):
    n_tok = gqkv.shape[0]
    tcb = chunks * GLA_CHUNK
    steps = seq_len // tcb
    pairs = GLA_KEY // LANES
    return pl.pallas_call(
        functools.partial(_gla_kernel, chunks=chunks),
        grid=(batch, pairs, steps),
        in_specs=[pl.BlockSpec((tcb, LANES), lambda b, p, c: (b * steps + c, p)),
                  pl.BlockSpec((tcb, LANES), lambda b, p, c: (b * steps + c, pairs + p)),
                  pl.BlockSpec((tcb, GLOW_PAD), lambda b, p, c: (b * steps + c, 0)),
                  pl.BlockSpec((tcb, 2 * GLA_DV), lambda b, p, c: (b * steps + c, pairs + p)),
                  pl.BlockSpec((GLOW_PAD, LANES), lambda b, p, c: (0, p)),
                  pl.BlockSpec((1, LANES), lambda b, p, c: (0, p)),
                  pl.BlockSpec((1, GLA_DV), lambda b, p, c: (0, 0))],
        out_specs=pl.BlockSpec((tcb, 2 * GLA_DV), lambda b, p, c: (b * steps + c, p)),
        out_shape=jax.ShapeDtypeStruct((n_tok, GLA_VAL), BF16),
        scratch_shapes=[pltpu.VMEM((2 * GLA_DV, LANES), F32)],
        compiler_params=_cparams(("parallel", "parallel", "arbitrary")),
        name="l0_gla",
    )(gqkv, gqkv, glow, gqkv, up_pad, bias, norm_w)


def _merge_masks(n):
    r = _iota((n, LANES), 0)
    c = _iota((n, LANES), 1) % HEAD
    masks = []
    s = 1
    while s < n:
        masks.append(((r // s) % 2 == 1) & ((c // s) == (r // s) - 1))
        s *= 2
    return (r == c).astype(F32), masks


def _run_interleaved(main, main_steps, side, side_steps):
    done = 0
    spread = max(1, (3 * main_steps) // 4)
    for i, _ in enumerate(main):
        target = -(-(i + 1) * side_steps // spread)
        while done < min(target, side_steps):
            next(side, None)
            done += 1
    for _ in side:
        pass


def _rwkv_chunk_kernel(r_ref, k_ref, v_ref, xwa_ref, w0_ref, wup_ref, a0_ref, aup_ref,
                       kk_ref, ka_ref, rk_ref,
                       rp_ref, op_ref, bonus_ref, m_ref, n_ref, *, chunks, group):
    C = RWKV_CHUNK
    tril = _tril_ones(C)
    rr = _iota((2 * C, LANES), 0)
    cc = _iota((2 * C, LANES), 1) % HEAD
    tri2 = ((rr < C) & (rr > cc)) | (rr - C >= cc)
    hb = _head_block_ones()
    sq_r = _iota((LANES, LANES), 0)
    sq_c = _iota((LANES, LANES), 1)
    same_head = (sq_r // HEAD) == (sq_c // HEAD)
    eye128 = sq_r == sq_c

    eye, merge = _merge_masks(C)
    zero = jnp.zeros((C, LANES), F32)
    n = range(group)

    def prepare(g, out):
        rows = slice(g * group * C, (g + 1) * group * C)
        r_all = r_ref[rows, :].astype(F32)
        k_all = k_ref[rows, :].astype(F32)
        v_all = v_ref[rows, :].astype(F32)
        xwa = xwa_ref[rows, :]
        w = -_softplus(-(w0_ref[...] + _dot(jnp.tanh(xwa), wup_ref[...]))) - 0.5
        lw_all = -jnp.exp(w)
        a_sig = _sigmoid(a0_ref[...] + _dot(xwa, aup_ref[...]))
        kk = k_all * kk_ref[...]
        kk = kk / jnp.maximum(jnp.sqrt(_dot_exact_lhs(kk * kk, hb)), 1e-12)
        k_all = k_all * (1.0 + (a_sig - 1.0) * ka_ref[...])
        bonus_ref[rows, :] = _dot_exact_lhs(r_all * k_all * rk_ref[...], hb) * v_all
        a_all = -kk
        b_all = kk * a_sig
        yield
        for j in n:
            rw = slice(j * C, (j + 1) * C)
            cum = _dot_exact_rhs(tril, lw_all[rw])
            cum_last = cum[C - 1:C, :]
            e_neg = jnp.exp(-cum)
            e_end = jnp.exp(cum_last - cum)
            out.append(dict(
                rt=r_all[rw] * jnp.exp(cum),
                at=a_all[rw] * jnp.exp(cum - lw_all[rw]),
                bt=b_all[rw] * e_neg,
                kt=k_all[rw] * e_neg,
                ends=jnp.concatenate([b_all[rw] * e_end, k_all[rw] * e_end], axis=0),
                v=v_all[rw],
                dec=jnp.exp(cum_last)))
            yield

    def solve(g, ops):
        lhs = [jnp.concatenate([o["at"], o["rt"]], axis=0) for o in ops]
        left = [jnp.where(tri2, _dot_nt(lhs[j], _head_stack(ops[j]["bt"])), 0.0) for j in n]
        yield
        right = [jnp.where(tri2, _dot_nt(lhs[j], _head_stack(ops[j]["kt"])), 0.0) for j in n]
        yield
        lows = [lf[:C] for lf in left]
        ts = [eye + jnp.where(merge[0], low, 0.0) for low in lows]
        for sub in merge[1:]:
            ys = [_dot(jnp.where(sub, low, 0.0), _head_stack(t)) for low, t in zip(lows, ts)]
            yield
            ts = [t + _dot(t, _head_stack(y)) for t, y in zip(ts, ys)]
            yield
        kv = [_dot(right[j], _head_stack(ops[j]["v"])) for j in n]
        yield
        wz = [_dot(ts[j], _head_stack(jnp.concatenate([ops[j]["at"], kv[j][:C]], axis=1))) for j in n]
        yield
        ro = [_dot(left[j][C:], _head_stack(wz[j])) for j in n]
        yield
        mn = [_dot_tn(ops[j]["ends"],
                      jnp.concatenate([wz[j], jnp.concatenate([zero, ops[j]["v"]], axis=1)], axis=0))
              for j in n]
        for j in n:
            c = g * group + j
            rows = slice(c * C, (c + 1) * C)
            rp_ref[rows, :] = (ops[j]["rt"] + ro[j][:, :LANES]).astype(rp_ref.dtype)
            op_ref[rows, :] = ro[j][:, LANES:] + kv[j][C:]
            m_ref[0, 0, c] = (jnp.where(eye128, ops[j]["dec"], 0.0)
                              + jnp.where(same_head, mn[j][:, :LANES], 0.0)).astype(m_ref.dtype)
            n_ref[0, 0, c] = jnp.where(same_head, mn[j][:, LANES:], 0.0)
        yield

    solve_stages = 2 + 2 * (len(merge) - 1) + 4
    groups = chunks // group
    ops = [[] for _ in range(groups + 1)]
    for _ in prepare(0, ops[0]):
        pass
    for g in range(groups):
        side = prepare(g + 1, ops[g + 1]) if g + 1 < groups else iter(())
        _run_interleaved(solve(g, ops[g]), solve_stages, side, group + 1)


def _rwkv_chunk_call(rkv, lora, w0, wup_pad, a0, aup_pad, k_k, k_a, r_k, batch, seq_len, chunks):
    n_tok = rkv.shape[0]
    tcb = chunks * RWKV_CHUNK
    steps = seq_len // tcb
    pairs = RWKV_W // LANES
    nc = seq_len // RWKV_CHUNK
    col = lambda off: (lambda b, p, c: (b * steps + c, off + p))
    par = lambda b, p, c: (0, p)
    tok = lambda b, p, c: (b * steps + c, p)
    mat = lambda b, p, c: (b, p, c, 0, 0)
    return pl.pallas_call(
        functools.partial(_rwkv_chunk_kernel, chunks=chunks, group=RWKV_GROUP),
        grid=(batch, pairs, steps),
        in_specs=[pl.BlockSpec((tcb, LANES), col(0)),
                  pl.BlockSpec((tcb, LANES), col(pairs)),
                  pl.BlockSpec((tcb, LANES), col(2 * pairs)),
                  pl.BlockSpec((tcb, RWKV_LORA), lambda b, p, c: (b * steps + c, 0)),
                  pl.BlockSpec((1, LANES), par),
                  pl.BlockSpec((RWKV_LORA, LANES), par),
                  pl.BlockSpec((1, LANES), par),
                  pl.BlockSpec((RWKV_LORA, LANES), par),
                  pl.BlockSpec((1, LANES), par),
                  pl.BlockSpec((1, LANES), par),
                  pl.BlockSpec((1, LANES), par)],
        out_specs=[pl.BlockSpec((tcb, LANES), tok),
                   pl.BlockSpec((tcb, LANES), tok),
                   pl.BlockSpec((tcb, LANES), tok),
                   pl.BlockSpec((1, 1, chunks, LANES, LANES), mat),
                   pl.BlockSpec((1, 1, chunks, LANES, LANES), mat)],
        out_shape=[jax.ShapeDtypeStruct((n_tok, RWKV_W), BF16),
                   jax.ShapeDtypeStruct((n_tok, RWKV_W), F32),
                   jax.ShapeDtypeStruct((n_tok, RWKV_W), F32),
                   jax.ShapeDtypeStruct((batch, pairs, nc, LANES, LANES), BF16),
                   jax.ShapeDtypeStruct((batch, pairs, nc, LANES, LANES), F32)],
        compiler_params=_cparams(("parallel", "parallel", "parallel")),
        name="l0_rwkv_chunks",
    )(rkv, rkv, rkv, lora, w0, wup_pad, a0, aup_pad, k_k, k_a, r_k)


def _rwkv_scan_kernel(rp_ref, op_ref, bonus_ref, m_ref, n_ref, lnw_ref, lnb_ref, o_ref, st_ref, *, chunks):
    c = pl.program_id(0)

    @pl.when(c == 0)
    def _():
        st_ref[...] = jnp.zeros_like(st_ref)

    C = RWKV_CHUNK
    batch = rp_ref.shape[0]
    pairs = RWKV_W // LANES
    hb = _head_block_ones()
    seqs = [(b, p) for b in range(batch) for p in range(pairs)]
    states = {bp: [st_ref[bp[0], bp[1]]] for bp in seqs}
    for j in range(chunks):
        for b, p in seqs:
            states[b, p].append(_dot(m_ref[b, p, j], states[b, p][j]) + n_ref[b, p, j])
    for b, p in seqs:
        st_ref[b, p] = states[b, p][chunks]
    cols = {bp: slice(bp[1] * LANES, (bp[1] + 1) * LANES) for bp in seqs}
    os = [jnp.concatenate([_dot(rp_ref[b, j * C:(j + 1) * C, cols[b, p]], states[b, p][j])
                           for j in range(chunks)], axis=0) + op_ref[b, :, cols[b, p]] for b, p in seqs]
    means = [_dot_exact_lhs(o, hb) * (1.0 / RWKV_HEAD) for o in os]
    ds = [o - mean for o, mean in zip(os, means)]
    variances = [_dot_exact_lhs(d * d, hb) * (1.0 / RWKV_HEAD) for d in ds]
    for (b, p), d, var in zip(seqs, ds, variances):
        c_ = cols[b, p]
        o_ref[b, :, c_] = (d * lax.rsqrt(var + RWKV_LN_EPS) * lnw_ref[:, c_] + lnb_ref[:, c_]
                           + bonus_ref[b, :, c_]).astype(o_ref.dtype)


def _rwkv_scan_call(rp, op, bonus, m, n, ln_w, ln_b, batch, seq_len, chunks):
    tcb = chunks * RWKV_CHUNK
    pairs = RWKV_W // LANES
    seq3 = lambda t: t.reshape(batch, seq_len, RWKV_W)
    tok = lambda c: (0, c, 0)
    const = lambda c: (0, 0)
    mat = lambda c: (0, 0, c, 0, 0)
    out = pl.pallas_call(
        functools.partial(_rwkv_scan_kernel, chunks=chunks),
        grid=(seq_len // tcb,),
        in_specs=[pl.BlockSpec((batch, tcb, RWKV_W), tok),
                  pl.BlockSpec((batch, tcb, RWKV_W), tok),
                  pl.BlockSpec((batch, tcb, RWKV_W), tok),
                  pl.BlockSpec((batch, pairs, chunks, LANES, LANES), mat),
                  pl.BlockSpec((batch, pairs, chunks, LANES, LANES), mat),
                  pl.BlockSpec((1, RWKV_W), const),
                  pl.BlockSpec((1, RWKV_W), const)],
        out_specs=pl.BlockSpec((batch, tcb, RWKV_W), tok),
        out_shape=jax.ShapeDtypeStruct((batch, seq_len, RWKV_W), BF16),
        scratch_shapes=[pltpu.VMEM((batch, pairs, LANES, LANES), F32)],
        compiler_params=_cparams(("arbitrary",)),
        name="l0_rwkv_scan",
    )(seq3(rp), seq3(op), seq3(bonus), m, n, ln_w, ln_b)
    return out.reshape(batch * seq_len, RWKV_W)


def _gated_out0(oa_ref, ob_ref, gate_ref, x_ref, w_ref):
    g = gate_ref[...].astype(F32)
    g = g * _sigmoid(g)
    ya = (oa_ref[...].astype(F32) * g[:, :GLA_VAL]).astype(BF16)
    yb = (ob_ref[...].astype(F32) * g[:, GLA_VAL:]).astype(BF16)
    return (x_ref[...]
            + jnp.dot(ya, w_ref[:GLA_VAL, :], preferred_element_type=F32)
            + jnp.dot(yb, w_ref[GLA_VAL:, :], preferred_element_type=F32))


def _rope_group(x, cos, sin_lo, sin_hi):
    half = ROPE_DIMS // 2
    return x * cos + pltpu.roll(x, LANES - half, 1) * sin_lo + pltpu.roll(x, half, 1) * sin_hi


def _paired_head_order():
    return [(2 * pp + e) * SWA_GROUP + g
            for pp in range(SWA_KV_HEADS // 2) for g in range(SWA_GROUP) for e in range(2)]


def _mid_kernel(oa_ref, ob_ref, gate0_ref, x_ref, wo32_ref,
                nw_ref, w1_ref, b_ref, cos_ref, slo_ref, shi_ref,
                h_ref, q_ref, k_ref, v_ref, gate_ref, wo_ref, wq_ref, wkv_ref, wg_ref):
    @pl.when(pl.program_id(0) == 0)
    def _():
        wo_ref[...] = wo32_ref[0].astype(BF16)
        wkv_ref[...] = w1_ref[0, :, MIX1:SWA_QKV].astype(BF16)
        for new, old in enumerate(_paired_head_order()):
            dst = slice(new * SWA_HEAD, (new + 1) * SWA_HEAD)
            wq_ref[:, dst] = w1_ref[0, :, old * SWA_HEAD:(old + 1) * SWA_HEAD].astype(BF16)
            wg_ref[:, dst] = w1_ref[0, :, SWA_QKV + old * SWA_HEAD:SWA_QKV + (old + 1) * SWA_HEAD].astype(BF16)

    h = _gated_out0(oa_ref, ob_ref, gate0_ref, x_ref, wo_ref)
    h_ref[...] = h
    hn = _rmsnorm_rows(h, nw_ref[...]).astype(BF16)
    cos = cos_ref[...]
    slo = slo_ref[...]
    shi = shi_ref[...]
    scale = SWA_HEAD ** -0.5
    q = jnp.dot(hn, wq_ref[...], preferred_element_type=F32) + b_ref[:, :MIX1]
    kv = jnp.dot(hn, wkv_ref[...], preferred_element_type=F32) + b_ref[:, MIX1:]
    gate_ref[...] = jnp.dot(hn, wg_ref[...], preferred_element_type=F32).astype(gate_ref.dtype)
    for g in range(MIX1 // LANES):
        cols = slice(g * LANES, (g + 1) * LANES)
        q_ref[:, cols] = (_rope_group(q[:, cols], cos, slo, shi) * scale).astype(q_ref.dtype)
    for g in range(SWA_KV // LANES):
        cols = slice(g * LANES, (g + 1) * LANES)
        k_ref[:, cols] = _rope_group(kv[:, cols], cos, slo, shi).astype(k_ref.dtype)
    v_ref[...] = kv[:, SWA_KV:].astype(v_ref.dtype)


def _mid_call(oa, ob, gate0, x2, w_out0, norm_w, w_in1, b_in, cos, slo, shi, seq_len, tm):
    n_tok = x2.shape[0]
    tps = seq_len // tm
    row = lambda i: (i, 0)
    const = lambda i: (0, 0)
    pos = lambda i: (i % tps, 0)
    whole = lambda t: pl.BlockSpec((1,) + t.shape[1:], lambda i: (0, 0, 0), pipeline_mode=pl.Buffered(1))
    return pl.pallas_call(
        _mid_kernel,
        grid=(n_tok // tm,),
        in_specs=[pl.BlockSpec((tm, GLA_VAL), row),
                  pl.BlockSpec((tm, RWKV_W), row),
                  pl.BlockSpec((tm, MIX0), row),
                  pl.BlockSpec((tm, D_MODEL), row),
                  whole(w_out0),
                  pl.BlockSpec((1, D_MODEL), const),
                  whole(w_in1),
                  pl.BlockSpec((1, SWA_QKV), const),
                  pl.BlockSpec((tm, LANES), pos),
                  pl.BlockSpec((tm, LANES), pos),
                  pl.BlockSpec((tm, LANES), pos)],
        out_specs=[pl.BlockSpec((tm, D_MODEL), row),
                   pl.BlockSpec((tm, MIX1), row),
                   pl.BlockSpec((tm, SWA_KV), row),
                   pl.BlockSpec((tm, SWA_KV), row),
                   pl.BlockSpec((tm, MIX1), row)],
        out_shape=[jax.ShapeDtypeStruct((n_tok, D_MODEL), F32),
                   jax.ShapeDtypeStruct((n_tok, MIX1), BF16),
                   jax.ShapeDtypeStruct((n_tok, SWA_KV), BF16),
                   jax.ShapeDtypeStruct((n_tok, SWA_KV), BF16),
                   jax.ShapeDtypeStruct((n_tok, MIX1), BF16)],
        scratch_shapes=[pltpu.VMEM((MIX0, D_MODEL), BF16),
                        pltpu.VMEM((D_MODEL, MIX1), BF16),
                        pltpu.VMEM((D_MODEL, 2 * SWA_KV), BF16),
                        pltpu.VMEM((D_MODEL, MIX1), BF16)],
        compiler_params=_cparams(("arbitrary",)),
        name="l0_out_l1_proj",
    )(oa, ob, gate0, x2, w_out0, norm_w, w_in1, b_in, cos, slo, shi)


def _swa_kernel(sink_ref, q_ref, kc_ref, kp_ref, vc_ref, vp_ref, gate_ref, h_ref, w32_ref, b_ref, nw_ref,
                y_ref, o_ref, w_ref, *, q_blocks):
    n = pl.program_id(1)

    @pl.when((pl.program_id(0) == 0) & (n == 0))
    def _():
        for new, old in enumerate(_paired_head_order()):
            w_ref[new * SWA_HEAD:(new + 1) * SWA_HEAD, :] = (
                w32_ref[0, old * SWA_HEAD:(old + 1) * SWA_HEAD, :].astype(BF16))

    W = WINDOW
    from_prev = _iota((W, 2 * W), 0) > (_iota((W, 2 * W), 1) % W)
    no_prev = jnp.where(n > 0, 0.0, -jnp.inf)
    col_row = _iota((1, 2 * W), 1)
    out_row = _iota((LANES, W), 0)
    kv_groups = SWA_KV // LANES
    groups = MIX1 // LANES // kv_groups
    tasks = [(j, pp, pp * groups + g) for j in range(q_blocks) for pp in range(kv_groups) for g in range(groups)]
    kk, vt = {}, {}
    for j in range(q_blocks):
        for pp in range(kv_groups):
            cols = slice(pp * LANES, (pp + 1) * LANES)
            if j == 0:
                kk[j, pp] = jnp.concatenate([kp_ref[:, cols], kc_ref[:W, cols]], axis=0)
                vv = jnp.concatenate([vp_ref[:, cols], vc_ref[:W, cols]], axis=0)
            else:
                kk[j, pp] = kc_ref[(j - 1) * W:(j + 1) * W, cols]
                vv = vc_ref[(j - 1) * W:(j + 1) * W, cols]
            vt[j, pp] = vv.astype(F32).T.astype(BF16)

    def scores(j, pp, blk):
        q = q_ref[j * W:(j + 1) * W, blk * LANES:(blk + 1) * LANES]
        return lax.dot_general(kk[j, pp], _head_stack(q), (((1,), (1,)), ((), ())), preferred_element_type=F32)

    ahead = 8
    pending = [scores(*t) for t in tasks[:ahead]]
    for i, (j, pp, blk) in enumerate(tasks):
        st = pending.pop(0)
        if i + ahead < len(tasks):
            pending.append(scores(*tasks[i + ahead]))
        s_prev = st[:W] + no_prev if j == 0 else st[:W]
        s = jnp.where(from_prev, s_prev, st[W:])
        sink = jnp.where(col_row < W, sink_ref[2 * blk], sink_ref[2 * blk + 1])
        m = jnp.maximum(jnp.max(s, axis=0, keepdims=True), sink)
        p = jnp.exp(s - m)
        denom = jnp.sum(p, axis=0, keepdims=True) + jnp.exp(sink - m)
        pb = p.astype(BF16)
        zero = jnp.zeros_like(pb)
        p2 = jnp.concatenate([jnp.where(from_prev, pb, zero), jnp.where(from_prev, zero, pb)], axis=0)
        ot = jnp.dot(vt[j, pp], p2, preferred_element_type=F32) * (1.0 / denom)
        ot = jnp.where(out_row < HEAD, ot[:, :W], ot[:, W:])
        o_ref[j * W:(j + 1) * W, blk * LANES:(blk + 1) * LANES] = ot.T.astype(o_ref.dtype)

    g = gate_ref[...].astype(F32)
    y = (o_ref[...].astype(F32) * (g * _sigmoid(g))).astype(BF16)
    h = h_ref[...] + jnp.dot(y, w_ref[...], preferred_element_type=F32) + b_ref[...]
    y_ref[...] = _rmsnorm_rows(h, nw_ref[...])


def _swa_call(sinks, q, k, v, gate, h1, w_out, b_out, norm_w, batch, seq_len, q_blocks):
    n_tok = q.shape[0]
    rows = q_blocks * WINDOW
    steps = seq_len // rows
    cur = lambda b, n: (b * steps + n, 0)
    prev = lambda b, n: (jnp.maximum((b * steps + n) * q_blocks - 1, 0), 0)
    const = lambda b, n: (0, 0)
    return pl.pallas_call(
        functools.partial(_swa_kernel, q_blocks=q_blocks),
        grid=(batch, steps),
        in_specs=[pl.BlockSpec(memory_space=pltpu.SMEM),
                  pl.BlockSpec((rows, MIX1), cur),
                  pl.BlockSpec((rows, SWA_KV), cur),
                  pl.BlockSpec((WINDOW, SWA_KV), prev),
                  pl.BlockSpec((rows, SWA_KV), cur),
                  pl.BlockSpec((WINDOW, SWA_KV), prev),
                  pl.BlockSpec((rows, MIX1), cur),
                  pl.BlockSpec((rows, D_MODEL), cur),
                  pl.BlockSpec((1,) + w_out.shape[1:], lambda b, n: (0, 0, 0), pipeline_mode=pl.Buffered(1)),
                  pl.BlockSpec((1, D_MODEL), const),
                  pl.BlockSpec((1, D_MODEL), const)],
        out_specs=pl.BlockSpec((rows, D_MODEL), cur),
        out_shape=jax.ShapeDtypeStruct((n_tok, D_MODEL), F32),
        scratch_shapes=[pltpu.VMEM((rows, MIX1), BF16),
                        pltpu.VMEM((MIX1, D_MODEL), BF16)],
        compiler_params=_cparams(("arbitrary", "arbitrary")),
        name="l1_swa_out",
    )(sinks, q, k, k, v, v, gate, h1, w_out, b_out, norm_w)


def _pad_rows(w, rows):
    return jnp.concatenate([w, jnp.zeros((rows - w.shape[0], w.shape[1]), w.dtype)], axis=0)


def _pair_heads(t, axis):
    shape = t.shape
    split = shape[:axis] + (SWA_KV_HEADS // 2, 2, SWA_GROUP, SWA_HEAD) + shape[axis + 1:]
    return jnp.swapaxes(t.reshape(split), axis + 1, axis + 2).reshape(shape)


def _rope_tables(seq_len):
    half = ROPE_DIMS // 2
    inv_freq = ROPE_THETA ** (-jnp.arange(half, dtype=F32) / half)
    ang = jnp.arange(seq_len).astype(F32)[:, None] * inv_freq
    trig = jnp.concatenate([jnp.cos(ang), jnp.sin(ang)], axis=1)
    d = jnp.arange(LANES) % SWA_HEAD
    src = jnp.arange(2 * half)[:, None]
    f = (d % half)[None, :]
    rot = (d < ROPE_DIMS)[None, :]
    lo = (d < half)[None, :]
    sel_cos = ((src == f) & rot).astype(F32)
    sel_lo = -((src == half + f) & lo).astype(F32)
    sel_hi = ((src == half + f) & rot & ~lo).astype(F32)
    sel = jnp.concatenate([sel_cos, sel_lo, sel_hi], axis=1)
    tab = jnp.dot(trig, sel, precision=lax.Precision.HIGHEST)
    cos = tab[:, :LANES] + (~rot).astype(F32)
    return cos, tab[:, LANES:2 * LANES], tab[:, 2 * LANES:]


def _forward(x, norm_w, w_in0, gla_gk_up, gla_gk_bias, gla_norm_w, rwkv_mu, rwkv_w0, rwkv_w_up,
             rwkv_a0, rwkv_a_up, rwkv_k_k, rwkv_k_a, rwkv_r_k, rwkv_ln_w, rwkv_ln_b, w_out0,
             w_in1, b_in1, attn_sinks, w_out1, b_out1, final_norm_w, *, tm, gla_chunks, rwkv_chunks, scan_chunks,
             swa_blocks):
    batch, seq_len, _ = x.shape
    x2 = x.reshape(batch * seq_len, D_MODEL)
    row = lambda t: t.reshape(1, -1)

    gqkv, glow, rkv, lora, gate0 = _in0_call(x2, row(norm_w[0]), w_in0, row(rwkv_mu[0]), seq_len, tm)

    up_pad = _pad_rows(gla_gk_up[0], GLOW_PAD).astype(BF16)
    o_a = _gla_call(gqkv, glow, up_pad, row(gla_gk_bias[0]), row(gla_norm_w[0]), batch, seq_len, gla_chunks)

    zeros_r = jnp.zeros((RWKV_DECAY_RANK, RWKV_W), F32)
    wup_pad = jnp.concatenate([rwkv_w_up[0], zeros_r], axis=0).astype(BF16)
    aup_pad = jnp.concatenate([zeros_r, rwkv_a_up[0]], axis=0).astype(BF16)
    rp, op, bonus, m, n = _rwkv_chunk_call(
        rkv, lora, row(rwkv_w0[0]), wup_pad, row(rwkv_a0[0]), aup_pad,
        row(rwkv_k_k[0]), row(rwkv_k_a[0]), row(rwkv_r_k[0]), batch, seq_len, rwkv_chunks)
    o_b = _rwkv_scan_call(rp, op, bonus, m, n, row(rwkv_ln_w[0]), row(rwkv_ln_b[0]),
                          batch, seq_len, scan_chunks)


    b1 = b_in1[0]
    b1p = row(jnp.concatenate([_pair_heads(b1[:MIX1], 0), b1[MIX1:]]))
    sinks_p = jnp.swapaxes(attn_sinks[0].reshape(SWA_KV_HEADS // 2, 2, SWA_GROUP), 1, 2).reshape(SWA_Q_HEADS)
    cos, slo, shi = _rope_tables(seq_len)
    h1, q, k, v, gate1 = _mid_call(o_a, o_b, gate0, x2, w_out0, row(norm_w[1]), w_in1,
                                   b1p, cos, slo, shi, seq_len, tm)
    y = _swa_call(sinks_p, q, k, v, gate1, h1, w_out1, row(b_out1[0]), row(final_norm_w),
                  batch, seq_len, swa_blocks)
    return y.reshape(batch, seq_len, D_MODEL)


def kernel(x, norm_w, w_in0, gla_gk_up, gla_gk_bias, gla_norm_w, rwkv_mu, rwkv_w0, rwkv_w_up, rwkv_a0,
           rwkv_a_up, rwkv_k_k, rwkv_k_a, rwkv_r_k, rwkv_ln_w, rwkv_ln_b, w_out0, w_in1, b_in1,
           attn_sinks, w_out1, b_out1, final_norm_w):
    return _forward(x, norm_w, w_in0, gla_gk_up, gla_gk_bias, gla_norm_w, rwkv_mu, rwkv_w0, rwkv_w_up,
                    rwkv_a0, rwkv_a_up, rwkv_k_k, rwkv_k_a, rwkv_r_k, rwkv_ln_w, rwkv_ln_b, w_out0,
                    w_in1, b_in1, attn_sinks, w_out1, b_out1, final_norm_w,
                    tm=512, gla_chunks=16, rwkv_chunks=64, scan_chunks=4, swa_blocks=4)
```

```python
import functools

import jax
import jax.numpy as jnp
from jax import lax
from jax.experimental import pallas as pl
from jax.experimental.pallas import tpu as pltpu

F32 = jnp.float32
BF16 = jnp.bfloat16

D_MODEL = 1024
NORM_EPS = 1e-5

GLA_HEADS = 4
GLA_DK = 64
GLA_DV = 128
GLA_KEY = GLA_HEADS * GLA_DK
GLA_VAL = GLA_HEADS * GLA_DV
GLA_GATE_RANK = 16
GLA_GATE_NORMALIZER = 16.0
GLA_CHUNK = 64

RWKV_HEADS = 8
RWKV_HEAD = 64
RWKV_W = RWKV_HEADS * RWKV_HEAD
RWKV_DECAY_RANK = 64
RWKV_A_RANK = 64
RWKV_LN_EPS = 64e-5
RWKV_RKV = 3 * RWKV_W
RWKV_LORA = RWKV_DECAY_RANK + RWKV_A_RANK
RWKV_SHIFT = RWKV_RKV + RWKV_LORA
RWKV_CHUNK = 64
RWKV_GROUP = 16

MIX0 = GLA_VAL + RWKV_W
GLA_QKV = 2 * GLA_KEY + GLA_VAL

SWA_Q_HEADS = 16
SWA_KV_HEADS = 4
SWA_GROUP = SWA_Q_HEADS // SWA_KV_HEADS
SWA_HEAD = 64
WINDOW = 128
ROPE_DIMS = SWA_HEAD // 4
ROPE_THETA = 500000.0
MIX1 = SWA_Q_HEADS * SWA_HEAD
SWA_KV = SWA_KV_HEADS * SWA_HEAD
SWA_QKV = MIX1 + 2 * SWA_KV
SWA_PROJ_BLOCKS = 2
SWA_PROJ_PIECES = 4
SWA_PROJ_EVERY = 3

LANES = 128
HEAD = 64
GLOW_PAD = LANES
VMEM_LIMIT = 56 * 1024 * 1024


def _cparams(sem):
    return pltpu.CompilerParams(dimension_semantics=sem, vmem_limit_bytes=VMEM_LIMIT)


def _dot(a, b):
    return jnp.dot(a.astype(BF16), b.astype(BF16), preferred_element_type=F32)


def _dot_nt(a, b):
    return lax.dot_general(a.astype(BF16), b.astype(BF16), (((1,), (1,)), ((), ())),
                           preferred_element_type=F32)


def _dot_tn(a, b):
    return lax.dot_general(a.astype(BF16), b.astype(BF16), (((0,), (0,)), ((), ())),
                           preferred_element_type=F32)


def _split2(x):
    hi = x.astype(BF16)
    lo = (x - hi.astype(F32)).astype(BF16)
    return hi, lo


def _dot_exact_rhs(a_bf16, x):
    hi, lo = _split2(x)
    return (jnp.dot(a_bf16, hi, preferred_element_type=F32)
            + jnp.dot(a_bf16, lo, preferred_element_type=F32))


def _dot_exact_lhs(x, b_bf16):
    hi, lo = _split2(x)
    return (jnp.dot(hi, b_bf16, preferred_element_type=F32)
            + jnp.dot(lo, b_bf16, preferred_element_type=F32))


def _iota(shape, dim):
    return lax.broadcasted_iota(jnp.int32, shape, dim)


def _tril_ones(n, dtype=BF16):
    return (_iota((n, n), 0) >= _iota((n, n), 1)).astype(dtype)


def _head_block_ones(n=LANES, dtype=BF16):
    return ((_iota((n, n), 0) // HEAD) == (_iota((n, n), 1) // HEAD)).astype(dtype)


def _head_stack(x):
    head = (_iota(x.shape, 1) % LANES) // HEAD
    return jnp.concatenate([jnp.where(head == 0, x, 0.0), jnp.where(head == 1, x, 0.0)], axis=0)


def _softplus(z):
    return jnp.maximum(z, 0.0) + jnp.log(1.0 + jnp.exp(-jnp.abs(z)))


def _sigmoid(z):
    return 1.0 / (1.0 + jnp.exp(-z))


def _silu_bf16(g):
    return g * (1.0 / (1.0 + jnp.exp(-g))).astype(BF16)


def _rmsnorm_rows(x, w):
    return x * lax.rsqrt(jnp.mean(x * x, axis=-1, keepdims=True) + NORM_EPS) * w


def _in0_kernel(x_ref, nw_ref, w_ref, mu_ref,
                gqkv_ref, glow_ref, rkv_ref, lora_ref, gate_ref, carry_ref, wg_ref, wl_ref, wr_ref,
                *, tiles_per_seq):
    i = pl.program_id(0)

    @pl.when(i == 0)
    def _():
        carry_ref[...] = jnp.zeros_like(carry_ref)
        wg_ref[...] = w_ref[0, :, :GLA_QKV].astype(BF16)
        wl_ref[...] = w_ref[0, :, GLA_QKV:GLA_QKV + GLOW_PAD].astype(BF16)
        wr_ref[...] = w_ref[0, :, GLA_QKV + GLA_GATE_RANK:].astype(BF16)

    xn = _rmsnorm_rows(x_ref[...], nw_ref[...]).astype(BF16)
    gqkv_ref[...] = jnp.dot(xn, wg_ref[...], preferred_element_type=F32).astype(gqkv_ref.dtype)
    glow_ref[...] = jnp.dot(xn, wl_ref[...], preferred_element_type=F32)
    rw = jnp.dot(xn, wr_ref[:, :RWKV_SHIFT], preferred_element_type=F32)
    gate_ref[...] = jnp.dot(xn, wr_ref[:, RWKV_SHIFT:], preferred_element_type=F32).astype(gate_ref.dtype)

    tm = rw.shape[0]
    first = (i % tiles_per_seq) == 0
    prev_last = jnp.where(first, 0.0, carry_ref[7:8, :])
    rolled = pltpu.roll(rw, 1, 0)
    prev = jnp.where(_iota(rw.shape, 0) == 0, prev_last, rolled)
    mixed = rw + (prev - rw) * mu_ref[...]
    rkv_ref[...] = mixed[:, :RWKV_RKV].astype(rkv_ref.dtype)
    lora_ref[...] = mixed[:, RWKV_RKV:]
    carry_ref[...] = rw[tm - 8:tm, :]


def _in0_call(x2, norm_w, w_in, mu, seq_len, tm):
    n_tok = x2.shape[0]
    row = lambda i: (i, 0)
    const = lambda i: (0, 0)
    outs = [(GLA_QKV, BF16), (GLOW_PAD, F32), (RWKV_RKV, BF16), (RWKV_LORA, F32), (MIX0, BF16)]
    return pl.pallas_call(
        functools.partial(_in0_kernel, tiles_per_seq=seq_len // tm),
        grid=(n_tok // tm,),
        in_specs=[pl.BlockSpec((tm, D_MODEL), row),
                  pl.BlockSpec((1, D_MODEL), const),
                  pl.BlockSpec((1,) + w_in.shape[1:], lambda i: (0, 0, 0), pipeline_mode=pl.Buffered(1)),
                  pl.BlockSpec((1, RWKV_SHIFT), const)],
        out_specs=[pl.BlockSpec((tm, n), row) for n, _ in outs],
        out_shape=[jax.ShapeDtypeStruct((n_tok, n), dt) for n, dt in outs],
        scratch_shapes=[pltpu.VMEM((8, RWKV_SHIFT), F32),
                        pltpu.VMEM((D_MODEL, GLA_QKV), BF16),
                        pltpu.VMEM((D_MODEL, GLOW_PAD), BF16),
                        pltpu.VMEM((D_MODEL, RWKV_SHIFT + MIX0), BF16)],
        compiler_params=_cparams(("arbitrary",)),
        name="l0_norm_proj",
    )(x2, norm_w, w_in, mu)


def _gla_kernel(q_ref, k_ref, glow_ref, v_ref, up_ref, bias_ref, nw_ref, o_ref, st_ref, *, chunks):
    c = pl.program_id(2)

    @pl.when(c == 0)
    def _():
        st_ref[...] = jnp.zeros_like(st_ref)

    C = GLA_CHUNK
    tril = _tril_ones(C)
    causal = _iota((C, LANES), 0) >= (_iota((C, LANES), 1) % HEAD)
    sr = _iota((2 * GLA_DV, LANES), 0)
    sl = _iota((2 * GLA_DV, LANES), 1)
    st_mask = (sr // GLA_DV) == (sl // HEAD)
    vl = _iota((C, 2 * GLA_DV), 1)
    scale = GLA_DK ** -0.5
    z = _dot(glow_ref[...], up_ref[...]) + bias_ref[...]
    g_all = -_softplus(-z) / GLA_GATE_NORMALIZER
    q_all = q_ref[...].astype(F32) * scale
    k_all = k_ref[...].astype(F32)
    rows = [slice(j * C, (j + 1) * C) for j in range(chunks)]
    bs = [_dot_exact_rhs(tril, g_all[rw]) for rw in rows]
    qe, ke, qb, kl, dec, vs = [], [], [], [], [], []
    for rw, b in zip(rows, bs):
        ref = b[C // 2:C // 2 + 1, :]
        b_last = b[C - 1:C, :]
        qe.append(q_all[rw] * jnp.exp(b - ref))
        ke.append(k_all[rw] * jnp.exp(ref - b))
        qb.append(q_all[rw] * jnp.exp(b))
        kl.append(k_all[rw] * jnp.exp(b_last - b))
        dec.append(jnp.exp(b_last))
        vs.append(v_ref[rw, :])
    att = [jnp.where(causal, _dot_nt(qe[j], _head_stack(ke[j])), 0.0) for j in range(chunks)]
    kv = [jnp.where(st_mask, _dot_tn(vs[j], kl[j]), 0.0) for j in range(chunks)]
    v_diag = [jnp.concatenate([jnp.where(vl < GLA_DV, vs[j], jnp.zeros_like(vs[j])),
                               jnp.where(vl >= GLA_DV, vs[j], jnp.zeros_like(vs[j]))], axis=0)
              for j in range(chunks)]
    intra = [jnp.dot(att[j].astype(BF16), v_diag[j], preferred_element_type=F32) for j in range(chunks)]
    states = [st_ref[...]]
    for j in range(chunks):
        states.append(states[j] * dec[j] + kv[j])
    st_ref[...] = states[chunks]
    for j in range(chunks):
        o = intra[j] + _dot_nt(qb[j], states[j])
        for h in range(2):
            oh = o[:, h * GLA_DV:(h + 1) * GLA_DV]
            oh = oh * lax.rsqrt(jnp.mean(oh * oh, axis=-1, keepdims=True) + NORM_EPS) * nw_ref[...]
            o_ref[rows[j], h * GLA_DV:(h + 1) * GLA_DV] = oh.astype(o_ref.dtype)


def _gla_call(gqkv, glow, up_pad, bias, norm_w, batch, seq_len, chunks):
    n_tok = gqkv.shape[0]
    tcb = chunks * GLA_CHUNK
    steps = seq_len // tcb
    pairs = GLA_KEY // LANES
    return pl.pallas_call(
        functools.partial(_gla_kernel, chunks=chunks),
        grid=(batch, pairs, steps),
        in_specs=[pl.BlockSpec((tcb, LANES), lambda b, p, c: (b * steps + c, p)),
                  pl.BlockSpec((tcb, LANES), lambda b, p, c: (b * steps + c, pairs + p)),
                  pl.BlockSpec((tcb, GLOW_PAD), lambda b, p, c: (b * steps + c, 0)),
                  pl.BlockSpec((tcb, 2 * GLA_DV), lambda b, p, c: (b * steps + c, pairs + p)),
                  pl.BlockSpec((GLOW_PAD, LANES), lambda b, p, c: (0, p)),
                  pl.BlockSpec((1, LANES), lambda b, p, c: (0, p)),
                  pl.BlockSpec((1, GLA_DV), lambda b, p, c: (0, 0))],
        out_specs=pl.BlockSpec((tcb, 2 * GLA_DV), lambda b, p, c: (b * steps + c, p)),
        out_shape=jax.ShapeDtypeStruct((n_tok, GLA_VAL), BF16),
        scratch_shapes=[pltpu.VMEM((2 * GLA_DV, LANES), F32)],
        compiler_params=_cparams(("parallel", "parallel", "arbitrary")),
        name="l0_gla",
    )(gqkv, gqkv, glow, gqkv, up_pad, bias, norm_w)


def _merge_masks(n):
    r = _iota((n, LANES), 0)
    c = _iota((n, LANES), 1) % HEAD
    masks = []
    s = 1
    while s < n:
        masks.append(((r // s) % 2 == 1) & ((c // s) == (r // s) - 1))
        s *= 2
    return (r == c).astype(F32), masks


def _run_interleaved(main, main_steps, side, side_steps):
    done = 0
    spread = max(1, (3 * main_steps) // 4)
    for i, _ in enumerate(main):
        target = -(-(i + 1) * side_steps // spread)
        while done < min(target, side_steps):
            next(side, None)
            done += 1
    for _ in side:
        pass


def _rwkv_chunk_kernel(r_ref, k_ref, v_ref, xwa_ref, w0_ref, wup_ref, a0_ref, aup_ref,
                       kk_ref, ka_ref, rk_ref,
                       rp_ref, op_ref, bonus_ref, m_ref, n_ref, *, chunks, group):
    C = RWKV_CHUNK
    tril = _tril_ones(C)
    rr = _iota((2 * C, LANES), 0)
    cc = _iota((2 * C, LANES), 1) % HEAD
    tri2 = ((rr < C) & (rr > cc)) | (rr - C >= cc)
    hb = _head_block_ones()
    sq_r = _iota((LANES, LANES), 0)
    sq_c = _iota((LANES, LANES), 1)
    same_head = (sq_r // HEAD) == (sq_c // HEAD)
    eye128 = sq_r == sq_c

    eye, merge = _merge_masks(C)
    zero = jnp.zeros((C, LANES), F32)
    n = range(group)

    def prepare(g, out):
        rows = slice(g * group * C, (g + 1) * group * C)
        r_all = r_ref[rows, :].astype(F32)
        k_all = k_ref[rows, :].astype(F32)
        v_all = v_ref[rows, :].astype(F32)
        xwa = xwa_ref[rows, :]
        w = -_softplus(-(w0_ref[...] + _dot(jnp.tanh(xwa), wup_ref[...]))) - 0.5
        lw_all = -jnp.exp(w)
        a_sig = _sigmoid(a0_ref[...] + _dot(xwa, aup_ref[...]))
        kk = k_all * kk_ref[...]
        kk = kk / jnp.maximum(jnp.sqrt(_dot_exact_lhs(kk * kk, hb)), 1e-12)
        k_all = k_all * (1.0 + (a_sig - 1.0) * ka_ref[...])
        bonus_ref[rows, :] = (_dot_exact_lhs(r_all * k_all * rk_ref[...], hb) * v_all).astype(bonus_ref.dtype)
        a_all = -kk
        b_all = kk * a_sig
        yield
        for j in n:
            rw = slice(j * C, (j + 1) * C)
            cum = _dot_exact_rhs(tril, lw_all[rw])
            cum_last = cum[C - 1:C, :]
            e_neg = jnp.exp(-cum)
            e_end = jnp.exp(cum_last - cum)
            out.append(dict(
                rt=r_all[rw] * jnp.exp(cum),
                at=a_all[rw] * jnp.exp(cum - lw_all[rw]),
                bt=b_all[rw] * e_neg,
                kt=k_all[rw] * e_neg,
                ends=jnp.concatenate([b_all[rw] * e_end, k_all[rw] * e_end], axis=0),
                v=v_all[rw],
                dec=jnp.exp(cum_last)))
            yield

    def solve(g, ops):
        lhs = [jnp.concatenate([o["at"], o["rt"]], axis=0) for o in ops]
        left = [jnp.where(tri2, _dot_nt(lhs[j], _head_stack(ops[j]["bt"])), 0.0) for j in n]
        yield
        right = [jnp.where(tri2, _dot_nt(lhs[j], _head_stack(ops[j]["kt"])), 0.0) for j in n]
        yield
        lows = [lf[:C] for lf in left]
        ts = [eye + jnp.where(merge[0], low, 0.0) for low in lows]
        for sub in merge[1:]:
            ys = [_dot(jnp.where(sub, low, 0.0), _head_stack(t)) for low, t in zip(lows, ts)]
            yield
            ts = [t + _dot(t, _head_stack(y)) for t, y in zip(ts, ys)]
            yield
        kv = [_dot(right[j], _head_stack(ops[j]["v"])) for j in n]
        yield
        wz = [_dot(ts[j], _head_stack(jnp.concatenate([ops[j]["at"], kv[j][:C]], axis=1))) for j in n]
        yield
        ro = [_dot(left[j][C:], _head_stack(wz[j])) for j in n]
        yield
        mn = [_dot_tn(ops[j]["ends"],
                      jnp.concatenate([wz[j], jnp.concatenate([zero, ops[j]["v"]], axis=1)], axis=0))
              for j in n]
        for j in n:
            c = g * group + j
            rows = slice(c * C, (c + 1) * C)
            rp_ref[rows, :] = (ops[j]["rt"] + ro[j][:, :LANES]).astype(rp_ref.dtype)
            op_ref[rows, :] = ro[j][:, LANES:] + kv[j][C:]
            m_ref[0, 0, c] = (jnp.where(eye128, ops[j]["dec"], 0.0)
                              + jnp.where(same_head, mn[j][:, :LANES], 0.0)).astype(m_ref.dtype)
            n_ref[0, 0, c] = jnp.where(same_head, mn[j][:, LANES:], 0.0).astype(n_ref.dtype)
        yield

    solve_stages = 2 + 2 * (len(merge) - 1) + 4
    groups = chunks // group
    ops = [[] for _ in range(groups + 1)]
    for _ in prepare(0, ops[0]):
        pass
    for g in range(groups):
        side = prepare(g + 1, ops[g + 1]) if g + 1 < groups else iter(())
        _run_interleaved(solve(g, ops[g]), solve_stages, side, group + 1)


def _rwkv_chunk_call(rkv, lora, w0, wup_pad, a0, aup_pad, k_k, k_a, r_k, batch, seq_len, chunks):
    n_tok = rkv.shape[0]
    tcb = chunks * RWKV_CHUNK
    steps = seq_len // tcb
    pairs = RWKV_W // LANES
    nc = seq_len // RWKV_CHUNK
    col = lambda off: (lambda b, p, c: (b * steps + c, off + p))
    par = lambda b, p, c: (0, p)
    tok = lambda b, p, c: (b * steps + c, p)
    mat = lambda b, p, c: (b, p, c, 0, 0)
    return pl.pallas_call(
        functools.partial(_rwkv_chunk_kernel, chunks=chunks, group=RWKV_GROUP),
        grid=(batch, pairs, steps),
        in_specs=[pl.BlockSpec((tcb, LANES), col(0)),
                  pl.BlockSpec((tcb, LANES), col(pairs)),
                  pl.BlockSpec((tcb, LANES), col(2 * pairs)),
                  pl.BlockSpec((tcb, RWKV_LORA), lambda b, p, c: (b * steps + c, 0)),
                  pl.BlockSpec((1, LANES), par),
                  pl.BlockSpec((RWKV_LORA, LANES), par),
                  pl.BlockSpec((1, LANES), par),
                  pl.BlockSpec((RWKV_LORA, LANES), par),
                  pl.BlockSpec((1, LANES), par),
                  pl.BlockSpec((1, LANES), par),
                  pl.BlockSpec((1, LANES), par)],
        out_specs=[pl.BlockSpec((tcb, LANES), tok),
                   pl.BlockSpec((tcb, LANES), tok),
                   pl.BlockSpec((tcb, LANES), tok),
                   pl.BlockSpec((1, 1, chunks, LANES, LANES), mat),
                   pl.BlockSpec((1, 1, chunks, LANES, LANES), mat)],
        out_shape=[jax.ShapeDtypeStruct((n_tok, RWKV_W), BF16),
                   jax.ShapeDtypeStruct((n_tok, RWKV_W), F32),
                   jax.ShapeDtypeStruct((n_tok, RWKV_W), BF16),
                   jax.ShapeDtypeStruct((batch, pairs, nc, LANES, LANES), BF16),
                   jax.ShapeDtypeStruct((batch, pairs, nc, LANES, LANES), BF16)],
        compiler_params=_cparams(("parallel", "parallel", "parallel")),
        name="l0_rwkv_chunks",
    )(rkv, rkv, rkv, lora, w0, wup_pad, a0, aup_pad, k_k, k_a, r_k)


def _rwkv_scan_kernel(rp_ref, op_ref, bonus_ref, m_ref, n_ref, lnw_ref, lnb_ref, o_ref, st_ref, *, chunks):
    c = pl.program_id(0)

    @pl.when(c == 0)
    def _():
        st_ref[...] = jnp.zeros_like(st_ref)

    C = RWKV_CHUNK
    batch = rp_ref.shape[0]
    pairs = RWKV_W // LANES
    hb = _head_block_ones()
    seqs = [(b, p) for b in range(batch) for p in range(pairs)]
    states = {bp: [st_ref[bp[0], bp[1]]] for bp in seqs}
    for j in range(chunks):
        for b, p in seqs:
            states[b, p].append(_dot(m_ref[b, p, j], states[b, p][j]) + n_ref[b, p, j])
    for b, p in seqs:
        st_ref[b, p] = states[b, p][chunks]
    cols = {bp: slice(bp[1] * LANES, (bp[1] + 1) * LANES) for bp in seqs}
    os = [jnp.concatenate([_dot(rp_ref[b, j * C:(j + 1) * C, cols[b, p]], states[b, p][j])
                           for j in range(chunks)], axis=0) + op_ref[b, :, cols[b, p]] for b, p in seqs]
    means = [_dot_exact_lhs(o, hb) * (1.0 / RWKV_HEAD) for o in os]
    ds = [o - mean for o, mean in zip(os, means)]
    variances = [_dot_exact_lhs(d * d, hb) * (1.0 / RWKV_HEAD) for d in ds]
    for (b, p), d, var in zip(seqs, ds, variances):
        c_ = cols[b, p]
        o_ref[b, :, c_] = (d * lax.rsqrt(var + RWKV_LN_EPS) * lnw_ref[:, c_] + lnb_ref[:, c_]
                           + bonus_ref[b, :, c_]).astype(o_ref.dtype)


def _rwkv_scan_call(rp, op, bonus, m, n, ln_w, ln_b, batch, seq_len, chunks):
    tcb = chunks * RWKV_CHUNK
    pairs = RWKV_W // LANES
    seq3 = lambda t: t.reshape(batch, seq_len, RWKV_W)
    tok = lambda c: (0, c, 0)
    const = lambda c: (0, 0)
    mat = lambda c: (0, 0, c, 0, 0)
    out = pl.pallas_call(
        functools.partial(_rwkv_scan_kernel, chunks=chunks),
        grid=(seq_len // tcb,),
        in_specs=[pl.BlockSpec((batch, tcb, RWKV_W), tok),
                  pl.BlockSpec((batch, tcb, RWKV_W), tok),
                  pl.BlockSpec((batch, tcb, RWKV_W), tok),
                  pl.BlockSpec((batch, pairs, chunks, LANES, LANES), mat),
                  pl.BlockSpec((batch, pairs, chunks, LANES, LANES), mat),
                  pl.BlockSpec((1, RWKV_W), const),
                  pl.BlockSpec((1, RWKV_W), const)],
        out_specs=pl.BlockSpec((batch, tcb, RWKV_W), tok),
        out_shape=jax.ShapeDtypeStruct((batch, seq_len, RWKV_W), BF16),
        scratch_shapes=[pltpu.VMEM((batch, pairs, LANES, LANES), F32)],
        compiler_params=_cparams(("arbitrary",)),
        name="l0_rwkv_scan",
    )(seq3(rp), seq3(op), seq3(bonus), m, n, ln_w, ln_b)
    return out.reshape(batch * seq_len, RWKV_W)


def _gated_out0(oa_ref, ob_ref, gate_ref, x_ref, w_ref):
    g = _silu_bf16(gate_ref[...])
    ya = oa_ref[...] * g[:, :GLA_VAL]
    yb = ob_ref[...] * g[:, GLA_VAL:]
    return (x_ref[...]
            + jnp.dot(ya, w_ref[:GLA_VAL, :], preferred_element_type=F32)
            + jnp.dot(yb, w_ref[GLA_VAL:, :], preferred_element_type=F32))


def _rope_group(x, cos, sin_lo, sin_hi):
    half = ROPE_DIMS // 2
    return x * cos + pltpu.roll(x, LANES - half, 1) * sin_lo + pltpu.roll(x, half, 1) * sin_hi


def _paired_head_order():
    return [(2 * pp + e) * SWA_GROUP + g
            for pp in range(SWA_KV_HEADS // 2) for g in range(SWA_GROUP) for e in range(2)]


def _mid_kernel(oa_ref, ob_ref, gate0_ref, x_ref, wo32_ref,
                nw_ref, w1_ref, b_ref, cos_ref, slo_ref, shi_ref,
                h_ref, q_ref, k_ref, v_ref, gate_ref, wo_ref, wq_ref, wkv_ref, wg_ref):
    @pl.when(pl.program_id(0) == 0)
    def _():
        wo_ref[...] = wo32_ref[0].astype(BF16)
        wkv_ref[...] = w1_ref[0, :, MIX1:SWA_QKV].astype(BF16)
        for new, old in enumerate(_paired_head_order()):
            dst = slice(new * SWA_HEAD, (new + 1) * SWA_HEAD)
            wq_ref[:, dst] = w1_ref[0, :, old * SWA_HEAD:(old + 1) * SWA_HEAD].astype(BF16)
            wg_ref[:, dst] = w1_ref[0, :, SWA_QKV + old * SWA_HEAD:SWA_QKV + (old + 1) * SWA_HEAD].astype(BF16)

    h = _gated_out0(oa_ref, ob_ref, gate0_ref, x_ref, wo_ref)
    h_ref[...] = h
    hn = _rmsnorm_rows(h, nw_ref[...]).astype(BF16)
    cos = cos_ref[...]
    slo = slo_ref[...]
    shi = shi_ref[...]
    scale = SWA_HEAD ** -0.5
    q = jnp.dot(hn, wq_ref[...], preferred_element_type=F32) + b_ref[:, :MIX1]
    kv = jnp.dot(hn, wkv_ref[...], preferred_element_type=F32) + b_ref[:, MIX1:]
    gate_ref[...] = jnp.dot(hn, wg_ref[...], preferred_element_type=F32).astype(gate_ref.dtype)
    for g in range(MIX1 // LANES):
        cols = slice(g * LANES, (g + 1) * LANES)
        q_ref[:, cols] = (_rope_group(q[:, cols], cos, slo, shi) * scale).astype(q_ref.dtype)
    for g in range(SWA_KV // LANES):
        cols = slice(g * LANES, (g + 1) * LANES)
        k_ref[:, cols] = _rope_group(kv[:, cols], cos, slo, shi).astype(k_ref.dtype)
    v_ref[...] = kv[:, SWA_KV:].astype(v_ref.dtype)


def _mid_call(oa, ob, gate0, x2, w_out0, norm_w, w_in1, b_in, cos, slo, shi, seq_len, tm):
    n_tok = x2.shape[0]
    tps = seq_len // tm
    row = lambda i: (i, 0)
    const = lambda i: (0, 0)
    pos = lambda i: (i % tps, 0)
    whole = lambda t: pl.BlockSpec((1,) + t.shape[1:], lambda i: (0, 0, 0), pipeline_mode=pl.Buffered(1))
    return pl.pallas_call(
        _mid_kernel,
        grid=(n_tok // tm,),
        in_specs=[pl.BlockSpec((tm, GLA_VAL), row),
                  pl.BlockSpec((tm, RWKV_W), row),
                  pl.BlockSpec((tm, MIX0), row),
                  pl.BlockSpec((tm, D_MODEL), row),
                  whole(w_out0),
                  pl.BlockSpec((1, D_MODEL), const),
                  whole(w_in1),
                  pl.BlockSpec((1, SWA_QKV), const),
                  pl.BlockSpec((tm, LANES), pos),
                  pl.BlockSpec((tm, LANES), pos),
                  pl.BlockSpec((tm, LANES), pos)],
        out_specs=[pl.BlockSpec((tm, D_MODEL), row),
                   pl.BlockSpec((tm, MIX1), row),
                   pl.BlockSpec((tm, SWA_KV), row),
                   pl.BlockSpec((tm, SWA_KV), row),
                   pl.BlockSpec((tm, MIX1), row)],
        out_shape=[jax.ShapeDtypeStruct((n_tok, D_MODEL), F32),
                   jax.ShapeDtypeStruct((n_tok, MIX1), BF16),
                   jax.ShapeDtypeStruct((n_tok, SWA_KV), BF16),
                   jax.ShapeDtypeStruct((n_tok, SWA_KV), BF16),
                   jax.ShapeDtypeStruct((n_tok, MIX1), BF16)],
        scratch_shapes=[pltpu.VMEM((MIX0, D_MODEL), BF16),
                        pltpu.VMEM((D_MODEL, MIX1), BF16),
                        pltpu.VMEM((D_MODEL, 2 * SWA_KV), BF16),
                        pltpu.VMEM((D_MODEL, MIX1), BF16)],
        compiler_params=_cparams(("arbitrary",)),
        name="l0_out_l1_proj",
    )(oa, ob, gate0, x2, w_out0, norm_w, w_in1, b_in, cos, slo, shi)


def _swa_kernel(sink_ref, q_ref, kc_ref, kp_ref, vc_ref, vp_ref, gate_ref, h_ref, w32_ref, b_ref, nw_ref,
                y_ref, o_ref, w_ref, *, q_blocks):
    n = pl.program_id(1)

    @pl.when((pl.program_id(0) == 0) & (n == 0))
    def _():
        for new, old in enumerate(_paired_head_order()):
            w_ref[new * SWA_HEAD:(new + 1) * SWA_HEAD, :] = (
                w32_ref[0, old * SWA_HEAD:(old + 1) * SWA_HEAD, :].astype(BF16))

    W = WINDOW
    from_prev = _iota((W, 2 * W), 0) > (_iota((W, 2 * W), 1) % W)
    no_prev = jnp.where(n > 0, 0.0, -jnp.inf)
    col_row = _iota((1, 2 * W), 1)
    out_row = _iota((LANES, W), 0)
    kv_groups = SWA_KV // LANES
    groups = MIX1 // LANES // kv_groups
    tasks = [(j, pp, pp * groups + g) for j in range(q_blocks) for pp in range(kv_groups) for g in range(groups)]
    kk, vt = {}, {}
    for j in range(q_blocks):
        for pp in range(kv_groups):
            cols = slice(pp * LANES, (pp + 1) * LANES)
            if j == 0:
                kk[j, pp] = jnp.concatenate([kp_ref[:, cols], kc_ref[:W, cols]], axis=0)
                vv = jnp.concatenate([vp_ref[:, cols], vc_ref[:W, cols]], axis=0)
            else:
                kk[j, pp] = kc_ref[(j - 1) * W:(j + 1) * W, cols]
                vv = vc_ref[(j - 1) * W:(j + 1) * W, cols]
            vt[j, pp] = vv.astype(F32).T.astype(BF16)

    def scores(j, pp, blk):
        q = q_ref[j * W:(j + 1) * W, blk * LANES:(blk + 1) * LANES]
        return lax.dot_general(kk[j, pp], _head_stack(q), (((1,), (1,)), ((), ())), preferred_element_type=F32)

    def projection_pieces(rows):
        width = D_MODEL // SWA_PROJ_PIECES
        gated = []

        def piece(c):
            def run():
                if not gated:
                    gated.append(o_ref[rows, :] * _silu_bf16(gate_ref[rows, :]))
                cols = slice(c * width, (c + 1) * width)
                y_ref[rows, cols] = (h_ref[rows, cols] + b_ref[:, cols]
                                     + jnp.dot(gated[0], w_ref[:, cols], preferred_element_type=F32))
            return run

        def norm():
            y_ref[rows, :] = _rmsnorm_rows(y_ref[rows, :], nw_ref[...])

        return [piece(c) for c in range(SWA_PROJ_PIECES)] + [norm]

    projections = []
    ahead = 8
    pending = [scores(*t) for t in tasks[:ahead]]
    for i, (j, pp, blk) in enumerate(tasks):
        st = pending.pop(0)
        if i + ahead < len(tasks):
            pending.append(scores(*tasks[i + ahead]))
        s_prev = st[:W] + no_prev if j == 0 else st[:W]
        s = jnp.where(from_prev, s_prev, st[W:])
        sink = jnp.where(col_row < W, sink_ref[2 * blk], sink_ref[2 * blk + 1])
        m = jnp.maximum(jnp.max(s, axis=0, keepdims=True), sink)
        p = jnp.exp(s - m)
        denom = jnp.sum(p, axis=0, keepdims=True) + jnp.exp(sink - m)
        pb = p.astype(BF16)
        zero = jnp.zeros_like(pb)
        p2 = jnp.concatenate([jnp.where(from_prev, pb, zero), jnp.where(from_prev, zero, pb)], axis=0)
        ot = jnp.dot(vt[j, pp], p2, preferred_element_type=F32) * (1.0 / denom)
        ot = jnp.where(out_row < HEAD, ot[:, :W], ot[:, W:])
        o_ref[j * W:(j + 1) * W, blk * LANES:(blk + 1) * LANES] = ot.T.astype(o_ref.dtype)

        last_of_block = i + 1 == len(tasks) or tasks[i + 1][0] != j
        if last_of_block and (j + 1) % SWA_PROJ_BLOCKS == 0:
            projections.extend(projection_pieces(slice((j + 1 - SWA_PROJ_BLOCKS) * W, (j + 1) * W)))
        if projections and (i % SWA_PROJ_EVERY == SWA_PROJ_EVERY - 1 or i + 1 == len(tasks)):
            projections.pop(0)()
    while projections:
        projections.pop(0)()


def _swa_call(sinks, q, k, v, gate, h1, w_out, b_out, norm_w, batch, seq_len, q_blocks):
    n_tok = q.shape[0]
    rows = q_blocks * WINDOW
    steps = seq_len // rows
    cur = lambda b, n: (b * steps + n, 0)
    prev = lambda b, n: (jnp.maximum((b * steps + n) * q_blocks - 1, 0), 0)
    const = lambda b, n: (0, 0)
    return pl.pallas_call(
        functools.partial(_swa_kernel, q_blocks=q_blocks),
        grid=(batch, steps),
        in_specs=[pl.BlockSpec(memory_space=pltpu.SMEM),
                  pl.BlockSpec((rows, MIX1), cur),
                  pl.BlockSpec((rows, SWA_KV), cur),
                  pl.BlockSpec((WINDOW, SWA_KV), prev),
                  pl.BlockSpec((rows, SWA_KV), cur),
                  pl.BlockSpec((WINDOW, SWA_KV), prev),
                  pl.BlockSpec((rows, MIX1), cur),
                  pl.BlockSpec((rows, D_MODEL), cur),
                  pl.BlockSpec((1,) + w_out.shape[1:], lambda b, n: (0, 0, 0), pipeline_mode=pl.Buffered(1)),
                  pl.BlockSpec((1, D_MODEL), const),
                  pl.BlockSpec((1, D_MODEL), const)],
        out_specs=pl.BlockSpec((rows, D_MODEL), cur),
        out_shape=jax.ShapeDtypeStruct((n_tok, D_MODEL), F32),
        scratch_shapes=[pltpu.VMEM((rows, MIX1), BF16),
                        pltpu.VMEM((MIX1, D_MODEL), BF16)],
        compiler_params=_cparams(("arbitrary", "arbitrary")),
        name="l1_swa_out",
    )(sinks, q, k, k, v, v, gate, h1, w_out, b_out, norm_w)


def _pad_rows(w, rows):
    return jnp.concatenate([w, jnp.zeros((rows - w.shape[0], w.shape[1]), w.dtype)], axis=0)


def _pair_heads(t, axis):
    shape = t.shape
    split = shape[:axis] + (SWA_KV_HEADS // 2, 2, SWA_GROUP, SWA_HEAD) + shape[axis + 1:]
    return jnp.swapaxes(t.reshape(split), axis + 1, axis + 2).reshape(shape)


def _rope_tables(seq_len):
    half = ROPE_DIMS // 2
    inv_freq = ROPE_THETA ** (-jnp.arange(half, dtype=F32) / half)
    ang = jnp.arange(seq_len).astype(F32)[:, None] * inv_freq
    trig = jnp.concatenate([jnp.cos(ang), jnp.sin(ang)], axis=1)
    d = jnp.arange(LANES) % SWA_HEAD
    src = jnp.arange(2 * half)[:, None]
    f = (d % half)[None, :]
    rot = (d < ROPE_DIMS)[None, :]
    lo = (d < half)[None, :]
    sel_cos = ((src == f) & rot).astype(F32)
    sel_lo = -((src == half + f) & lo).astype(F32)
    sel_hi = ((src == half + f) & rot & ~lo).astype(F32)
    sel = jnp.concatenate([sel_cos, sel_lo, sel_hi], axis=1)
    tab = jnp.dot(trig, sel, precision=lax.Precision.HIGHEST)
    cos = tab[:, :LANES] + (~rot).astype(F32)
    return cos, tab[:, LANES:2 * LANES], tab[:, 2 * LANES:]


def _forward(x, norm_w, w_in0, gla_gk_up, gla_gk_bias, gla_norm_w, rwkv_mu, rwkv_w0, rwkv_w_up,
             rwkv_a0, rwkv_a_up, rwkv_k_k, rwkv_k_a, rwkv_r_k, rwkv_ln_w, rwkv_ln_b, w_out0,
             w_in1, b_in1, attn_sinks, w_out1, b_out1, final_norm_w, *, tm, gla_chunks, rwkv_chunks, scan_chunks,
             swa_blocks):
    batch, seq_len, _ = x.shape
    x2 = x.reshape(batch * seq_len, D_MODEL)
    row = lambda t: t.reshape(1, -1)

    gqkv, glow, rkv, lora, gate0 = _in0_call(x2, row(norm_w[0]), w_in0, row(rwkv_mu[0]), seq_len, tm)

    up_pad = _pad_rows(gla_gk_up[0], GLOW_PAD).astype(BF16)
    o_a = _gla_call(gqkv, glow, up_pad, row(gla_gk_bias[0]), row(gla_norm_w[0]), batch, seq_len, gla_chunks)

    zeros_r = jnp.zeros((RWKV_DECAY_RANK, RWKV_W), F32)
    wup_pad = jnp.concatenate([rwkv_w_up[0], zeros_r], axis=0).astype(BF16)
    aup_pad = jnp.concatenate([zeros_r, rwkv_a_up[0]], axis=0).astype(BF16)
    rp, op, bonus, m, n = _rwkv_chunk_call(
        rkv, lora, row(rwkv_w0[0]), wup_pad, row(rwkv_a0[0]), aup_pad,
        row(rwkv_k_k[0]), row(rwkv_k_a[0]), row(rwkv_r_k[0]), batch, seq_len, rwkv_chunks)
    o_b = _rwkv_scan_call(rp, op, bonus, m, n, row(rwkv_ln_w[0]), row(rwkv_ln_b[0]),
                          batch, seq_len, scan_chunks)


    b1 = b_in1[0]
    b1p = row(jnp.concatenate([_pair_heads(b1[:MIX1], 0), b1[MIX1:]]))
    sinks_p = jnp.swapaxes(attn_sinks[0].reshape(SWA_KV_HEADS // 2, 2, SWA_GROUP), 1, 2).reshape(SWA_Q_HEADS)
    cos, slo, shi = _rope_tables(seq_len)
    h1, q, k, v, gate1 = _mid_call(o_a, o_b, gate0, x2, w_out0, row(norm_w[1]), w_in1,
                                   b1p, cos, slo, shi, seq_len, tm)
    y = _swa_call(sinks_p, q, k, v, gate1, h1, w_out1, row(b_out1[0]), row(final_norm_w),
                  batch, seq_len, swa_blocks)
    return y.reshape(batch, seq_len, D_MODEL)


def kernel(x, norm_w, w_in0, gla_gk_up, gla_gk_bias, gla_norm_w, rwkv_mu, rwkv_w0, rwkv_w_up, rwkv_a0,
           rwkv_a_up, rwkv_k_k, rwkv_k_a, rwkv_r_k, rwkv_ln_w, rwkv_ln_b, w_out0, w_in1, b_in1,
           attn_sinks, w_out1, b_out1, final_norm_w):
    return _forward(x, norm_w, w_in0, gla_gk_up, gla_gk_bias, gla_norm_w, rwkv_mu, rwkv_w0, rwkv_w_up,
                    rwkv_a0, rwkv_a_up, rwkv_k_k, rwkv_k_a, rwkv_r_k, rwkv_ln_w, rwkv_ln_b, w_out0,
                    w_in1, b_in1, attn_sinks, w_out1, b_out1, final_norm_w,
                    tm=512, gla_chunks=16, rwkv_chunks=64, scan_chunks=4, swa_blocks=4)
```

```python
import functools

import jax
import jax.numpy as jnp
from jax import lax
from jax.experimental import pallas as pl
from jax.experimental.pallas import tpu as pltpu

F32 = jnp.float32
BF16 = jnp.bfloat16

D_MODEL = 1024
NORM_EPS = 1e-5

GLA_HEADS = 4
GLA_DK = 64
GLA_DV = 128
GLA_KEY = GLA_HEADS * GLA_DK
GLA_VAL = GLA_HEADS * GLA_DV
GLA_GATE_RANK = 16
GLA_GATE_NORMALIZER = 16.0
GLA_CHUNK = 64

RWKV_HEADS = 8
RWKV_HEAD = 64
RWKV_W = RWKV_HEADS * RWKV_HEAD
RWKV_DECAY_RANK = 64
RWKV_A_RANK = 64
RWKV_LN_EPS = 64e-5
RWKV_RKV = 3 * RWKV_W
RWKV_LORA = RWKV_DECAY_RANK + RWKV_A_RANK
RWKV_SHIFT = RWKV_RKV + RWKV_LORA
RWKV_CHUNK = 64
RWKV_GROUP = 16

MIX0 = GLA_VAL + RWKV_W
GLA_QKV = 2 * GLA_KEY + GLA_VAL

SWA_Q_HEADS = 16
SWA_KV_HEADS = 4
SWA_GROUP = SWA_Q_HEADS // SWA_KV_HEADS
SWA_HEAD = 64
WINDOW = 128
ROPE_DIMS = SWA_HEAD // 4
ROPE_THETA = 500000.0
MIX1 = SWA_Q_HEADS * SWA_HEAD
SWA_KV = SWA_KV_HEADS * SWA_HEAD
SWA_QKV = MIX1 + 2 * SWA_KV
SWA_PROJ_BLOCKS = 2
SWA_PROJ_PIECES = 4
SWA_PROJ_EVERY = 3

LANES = 128
HEAD = 64
GLOW_PAD = LANES
VMEM_LIMIT = 56 * 1024 * 1024


def _cparams(sem):
    return pltpu.CompilerParams(dimension_semantics=sem, vmem_limit_bytes=VMEM_LIMIT)


def _dot(a, b):
    return jnp.dot(a.astype(BF16), b.astype(BF16), preferred_element_type=F32)


def _dot_nt(a, b):
    return lax.dot_general(a.astype(BF16), b.astype(BF16), (((1,), (1,)), ((), ())),
                           preferred_element_type=F32)


def _dot_tn(a, b):
    return lax.dot_general(a.astype(BF16), b.astype(BF16), (((0,), (0,)), ((), ())),
                           preferred_element_type=F32)


def _split2(x):
    hi = x.astype(BF16)
    lo = (x - hi.astype(F32)).astype(BF16)
    return hi, lo


def _dot_exact_rhs(a_bf16, x):
    hi, lo = _split2(x)
    return (jnp.dot(a_bf16, hi, preferred_element_type=F32)
            + jnp.dot(a_bf16, lo, preferred_element_type=F32))


def _dot_exact_lhs(x, b_bf16):
    hi, lo = _split2(x)
    return (jnp.dot(hi, b_bf16, preferred_element_type=F32)
            + jnp.dot(lo, b_bf16, preferred_element_type=F32))


def _iota(shape, dim):
    return lax.broadcasted_iota(jnp.int32, shape, dim)


def _tril_ones(n, dtype=BF16):
    return (_iota((n, n), 0) >= _iota((n, n), 1)).astype(dtype)


def _head_block_ones(n=LANES, dtype=BF16):
    return ((_iota((n, n), 0) // HEAD) == (_iota((n, n), 1) // HEAD)).astype(dtype)


def _head_stack(x):
    head = (_iota(x.shape, 1) % LANES) // HEAD
    return jnp.concatenate([jnp.where(head == 0, x, 0.0), jnp.where(head == 1, x, 0.0)], axis=0)


def _softplus(z):
    return jnp.maximum(z, 0.0) + jnp.log(1.0 + jnp.exp(-jnp.abs(z)))


def _sigmoid(z):
    return 1.0 / (1.0 + jnp.exp(-z))


def _silu_bf16(g):
    return g * (1.0 / (1.0 + jnp.exp(-g))).astype(BF16)


def _rmsnorm_rows(x, w):
    return x * lax.rsqrt(jnp.mean(x * x, axis=-1, keepdims=True) + NORM_EPS) * w


def _in0_kernel(x_ref, nw_ref, w_ref, mu_ref,
                gqkv_ref, glow_ref, rkv_ref, lora_ref, gate_ref, carry_ref, wg_ref, wl_ref, wr_ref,
                *, tiles_per_seq):
    i = pl.program_id(0)

    @pl.when(i == 0)
    def _():
        carry_ref[...] = jnp.zeros_like(carry_ref)
        def put(dst, row0, width):
            step = 4 * LANES if width % (4 * LANES) == 0 else LANES
            for c in range(0, width, step):
                dst[:, c:c + step] = w_ref[0, row0 + c:row0 + c + step, :].T.astype(BF16)

        put(wg_ref, 0, GLA_QKV)
        put(wl_ref, GLA_QKV, GLOW_PAD)
        put(wr_ref, GLA_QKV + GLA_GATE_RANK, RWKV_SHIFT + MIX0)

    xn = _rmsnorm_rows(x_ref[...], nw_ref[...]).astype(BF16)
    gqkv_ref[...] = jnp.dot(xn, wg_ref[...], preferred_element_type=F32).astype(gqkv_ref.dtype)
    glow_ref[...] = jnp.dot(xn, wl_ref[...], preferred_element_type=F32)
    rw = jnp.dot(xn, wr_ref[:, :RWKV_SHIFT], preferred_element_type=F32)
    gate_ref[...] = jnp.dot(xn, wr_ref[:, RWKV_SHIFT:], preferred_element_type=F32).astype(gate_ref.dtype)

    tm = rw.shape[0]
    first = (i % tiles_per_seq) == 0
    prev_last = jnp.where(first, 0.0, carry_ref[7:8, :])
    rolled = pltpu.roll(rw, 1, 0)
    prev = jnp.where(_iota(rw.shape, 0) == 0, prev_last, rolled)
    mixed = rw + (prev - rw) * mu_ref[...]
    rkv_ref[...] = mixed[:, :RWKV_RKV].astype(rkv_ref.dtype)
    lora_ref[...] = mixed[:, RWKV_RKV:]
    carry_ref[...] = rw[tm - 8:tm, :]


def _in0_call(x2, norm_w, w_in, mu, seq_len, tm):
    n_tok = x2.shape[0]
    row = lambda i: (i, 0)
    const = lambda i: (0, 0)
    outs = [(GLA_QKV, BF16), (GLOW_PAD, F32), (RWKV_RKV, BF16), (RWKV_LORA, F32), (MIX0, BF16)]
    return pl.pallas_call(
        functools.partial(_in0_kernel, tiles_per_seq=seq_len // tm),
        grid=(n_tok // tm,),
        in_specs=[pl.BlockSpec((tm, D_MODEL), row),
                  pl.BlockSpec((1, D_MODEL), const),
                  pl.BlockSpec((1,) + w_in.shape[1:], lambda i: (0, 0, 0), pipeline_mode=pl.Buffered(1)),
                  pl.BlockSpec((1, RWKV_SHIFT), const)],
        out_specs=[pl.BlockSpec((tm, n), row) for n, _ in outs],
        out_shape=[jax.ShapeDtypeStruct((n_tok, n), dt) for n, dt in outs],
        scratch_shapes=[pltpu.VMEM((8, RWKV_SHIFT), F32),
                        pltpu.VMEM((D_MODEL, GLA_QKV), BF16),
                        pltpu.VMEM((D_MODEL, GLOW_PAD), BF16),
                        pltpu.VMEM((D_MODEL, RWKV_SHIFT + MIX0), BF16)],
        compiler_params=_cparams(("arbitrary",)),
        name="l0_norm_proj",
    )(x2, norm_w, w_in, mu)


def _gla_kernel(q_ref, k_ref, glow_ref, v_ref, up_ref, bias_ref, nw_ref, o_ref, st_ref, *, chunks):
    c = pl.program_id(2)

    @pl.when(c == 0)
    def _():
        st_ref[...] = jnp.zeros_like(st_ref)

    C = GLA_CHUNK
    tril = _tril_ones(C)
    causal = _iota((C, LANES), 0) >= (_iota((C, LANES), 1) % HEAD)
    sr = _iota((2 * GLA_DV, LANES), 0)
    sl = _iota((2 * GLA_DV, LANES), 1)
    st_mask = (sr // GLA_DV) == (sl // HEAD)
    vl = _iota((C, 2 * GLA_DV), 1)
    scale = GLA_DK ** -0.5
    z = _dot(glow_ref[...], up_ref[...]) + bias_ref[...]
    g_all = -_softplus(-z) / GLA_GATE_NORMALIZER
    q_all = q_ref[...].astype(F32) * scale
    k_all = k_ref[...].astype(F32)
    rows = [slice(j * C, (j + 1) * C) for j in range(chunks)]
    bs = [_dot_exact_rhs(tril, g_all[rw]) for rw in rows]
    qe, ke, qb, kl, dec, vs = [], [], [], [], [], []
    for rw, b in zip(rows, bs):
        ref = b[C // 2:C // 2 + 1, :]
        b_last = b[C - 1:C, :]
        qe.append(q_all[rw] * jnp.exp(b - ref))
        ke.append(k_all[rw] * jnp.exp(ref - b))
        qb.append(q_all[rw] * jnp.exp(b))
        kl.append(k_all[rw] * jnp.exp(b_last - b))
        dec.append(jnp.exp(b_last))
        vs.append(v_ref[rw, :])
    att = [jnp.where(causal, _dot_nt(qe[j], _head_stack(ke[j])), 0.0) for j in range(chunks)]
    kv = [jnp.where(st_mask, _dot_tn(vs[j], kl[j]), 0.0) for j in range(chunks)]
    v_diag = [jnp.concatenate([jnp.where(vl < GLA_DV, vs[j], jnp.zeros_like(vs[j])),
                               jnp.where(vl >= GLA_DV, vs[j], jnp.zeros_like(vs[j]))], axis=0)
              for j in range(chunks)]
    intra = [jnp.dot(att[j].astype(BF16), v_diag[j], preferred_element_type=F32) for j in range(chunks)]
    states = [st_ref[...]]
    for j in range(chunks):
        states.append(states[j] * dec[j] + kv[j])
    st_ref[...] = states[chunks]
    for j in range(chunks):
        o = intra[j] + _dot_nt(qb[j], states[j])
        for h in range(2):
            oh = o[:, h * GLA_DV:(h + 1) * GLA_DV]
            oh = oh * lax.rsqrt(jnp.mean(oh * oh, axis=-1, keepdims=True) + NORM_EPS) * nw_ref[...]
            o_ref[rows[j], h * GLA_DV:(h + 1) * GLA_DV] = oh.astype(o_ref.dtype)


def _gla_call(gqkv, glow, up_pad, bias, norm_w, batch, seq_len, chunks):
    n_tok = gqkv.shape[0]
    tcb = chunks * GLA_CHUNK
    steps = seq_len // tcb
    pairs = GLA_KEY // LANES
    return pl.pallas_call(
        functools.partial(_gla_kernel, chunks=chunks),
        grid=(batch, pairs, steps),
        in_specs=[pl.BlockSpec((tcb, LANES), lambda b, p, c: (b * steps + c, p)),
                  pl.BlockSpec((tcb, LANES), lambda b, p, c: (b * steps + c, pairs + p)),
                  pl.BlockSpec((tcb, GLOW_PAD), lambda b, p, c: (b * steps + c, 0)),
                  pl.BlockSpec((tcb, 2 * GLA_DV), lambda b, p, c: (b * steps + c, pairs + p)),
                  pl.BlockSpec((GLOW_PAD, LANES), lambda b, p, c: (0, p)),
                  pl.BlockSpec((1, LANES), lambda b, p, c: (0, p)),
                  pl.BlockSpec((1, GLA_DV), lambda b, p, c: (0, 0))],
        out_specs=pl.BlockSpec((tcb, 2 * GLA_DV), lambda b, p, c: (b * steps + c, p)),
        out_shape=jax.ShapeDtypeStruct((n_tok, GLA_VAL), BF16),
        scratch_shapes=[pltpu.VMEM((2 * GLA_DV, LANES), F32)],
        compiler_params=_cparams(("parallel", "parallel", "arbitrary")),
        name="l0_gla",
    )(gqkv, gqkv, glow, gqkv, up_pad, bias, norm_w)


def _merge_masks(n):
    r = _iota((n, LANES), 0)
    c = _iota((n, LANES), 1) % HEAD
    masks = []
    s = 1
    while s < n:
        masks.append(((r // s) % 2 == 1) & ((c // s) == (r // s) - 1))
        s *= 2
    return (r == c).astype(F32), masks


def _run_interleaved(main, main_steps, side, side_steps):
    done = 0
    spread = max(1, (3 * main_steps) // 4)
    for i, _ in enumerate(main):
        target = -(-(i + 1) * side_steps // spread)
        while done < min(target, side_steps):
            next(side, None)
            done += 1
    for _ in side:
        pass


def _rwkv_chunk_kernel(r_ref, k_ref, v_ref, xwa_ref, w0_ref, wup_ref, a0_ref, aup_ref,
                       kk_ref, ka_ref, rk_ref,
                       rp_ref, op_ref, bonus_ref, m_ref, n_ref, *, chunks, group):
    C = RWKV_CHUNK
    tril = _tril_ones(C)
    rr = _iota((2 * C, LANES), 0)
    cc = _iota((2 * C, LANES), 1) % HEAD
    tri2 = ((rr < C) & (rr > cc)) | (rr - C >= cc)
    hb = _head_block_ones()
    sq_r = _iota((LANES, LANES), 0)
    sq_c = _iota((LANES, LANES), 1)
    same_head = (sq_r // HEAD) == (sq_c // HEAD)
    eye128 = sq_r == sq_c

    eye, merge = _merge_masks(C)
    zero = jnp.zeros((C, LANES), F32)
    n = range(group)

    def prepare(g, out):
        rows = slice(g * group * C, (g + 1) * group * C)
        r_all = r_ref[rows, :].astype(F32)
        k_all = k_ref[rows, :].astype(F32)
        v_all = v_ref[rows, :].astype(F32)
        xwa = xwa_ref[rows, :]
        w = -_softplus(-(w0_ref[...] + _dot(jnp.tanh(xwa), wup_ref[...]))) - 0.5
        lw_all = -jnp.exp(w)
        a_sig = _sigmoid(a0_ref[...] + _dot(xwa, aup_ref[...]))
        kk = k_all * kk_ref[...]
        kk = kk / jnp.maximum(jnp.sqrt(_dot_exact_lhs(kk * kk, hb)), 1e-12)
        k_all = k_all * (1.0 + (a_sig - 1.0) * ka_ref[...])
        bonus_ref[rows, :] = (_dot_exact_lhs(r_all * k_all * rk_ref[...], hb) * v_all).astype(bonus_ref.dtype)
        a_all = -kk
        b_all = kk * a_sig
        yield
        for j in n:
            rw = slice(j * C, (j + 1) * C)
            cum = _dot_exact_rhs(tril, lw_all[rw])
            cum_last = cum[C - 1:C, :]
            e_neg = jnp.exp(-cum)
            e_end = jnp.exp(cum_last - cum)
            out.append(dict(
                rt=r_all[rw] * jnp.exp(cum),
                at=a_all[rw] * jnp.exp(cum - lw_all[rw]),
                bt=b_all[rw] * e_neg,
                kt=k_all[rw] * e_neg,
                ends=jnp.concatenate([b_all[rw] * e_end, k_all[rw] * e_end], axis=0),
                v=v_all[rw],
                dec=jnp.exp(cum_last)))
            yield

    def solve(g, ops):
        lhs = [jnp.concatenate([o["at"], o["rt"]], axis=0) for o in ops]
        left = [jnp.where(tri2, _dot_nt(lhs[j], _head_stack(ops[j]["bt"])), 0.0) for j in n]
        yield
        right = [jnp.where(tri2, _dot_nt(lhs[j], _head_stack(ops[j]["kt"])), 0.0) for j in n]
        yield
        lows = [lf[:C] for lf in left]
        ts = [eye + jnp.where(merge[0], low, 0.0) for low in lows]
        for sub in merge[1:]:
            ys = [_dot(jnp.where(sub, low, 0.0), _head_stack(t)) for low, t in zip(lows, ts)]
            yield
            ts = [t + _dot(t, _head_stack(y)) for t, y in zip(ts, ys)]
            yield
        kv = [_dot(right[j], _head_stack(ops[j]["v"])) for j in n]
        yield
        wz = [_dot(ts[j], _head_stack(jnp.concatenate([ops[j]["at"], kv[j][:C]], axis=1))) for j in n]
        yield
        ro = [_dot(left[j][C:], _head_stack(wz[j])) for j in n]
        yield
        mn = [_dot_tn(ops[j]["ends"],
                      jnp.concatenate([wz[j], jnp.concatenate([zero, ops[j]["v"]], axis=1)], axis=0))
              for j in n]
        for j in n:
            c = g * group + j
            rows = slice(c * C, (c + 1) * C)
            rp_ref[rows, :] = (ops[j]["rt"] + ro[j][:, :LANES]).astype(rp_ref.dtype)
            op_ref[rows, :] = ro[j][:, LANES:] + kv[j][C:]
            m_ref[0, 0, c] = (jnp.where(eye128, ops[j]["dec"], 0.0)
                              + jnp.where(same_head, mn[j][:, :LANES], 0.0)).astype(m_ref.dtype)
            n_ref[0, 0, c] = jnp.where(same_head, mn[j][:, LANES:], 0.0).astype(n_ref.dtype)
        yield

    solve_stages = 2 + 2 * (len(merge) - 1) + 4
    groups = chunks // group
    ops = [[] for _ in range(groups + 1)]
    for _ in prepare(0, ops[0]):
        pass
    for g in range(groups):
        side = prepare(g + 1, ops[g + 1]) if g + 1 < groups else iter(())
        _run_interleaved(solve(g, ops[g]), solve_stages, side, group + 1)


def _rwkv_chunk_call(rkv, lora, w0, wup_pad, a0, aup_pad, k_k, k_a, r_k, batch, seq_len, chunks):
    n_tok = rkv.shape[0]
    tcb = chunks * RWKV_CHUNK
    steps = seq_len // tcb
    pairs = RWKV_W // LANES
    nc = seq_len // RWKV_CHUNK
    col = lambda off: (lambda b, p, c: (b * steps + c, off + p))
    par = lambda b, p, c: (0, p)
    tok = lambda b, p, c: (b * steps + c, p)
    mat = lambda b, p, c: (b, p, c, 0, 0)
    return pl.pallas_call(
        functools.partial(_rwkv_chunk_kernel, chunks=chunks, group=RWKV_GROUP),
        grid=(batch, pairs, steps),
        in_specs=[pl.BlockSpec((tcb, LANES), col(0)),
                  pl.BlockSpec((tcb, LANES), col(pairs)),
                  pl.BlockSpec((tcb, LANES), col(2 * pairs)),
                  pl.BlockSpec((tcb, RWKV_LORA), lambda b, p, c: (b * steps + c, 0)),
                  pl.BlockSpec((1, LANES), par),
                  pl.BlockSpec((RWKV_LORA, LANES), par),
                  pl.BlockSpec((1, LANES), par),
                  pl.BlockSpec((RWKV_LORA, LANES), par),
                  pl.BlockSpec((1, LANES), par),
                  pl.BlockSpec((1, LANES), par),
                  pl.BlockSpec((1, LANES), par)],
        out_specs=[pl.BlockSpec((tcb, LANES), tok),
                   pl.BlockSpec((tcb, LANES), tok),
                   pl.BlockSpec((tcb, LANES), tok),
                   pl.BlockSpec((1, 1, chunks, LANES, LANES), mat),
                   pl.BlockSpec((1, 1, chunks, LANES, LANES), mat)],
        out_shape=[jax.ShapeDtypeStruct((n_tok, RWKV_W), BF16),
                   jax.ShapeDtypeStruct((n_tok, RWKV_W), F32),
                   jax.ShapeDtypeStruct((n_tok, RWKV_W), BF16),
                   jax.ShapeDtypeStruct((batch, pairs, nc, LANES, LANES), BF16),
                   jax.ShapeDtypeStruct((batch, pairs, nc, LANES, LANES), BF16)],
        compiler_params=_cparams(("parallel", "parallel", "parallel")),
        name="l0_rwkv_chunks",
    )(rkv, rkv, rkv, lora, w0, wup_pad, a0, aup_pad, k_k, k_a, r_k)


def _rwkv_scan_kernel(rp_ref, op_ref, bonus_ref, m_ref, n_ref, lnw_ref, lnb_ref, o_ref, st_ref, *, chunks):
    c = pl.program_id(0)

    @pl.when(c == 0)
    def _():
        st_ref[...] = jnp.zeros_like(st_ref)

    C = RWKV_CHUNK
    batch = rp_ref.shape[0]
    pairs = RWKV_W // LANES
    hb = _head_block_ones()
    seqs = [(b, p) for b in range(batch) for p in range(pairs)]
    states = {bp: [st_ref[bp[0], bp[1]]] for bp in seqs}
    for j in range(chunks):
        for b, p in seqs:
            states[b, p].append(_dot(m_ref[b, p, j], states[b, p][j]) + n_ref[b, p, j])
    for b, p in seqs:
        st_ref[b, p] = states[b, p][chunks]
    cols = {bp: slice(bp[1] * LANES, (bp[1] + 1) * LANES) for bp in seqs}
    os = [jnp.concatenate([_dot(rp_ref[b, j * C:(j + 1) * C, cols[b, p]], states[b, p][j])
                           for j in range(chunks)], axis=0) + op_ref[b, :, cols[b, p]] for b, p in seqs]
    means = [_dot_exact_lhs(o, hb) * (1.0 / RWKV_HEAD) for o in os]
    ds = [o - mean for o, mean in zip(os, means)]
    variances = [_dot_exact_lhs(d * d, hb) * (1.0 / RWKV_HEAD) for d in ds]
    for (b, p), d, var in zip(seqs, ds, variances):
        c_ = cols[b, p]
        o_ref[b, :, c_] = (d * lax.rsqrt(var + RWKV_LN_EPS) * lnw_ref[:, c_] + lnb_ref[:, c_]
                           + bonus_ref[b, :, c_]).astype(o_ref.dtype)


def _rwkv_scan_call(rp, op, bonus, m, n, ln_w, ln_b, batch, seq_len, chunks):
    tcb = chunks * RWKV_CHUNK
    pairs = RWKV_W // LANES
    seq3 = lambda t: t.reshape(batch, seq_len, RWKV_W)
    tok = lambda c: (0, c, 0)
    const = lambda c: (0, 0)
    mat = lambda c: (0, 0, c, 0, 0)
    out = pl.pallas_call(
        functools.partial(_rwkv_scan_kernel, chunks=chunks),
        grid=(seq_len // tcb,),
        in_specs=[pl.BlockSpec((batch, tcb, RWKV_W), tok),
                  pl.BlockSpec((batch, tcb, RWKV_W), tok),
                  pl.BlockSpec((batch, tcb, RWKV_W), tok),
                  pl.BlockSpec((batch, pairs, chunks, LANES, LANES), mat),
                  pl.BlockSpec((batch, pairs, chunks, LANES, LANES), mat),
                  pl.BlockSpec((1, RWKV_W), const),
                  pl.BlockSpec((1, RWKV_W), const)],
        out_specs=pl.BlockSpec((batch, tcb, RWKV_W), tok),
        out_shape=jax.ShapeDtypeStruct((batch, seq_len, RWKV_W), BF16),
        scratch_shapes=[pltpu.VMEM((batch, pairs, LANES, LANES), F32)],
        compiler_params=_cparams(("arbitrary",)),
        name="l0_rwkv_scan",
    )(seq3(rp), seq3(op), seq3(bonus), m, n, ln_w, ln_b)
    return out.reshape(batch * seq_len, RWKV_W)


def _gated_out0(oa_ref, ob_ref, gate_ref, x_ref, w_ref):
    g = _silu_bf16(gate_ref[...])
    ya = oa_ref[...] * g[:, :GLA_VAL]
    yb = ob_ref[...] * g[:, GLA_VAL:]
    return (x_ref[...]
            + jnp.dot(ya, w_ref[:GLA_VAL, :], preferred_element_type=F32)
            + jnp.dot(yb, w_ref[GLA_VAL:, :], preferred_element_type=F32))


def _rope_group(x, cos, sin_lo, sin_hi):
    half = ROPE_DIMS // 2
    return x * cos + pltpu.roll(x, LANES - half, 1) * sin_lo + pltpu.roll(x, half, 1) * sin_hi


def _paired_head_order():
    return [(2 * pp + e) * SWA_GROUP + g
            for pp in range(SWA_KV_HEADS // 2) for g in range(SWA_GROUP) for e in range(2)]


def _mid_kernel(oa_ref, ob_ref, gate0_ref, x_ref, wo32_ref,
                nw_ref, w1_ref, b_ref, cos_ref, slo_ref, shi_ref,
                h_ref, q_ref, k_ref, v_ref, gate_ref, wo_ref, wq_ref, wkv_ref, wg_ref):
    @pl.when(pl.program_id(0) == 0)
    def _():
        wo_ref[...] = wo32_ref[0].astype(BF16)
        wkv_ref[...] = w1_ref[0, :, MIX1:SWA_QKV].astype(BF16)
        for new, old in enumerate(_paired_head_order()):
            dst = slice(new * SWA_HEAD, (new + 1) * SWA_HEAD)
            wq_ref[:, dst] = w1_ref[0, :, old * SWA_HEAD:(old + 1) * SWA_HEAD].astype(BF16)
            wg_ref[:, dst] = w1_ref[0, :, SWA_QKV + old * SWA_HEAD:SWA_QKV + (old + 1) * SWA_HEAD].astype(BF16)

    h = _gated_out0(oa_ref, ob_ref, gate0_ref, x_ref, wo_ref)
    h_ref[...] = h
    hn = _rmsnorm_rows(h, nw_ref[...]).astype(BF16)
    cos = cos_ref[...]
    slo = slo_ref[...]
    shi = shi_ref[...]
    scale = SWA_HEAD ** -0.5
    q = jnp.dot(hn, wq_ref[...], preferred_element_type=F32) + b_ref[:, :MIX1]
    kv = jnp.dot(hn, wkv_ref[...], preferred_element_type=F32) + b_ref[:, MIX1:]
    gate_ref[...] = jnp.dot(hn, wg_ref[...], preferred_element_type=F32).astype(gate_ref.dtype)
    for g in range(MIX1 // LANES):
        cols = slice(g * LANES, (g + 1) * LANES)
        q_ref[:, cols] = (_rope_group(q[:, cols], cos, slo, shi) * scale).astype(q_ref.dtype)
    for g in range(SWA_KV // LANES):
        cols = slice(g * LANES, (g + 1) * LANES)
        k_ref[:, cols] = _rope_group(kv[:, cols], cos, slo, shi).astype(k_ref.dtype)
    v_ref[...] = kv[:, SWA_KV:].astype(v_ref.dtype)


def _mid_call(oa, ob, gate0, x2, w_out0, norm_w, w_in1, b_in, cos, slo, shi, seq_len, tm):
    n_tok = x2.shape[0]
    tps = seq_len // tm
    row = lambda i: (i, 0)
    const = lambda i: (0, 0)
    pos = lambda i: (i % tps, 0)
    whole = lambda t: pl.BlockSpec((1,) + t.shape[1:], lambda i: (0, 0, 0), pipeline_mode=pl.Buffered(1))
    return pl.pallas_call(
        _mid_kernel,
        grid=(n_tok // tm,),
        in_specs=[pl.BlockSpec((tm, GLA_VAL), row),
                  pl.BlockSpec((tm, RWKV_W), row),
                  pl.BlockSpec((tm, MIX0), row),
                  pl.BlockSpec((tm, D_MODEL), row),
                  whole(w_out0),
                  pl.BlockSpec((1, D_MODEL), const),
                  whole(w_in1),
                  pl.BlockSpec((1, SWA_QKV), const),
                  pl.BlockSpec((tm, LANES), pos),
                  pl.BlockSpec((tm, LANES), pos),
                  pl.BlockSpec((tm, LANES), pos)],
        out_specs=[pl.BlockSpec((tm, D_MODEL), row),
                   pl.BlockSpec((tm, MIX1), row),
                   pl.BlockSpec((tm, SWA_KV), row),
                   pl.BlockSpec((tm, SWA_KV), row),
                   pl.BlockSpec((tm, MIX1), row)],
        out_shape=[jax.ShapeDtypeStruct((n_tok, D_MODEL), F32),
                   jax.ShapeDtypeStruct((n_tok, MIX1), BF16),
                   jax.ShapeDtypeStruct((n_tok, SWA_KV), BF16),
                   jax.ShapeDtypeStruct((n_tok, SWA_KV), BF16),
                   jax.ShapeDtypeStruct((n_tok, MIX1), BF16)],
        scratch_shapes=[pltpu.VMEM((MIX0, D_MODEL), BF16),
                        pltpu.VMEM((D_MODEL, MIX1), BF16),
                        pltpu.VMEM((D_MODEL, 2 * SWA_KV), BF16),
                        pltpu.VMEM((D_MODEL, MIX1), BF16)],
        compiler_params=_cparams(("arbitrary",)),
        name="l0_out_l1_proj",
    )(oa, ob, gate0, x2, w_out0, norm_w, w_in1, b_in, cos, slo, shi)


def _swa_kernel(sink_ref, q_ref, kc_ref, kp_ref, vc_ref, vp_ref, gate_ref, h_ref, w32_ref, b_ref, nw_ref,
                y_ref, o_ref, w_ref, *, q_blocks):
    n = pl.program_id(1)

    @pl.when((pl.program_id(0) == 0) & (n == 0))
    def _():
        for new, old in enumerate(_paired_head_order()):
            w_ref[new * SWA_HEAD:(new + 1) * SWA_HEAD, :] = (
                w32_ref[0, old * SWA_HEAD:(old + 1) * SWA_HEAD, :].astype(BF16))

    W = WINDOW
    from_prev = _iota((W, 2 * W), 0) > (_iota((W, 2 * W), 1) % W)
    no_prev = jnp.where(n > 0, 0.0, -jnp.inf)
    col_row = _iota((1, 2 * W), 1)
    out_row = _iota((LANES, W), 0)
    kv_groups = SWA_KV // LANES
    groups = MIX1 // LANES // kv_groups
    tasks = [(j, pp, pp * groups + g) for j in range(q_blocks) for pp in range(kv_groups) for g in range(groups)]
    kk, vt = {}, {}
    for j in range(q_blocks):
        for pp in range(kv_groups):
            cols = slice(pp * LANES, (pp + 1) * LANES)
            if j == 0:
                kk[j, pp] = jnp.concatenate([kp_ref[:, cols], kc_ref[:W, cols]], axis=0)
                vv = jnp.concatenate([vp_ref[:, cols], vc_ref[:W, cols]], axis=0)
            else:
                kk[j, pp] = kc_ref[(j - 1) * W:(j + 1) * W, cols]
                vv = vc_ref[(j - 1) * W:(j + 1) * W, cols]
            vt[j, pp] = vv.astype(F32).T.astype(BF16)

    def scores(j, pp, blk):
        q = q_ref[j * W:(j + 1) * W, blk * LANES:(blk + 1) * LANES]
        return lax.dot_general(kk[j, pp], _head_stack(q), (((1,), (1,)), ((), ())), preferred_element_type=F32)

    def projection_pieces(rows):
        width = D_MODEL // SWA_PROJ_PIECES
        gated = []

        def piece(c):
            def run():
                if not gated:
                    gated.append(o_ref[rows, :] * _silu_bf16(gate_ref[rows, :]))
                cols = slice(c * width, (c + 1) * width)
                y_ref[rows, cols] = (h_ref[rows, cols] + b_ref[:, cols]
                                     + jnp.dot(gated[0], w_ref[:, cols], preferred_element_type=F32))
            return run

        def norm():
            y_ref[rows, :] = _rmsnorm_rows(y_ref[rows, :], nw_ref[...])

        return [piece(c) for c in range(SWA_PROJ_PIECES)] + [norm]

    projections = []
    ahead = 8
    pending = [scores(*t) for t in tasks[:ahead]]
    for i, (j, pp, blk) in enumerate(tasks):
        st = pending.pop(0)
        if i + ahead < len(tasks):
            pending.append(scores(*tasks[i + ahead]))
        s_prev = st[:W] + no_prev if j == 0 else st[:W]
        s = jnp.where(from_prev, s_prev, st[W:])
        sink = jnp.where(col_row < W, sink_ref[2 * blk], sink_ref[2 * blk + 1])
        m = jnp.maximum(jnp.max(s, axis=0, keepdims=True), sink)
        p = jnp.exp(s - m)
        denom = jnp.sum(p, axis=0, keepdims=True) + jnp.exp(sink - m)
        pb = p.astype(BF16)
        zero = jnp.zeros_like(pb)
        p2 = jnp.concatenate([jnp.where(from_prev, pb, zero), jnp.where(from_prev, zero, pb)], axis=0)
        ot = jnp.dot(vt[j, pp], p2, preferred_element_type=F32) * (1.0 / denom)
        ot = jnp.where(out_row < HEAD, ot[:, :W], ot[:, W:])
        o_ref[j * W:(j + 1) * W, blk * LANES:(blk + 1) * LANES] = ot.T.astype(o_ref.dtype)

        last_of_block = i + 1 == len(tasks) or tasks[i + 1][0] != j
        if last_of_block and (j + 1) % SWA_PROJ_BLOCKS == 0:
            projections.extend(projection_pieces(slice((j + 1 - SWA_PROJ_BLOCKS) * W, (j + 1) * W)))
        if projections and (i % SWA_PROJ_EVERY == SWA_PROJ_EVERY - 1 or i + 1 == len(tasks)):
            projections.pop(0)()
    while projections:
        projections.pop(0)()


def _swa_call(sinks, q, k, v, gate, h1, w_out, b_out, norm_w, batch, seq_len, q_blocks):
    n_tok = q.shape[0]
    rows = q_blocks * WINDOW
    steps = seq_len // rows
    cur = lambda b, n: (b * steps + n, 0)
    prev = lambda b, n: (jnp.maximum((b * steps + n) * q_blocks - 1, 0), 0)
    const = lambda b, n: (0, 0)
    return pl.pallas_call(
        functools.partial(_swa_kernel, q_blocks=q_blocks),
        grid=(batch, steps),
        in_specs=[pl.BlockSpec(memory_space=pltpu.SMEM),
                  pl.BlockSpec((rows, MIX1), cur),
                  pl.BlockSpec((rows, SWA_KV), cur),
                  pl.BlockSpec((WINDOW, SWA_KV), prev),
                  pl.BlockSpec((rows, SWA_KV), cur),
                  pl.BlockSpec((WINDOW, SWA_KV), prev),
                  pl.BlockSpec((rows, MIX1), cur),
                  pl.BlockSpec((rows, D_MODEL), cur),
                  pl.BlockSpec((1,) + w_out.shape[1:], lambda b, n: (0, 0, 0), pipeline_mode=pl.Buffered(1)),
                  pl.BlockSpec((1, D_MODEL), const),
                  pl.BlockSpec((1, D_MODEL), const)],
        out_specs=pl.BlockSpec((rows, D_MODEL), cur),
        out_shape=jax.ShapeDtypeStruct((n_tok, D_MODEL), F32),
        scratch_shapes=[pltpu.VMEM((rows, MIX1), BF16),
                        pltpu.VMEM((MIX1, D_MODEL), BF16)],
        compiler_params=_cparams(("arbitrary", "arbitrary")),
        name="l1_swa_out",
    )(sinks, q, k, k, v, v, gate, h1, w_out, b_out, norm_w)


def _pad_rows(w, rows):
    return jnp.concatenate([w, jnp.zeros((rows - w.shape[0], w.shape[1]), w.dtype)], axis=0)


def _pair_heads(t, axis):
    shape = t.shape
    split = shape[:axis] + (SWA_KV_HEADS // 2, 2, SWA_GROUP, SWA_HEAD) + shape[axis + 1:]
    return jnp.swapaxes(t.reshape(split), axis + 1, axis + 2).reshape(shape)


def _rope_tables(seq_len):
    half = ROPE_DIMS // 2
    inv_freq = ROPE_THETA ** (-jnp.arange(half, dtype=F32) / half)
    ang = jnp.arange(seq_len).astype(F32)[:, None] * inv_freq
    trig = jnp.concatenate([jnp.cos(ang), jnp.sin(ang)], axis=1)
    d = jnp.arange(LANES) % SWA_HEAD
    src = jnp.arange(2 * half)[:, None]
    f = (d % half)[None, :]
    rot = (d < ROPE_DIMS)[None, :]
    lo = (d < half)[None, :]
    sel_cos = ((src == f) & rot).astype(F32)
    sel_lo = -((src == half + f) & lo).astype(F32)
    sel_hi = ((src == half + f) & rot & ~lo).astype(F32)
    sel = jnp.concatenate([sel_cos, sel_lo, sel_hi], axis=1)
    tab = jnp.dot(trig, sel, precision=lax.Precision.HIGHEST)
    cos = tab[:, :LANES] + (~rot).astype(F32)
    return cos, tab[:, LANES:2 * LANES], tab[:, 2 * LANES:]


def _forward(x, norm_w, w_in0, gla_gk_up, gla_gk_bias, gla_norm_w, rwkv_mu, rwkv_w0, rwkv_w_up,
             rwkv_a0, rwkv_a_up, rwkv_k_k, rwkv_k_a, rwkv_r_k, rwkv_ln_w, rwkv_ln_b, w_out0,
             w_in1, b_in1, attn_sinks, w_out1, b_out1, final_norm_w, *, tm, gla_chunks, rwkv_chunks, scan_chunks,
             swa_blocks):
    batch, seq_len, _ = x.shape
    x2 = x.reshape(batch * seq_len, D_MODEL)
    row = lambda t: t.reshape(1, -1)

    gqkv, glow, rkv, lora, gate0 = _in0_call(x2, row(norm_w[0]), jnp.swapaxes(w_in0, 1, 2), row(rwkv_mu[0]),
                                             seq_len, tm)

    up_pad = _pad_rows(gla_gk_up[0], GLOW_PAD).astype(BF16)
    o_a = _gla_call(gqkv, glow, up_pad, row(gla_gk_bias[0]), row(gla_norm_w[0]), batch, seq_len, gla_chunks)

    zeros_r = jnp.zeros((RWKV_DECAY_RANK, RWKV_W), F32)
    wup_pad = jnp.concatenate([rwkv_w_up[0], zeros_r], axis=0).astype(BF16)
    aup_pad = jnp.concatenate([zeros_r, rwkv_a_up[0]], axis=0).astype(BF16)
    rp, op, bonus, m, n = _rwkv_chunk_call(
        rkv, lora, row(rwkv_w0[0]), wup_pad, row(rwkv_a0[0]), aup_pad,
        row(rwkv_k_k[0]), row(rwkv_k_a[0]), row(rwkv_r_k[0]), batch, seq_len, rwkv_chunks)
    o_b = _rwkv_scan_call(rp, op, bonus, m, n, row(rwkv_ln_w[0]), row(rwkv_ln_b[0]),
                          batch, seq_len, scan_chunks)


    b1 = b_in1[0]
    b1p = row(jnp.concatenate([_pair_heads(b1[:MIX1], 0), b1[MIX1:]]))
    sinks_p = jnp.swapaxes(attn_sinks[0].reshape(SWA_KV_HEADS // 2, 2, SWA_GROUP), 1, 2).reshape(SWA_Q_HEADS)
    cos, slo, shi = _rope_tables(seq_len)
    h1, q, k, v, gate1 = _mid_call(o_a, o_b, gate0, x2, w_out0, row(norm_w[1]), w_in1,
                                   b1p, cos, slo, shi, seq_len, tm)
    y = _swa_call(sinks_p, q, k, v, gate1, h1, w_out1, row(b_out1[0]), row(final_norm_w),
                  batch, seq_len, swa_blocks)
    return y.reshape(batch, seq_len, D_MODEL)


def kernel(x, norm_w, w_in0, gla_gk_up, gla_gk_bias, gla_norm_w, rwkv_mu, rwkv_w0, rwkv_w_up, rwkv_a0,
           rwkv_a_up, rwkv_k_k, rwkv_k_a, rwkv_r_k, rwkv_ln_w, rwkv_ln_b, w_out0, w_in1, b_in1,
           attn_sinks, w_out1, b_out1, final_norm_w):
    return _forward(x, norm_w, w_in0, gla_gk_up, gla_gk_bias, gla_norm_w, rwkv_mu, rwkv_w0, rwkv_w_up,
                    rwkv_a0, rwkv_a_up, rwkv_k_k, rwkv_k_a, rwkv_r_k, rwkv_ln_w, rwkv_ln_b, w_out0,
                    w_in1, b_in1, attn_sinks, w_out1, b_out1, final_norm_w,
                    tm=512, gla_chunks=16, rwkv_chunks=64, scan_chunks=4, swa_blocks=4)
```

```python
import functools

import jax
import jax.numpy as jnp
from jax import lax
from jax.experimental import pallas as pl
from jax.experimental.pallas import tpu as pltpu

F32 = jnp.float32
BF16 = jnp.bfloat16

D_MODEL = 1024
NORM_EPS = 1e-5

GLA_HEADS = 4
GLA_DK = 64
GLA_DV = 128
GLA_KEY = GLA_HEADS * GLA_DK
GLA_VAL = GLA_HEADS * GLA_DV
GLA_GATE_RANK = 16
GLA_GATE_NORMALIZER = 16.0
GLA_CHUNK = 64

RWKV_HEADS = 8
RWKV_HEAD = 64
RWKV_W = RWKV_HEADS * RWKV_HEAD
RWKV_DECAY_RANK = 64
RWKV_A_RANK = 64
RWKV_LN_EPS = 64e-5
RWKV_RKV = 3 * RWKV_W
RWKV_LORA = RWKV_DECAY_RANK + RWKV_A_RANK
RWKV_SHIFT = RWKV_RKV + RWKV_LORA
RWKV_CHUNK = 64
RWKV_GROUP = 16

MIX0 = GLA_VAL + RWKV_W
GLA_QKV = 2 * GLA_KEY + GLA_VAL

SWA_Q_HEADS = 16
SWA_KV_HEADS = 4
SWA_GROUP = SWA_Q_HEADS // SWA_KV_HEADS
SWA_HEAD = 64
WINDOW = 128
ROPE_DIMS = SWA_HEAD // 4
ROPE_THETA = 500000.0
MIX1 = SWA_Q_HEADS * SWA_HEAD
SWA_KV = SWA_KV_HEADS * SWA_HEAD
SWA_QKV = MIX1 + 2 * SWA_KV
SWA_PROJ_BLOCKS = 2
SWA_PROJ_PIECES = 4
SWA_PROJ_EVERY = 3

LOG2_E = 1.4426950408889634
LANES = 128
HEAD = 64
GLOW_PAD = LANES
VMEM_LIMIT = 56 * 1024 * 1024


def _cparams(sem):
    return pltpu.CompilerParams(dimension_semantics=sem, vmem_limit_bytes=VMEM_LIMIT)


def _dot(a, b):
    return jnp.dot(a.astype(BF16), b.astype(BF16), preferred_element_type=F32)


def _dot_nt(a, b):
    return lax.dot_general(a.astype(BF16), b.astype(BF16), (((1,), (1,)), ((), ())),
                           preferred_element_type=F32)


def _dot_tn(a, b):
    return lax.dot_general(a.astype(BF16), b.astype(BF16), (((0,), (0,)), ((), ())),
                           preferred_element_type=F32)


def _split2(x):
    hi = x.astype(BF16)
    lo = (x - hi.astype(F32)).astype(BF16)
    return hi, lo


def _dot_exact_rhs(a_bf16, x):
    hi, lo = _split2(x)
    return (jnp.dot(a_bf16, hi, preferred_element_type=F32)
            + jnp.dot(a_bf16, lo, preferred_element_type=F32))


def _dot_exact_lhs(x, b_bf16):
    hi, lo = _split2(x)
    return (jnp.dot(hi, b_bf16, preferred_element_type=F32)
            + jnp.dot(lo, b_bf16, preferred_element_type=F32))


def _iota(shape, dim):
    return lax.broadcasted_iota(jnp.int32, shape, dim)


def _tril_ones(n, dtype=BF16):
    return (_iota((n, n), 0) >= _iota((n, n), 1)).astype(dtype)


def _head_block_ones(n=LANES, dtype=BF16):
    return ((_iota((n, n), 0) // HEAD) == (_iota((n, n), 1) // HEAD)).astype(dtype)


def _head_stack(x):
    head = (_iota(x.shape, 1) % LANES) // HEAD
    return jnp.concatenate([jnp.where(head == 0, x, 0.0), jnp.where(head == 1, x, 0.0)], axis=0)


def _softplus(z):
    return jnp.maximum(z, 0.0) + jnp.log(1.0 + jnp.exp(-jnp.abs(z)))


def _sigmoid(z):
    return 1.0 / (1.0 + jnp.exp(-z))


def _silu_bf16(g):
    return g * (1.0 / (1.0 + jnp.exp(-g))).astype(BF16)


def _rmsnorm_rows(x, w):
    return x * lax.rsqrt(jnp.mean(x * x, axis=-1, keepdims=True) + NORM_EPS) * w


def _in0_kernel(x_ref, nw_ref, w_ref, mu_ref,
                gqkv_ref, glow_ref, rkv_ref, lora_ref, gate_ref, carry_ref, wg_ref, wl_ref, wr_ref,
                *, tiles_per_seq):
    i = pl.program_id(0)

    @pl.when(i == 0)
    def _():
        carry_ref[...] = jnp.zeros_like(carry_ref)
        def put(dst, row0, width):
            step = 4 * LANES if width % (4 * LANES) == 0 else LANES
            for c in range(0, width, step):
                dst[:, c:c + step] = w_ref[0, row0 + c:row0 + c + step, :].T.astype(BF16)

        put(wg_ref, 0, GLA_QKV)
        put(wl_ref, GLA_QKV, GLOW_PAD)
        put(wr_ref, GLA_QKV + GLA_GATE_RANK, RWKV_SHIFT + MIX0)

    xn = _rmsnorm_rows(x_ref[...], nw_ref[...]).astype(BF16)
    gqkv_ref[...] = jnp.dot(xn, wg_ref[...], preferred_element_type=F32).astype(gqkv_ref.dtype)
    glow_ref[...] = jnp.dot(xn, wl_ref[...], preferred_element_type=F32)
    rw = jnp.dot(xn, wr_ref[:, :RWKV_SHIFT], preferred_element_type=F32)
    gate_ref[...] = jnp.dot(xn, wr_ref[:, RWKV_SHIFT:], preferred_element_type=F32).astype(gate_ref.dtype)

    tm = rw.shape[0]
    first = (i % tiles_per_seq) == 0
    prev_last = jnp.where(first, 0.0, carry_ref[7:8, :])
    rolled = pltpu.roll(rw, 1, 0)
    prev = jnp.where(_iota(rw.shape, 0) == 0, prev_last, rolled)
    mixed = rw + (prev - rw) * mu_ref[...]
    rkv_ref[...] = mixed[:, :RWKV_RKV].astype(rkv_ref.dtype)
    lora_ref[...] = mixed[:, RWKV_RKV:]
    carry_ref[...] = rw[tm - 8:tm, :]


def _in0_call(x2, norm_w, w_in, mu, seq_len, tm):
    n_tok = x2.shape[0]
    row = lambda i: (i, 0)
    const = lambda i: (0, 0)
    outs = [(GLA_QKV, BF16), (GLOW_PAD, F32), (RWKV_RKV, BF16), (RWKV_LORA, F32), (MIX0, BF16)]
    return pl.pallas_call(
        functools.partial(_in0_kernel, tiles_per_seq=seq_len // tm),
        grid=(n_tok // tm,),
        in_specs=[pl.BlockSpec((tm, D_MODEL), row),
                  pl.BlockSpec((1, D_MODEL), const),
                  pl.BlockSpec((1,) + w_in.shape[1:], lambda i: (0, 0, 0), pipeline_mode=pl.Buffered(1)),
                  pl.BlockSpec((1, RWKV_SHIFT), const)],
        out_specs=[pl.BlockSpec((tm, n), row) for n, _ in outs],
        out_shape=[jax.ShapeDtypeStruct((n_tok, n), dt) for n, dt in outs],
        scratch_shapes=[pltpu.VMEM((8, RWKV_SHIFT), F32),
                        pltpu.VMEM((D_MODEL, GLA_QKV), BF16),
                        pltpu.VMEM((D_MODEL, GLOW_PAD), BF16),
                        pltpu.VMEM((D_MODEL, RWKV_SHIFT + MIX0), BF16)],
        compiler_params=_cparams(("arbitrary",)),
        name="l0_norm_proj",
    )(x2, norm_w, w_in, mu)


def _gla_kernel(q_ref, k_ref, glow_ref, v_ref, up_ref, bias_ref, nw_ref, o_ref, st_ref, *, chunks):
    c = pl.program_id(2)

    @pl.when(c == 0)
    def _():
        st_ref[...] = jnp.zeros_like(st_ref)

    C = GLA_CHUNK
    tril = _tril_ones(C)
    causal = _iota((C, LANES), 0) >= (_iota((C, LANES), 1) % HEAD)
    sr = _iota((2 * GLA_DV, LANES), 0)
    sl = _iota((2 * GLA_DV, LANES), 1)
    st_mask = (sr // GLA_DV) == (sl // HEAD)
    vl = _iota((C, 2 * GLA_DV), 1)
    scale = GLA_DK ** -0.5
    z = _dot(glow_ref[...], up_ref[...]) + bias_ref[...]
    g_all = -_softplus(-z) * (LOG2_E / GLA_GATE_NORMALIZER)
    q_all = q_ref[...].astype(F32) * scale
    k_all = k_ref[...].astype(F32)
    rows = [slice(j * C, (j + 1) * C) for j in range(chunks)]
    bs = [_dot_exact_rhs(tril, g_all[rw]) for rw in rows]
    qe, ke, qb, kl, dec, vs = [], [], [], [], [], []
    for rw, b in zip(rows, bs):
        ref = b[C // 2:C // 2 + 1, :]
        b_last = b[C - 1:C, :]
        qe.append(q_all[rw] * jnp.exp2(b - ref))
        ke.append(k_all[rw] * jnp.exp2(ref - b))
        qb.append(q_all[rw] * jnp.exp2(b))
        kl.append(k_all[rw] * jnp.exp2(b_last - b))
        dec.append(jnp.exp2(b_last))
        vs.append(v_ref[rw, :])
    att = [jnp.where(causal, _dot_nt(qe[j], _head_stack(ke[j])), 0.0) for j in range(chunks)]
    kv = [jnp.where(st_mask, _dot_tn(vs[j], kl[j]), 0.0) for j in range(chunks)]
    v_diag = [jnp.concatenate([jnp.where(vl < GLA_DV, vs[j], jnp.zeros_like(vs[j])),
                               jnp.where(vl >= GLA_DV, vs[j], jnp.zeros_like(vs[j]))], axis=0)
              for j in range(chunks)]
    intra = [jnp.dot(att[j].astype(BF16), v_diag[j], preferred_element_type=F32) for j in range(chunks)]
    states = [st_ref[...]]
    for j in range(chunks):
        states.append(states[j] * dec[j] + kv[j])
    st_ref[...] = states[chunks]
    for j in range(chunks):
        o = intra[j] + _dot_nt(qb[j], states[j])
        for h in range(2):
            oh = o[:, h * GLA_DV:(h + 1) * GLA_DV]
            oh = oh * lax.rsqrt(jnp.mean(oh * oh, axis=-1, keepdims=True) + NORM_EPS) * nw_ref[...]
            o_ref[rows[j], h * GLA_DV:(h + 1) * GLA_DV] = oh.astype(o_ref.dtype)


def _gla_call(gqkv, glow, up_pad, bias, norm_w, batch, seq_len, chunks):
    n_tok = gqkv.shape[0]
    tcb = chunks * GLA_CHUNK
    steps = seq_len // tcb
    pairs = GLA_KEY // LANES
    return pl.pallas_call(
        functools.partial(_gla_kernel, chunks=chunks),
        grid=(batch, pairs, steps),
        in_specs=[pl.BlockSpec((tcb, LANES), lambda b, p, c: (b * steps + c, p)),
                  pl.BlockSpec((tcb, LANES), lambda b, p, c: (b * steps + c, pairs + p)),
                  pl.BlockSpec((tcb, GLOW_PAD), lambda b, p, c: (b * steps + c, 0)),
                  pl.BlockSpec((tcb, 2 * GLA_DV), lambda b, p, c: (b * steps + c, pairs + p)),
                  pl.BlockSpec((GLOW_PAD, LANES), lambda b, p, c: (0, p)),
                  pl.BlockSpec((1, LANES), lambda b, p, c: (0, p)),
                  pl.BlockSpec((1, GLA_DV), lambda b, p, c: (0, 0))],
        out_specs=pl.BlockSpec((tcb, 2 * GLA_DV), lambda b, p, c: (b * steps + c, p)),
        out_shape=jax.ShapeDtypeStruct((n_tok, GLA_VAL), BF16),
        scratch_shapes=[pltpu.VMEM((2 * GLA_DV, LANES), F32)],
        compiler_params=_cparams(("parallel", "parallel", "arbitrary")),
        name="l0_gla",
    )(gqkv, gqkv, glow, gqkv, up_pad, bias, norm_w)


def _merge_masks(n):
    r = _iota((n, LANES), 0)
    c = _iota((n, LANES), 1) % HEAD
    masks = []
    s = 1
    while s < n:
        masks.append(((r // s) % 2 == 1) & ((c // s) == (r // s) - 1))
        s *= 2
    return (r == c).astype(F32), masks


def _run_interleaved(main, main_steps, side, side_steps):
    done = 0
    spread = max(1, (3 * main_steps) // 4)
    for i, _ in enumerate(main):
        target = -(-(i + 1) * side_steps // spread)
        while done < min(target, side_steps):
            next(side, None)
            done += 1
    for _ in side:
        pass


def _rwkv_chunk_kernel(r_ref, k_ref, v_ref, xwa_ref, w0_ref, wup_ref, a0_ref, aup_ref,
                       kk_ref, ka_ref, rk_ref,
                       rp_ref, op_ref, bonus_ref, m_ref, n_ref, *, chunks, group):
    C = RWKV_CHUNK
    tril = _tril_ones(C)
    rr = _iota((2 * C, LANES), 0)
    cc = _iota((2 * C, LANES), 1) % HEAD
    tri2 = ((rr < C) & (rr > cc)) | (rr - C >= cc)
    hb = _head_block_ones()
    sq_r = _iota((LANES, LANES), 0)
    sq_c = _iota((LANES, LANES), 1)
    same_head = (sq_r // HEAD) == (sq_c // HEAD)
    eye128 = sq_r == sq_c

    eye, merge = _merge_masks(C)
    zero = jnp.zeros((C, LANES), F32)
    n = range(group)

    def prepare(g, out):
        rows = slice(g * group * C, (g + 1) * group * C)
        r_all = r_ref[rows, :].astype(F32)
        k_all = k_ref[rows, :].astype(F32)
        v_all = v_ref[rows, :].astype(F32)
        xwa = xwa_ref[rows, :]
        w = -_softplus(-(w0_ref[...] + _dot(jnp.tanh(xwa), wup_ref[...]))) - 0.5
        lw_all = jnp.exp(w) * -LOG2_E
        a_sig = _sigmoid(a0_ref[...] + _dot(xwa, aup_ref[...]))
        kk = k_all * kk_ref[...]
        kk = kk / jnp.maximum(jnp.sqrt(_dot_exact_lhs(kk * kk, hb)), 1e-12)
        k_all = k_all * (1.0 + (a_sig - 1.0) * ka_ref[...])
        bonus_ref[rows, :] = (_dot_exact_lhs(r_all * k_all * rk_ref[...], hb) * v_all).astype(bonus_ref.dtype)
        a_all = -kk
        b_all = kk * a_sig
        yield
        for j in n:
            rw = slice(j * C, (j + 1) * C)
            cum = _dot_exact_rhs(tril, lw_all[rw])
            cum_last = cum[C - 1:C, :]
            e_neg = jnp.exp2(-cum)
            e_end = jnp.exp2(cum_last - cum)
            out.append(dict(
                rt=r_all[rw] * jnp.exp2(cum),
                at=a_all[rw] * jnp.exp2(cum - lw_all[rw]),
                bt=b_all[rw] * e_neg,
                kt=k_all[rw] * e_neg,
                ends=jnp.concatenate([b_all[rw] * e_end, k_all[rw] * e_end], axis=0),
                v=v_all[rw],
                dec=jnp.exp2(cum_last)))
            yield

    def solve(g, ops):
        lhs = [jnp.concatenate([o["at"], o["rt"]], axis=0) for o in ops]
        left = [jnp.where(tri2, _dot_nt(lhs[j], _head_stack(ops[j]["bt"])), 0.0) for j in n]
        yield
        right = [jnp.where(tri2, _dot_nt(lhs[j], _head_stack(ops[j]["kt"])), 0.0) for j in n]
        yield
        lows = [lf[:C] for lf in left]
        ts = [eye + jnp.where(merge[0], low, 0.0) for low in lows]
        for sub in merge[1:]:
            ys = [_dot(jnp.where(sub, low, 0.0), _head_stack(t)) for low, t in zip(lows, ts)]
            yield
            ts = [t + _dot(t, _head_stack(y)) for t, y in zip(ts, ys)]
            yield
        kv = [_dot(right[j], _head_stack(ops[j]["v"])) for j in n]
        yield
        wz = [_dot(ts[j], _head_stack(jnp.concatenate([ops[j]["at"], kv[j][:C]], axis=1))) for j in n]
        yield
        ro = [_dot(left[j][C:], _head_stack(wz[j])) for j in n]
        yield
        mn = [_dot_tn(ops[j]["ends"],
                      jnp.concatenate([wz[j], jnp.concatenate([zero, ops[j]["v"]], axis=1)], axis=0))
              for j in n]
        for j in n:
            c = g * group + j
            rows = slice(c * C, (c + 1) * C)
            rp_ref[rows, :] = (ops[j]["rt"] + ro[j][:, :LANES]).astype(rp_ref.dtype)
            op_ref[rows, :] = ro[j][:, LANES:] + kv[j][C:]
            m_ref[0, 0, c] = (jnp.where(eye128, ops[j]["dec"], 0.0)
                              + jnp.where(same_head, mn[j][:, :LANES], 0.0)).astype(m_ref.dtype)
            n_ref[0, 0, c] = jnp.where(same_head, mn[j][:, LANES:], 0.0).astype(n_ref.dtype)
        yield

    solve_stages = 2 + 2 * (len(merge) - 1) + 4
    groups = chunks // group
    ops = [[] for _ in range(groups + 1)]
    for _ in prepare(0, ops[0]):
        pass
    for g in range(groups):
        side = prepare(g + 1, ops[g + 1]) if g + 1 < groups else iter(())
        _run_interleaved(solve(g, ops[g]), solve_stages, side, group + 1)


def _rwkv_chunk_call(rkv, lora, w0, wup_pad, a0, aup_pad, k_k, k_a, r_k, batch, seq_len, chunks):
    n_tok = rkv.shape[0]
    tcb = chunks * RWKV_CHUNK
    steps = seq_len // tcb
    pairs = RWKV_W // LANES
    nc = seq_len // RWKV_CHUNK
    col = lambda off: (lambda b, p, c: (b * steps + c, off + p))
    par = lambda b, p, c: (0, p)
    tok = lambda b, p, c: (b * steps + c, p)
    mat = lambda b, p, c: (b, p, c, 0, 0)
    return pl.pallas_call(
        functools.partial(_rwkv_chunk_kernel, chunks=chunks, group=RWKV_GROUP),
        grid=(batch, pairs, steps),
        in_specs=[pl.BlockSpec((tcb, LANES), col(0)),
                  pl.BlockSpec((tcb, LANES), col(pairs)),
                  pl.BlockSpec((tcb, LANES), col(2 * pairs)),
                  pl.BlockSpec((tcb, RWKV_LORA), lambda b, p, c: (b * steps + c, 0)),
                  pl.BlockSpec((1, LANES), par),
                  pl.BlockSpec((RWKV_LORA, LANES), par),
                  pl.BlockSpec((1, LANES), par),
                  pl.BlockSpec((RWKV_LORA, LANES), par),
                  pl.BlockSpec((1, LANES), par),
                  pl.BlockSpec((1, LANES), par),
                  pl.BlockSpec((1, LANES), par)],
        out_specs=[pl.BlockSpec((tcb, LANES), tok),
                   pl.BlockSpec((tcb, LANES), tok),
                   pl.BlockSpec((tcb, LANES), tok),
                   pl.BlockSpec((1, 1, chunks, LANES, LANES), mat),
                   pl.BlockSpec((1, 1, chunks, LANES, LANES), mat)],
        out_shape=[jax.ShapeDtypeStruct((n_tok, RWKV_W), BF16),
                   jax.ShapeDtypeStruct((n_tok, RWKV_W), F32),
                   jax.ShapeDtypeStruct((n_tok, RWKV_W), BF16),
                   jax.ShapeDtypeStruct((batch, pairs, nc, LANES, LANES), BF16),
                   jax.ShapeDtypeStruct((batch, pairs, nc, LANES, LANES), BF16)],
        compiler_params=_cparams(("parallel", "parallel", "parallel")),
        name="l0_rwkv_chunks",
    )(rkv, rkv, rkv, lora, w0, wup_pad, a0, aup_pad, k_k, k_a, r_k)


def _rwkv_scan_kernel(rp_ref, op_ref, bonus_ref, m_ref, n_ref, lnw_ref, lnb_ref, o_ref, st_ref, *, chunks):
    c = pl.program_id(0)

    @pl.when(c == 0)
    def _():
        st_ref[...] = jnp.zeros_like(st_ref)

    C = RWKV_CHUNK
    batch = rp_ref.shape[0]
    pairs = RWKV_W // LANES
    hb = _head_block_ones()
    seqs = [(b, p) for b in range(batch) for p in range(pairs)]
    states = {bp: [st_ref[bp[0], bp[1]]] for bp in seqs}
    for j in range(chunks):
        for b, p in seqs:
            states[b, p].append(_dot(m_ref[b, p, j], states[b, p][j]) + n_ref[b, p, j])
    for b, p in seqs:
        st_ref[b, p] = states[b, p][chunks]
    cols = {bp: slice(bp[1] * LANES, (bp[1] + 1) * LANES) for bp in seqs}
    os = [jnp.concatenate([_dot(rp_ref[b, j * C:(j + 1) * C, cols[b, p]], states[b, p][j])
                           for j in range(chunks)], axis=0) + op_ref[b, :, cols[b, p]] for b, p in seqs]
    means = [_dot_exact_lhs(o, hb) * (1.0 / RWKV_HEAD) for o in os]
    ds = [o - mean for o, mean in zip(os, means)]
    variances = [_dot_exact_lhs(d * d, hb) * (1.0 / RWKV_HEAD) for d in ds]
    for (b, p), d, var in zip(seqs, ds, variances):
        c_ = cols[b, p]
        o_ref[b, :, c_] = (d * lax.rsqrt(var + RWKV_LN_EPS) * lnw_ref[:, c_] + lnb_ref[:, c_]
                           + bonus_ref[b, :, c_]).astype(o_ref.dtype)


def _rwkv_scan_call(rp, op, bonus, m, n, ln_w, ln_b, batch, seq_len, chunks):
    tcb = chunks * RWKV_CHUNK
    pairs = RWKV_W // LANES
    seq3 = lambda t: t.reshape(batch, seq_len, RWKV_W)
    tok = lambda c: (0, c, 0)
    const = lambda c: (0, 0)
    mat = lambda c: (0, 0, c, 0, 0)
    out = pl.pallas_call(
        functools.partial(_rwkv_scan_kernel, chunks=chunks),
        grid=(seq_len // tcb,),
        in_specs=[pl.BlockSpec((batch, tcb, RWKV_W), tok),
                  pl.BlockSpec((batch, tcb, RWKV_W), tok),
                  pl.BlockSpec((batch, tcb, RWKV_W), tok),
                  pl.BlockSpec((batch, pairs, chunks, LANES, LANES), mat),
                  pl.BlockSpec((batch, pairs, chunks, LANES, LANES), mat),
                  pl.BlockSpec((1, RWKV_W), const),
                  pl.BlockSpec((1, RWKV_W), const)],
        out_specs=pl.BlockSpec((batch, tcb, RWKV_W), tok),
        out_shape=jax.ShapeDtypeStruct((batch, seq_len, RWKV_W), BF16),
        scratch_shapes=[pltpu.VMEM((batch, pairs, LANES, LANES), F32)],
        compiler_params=_cparams(("arbitrary",)),
        name="l0_rwkv_scan",
    )(seq3(rp), seq3(op), seq3(bonus), m, n, ln_w, ln_b)
    return out.reshape(batch * seq_len, RWKV_W)


def _gated_out0(oa_ref, ob_ref, gate_ref, x_ref, w_ref):
    g = _silu_bf16(gate_ref[...])
    ya = oa_ref[...] * g[:, :GLA_VAL]
    yb = ob_ref[...] * g[:, GLA_VAL:]
    return (x_ref[...]
            + jnp.dot(ya, w_ref[:GLA_VAL, :], preferred_element_type=F32)
            + jnp.dot(yb, w_ref[GLA_VAL:, :], preferred_element_type=F32))


def _rope_group(x, cos, sin_lo, sin_hi):
    half = ROPE_DIMS // 2
    return x * cos + pltpu.roll(x, LANES - half, 1) * sin_lo + pltpu.roll(x, half, 1) * sin_hi


def _paired_head_order():
    return [(2 * pp + e) * SWA_GROUP + g
            for pp in range(SWA_KV_HEADS // 2) for g in range(SWA_GROUP) for e in range(2)]


def _mid_kernel(oa_ref, ob_ref, gate0_ref, x_ref, wo32_ref,
                nw_ref, w1_ref, b_ref, cos_ref, slo_ref, shi_ref,
                h_ref, q_ref, k_ref, v_ref, gate_ref, wo_ref, wq_ref, wkv_ref, wg_ref):
    @pl.when(pl.program_id(0) == 0)
    def _():
        wo_ref[...] = wo32_ref[0].astype(BF16)
        wkv_ref[...] = w1_ref[0, :, MIX1:SWA_QKV].astype(BF16)
        for new, old in enumerate(_paired_head_order()):
            dst = slice(new * SWA_HEAD, (new + 1) * SWA_HEAD)
            wq_ref[:, dst] = w1_ref[0, :, old * SWA_HEAD:(old + 1) * SWA_HEAD].astype(BF16)
            wg_ref[:, dst] = w1_ref[0, :, SWA_QKV + old * SWA_HEAD:SWA_QKV + (old + 1) * SWA_HEAD].astype(BF16)

    h = _gated_out0(oa_ref, ob_ref, gate0_ref, x_ref, wo_ref)
    h_ref[...] = h
    hn = _rmsnorm_rows(h, nw_ref[...]).astype(BF16)
    cos = cos_ref[...]
    slo = slo_ref[...]
    shi = shi_ref[...]
    scale = SWA_HEAD ** -0.5 * LOG2_E
    q = jnp.dot(hn, wq_ref[...], preferred_element_type=F32) + b_ref[:, :MIX1]
    kv = jnp.dot(hn, wkv_ref[...], preferred_element_type=F32) + b_ref[:, MIX1:]
    gate_ref[...] = jnp.dot(hn, wg_ref[...], preferred_element_type=F32).astype(gate_ref.dtype)
    for g in range(MIX1 // LANES):
        cols = slice(g * LANES, (g + 1) * LANES)
        q_ref[:, cols] = (_rope_group(q[:, cols], cos, slo, shi) * scale).astype(q_ref.dtype)
    for g in range(SWA_KV // LANES):
        cols = slice(g * LANES, (g + 1) * LANES)
        k_ref[:, cols] = _rope_group(kv[:, cols], cos, slo, shi).astype(k_ref.dtype)
    v_ref[...] = kv[:, SWA_KV:].astype(v_ref.dtype)


def _mid_call(oa, ob, gate0, x2, w_out0, norm_w, w_in1, b_in, cos, slo, shi, seq_len, tm):
    n_tok = x2.shape[0]
    tps = seq_len // tm
    row = lambda i: (i, 0)
    const = lambda i: (0, 0)
    pos = lambda i: (i % tps, 0)
    whole = lambda t: pl.BlockSpec((1,) + t.shape[1:], lambda i: (0, 0, 0), pipeline_mode=pl.Buffered(1))
    return pl.pallas_call(
        _mid_kernel,
        grid=(n_tok // tm,),
        in_specs=[pl.BlockSpec((tm, GLA_VAL), row),
                  pl.BlockSpec((tm, RWKV_W), row),
                  pl.BlockSpec((tm, MIX0), row),
                  pl.BlockSpec((tm, D_MODEL), row),
                  whole(w_out0),
                  pl.BlockSpec((1, D_MODEL), const),
                  whole(w_in1),
                  pl.BlockSpec((1, SWA_QKV), const),
                  pl.BlockSpec((tm, LANES), pos),
                  pl.BlockSpec((tm, LANES), pos),
                  pl.BlockSpec((tm, LANES), pos)],
        out_specs=[pl.BlockSpec((tm, D_MODEL), row),
                   pl.BlockSpec((tm, MIX1), row),
                   pl.BlockSpec((tm, SWA_KV), row),
                   pl.BlockSpec((tm, SWA_KV), row),
                   pl.BlockSpec((tm, MIX1), row)],
        out_shape=[jax.ShapeDtypeStruct((n_tok, D_MODEL), F32),
                   jax.ShapeDtypeStruct((n_tok, MIX1), BF16),
                   jax.ShapeDtypeStruct((n_tok, SWA_KV), BF16),
                   jax.ShapeDtypeStruct((n_tok, SWA_KV), BF16),
                   jax.ShapeDtypeStruct((n_tok, MIX1), BF16)],
        scratch_shapes=[pltpu.VMEM((MIX0, D_MODEL), BF16),
                        pltpu.VMEM((D_MODEL, MIX1), BF16),
                        pltpu.VMEM((D_MODEL, 2 * SWA_KV), BF16),
                        pltpu.VMEM((D_MODEL, MIX1), BF16)],
        compiler_params=_cparams(("arbitrary",)),
        name="l0_out_l1_proj",
    )(oa, ob, gate0, x2, w_out0, norm_w, w_in1, b_in, cos, slo, shi)


def _swa_kernel(sink_ref, q_ref, kc_ref, kp_ref, vc_ref, vp_ref, gate_ref, h_ref, w32_ref, b_ref, nw_ref,
                y_ref, o_ref, w_ref, *, q_blocks):
    n = pl.program_id(1)

    @pl.when((pl.program_id(0) == 0) & (n == 0))
    def _():
        for new, old in enumerate(_paired_head_order()):
            w_ref[new * SWA_HEAD:(new + 1) * SWA_HEAD, :] = (
                w32_ref[0, old * SWA_HEAD:(old + 1) * SWA_HEAD, :].astype(BF16))

    W = WINDOW
    from_prev = _iota((W, 2 * W), 0) > (_iota((W, 2 * W), 1) % W)
    no_prev = jnp.where(n > 0, 0.0, -jnp.inf)
    col_row = _iota((1, 2 * W), 1)
    out_row = _iota((LANES, W), 0)
    kv_groups = SWA_KV // LANES
    groups = MIX1 // LANES // kv_groups
    tasks = [(j, pp, pp * groups + g) for j in range(q_blocks) for pp in range(kv_groups) for g in range(groups)]
    kk, vt = {}, {}
    for j in range(q_blocks):
        for pp in range(kv_groups):
            cols = slice(pp * LANES, (pp + 1) * LANES)
            if j == 0:
                kk[j, pp] = jnp.concatenate([kp_ref[:, cols], kc_ref[:W, cols]], axis=0)
                vv = jnp.concatenate([vp_ref[:, cols], vc_ref[:W, cols]], axis=0)
            else:
                kk[j, pp] = kc_ref[(j - 1) * W:(j + 1) * W, cols]
                vv = vc_ref[(j - 1) * W:(j + 1) * W, cols]
            vt[j, pp] = vv.astype(F32).T.astype(BF16)

    def scores(j, pp, blk):
        q = q_ref[j * W:(j + 1) * W, blk * LANES:(blk + 1) * LANES]
        return lax.dot_general(kk[j, pp], _head_stack(q), (((1,), (1,)), ((), ())), preferred_element_type=F32)

    def projection_pieces(rows):
        width = D_MODEL // SWA_PROJ_PIECES
        gated = []

        def piece(c):
            def run():
                if not gated:
                    gated.append(o_ref[rows, :] * _silu_bf16(gate_ref[rows, :]))
                cols = slice(c * width, (c + 1) * width)
                y_ref[rows, cols] = (h_ref[rows, cols] + b_ref[:, cols]
                                     + jnp.dot(gated[0], w_ref[:, cols], preferred_element_type=F32))
            return run

        def norm():
            y_ref[rows, :] = _rmsnorm_rows(y_ref[rows, :], nw_ref[...])

        return [piece(c) for c in range(SWA_PROJ_PIECES)] + [norm]

    projections = []
    ahead = 8
    pending = [scores(*t) for t in tasks[:ahead]]
    for i, (j, pp, blk) in enumerate(tasks):
        st = pending.pop(0)
        if i + ahead < len(tasks):
            pending.append(scores(*tasks[i + ahead]))
        s_prev = st[:W] + no_prev if j == 0 else st[:W]
        s = jnp.where(from_prev, s_prev, st[W:])
        sink = jnp.where(col_row < W, sink_ref[2 * blk], sink_ref[2 * blk + 1]) * LOG2_E
        m = jnp.maximum(jnp.max(s, axis=0, keepdims=True), sink)
        p = jnp.exp2(s - m)
        denom = jnp.sum(p, axis=0, keepdims=True) + jnp.exp2(sink - m)
        pb = p.astype(BF16)
        zero = jnp.zeros_like(pb)
        p2 = jnp.concatenate([jnp.where(from_prev, pb, zero), jnp.where(from_prev, zero, pb)], axis=0)
        ot = jnp.dot(vt[j, pp], p2, preferred_element_type=F32) * (1.0 / denom)
        ot = jnp.where(out_row < HEAD, ot[:, :W], ot[:, W:])
        o_ref[j * W:(j + 1) * W, blk * LANES:(blk + 1) * LANES] = ot.T.astype(o_ref.dtype)

        last_of_block = i + 1 == len(tasks) or tasks[i + 1][0] != j
        if last_of_block and (j + 1) % SWA_PROJ_BLOCKS == 0:
            projections.extend(projection_pieces(slice((j + 1 - SWA_PROJ_BLOCKS) * W, (j + 1) * W)))
        if projections and (i % SWA_PROJ_EVERY == SWA_PROJ_EVERY - 1 or i + 1 == len(tasks)):
            projections.pop(0)()
    while projections:
        projections.pop(0)()


def _swa_call(sinks, q, k, v, gate, h1, w_out, b_out, norm_w, batch, seq_len, q_blocks):
    n_tok = q.shape[0]
    rows = q_blocks * WINDOW
    steps = seq_len // rows
    cur = lambda b, n: (b * steps + n, 0)
    prev = lambda b, n: (jnp.maximum((b * steps + n) * q_blocks - 1, 0), 0)
    const = lambda b, n: (0, 0)
    return pl.pallas_call(
        functools.partial(_swa_kernel, q_blocks=q_blocks),
        grid=(batch, steps),
        in_specs=[pl.BlockSpec(memory_space=pltpu.SMEM),
                  pl.BlockSpec((rows, MIX1), cur),
                  pl.BlockSpec((rows, SWA_KV), cur),
                  pl.BlockSpec((WINDOW, SWA_KV), prev),
                  pl.BlockSpec((rows, SWA_KV), cur),
                  pl.BlockSpec((WINDOW, SWA_KV), prev),
                  pl.BlockSpec((rows, MIX1), cur),
                  pl.BlockSpec((rows, D_MODEL), cur),
                  pl.BlockSpec((1,) + w_out.shape[1:], lambda b, n: (0, 0, 0), pipeline_mode=pl.Buffered(1)),
                  pl.BlockSpec((1, D_MODEL), const),
                  pl.BlockSpec((1, D_MODEL), const)],
        out_specs=pl.BlockSpec((rows, D_MODEL), cur),
        out_shape=jax.ShapeDtypeStruct((n_tok, D_MODEL), F32),
        scratch_shapes=[pltpu.VMEM((rows, MIX1), BF16),
                        pltpu.VMEM((MIX1, D_MODEL), BF16)],
        compiler_params=_cparams(("arbitrary", "arbitrary")),
        name="l1_swa_out",
    )(sinks, q, k, k, v, v, gate, h1, w_out, b_out, norm_w)


def _pad_rows(w, rows):
    return jnp.concatenate([w, jnp.zeros((rows - w.shape[0], w.shape[1]), w.dtype)], axis=0)


def _pair_heads(t, axis):
    shape = t.shape
    split = shape[:axis] + (SWA_KV_HEADS // 2, 2, SWA_GROUP, SWA_HEAD) + shape[axis + 1:]
    return jnp.swapaxes(t.reshape(split), axis + 1, axis + 2).reshape(shape)


def _rope_tables(seq_len):
    half = ROPE_DIMS // 2
    inv_freq = ROPE_THETA ** (-jnp.arange(half, dtype=F32) / half)
    ang = jnp.arange(seq_len).astype(F32)[:, None] * inv_freq
    trig = jnp.concatenate([jnp.cos(ang), jnp.sin(ang)], axis=1)
    d = jnp.arange(LANES) % SWA_HEAD
    src = jnp.arange(2 * half)[:, None]
    f = (d % half)[None, :]
    rot = (d < ROPE_DIMS)[None, :]
    lo = (d < half)[None, :]
    sel_cos = ((src == f) & rot).astype(F32)
    sel_lo = -((src == half + f) & lo).astype(F32)
    sel_hi = ((src == half + f) & rot & ~lo).astype(F32)
    sel = jnp.concatenate([sel_cos, sel_lo, sel_hi], axis=1)
    tab = jnp.dot(trig, sel, precision=lax.Precision.HIGHEST)
    cos = tab[:, :LANES] + (~rot).astype(F32)
    return cos, tab[:, LANES:2 * LANES], tab[:, 2 * LANES:]


def _forward(x, norm_w, w_in0, gla_gk_up, gla_gk_bias, gla_norm_w, rwkv_mu, rwkv_w0, rwkv_w_up,
             rwkv_a0, rwkv_a_up, rwkv_k_k, rwkv_k_a, rwkv_r_k, rwkv_ln_w, rwkv_ln_b, w_out0,
             w_in1, b_in1, attn_sinks, w_out1, b_out1, final_norm_w, *, tm, gla_chunks, rwkv_chunks, scan_chunks,
             swa_blocks):
    batch, seq_len, _ = x.shape
    x2 = x.reshape(batch * seq_len, D_MODEL)
    row = lambda t: t.reshape(1, -1)

    gqkv, glow, rkv, lora, gate0 = _in0_call(x2, row(norm_w[0]), jnp.swapaxes(w_in0, 1, 2), row(rwkv_mu[0]),
                                             seq_len, tm)

    up_pad = _pad_rows(gla_gk_up[0], GLOW_PAD).astype(BF16)
    o_a = _gla_call(gqkv, glow, up_pad, row(gla_gk_bias[0]), row(gla_norm_w[0]), batch, seq_len, gla_chunks)

    zeros_r = jnp.zeros((RWKV_DECAY_RANK, RWKV_W), F32)
    wup_pad = jnp.concatenate([rwkv_w_up[0], zeros_r], axis=0).astype(BF16)
    aup_pad = jnp.concatenate([zeros_r, rwkv_a_up[0]], axis=0).astype(BF16)
    rp, op, bonus, m, n = _rwkv_chunk_call(
        rkv, lora, row(rwkv_w0[0]), wup_pad, row(rwkv_a0[0]), aup_pad,
        row(rwkv_k_k[0]), row(rwkv_k_a[0]), row(rwkv_r_k[0]), batch, seq_len, rwkv_chunks)
    o_b = _rwkv_scan_call(rp, op, bonus, m, n, row(rwkv_ln_w[0]), row(rwkv_ln_b[0]),
                          batch, seq_len, scan_chunks)


    b1 = b_in1[0]
    b1p = row(jnp.concatenate([_pair_heads(b1[:MIX1], 0), b1[MIX1:]]))
    sinks_p = jnp.swapaxes(attn_sinks[0].reshape(SWA_KV_HEADS // 2, 2, SWA_GROUP), 1, 2).reshape(SWA_Q_HEADS)
    cos, slo, shi = _rope_tables(seq_len)
    h1, q, k, v, gate1 = _mid_call(o_a, o_b, gate0, x2, w_out0, row(norm_w[1]), w_in1,
                                   b1p, cos, slo, shi, seq_len, tm)
    y = _swa_call(sinks_p, q, k, v, gate1, h1, w_out1, row(b_out1[0]), row(final_norm_w),
                  batch, seq_len, swa_blocks)
    return y.reshape(batch, seq_len, D_MODEL)


def kernel(x, norm_w, w_in0, gla_gk_up, gla_gk_bias, gla_norm_w, rwkv_mu, rwkv_w0, rwkv_w_up, rwkv_a0,
           rwkv_a_up, rwkv_k_k, rwkv_k_a, rwkv_r_k, rwkv_ln_w, rwkv_ln_b, w_out0, w_in1, b_in1,
           attn_sinks, w_out1, b_out1, final_norm_w):
    return _forward(x, norm_w, w_in0, gla_gk_up, gla_gk_bias, gla_norm_w, rwkv_mu, rwkv_w0, rwkv_w_up,
                    rwkv_a0, rwkv_a_up, rwkv_k_k, rwkv_k_a, rwkv_r_k, rwkv_ln_w, rwkv_ln_b, w_out0,
                    w_in1, b_in1, attn_sinks, w_out1, b_out1, final_norm_w,
                    tm=512, gla_chunks=16, rwkv_chunks=64, scan_chunks=4, swa_blocks=4)
```

```python
import functools

import jax
import jax.numpy as jnp
from jax import lax
from jax.experimental import pallas as pl
from jax.experimental.pallas import tpu as pltpu

F32 = jnp.float32
BF16 = jnp.bfloat16

D_MODEL = 1024
NORM_EPS = 1e-5

GLA_HEADS = 4
GLA_DK = 64
GLA_DV = 128
GLA_KEY = GLA_HEADS * GLA_DK
GLA_VAL = GLA_HEADS * GLA_DV
GLA_GATE_RANK = 16
GLA_GATE_NORMALIZER = 16.0
GLA_CHUNK = 64

RWKV_HEADS = 8
RWKV_HEAD = 64
RWKV_W = RWKV_HEADS * RWKV_HEAD
RWKV_DECAY_RANK = 64
RWKV_A_RANK = 64
RWKV_LN_EPS = 64e-5
RWKV_RKV = 3 * RWKV_W
RWKV_LORA = RWKV_DECAY_RANK + RWKV_A_RANK
RWKV_SHIFT = RWKV_RKV + RWKV_LORA
RWKV_CHUNK = 64
RWKV_GROUP = 16

MIX0 = GLA_VAL + RWKV_W
GLA_QKV = 2 * GLA_KEY + GLA_VAL

SWA_Q_HEADS = 16
SWA_KV_HEADS = 4
SWA_GROUP = SWA_Q_HEADS // SWA_KV_HEADS
SWA_HEAD = 64
WINDOW = 128
ROPE_DIMS = SWA_HEAD // 4
ROPE_THETA = 500000.0
MIX1 = SWA_Q_HEADS * SWA_HEAD
SWA_KV = SWA_KV_HEADS * SWA_HEAD
SWA_QKV = MIX1 + 2 * SWA_KV
SWA_PROJ_BLOCKS = 2
SWA_PROJ_PIECES = 4
SWA_PROJ_EVERY = 3

LOG2_E = 1.4426950408889634
LANES = 128
HEAD = 64
GLOW_PAD = LANES
VMEM_LIMIT = 56 * 1024 * 1024


def _cparams(sem):
    return pltpu.CompilerParams(dimension_semantics=sem, vmem_limit_bytes=VMEM_LIMIT)


def _dot(a, b):
    return jnp.dot(a.astype(BF16), b.astype(BF16), preferred_element_type=F32)


def _dot_nt(a, b):
    return lax.dot_general(a.astype(BF16), b.astype(BF16), (((1,), (1,)), ((), ())),
                           preferred_element_type=F32)


def _dot_tn(a, b):
    return lax.dot_general(a.astype(BF16), b.astype(BF16), (((0,), (0,)), ((), ())),
                           preferred_element_type=F32)


def _split2(x):
    hi = x.astype(BF16)
    lo = (x - hi.astype(F32)).astype(BF16)
    return hi, lo


def _dot_exact_rhs(a_bf16, x):
    hi, lo = _split2(x)
    return (jnp.dot(a_bf16, hi, preferred_element_type=F32)
            + jnp.dot(a_bf16, lo, preferred_element_type=F32))


def _dot_exact_lhs(x, b_bf16):
    hi, lo = _split2(x)
    return (jnp.dot(hi, b_bf16, preferred_element_type=F32)
            + jnp.dot(lo, b_bf16, preferred_element_type=F32))


def _iota(shape, dim):
    return lax.broadcasted_iota(jnp.int32, shape, dim)


def _tril_ones(n, dtype=BF16):
    return (_iota((n, n), 0) >= _iota((n, n), 1)).astype(dtype)


def _head_block_ones(n=LANES, dtype=BF16):
    return ((_iota((n, n), 0) // HEAD) == (_iota((n, n), 1) // HEAD)).astype(dtype)


def _head_stack(x):
    head = (_iota(x.shape, 1) % LANES) // HEAD
    return jnp.concatenate([jnp.where(head == 0, x, 0.0), jnp.where(head == 1, x, 0.0)], axis=0)


def _softplus(z):
    return jnp.maximum(z, 0.0) + jnp.log(1.0 + jnp.exp(-jnp.abs(z)))


def _sigmoid(z):
    return 1.0 / (1.0 + jnp.exp(-z))


def _silu_bf16(g):
    return g * (1.0 / (1.0 + jnp.exp(-g))).astype(BF16)


def _rmsnorm_rows(x, w):
    return x * lax.rsqrt(jnp.mean(x * x, axis=-1, keepdims=True) + NORM_EPS) * w


def _in0_kernel(x_ref, nw_ref, w_ref, mu_ref,
                gqkv_ref, glow_ref, rkv_ref, lora_ref, gate_ref, carry_ref, wg_ref, wl_ref, wr_ref,
                *, tiles_per_seq):
    i = pl.program_id(0)

    @pl.when(i == 0)
    def _():
        carry_ref[...] = jnp.zeros_like(carry_ref)
        def put(dst, row0, width):
            step = 4 * LANES if width % (4 * LANES) == 0 else LANES
            for c in range(0, width, step):
                dst[:, c:c + step] = w_ref[0, row0 + c:row0 + c + step, :].T.astype(BF16)

        put(wg_ref, 0, GLA_QKV)
        put(wl_ref, GLA_QKV, GLOW_PAD)
        put(wr_ref, GLA_QKV + GLA_GATE_RANK, RWKV_SHIFT + MIX0)

    xn = _rmsnorm_rows(x_ref[...], nw_ref[...]).astype(BF16)
    gqkv_ref[...] = jnp.dot(xn, wg_ref[...], preferred_element_type=F32).astype(gqkv_ref.dtype)
    glow_ref[...] = jnp.dot(xn, wl_ref[...], preferred_element_type=F32)
    rw = jnp.dot(xn, wr_ref[:, :RWKV_SHIFT], preferred_element_type=F32)
    gate_ref[...] = jnp.dot(xn, wr_ref[:, RWKV_SHIFT:], preferred_element_type=F32).astype(gate_ref.dtype)

    tm = rw.shape[0]
    first = (i % tiles_per_seq) == 0
    prev_last = jnp.where(first, 0.0, carry_ref[7:8, :])
    rolled = pltpu.roll(rw, 1, 0)
    prev = jnp.where(_iota(rw.shape, 0) == 0, prev_last, rolled)
    mixed = rw + (prev - rw) * mu_ref[...]
    rkv_ref[...] = mixed[:, :RWKV_RKV].astype(rkv_ref.dtype)
    lora_ref[...] = mixed[:, RWKV_RKV:]
    carry_ref[...] = rw[tm - 8:tm, :]


def _in0_call(x2, norm_w, w_in, mu, seq_len, tm):
    n_tok = x2.shape[0]
    row = lambda i: (i, 0)
    const = lambda i: (0, 0)
    outs = [(GLA_QKV, BF16), (GLOW_PAD, F32), (RWKV_RKV, BF16), (RWKV_LORA, F32), (MIX0, BF16)]
    return pl.pallas_call(
        functools.partial(_in0_kernel, tiles_per_seq=seq_len // tm),
        grid=(n_tok // tm,),
        in_specs=[pl.BlockSpec((tm, D_MODEL), row),
                  pl.BlockSpec((1, D_MODEL), const),
                  pl.BlockSpec((1,) + w_in.shape[1:], lambda i: (0, 0, 0), pipeline_mode=pl.Buffered(1)),
                  pl.BlockSpec((1, RWKV_SHIFT), const)],
        out_specs=[pl.BlockSpec((tm, n), row) for n, _ in outs],
        out_shape=[jax.ShapeDtypeStruct((n_tok, n), dt) for n, dt in outs],
        scratch_shapes=[pltpu.VMEM((8, RWKV_SHIFT), F32),
                        pltpu.VMEM((D_MODEL, GLA_QKV), BF16),
                        pltpu.VMEM((D_MODEL, GLOW_PAD), BF16),
                        pltpu.VMEM((D_MODEL, RWKV_SHIFT + MIX0), BF16)],
        compiler_params=_cparams(("arbitrary",)),
        name="l0_norm_proj",
    )(x2, norm_w, w_in, mu)


def _gla_kernel(q_ref, k_ref, glow_ref, v_ref, up_ref, bias_ref, nw_ref, o_ref, st_ref, *, chunks):
    c = pl.program_id(2)

    @pl.when(c == 0)
    def _():
        st_ref[...] = jnp.zeros_like(st_ref)

    C = GLA_CHUNK
    tril = _tril_ones(C)
    causal = _iota((C, LANES), 0) >= (_iota((C, LANES), 1) % HEAD)
    sr = _iota((2 * GLA_DV, LANES), 0)
    sl = _iota((2 * GLA_DV, LANES), 1)
    st_mask = (sr // GLA_DV) == (sl // HEAD)
    vl = _iota((C, 2 * GLA_DV), 1)
    scale = GLA_DK ** -0.5
    z = _dot(glow_ref[...], up_ref[...]) + bias_ref[...]
    g_all = -_softplus(-z) * (LOG2_E / GLA_GATE_NORMALIZER)
    q_all = q_ref[...].astype(F32) * scale
    k_all = k_ref[...].astype(F32)
    rows = [slice(j * C, (j + 1) * C) for j in range(chunks)]
    bs = [_dot_exact_rhs(tril, g_all[rw]) for rw in rows]
    qe, ke, qb, kl, dec, vs = [], [], [], [], [], []
    for rw, b in zip(rows, bs):
        ref = b[C // 2:C // 2 + 1, :]
        b_last = b[C - 1:C, :]
        qe.append(q_all[rw] * jnp.exp2(b - ref))
        ke.append(k_all[rw] * jnp.exp2(ref - b))
        qb.append(q_all[rw] * jnp.exp2(b))
        kl.append(k_all[rw] * jnp.exp2(b_last - b))
        dec.append(jnp.exp2(b_last))
        vs.append(v_ref[rw, :])
    att = [jnp.where(causal, _dot_nt(qe[j], _head_stack(ke[j])), 0.0) for j in range(chunks)]
    kv = [jnp.where(st_mask, _dot_tn(vs[j], kl[j]), 0.0) for j in range(chunks)]
    v_diag = [jnp.concatenate([jnp.where(vl < GLA_DV, vs[j], jnp.zeros_like(vs[j])),
                               jnp.where(vl >= GLA_DV, vs[j], jnp.zeros_like(vs[j]))], axis=0)
              for j in range(chunks)]
    intra = [jnp.dot(att[j].astype(BF16), v_diag[j], preferred_element_type=F32) for j in range(chunks)]
    states = [st_ref[...]]
    for j in range(chunks):
        states.append(states[j] * dec[j] + kv[j])
    st_ref[...] = states[chunks]
    for j in range(chunks):
        o = intra[j] + _dot_nt(qb[j], states[j])
        for h in range(2):
            oh = o[:, h * GLA_DV:(h + 1) * GLA_DV]
            oh = oh * lax.rsqrt(jnp.mean(oh * oh, axis=-1, keepdims=True) + NORM_EPS) * nw_ref[...]
            o_ref[rows[j], h * GLA_DV:(h + 1) * GLA_DV] = oh.astype(o_ref.dtype)


def _gla_call(gqkv, glow, up_pad, bias, norm_w, batch, seq_len, chunks):
    n_tok = gqkv.shape[0]
    tcb = chunks * GLA_CHUNK
    steps = seq_len // tcb
    pairs = GLA_KEY // LANES
    return pl.pallas_call(
        functools.partial(_gla_kernel, chunks=chunks),
        grid=(batch, pairs, steps),
        in_specs=[pl.BlockSpec((tcb, LANES), lambda b, p, c: (b * steps + c, p)),
                  pl.BlockSpec((tcb, LANES), lambda b, p, c: (b * steps + c, pairs + p)),
                  pl.BlockSpec((tcb, GLOW_PAD), lambda b, p, c: (b * steps + c, 0)),
                  pl.BlockSpec((tcb, 2 * GLA_DV), lambda b, p, c: (b * steps + c, pairs + p)),
                  pl.BlockSpec((GLOW_PAD, LANES), lambda b, p, c: (0, p)),
                  pl.BlockSpec((1, LANES), lambda b, p, c: (0, p)),
                  pl.BlockSpec((1, GLA_DV), lambda b, p, c: (0, 0))],
        out_specs=pl.BlockSpec((tcb, 2 * GLA_DV), lambda b, p, c: (b * steps + c, p)),
        out_shape=jax.ShapeDtypeStruct((n_tok, GLA_VAL), BF16),
        scratch_shapes=[pltpu.VMEM((2 * GLA_DV, LANES), F32)],
        compiler_params=_cparams(("parallel", "parallel", "arbitrary")),
        name="l0_gla",
    )(gqkv, gqkv, glow, gqkv, up_pad, bias, norm_w)


def _merge_masks(n):
    r = _iota((n, LANES), 0)
    c = _iota((n, LANES), 1) % HEAD
    masks = []
    s = 1
    while s < n:
        masks.append(((r // s) % 2 == 1) & ((c // s) == (r // s) - 1))
        s *= 2
    return (r == c).astype(F32), masks


def _run_interleaved(main, main_steps, side, side_steps):
    done = 0
    spread = max(1, (3 * main_steps) // 4)
    for i, _ in enumerate(main):
        target = -(-(i + 1) * side_steps // spread)
        while done < min(target, side_steps):
            next(side, None)
            done += 1
    for _ in side:
        pass


def _rwkv_chunk_kernel(r_ref, k_ref, v_ref, xwa_ref, w0_ref, wup_ref, a0_ref, aup_ref,
                       kk_ref, ka_ref, rk_ref,
                       rp_ref, op_ref, bonus_ref, m_ref, n_ref, *, chunks, group):
    C = RWKV_CHUNK
    tril = _tril_ones(C)
    rr = _iota((2 * C, LANES), 0)
    cc = _iota((2 * C, LANES), 1) % HEAD
    tri2 = ((rr < C) & (rr > cc)) | (rr - C >= cc)
    hb = _head_block_ones()
    sq_r = _iota((LANES, LANES), 0)
    sq_c = _iota((LANES, LANES), 1)
    same_head = (sq_r // HEAD) == (sq_c // HEAD)
    eye128 = sq_r == sq_c

    eye, merge = _merge_masks(C)
    zero = jnp.zeros((C, LANES), F32)
    n = range(group)

    def prepare(g, out):
        rows = slice(g * group * C, (g + 1) * group * C)
        r_all = r_ref[rows, :].astype(F32)
        k_all = k_ref[rows, :].astype(F32)
        v_all = v_ref[rows, :].astype(F32)
        xwa = xwa_ref[rows, :]
        w = -_softplus(-(w0_ref[...] + _dot(jnp.tanh(xwa), wup_ref[...]))) - 0.5
        lw_all = jnp.exp(w) * -LOG2_E
        a_sig = _sigmoid(a0_ref[...] + _dot(xwa, aup_ref[...]))
        kk = k_all * kk_ref[...]
        kk = kk / jnp.maximum(jnp.sqrt(_dot_exact_lhs(kk * kk, hb)), 1e-12)
        k_all = k_all * (1.0 + (a_sig - 1.0) * ka_ref[...])
        bonus_ref[rows, :] = (_dot_exact_lhs(r_all * k_all * rk_ref[...], hb) * v_all).astype(bonus_ref.dtype)
        a_all = -kk
        b_all = kk * a_sig
        yield
        for j in n:
            rw = slice(j * C, (j + 1) * C)
            cum = _dot_exact_rhs(tril, lw_all[rw])
            cum_last = cum[C - 1:C, :]
            e_neg = jnp.exp2(-cum)
            e_end = jnp.exp2(cum_last - cum)
            out.append(dict(
                rt=r_all[rw] * jnp.exp2(cum),
                at=a_all[rw] * jnp.exp2(cum - lw_all[rw]),
                bt=b_all[rw] * e_neg,
                kt=k_all[rw] * e_neg,
                ends=jnp.concatenate([b_all[rw] * e_end, k_all[rw] * e_end], axis=0),
                v=v_all[rw],
                dec=jnp.exp2(cum_last)))
            yield

    def solve(g, ops):
        lhs = [jnp.concatenate([o["at"], o["rt"]], axis=0) for o in ops]
        left = [jnp.where(tri2, _dot_nt(lhs[j], _head_stack(ops[j]["bt"])), 0.0) for j in n]
        yield
        right = [jnp.where(tri2, _dot_nt(lhs[j], _head_stack(ops[j]["kt"])), 0.0) for j in n]
        yield
        lows = [lf[:C] for lf in left]
        ts = [eye + jnp.where(merge[0], low, 0.0) for low in lows]
        for sub in merge[1:]:
            ys = [_dot(jnp.where(sub, low, 0.0), _head_stack(t)) for low, t in zip(lows, ts)]
            yield
            ts = [t + _dot(t, _head_stack(y)) for t, y in zip(ts, ys)]
            yield
        kv = [_dot(right[j], _head_stack(ops[j]["v"])) for j in n]
        yield
        wz = [_dot(ts[j], _head_stack(jnp.concatenate([ops[j]["at"], kv[j][:C]], axis=1))) for j in n]
        yield
        ro = [_dot(left[j][C:], _head_stack(wz[j])) for j in n]
        yield
        mn = [_dot_tn(ops[j]["ends"],
                      jnp.concatenate([wz[j], jnp.concatenate([zero, ops[j]["v"]], axis=1)], axis=0))
              for j in n]
        for j in n:
            c = g * group + j
            rows = slice(c * C, (c + 1) * C)
            rp_ref[rows, :] = (ops[j]["rt"] + ro[j][:, :LANES]).astype(rp_ref.dtype)
            op_ref[rows, :] = ro[j][:, LANES:] + kv[j][C:]
            m_ref[0, 0, c] = (jnp.where(eye128, ops[j]["dec"], 0.0)
                              + jnp.where(same_head, mn[j][:, :LANES], 0.0)).astype(m_ref.dtype)
            n_ref[0, 0, c] = jnp.where(same_head, mn[j][:, LANES:], 0.0).astype(n_ref.dtype)
        yield

    solve_stages = 2 + 2 * (len(merge) - 1) + 4
    groups = chunks // group
    ops = [[] for _ in range(groups + 1)]
    for _ in prepare(0, ops[0]):
        pass
    for g in range(groups):
        side = prepare(g + 1, ops[g + 1]) if g + 1 < groups else iter(())
        _run_interleaved(solve(g, ops[g]), solve_stages, side, group + 1)


def _rwkv_chunk_call(rkv, lora, w0, wup_pad, a0, aup_pad, k_k, k_a, r_k, batch, seq_len, chunks):
    n_tok = rkv.shape[0]
    tcb = chunks * RWKV_CHUNK
    steps = seq_len // tcb
    pairs = RWKV_W // LANES
    nc = seq_len // RWKV_CHUNK
    col = lambda off: (lambda b, p, c: (b * steps + c, off + p))
    par = lambda b, p, c: (0, p)
    tok = lambda b, p, c: (b * steps + c, p)
    mat = lambda b, p, c: (b, p, c, 0, 0)
    return pl.pallas_call(
        functools.partial(_rwkv_chunk_kernel, chunks=chunks, group=RWKV_GROUP),
        grid=(batch, pairs, steps),
        in_specs=[pl.BlockSpec((tcb, LANES), col(0)),
                  pl.BlockSpec((tcb, LANES), col(pairs)),
                  pl.BlockSpec((tcb, LANES), col(2 * pairs)),
                  pl.BlockSpec((tcb, RWKV_LORA), lambda b, p, c: (b * steps + c, 0)),
                  pl.BlockSpec((1, LANES), par),
                  pl.BlockSpec((RWKV_LORA, LANES), par),
                  pl.BlockSpec((1, LANES), par),
                  pl.BlockSpec((RWKV_LORA, LANES), par),
                  pl.BlockSpec((1, LANES), par),
                  pl.BlockSpec((1, LANES), par),
                  pl.BlockSpec((1, LANES), par)],
        out_specs=[pl.BlockSpec((tcb, LANES), tok),
                   pl.BlockSpec((tcb, LANES), tok),
                   pl.BlockSpec((tcb, LANES), tok),
                   pl.BlockSpec((1, 1, chunks, LANES, LANES), mat),
                   pl.BlockSpec((1, 1, chunks, LANES, LANES), mat)],
        out_shape=[jax.ShapeDtypeStruct((n_tok, RWKV_W), BF16),
                   jax.ShapeDtypeStruct((n_tok, RWKV_W), F32),
                   jax.ShapeDtypeStruct((n_tok, RWKV_W), BF16),
                   jax.ShapeDtypeStruct((batch, pairs, nc, LANES, LANES), BF16),
                   jax.ShapeDtypeStruct((batch, pairs, nc, LANES, LANES), BF16)],
        compiler_params=_cparams(("parallel", "parallel", "parallel")),
        name="l0_rwkv_chunks",
    )(rkv, rkv, rkv, lora, w0, wup_pad, a0, aup_pad, k_k, k_a, r_k)


def _rwkv_scan_kernel(rp_ref, op_ref, bonus_ref, m_ref, n_ref, lnw_ref, lnb_ref, o_ref, st_ref, *, chunks):
    c = pl.program_id(0)

    @pl.when(c == 0)
    def _():
        st_ref[...] = jnp.zeros_like(st_ref)

    C = RWKV_CHUNK
    batch = rp_ref.shape[0]
    pairs = RWKV_W // LANES
    hb = _head_block_ones()
    seqs = [(b, p) for b in range(batch) for p in range(pairs)]
    states = {bp: [st_ref[bp[0], bp[1]]] for bp in seqs}
    for j in range(chunks):
        for b, p in seqs:
            states[b, p].append(_dot(m_ref[b, p, j], states[b, p][j]) + n_ref[b, p, j])
    for b, p in seqs:
        st_ref[b, p] = states[b, p][chunks]
    cols = {bp: slice(bp[1] * LANES, (bp[1] + 1) * LANES) for bp in seqs}
    os = [jnp.concatenate([_dot(rp_ref[b, j * C:(j + 1) * C, cols[b, p]], states[b, p][j])
                           for j in range(chunks)], axis=0) + op_ref[b, :, cols[b, p]] for b, p in seqs]
    means = [_dot_exact_lhs(o, hb) * (1.0 / RWKV_HEAD) for o in os]
    ds = [o - mean for o, mean in zip(os, means)]
    variances = [_dot_exact_lhs(d * d, hb) * (1.0 / RWKV_HEAD) for d in ds]
    for (b, p), d, var in zip(seqs, ds, variances):
        c_ = cols[b, p]
        o_ref[b, :, c_] = (d * lax.rsqrt(var + RWKV_LN_EPS) * lnw_ref[:, c_] + lnb_ref[:, c_]
                           + bonus_ref[b, :, c_]).astype(o_ref.dtype)


def _rwkv_scan_call(rp, op, bonus, m, n, ln_w, ln_b, batch, seq_len, chunks):
    tcb = chunks * RWKV_CHUNK
    pairs = RWKV_W // LANES
    seq3 = lambda t: t.reshape(batch, seq_len, RWKV_W)
    tok = lambda c: (0, c, 0)
    const = lambda c: (0, 0)
    mat = lambda c: (0, 0, c, 0, 0)
    out = pl.pallas_call(
        functools.partial(_rwkv_scan_kernel, chunks=chunks),
        grid=(seq_len // tcb,),
        in_specs=[pl.BlockSpec((batch, tcb, RWKV_W), tok),
                  pl.BlockSpec((batch, tcb, RWKV_W), tok),
                  pl.BlockSpec((batch, tcb, RWKV_W), tok),
                  pl.BlockSpec((batch, pairs, chunks, LANES, LANES), mat),
                  pl.BlockSpec((batch, pairs, chunks, LANES, LANES), mat),
                  pl.BlockSpec((1, RWKV_W), const),
                  pl.BlockSpec((1, RWKV_W), const)],
        out_specs=pl.BlockSpec((batch, tcb, RWKV_W), tok),
        out_shape=jax.ShapeDtypeStruct((batch, seq_len, RWKV_W), BF16),
        scratch_shapes=[pltpu.VMEM((batch, pairs, LANES, LANES), F32)],
        compiler_params=_cparams(("arbitrary",)),
        name="l0_rwkv_scan",
    )(seq3(rp), seq3(op), seq3(bonus), m, n, ln_w, ln_b)
    return out.reshape(batch * seq_len, RWKV_W)


def _gated_out0(oa_ref, ob_ref, gate_ref, x_ref, w_ref):
    g = _silu_bf16(gate_ref[...])
    ya = oa_ref[...] * g[:, :GLA_VAL]
    yb = ob_ref[...] * g[:, GLA_VAL:]
    return (x_ref[...]
            + jnp.dot(ya, w_ref[:GLA_VAL, :], preferred_element_type=F32)
            + jnp.dot(yb, w_ref[GLA_VAL:, :], preferred_element_type=F32))


def _rope_group(x, cos, sin_lo, sin_hi):
    half = ROPE_DIMS // 2
    return x * cos + pltpu.roll(x, LANES - half, 1) * sin_lo + pltpu.roll(x, half, 1) * sin_hi


def _paired_head_order():
    return [(2 * pp + e) * SWA_GROUP + g
            for pp in range(SWA_KV_HEADS // 2) for g in range(SWA_GROUP) for e in range(2)]


def _mid_kernel(oa_ref, ob_ref, gate0_ref, x_ref, wo32_ref,
                nw_ref, w1_ref, b_ref, cos_ref, slo_ref, shi_ref,
                h_ref, q_ref, k_ref, v_ref, gate_ref, wo_ref, wq_ref, wkv_ref, wg_ref):
    @pl.when(pl.program_id(0) == 0)
    def _():
        wo_ref[...] = wo32_ref[0].astype(BF16)
        wkv_ref[...] = w1_ref[0, :, MIX1:SWA_QKV].astype(BF16)
        for new, old in enumerate(_paired_head_order()):
            dst = slice(new * SWA_HEAD, (new + 1) * SWA_HEAD)
            wq_ref[:, dst] = w1_ref[0, :, old * SWA_HEAD:(old + 1) * SWA_HEAD].astype(BF16)
            wg_ref[:, dst] = w1_ref[0, :, SWA_QKV + old * SWA_HEAD:SWA_QKV + (old + 1) * SWA_HEAD].astype(BF16)

    h = _gated_out0(oa_ref, ob_ref, gate0_ref, x_ref, wo_ref)
    h_ref[...] = h
    hn = _rmsnorm_rows(h, nw_ref[...]).astype(BF16)
    cos = cos_ref[...]
    slo = slo_ref[...]
    shi = shi_ref[...]
    scale = SWA_HEAD ** -0.5 * LOG2_E
    q = jnp.dot(hn, wq_ref[...], preferred_element_type=F32) + b_ref[:, :MIX1]
    kv = jnp.dot(hn, wkv_ref[...], preferred_element_type=F32) + b_ref[:, MIX1:]
    gate_ref[...] = jnp.dot(hn, wg_ref[...], preferred_element_type=F32).astype(gate_ref.dtype)
    for g in range(MIX1 // LANES):
        cols = slice(g * LANES, (g + 1) * LANES)
        q_ref[:, cols] = (_rope_group(q[:, cols], cos, slo, shi) * scale).astype(q_ref.dtype)
    for g in range(SWA_KV // LANES):
        cols = slice(g * LANES, (g + 1) * LANES)
        k_ref[:, cols] = _rope_group(kv[:, cols], cos, slo, shi).astype(k_ref.dtype)
    v_ref[...] = kv[:, SWA_KV:].astype(v_ref.dtype)


def _mid_call(oa, ob, gate0, x2, w_out0, norm_w, w_in1, b_in, cos, slo, shi, seq_len, tm):
    n_tok = x2.shape[0]
    tps = seq_len // tm
    row = lambda i: (i, 0)
    const = lambda i: (0, 0)
    pos = lambda i: (i % tps, 0)
    whole = lambda t: pl.BlockSpec((1,) + t.shape[1:], lambda i: (0, 0, 0), pipeline_mode=pl.Buffered(1))
    return pl.pallas_call(
        _mid_kernel,
        grid=(n_tok // tm,),
        in_specs=[pl.BlockSpec((tm, GLA_VAL), row),
                  pl.BlockSpec((tm, RWKV_W), row),
                  pl.BlockSpec((tm, MIX0), row),
                  pl.BlockSpec((tm, D_MODEL), row),
                  whole(w_out0),
                  pl.BlockSpec((1, D_MODEL), const),
                  whole(w_in1),
                  pl.BlockSpec((1, SWA_QKV), const),
                  pl.BlockSpec((tm, LANES), pos),
                  pl.BlockSpec((tm, LANES), pos),
                  pl.BlockSpec((tm, LANES), pos)],
        out_specs=[pl.BlockSpec((tm, D_MODEL), row),
                   pl.BlockSpec((tm, MIX1), row),
                   pl.BlockSpec((tm, SWA_KV), row),
                   pl.BlockSpec((tm, SWA_KV), row),
                   pl.BlockSpec((tm, MIX1), row)],
        out_shape=[jax.ShapeDtypeStruct((n_tok, D_MODEL), F32),
                   jax.ShapeDtypeStruct((n_tok, MIX1), BF16),
                   jax.ShapeDtypeStruct((n_tok, SWA_KV), BF16),
                   jax.ShapeDtypeStruct((n_tok, SWA_KV), BF16),
                   jax.ShapeDtypeStruct((n_tok, MIX1), BF16)],
        scratch_shapes=[pltpu.VMEM((MIX0, D_MODEL), BF16),
                        pltpu.VMEM((D_MODEL, MIX1), BF16),
                        pltpu.VMEM((D_MODEL, 2 * SWA_KV), BF16),
                        pltpu.VMEM((D_MODEL, MIX1), BF16)],
        compiler_params=_cparams(("arbitrary",)),
        name="l0_out_l1_proj",
    )(oa, ob, gate0, x2, w_out0, norm_w, w_in1, b_in, cos, slo, shi)


def _swa_kernel(sink_ref, q_ref, kc_ref, kp_ref, vc_ref, vp_ref, gate_ref, h_ref, w32_ref, b_ref, nw_ref,
                y_ref, o_ref, w_ref, *, q_blocks):
    n = pl.program_id(1)

    @pl.when((pl.program_id(0) == 0) & (n == 0))
    def _():
        for new, old in enumerate(_paired_head_order()):
            w_ref[new * SWA_HEAD:(new + 1) * SWA_HEAD, :] = (
                w32_ref[0, old * SWA_HEAD:(old + 1) * SWA_HEAD, :].astype(BF16))

    W = WINDOW
    from_prev = _iota((W, 2 * W), 0) > (_iota((W, 2 * W), 1) % W)
    no_prev = jnp.where(n > 0, 0.0, -jnp.inf)
    col_row = _iota((1, 2 * W), 1)
    out_row = _iota((LANES, W), 0)
    kv_groups = SWA_KV // LANES
    groups = MIX1 // LANES // kv_groups
    tasks = [(j, pp, pp * groups + g) for j in range(q_blocks) for pp in range(kv_groups) for g in range(groups)]
    kk, vt = {}, {}
    for j in range(q_blocks):
        for pp in range(kv_groups):
            cols = slice(pp * LANES, (pp + 1) * LANES)
            if j == 0:
                kk[j, pp] = jnp.concatenate([kp_ref[:, cols], kc_ref[:W, cols]], axis=0)
                vv = jnp.concatenate([vp_ref[:, cols], vc_ref[:W, cols]], axis=0)
            else:
                kk[j, pp] = kc_ref[(j - 1) * W:(j + 1) * W, cols]
                vv = vc_ref[(j - 1) * W:(j + 1) * W, cols]
            vt[j, pp] = vv.astype(F32).T.astype(BF16)

    def scores(j, pp, blk):
        q = q_ref[j * W:(j + 1) * W, blk * LANES:(blk + 1) * LANES]
        return lax.dot_general(kk[j, pp], _head_stack(q), (((1,), (1,)), ((), ())), preferred_element_type=F32)

    def projection_pieces(rows):
        width = D_MODEL // SWA_PROJ_PIECES
        gated = []

        def piece(c):
            def run():
                if not gated:
                    gated.append(o_ref[rows, :] * _silu_bf16(gate_ref[rows, :]))
                cols = slice(c * width, (c + 1) * width)
                y_ref[rows, cols] = (h_ref[rows, cols] + b_ref[:, cols]
                                     + jnp.dot(gated[0], w_ref[:, cols], preferred_element_type=F32))
            return run

        def norm():
            y_ref[rows, :] = _rmsnorm_rows(y_ref[rows, :], nw_ref[...])

        return [piece(c) for c in range(SWA_PROJ_PIECES)] + [norm]

    projections = []
    ahead = 8
    pending = [scores(*t) for t in tasks[:ahead]]
    for i, (j, pp, blk) in enumerate(tasks):
        st = pending.pop(0)
        if i + ahead < len(tasks):
            pending.append(scores(*tasks[i + ahead]))
        s_prev = st[:W] + no_prev if j == 0 else st[:W]
        s = jnp.where(from_prev, s_prev, st[W:])
        sink = jnp.where(col_row < W, sink_ref[2 * blk], sink_ref[2 * blk + 1]) * LOG2_E
        m = jnp.maximum(jnp.max(s, axis=0, keepdims=True), sink)
        p = jnp.exp2(s - m)
        denom = jnp.sum(p, axis=0, keepdims=True) + jnp.exp2(sink - m)
        pb = p.astype(BF16)
        zero = jnp.zeros_like(pb)
        p2 = jnp.concatenate([jnp.where(from_prev, pb, zero), jnp.where(from_prev, zero, pb)], axis=0)
        ot = jnp.dot(vt[j, pp], p2, preferred_element_type=F32) * (1.0 / denom)
        ot = jnp.where(out_row < HEAD, ot[:, :W], ot[:, W:])
        o_ref[j * W:(j + 1) * W, blk * LANES:(blk + 1) * LANES] = ot.T.astype(o_ref.dtype)

        last_of_block = i + 1 == len(tasks) or tasks[i + 1][0] != j
        if last_of_block and (j + 1) % SWA_PROJ_BLOCKS == 0:
            projections.extend(projection_pieces(slice((j + 1 - SWA_PROJ_BLOCKS) * W, (j + 1) * W)))
        if projections and (i % SWA_PROJ_EVERY == SWA_PROJ_EVERY - 1 or i + 1 == len(tasks)):
            projections.pop(0)()
    while projections:
        projections.pop(0)()


def _swa_call(sinks, q, k, v, gate, h1, w_out, b_out, norm_w, batch, seq_len, q_blocks):
    n_tok = q.shape[0]
    rows = q_blocks * WINDOW
    steps = seq_len // rows
    cur = lambda b, n: (b * steps + n, 0)
    prev = lambda b, n: (jnp.maximum((b * steps + n) * q_blocks - 1, 0), 0)
    const = lambda b, n: (0, 0)
    return pl.pallas_call(
        functools.partial(_swa_kernel, q_blocks=q_blocks),
        grid=(batch, steps),
        in_specs=[pl.BlockSpec(memory_space=pltpu.SMEM),
                  pl.BlockSpec((rows, MIX1), cur),
                  pl.BlockSpec((rows, SWA_KV), cur),
                  pl.BlockSpec((WINDOW, SWA_KV), prev),
                  pl.BlockSpec((rows, SWA_KV), cur),
                  pl.BlockSpec((WINDOW, SWA_KV), prev),
                  pl.BlockSpec((rows, MIX1), cur),
                  pl.BlockSpec((rows, D_MODEL), cur),
                  pl.BlockSpec((1,) + w_out.shape[1:], lambda b, n: (0, 0, 0), pipeline_mode=pl.Buffered(1)),
                  pl.BlockSpec((1, D_MODEL), const),
                  pl.BlockSpec((1, D_MODEL), const)],
        out_specs=pl.BlockSpec((rows, D_MODEL), cur),
        out_shape=jax.ShapeDtypeStruct((n_tok, D_MODEL), F32),
        scratch_shapes=[pltpu.VMEM((rows, MIX1), BF16),
                        pltpu.VMEM((MIX1, D_MODEL), BF16)],
        compiler_params=_cparams(("arbitrary", "arbitrary")),
        name="l1_swa_out",
    )(sinks, q, k, k, v, v, gate, h1, w_out, b_out, norm_w)


def _pad_rows(w, rows):
    return jnp.concatenate([w, jnp.zeros((rows - w.shape[0], w.shape[1]), w.dtype)], axis=0)


def _pair_heads(t, axis):
    shape = t.shape
    split = shape[:axis] + (SWA_KV_HEADS // 2, 2, SWA_GROUP, SWA_HEAD) + shape[axis + 1:]
    return jnp.swapaxes(t.reshape(split), axis + 1, axis + 2).reshape(shape)


def _rope_tables(seq_len):
    half = ROPE_DIMS // 2
    inv_freq = ROPE_THETA ** (-jnp.arange(half, dtype=F32) / half)
    ang = jnp.arange(seq_len).astype(F32)[:, None] * inv_freq
    trig = jnp.concatenate([jnp.cos(ang), jnp.sin(ang)], axis=1)
    d = jnp.arange(LANES) % SWA_HEAD
    src = jnp.arange(2 * half)[:, None]
    f = (d % half)[None, :]
    rot = (d < ROPE_DIMS)[None, :]
    lo = (d < half)[None, :]
    sel_cos = ((src == f) & rot).astype(F32)
    sel_lo = -((src == half + f) & lo).astype(F32)
    sel_hi = ((src == half + f) & rot & ~lo).astype(F32)
    sel = jnp.concatenate([sel_cos, sel_lo, sel_hi], axis=1)
    tab = jnp.dot(trig, sel, precision=lax.Precision.HIGHEST)
    cos = tab[:, :LANES] + (~rot).astype(F32)
    return cos, tab[:, LANES:2 * LANES], tab[:, 2 * LANES:]


def _forward(x, norm_w, w_in0, gla_gk_up, gla_gk_bias, gla_norm_w, rwkv_mu, rwkv_w0, rwkv_w_up,
             rwkv_a0, rwkv_a_up, rwkv_k_k, rwkv_k_a, rwkv_r_k, rwkv_ln_w, rwkv_ln_b, w_out0,
             w_in1, b_in1, attn_sinks, w_out1, b_out1, final_norm_w, *, tm, gla_chunks, rwkv_chunks, scan_chunks,
             swa_blocks):
    batch, seq_len, _ = x.shape
    x2 = x.reshape(batch * seq_len, D_MODEL)
    row = lambda t: t.reshape(1, -1)

    gqkv, glow, rkv, lora, gate0 = _in0_call(x2, row(norm_w[0]), jnp.swapaxes(w_in0, 1, 2), row(rwkv_mu[0]),
                                             seq_len, tm)

    up_pad = _pad_rows(gla_gk_up[0], GLOW_PAD).astype(BF16)
    o_a = _gla_call(gqkv, glow, up_pad, row(gla_gk_bias[0]), row(gla_norm_w[0]), batch, seq_len, gla_chunks)

    zeros_r = jnp.zeros((RWKV_DECAY_RANK, RWKV_W), F32)
    wup_pad = jnp.concatenate([rwkv_w_up[0], zeros_r], axis=0).astype(BF16)
    aup_pad = jnp.concatenate([zeros_r, rwkv_a_up[0]], axis=0).astype(BF16)
    rp, op, bonus, m, n = _rwkv_chunk_call(
        rkv, lora, row(rwkv_w0[0]), wup_pad, row(rwkv_a0[0]), aup_pad,
        row(rwkv_k_k[0]), row(rwkv_k_a[0]), row(rwkv_r_k[0]), batch, seq_len, rwkv_chunks)
    o_b = _rwkv_scan_call(rp, op, bonus, m, n, row(rwkv_ln_w[0]), row(rwkv_ln_b[0]),
                          batch, seq_len, scan_chunks)


    b1 = b_in1[0]
    b1p = row(jnp.concatenate([_pair_heads(b1[:MIX1], 0), b1[MIX1:]]))
    sinks_p = jnp.swapaxes(attn_sinks[0].reshape(SWA_KV_HEADS // 2, 2, SWA_GROUP), 1, 2).reshape(SWA_Q_HEADS)
    cos, slo, shi = _rope_tables(seq_len)
    h1, q, k, v, gate1 = _mid_call(o_a, o_b, gate0, x2, w_out0, row(norm_w[1]), w_in1,
                                   b1p, cos, slo, shi, seq_len, tm)
    y = _swa_call(sinks_p, q, k, v, gate1, h1, w_out1, row(b_out1[0]), row(final_norm_w),
                  batch, seq_len, swa_blocks)
    return y.reshape(batch, seq_len, D_MODEL)


def kernel(x, norm_w, w_in0, gla_gk_up, gla_gk_bias, gla_norm_w, rwkv_mu, rwkv_w0, rwkv_w_up, rwkv_a0,
           rwkv_a_up, rwkv_k_k, rwkv_k_a, rwkv_r_k, rwkv_ln_w, rwkv_ln_b, w_out0, w_in1, b_in1,
           attn_sinks, w_out1, b_out1, final_norm_w):
    return _forward(x, norm_w, w_in0, gla_gk_up, gla_gk_bias, gla_norm_w, rwkv_mu, rwkv_w0, rwkv_w_up,
                    rwkv_a0, rwkv_a_up, rwkv_k_k, rwkv_k_a, rwkv_r_k, rwkv_ln_w, rwkv_ln_b, w_out0,
                    w_in1, b_in1, attn_sinks, w_out1, b_out1, final_norm_w,
                    tm=512, gla_chunks=32, rwkv_chunks=64, scan_chunks=8, swa_blocks=8)
```

```python
import functools
import math

import jax
import jax.numpy as jnp
from jax import lax
from jax.experimental import pallas as pl
from jax.experimental.pallas import tpu as pltpu

F32 = jnp.float32
BF16 = jnp.bfloat16

D_MODEL = 1024
NORM_EPS = 1e-5

GLA_HEADS = 4
GLA_DK = 64
GLA_DV = 128
GLA_KEY = GLA_HEADS * GLA_DK
GLA_VAL = GLA_HEADS * GLA_DV
GLA_GATE_RANK = 16
GLA_GATE_NORMALIZER = 16.0
GLA_CHUNK = 64

RWKV_HEADS = 8
RWKV_HEAD = 64
RWKV_W = RWKV_HEADS * RWKV_HEAD
RWKV_DECAY_RANK = 64
RWKV_A_RANK = 64
RWKV_LN_EPS = 64e-5
RWKV_RKV = 3 * RWKV_W
RWKV_LORA = RWKV_DECAY_RANK + RWKV_A_RANK
RWKV_SHIFT = RWKV_RKV + RWKV_LORA
RWKV_CHUNK = 64
RWKV_GROUP = 16

MIX0 = GLA_VAL + RWKV_W
GLA_QKV = 2 * GLA_KEY + GLA_VAL

SWA_Q_HEADS = 16
SWA_KV_HEADS = 4
SWA_GROUP = SWA_Q_HEADS // SWA_KV_HEADS
SWA_HEAD = 64
WINDOW = 128
ROPE_DIMS = SWA_HEAD // 4
ROPE_THETA = 500000.0
MIX1 = SWA_Q_HEADS * SWA_HEAD
SWA_KV = SWA_KV_HEADS * SWA_HEAD
SWA_QKV = MIX1 + 2 * SWA_KV
SWA_PROJ_BLOCKS = 2
SWA_PROJ_PIECES = 4
SWA_PROJ_EVERY = 3

LOG2_E = 1.4426950408889634
LANES = 128
HEAD = 64
GLOW_PAD = LANES
VMEM_LIMIT = 56 * 1024 * 1024


def _cparams(sem):
    return pltpu.CompilerParams(dimension_semantics=sem, vmem_limit_bytes=VMEM_LIMIT)


def _dot(a, b):
    return jnp.dot(a.astype(BF16), b.astype(BF16), preferred_element_type=F32)


def _dot_nt(a, b):
    return lax.dot_general(a.astype(BF16), b.astype(BF16), (((1,), (1,)), ((), ())),
                           preferred_element_type=F32)


def _dot_tn(a, b):
    return lax.dot_general(a.astype(BF16), b.astype(BF16), (((0,), (0,)), ((), ())),
                           preferred_element_type=F32)


def _split2(x):
    hi = x.astype(BF16)
    lo = (x - hi.astype(F32)).astype(BF16)
    return hi, lo


def _dot_exact_rhs(a_bf16, x):
    hi, lo = _split2(x)
    return (jnp.dot(a_bf16, hi, preferred_element_type=F32)
            + jnp.dot(a_bf16, lo, preferred_element_type=F32))


def _dot_exact_lhs(x, b_bf16):
    hi, lo = _split2(x)
    return (jnp.dot(hi, b_bf16, preferred_element_type=F32)
            + jnp.dot(lo, b_bf16, preferred_element_type=F32))


def _iota(shape, dim):
    return lax.broadcasted_iota(jnp.int32, shape, dim)


def _tril_ones(n, dtype=BF16):
    return (_iota((n, n), 0) >= _iota((n, n), 1)).astype(dtype)


def _head_block_ones(n=LANES, dtype=BF16):
    return ((_iota((n, n), 0) // HEAD) == (_iota((n, n), 1) // HEAD)).astype(dtype)


def _head_stack(x):
    head = (_iota(x.shape, 1) % LANES) // HEAD
    return jnp.concatenate([jnp.where(head == 0, x, 0.0), jnp.where(head == 1, x, 0.0)], axis=0)


def _softplus(z):
    return jnp.maximum(z, 0.0) + jnp.log(1.0 + jnp.exp(-jnp.abs(z)))


def _sigmoid(z):
    return 1.0 / (1.0 + jnp.exp(-z))


def _silu_bf16(g):
    return g * (1.0 / (1.0 + jnp.exp(-g))).astype(BF16)


def _rmsnorm_rows(x, w):
    return x * lax.rsqrt(jnp.mean(x * x, axis=-1, keepdims=True) + NORM_EPS) * w


def _in0_kernel(x_ref, nw_ref, w_ref, mu_ref,
                gqkv_ref, glow_ref, rkv_ref, lora_ref, gate_ref, carry_ref, wg_ref, wl_ref, wr_ref,
                *, tiles_per_seq):
    i = pl.program_id(0)

    @pl.when(i == 0)
    def _():
        carry_ref[...] = jnp.zeros_like(carry_ref)
        def put(dst, row0, width):
            step = 4 * LANES if width % (4 * LANES) == 0 else LANES
            for c in range(0, width, step):
                dst[:, c:c + step] = w_ref[0, row0 + c:row0 + c + step, :].T.astype(BF16)

        put(wg_ref, 0, GLA_QKV)
        put(wl_ref, GLA_QKV, GLOW_PAD)
        put(wr_ref, GLA_QKV + GLA_GATE_RANK, RWKV_SHIFT + MIX0)

    xn = _rmsnorm_rows(x_ref[...], nw_ref[...]).astype(BF16)
    gqkv_ref[...] = jnp.dot(xn, wg_ref[...], preferred_element_type=F32).astype(gqkv_ref.dtype)
    glow_ref[...] = jnp.dot(xn, wl_ref[...], preferred_element_type=F32)
    rw = jnp.dot(xn, wr_ref[:, :RWKV_SHIFT], preferred_element_type=F32)
    gate_ref[...] = jnp.dot(xn, wr_ref[:, RWKV_SHIFT:], preferred_element_type=F32).astype(gate_ref.dtype)

    tm = rw.shape[0]
    first = (i % tiles_per_seq) == 0
    prev_last = jnp.where(first, 0.0, carry_ref[7:8, :])
    rolled = pltpu.roll(rw, 1, 0)
    prev = jnp.where(_iota(rw.shape, 0) == 0, prev_last, rolled)
    mixed = rw + (prev - rw) * mu_ref[...]
    rkv_ref[...] = mixed[:, :RWKV_RKV].astype(rkv_ref.dtype)
    lora_ref[...] = mixed[:, RWKV_RKV:]
    carry_ref[...] = rw[tm - 8:tm, :]


def _in0_call(x2, norm_w, w_in, mu, seq_len, tm):
    n_tok = x2.shape[0]
    row = lambda i: (i, 0)
    const = lambda i: (0, 0)
    outs = [(GLA_QKV, BF16), (GLOW_PAD, F32), (RWKV_RKV, BF16), (RWKV_LORA, F32), (MIX0, BF16)]
    return pl.pallas_call(
        functools.partial(_in0_kernel, tiles_per_seq=seq_len // tm),
        grid=(n_tok // tm,),
        in_specs=[pl.BlockSpec((tm, D_MODEL), row),
                  pl.BlockSpec((1, D_MODEL), const),
                  pl.BlockSpec((1,) + w_in.shape[1:], lambda i: (0, 0, 0), pipeline_mode=pl.Buffered(1)),
                  pl.BlockSpec((1, RWKV_SHIFT), const)],
        out_specs=[pl.BlockSpec((tm, n), row) for n, _ in outs],
        out_shape=[jax.ShapeDtypeStruct((n_tok, n), dt) for n, dt in outs],
        scratch_shapes=[pltpu.VMEM((8, RWKV_SHIFT), F32),
                        pltpu.VMEM((D_MODEL, GLA_QKV), BF16),
                        pltpu.VMEM((D_MODEL, GLOW_PAD), BF16),
                        pltpu.VMEM((D_MODEL, RWKV_SHIFT + MIX0), BF16)],
        compiler_params=_cparams(("arbitrary",)),
        name="l0_norm_proj",
    )(x2, norm_w, w_in, mu)


def _gla_kernel(q_ref, k_ref, glow_ref, v_ref, up_ref, bias_ref, nw_ref, o_ref, st_ref, *, chunks):
    c = pl.program_id(2)

    @pl.when(c == 0)
    def _():
        st_ref[...] = jnp.zeros_like(st_ref)

    C = GLA_CHUNK
    tril = _tril_ones(C)
    causal = _iota((C, LANES), 0) >= (_iota((C, LANES), 1) % HEAD)
    sr = _iota((2 * GLA_DV, LANES), 0)
    sl = _iota((2 * GLA_DV, LANES), 1)
    st_mask = (sr // GLA_DV) == (sl // HEAD)
    vl = _iota((C, 2 * GLA_DV), 1)
    scale = GLA_DK ** -0.5
    z = _dot(glow_ref[...], up_ref[...]) + bias_ref[...]
    g_all = -_softplus(-z) * (LOG2_E / GLA_GATE_NORMALIZER)
    q_all = q_ref[...].astype(F32) * scale
    k_all = k_ref[...].astype(F32)
    rows = [slice(j * C, (j + 1) * C) for j in range(chunks)]
    bs = [_dot_exact_rhs(tril, g_all[rw]) for rw in rows]
    qe, ke, qb, kl, dec, vs = [], [], [], [], [], []
    for rw, b in zip(rows, bs):
        ref = b[C // 2:C // 2 + 1, :]
        b_last = b[C - 1:C, :]
        qe.append(q_all[rw] * jnp.exp2(b - ref))
        ke.append(k_all[rw] * jnp.exp2(ref - b))
        qb.append(qe[-1] * jnp.exp2(ref))
        kl.append(ke[-1] * jnp.exp2(b_last - ref))
        dec.append(jnp.exp2(b_last))
        vs.append(v_ref[rw, :])
    att = [jnp.where(causal, _dot_nt(qe[j], _head_stack(ke[j])), 0.0) for j in range(chunks)]
    kv = [jnp.where(st_mask, _dot_tn(vs[j], kl[j]), 0.0) for j in range(chunks)]
    v_diag = [jnp.concatenate([jnp.where(vl < GLA_DV, vs[j], jnp.zeros_like(vs[j])),
                               jnp.where(vl >= GLA_DV, vs[j], jnp.zeros_like(vs[j]))], axis=0)
              for j in range(chunks)]
    intra = [jnp.dot(att[j].astype(BF16), v_diag[j], preferred_element_type=F32) for j in range(chunks)]
    states = [st_ref[...]]
    for j in range(chunks):
        states.append(states[j] * dec[j] + kv[j])
    st_ref[...] = states[chunks]
    for j in range(chunks):
        o = intra[j] + _dot_nt(qb[j], states[j])
        for h in range(2):
            oh = o[:, h * GLA_DV:(h + 1) * GLA_DV]
            oh = oh * lax.rsqrt(jnp.mean(oh * oh, axis=-1, keepdims=True) + NORM_EPS) * nw_ref[...]
            o_ref[rows[j], h * GLA_DV:(h + 1) * GLA_DV] = oh.astype(o_ref.dtype)


def _gla_call(gqkv, glow, up_pad, bias, norm_w, batch, seq_len, chunks):
    n_tok = gqkv.shape[0]
    tcb = chunks * GLA_CHUNK
    steps = seq_len // tcb
    pairs = GLA_KEY // LANES
    return pl.pallas_call(
        functools.partial(_gla_kernel, chunks=chunks),
        grid=(batch, pairs, steps),
        in_specs=[pl.BlockSpec((tcb, LANES), lambda b, p, c: (b * steps + c, p)),
                  pl.BlockSpec((tcb, LANES), lambda b, p, c: (b * steps + c, pairs + p)),
                  pl.BlockSpec((tcb, GLOW_PAD), lambda b, p, c: (b * steps + c, 0)),
                  pl.BlockSpec((tcb, 2 * GLA_DV), lambda b, p, c: (b * steps + c, pairs + p)),
                  pl.BlockSpec((GLOW_PAD, LANES), lambda b, p, c: (0, p)),
                  pl.BlockSpec((1, LANES), lambda b, p, c: (0, p)),
                  pl.BlockSpec((1, GLA_DV), lambda b, p, c: (0, 0))],
        out_specs=pl.BlockSpec((tcb, 2 * GLA_DV), lambda b, p, c: (b * steps + c, p)),
        out_shape=jax.ShapeDtypeStruct((n_tok, GLA_VAL), BF16),
        scratch_shapes=[pltpu.VMEM((2 * GLA_DV, LANES), F32)],
        compiler_params=_cparams(("parallel", "parallel", "arbitrary")),
        name="l0_gla",
    )(gqkv, gqkv, glow, gqkv, up_pad, bias, norm_w)


def _merge_masks(n):
    r = _iota((n, LANES), 0)
    c = _iota((n, LANES), 1) % HEAD
    masks = []
    s = 1
    while s < n:
        masks.append(((r // s) % 2 == 1) & ((c // s) == (r // s) - 1))
        s *= 2
    return (r == c).astype(F32), masks


def _run_interleaved(main, main_steps, side, side_steps):
    done = 0
    spread = max(1, (3 * main_steps) // 4)
    for i, _ in enumerate(main):
        target = -(-(i + 1) * side_steps // spread)
        while done < min(target, side_steps):
            next(side, None)
            done += 1
    for _ in side:
        pass


def _rwkv_chunk_kernel(r_ref, k_ref, v_ref, xwa_ref, w0_ref, wup_ref, a0_ref, aup_ref,
                       kk_ref, ka_ref, rk_ref,
                       rp_ref, op_ref, bonus_ref, m_ref, n_ref, *, chunks, group):
    C = RWKV_CHUNK
    tril = _tril_ones(C)
    rr = _iota((2 * C, LANES), 0)
    cc = _iota((2 * C, LANES), 1) % HEAD
    tri2 = ((rr < C) & (rr > cc)) | (rr - C >= cc)
    hb = _head_block_ones()
    sq_r = _iota((LANES, LANES), 0)
    sq_c = _iota((LANES, LANES), 1)
    same_head = (sq_r // HEAD) == (sq_c // HEAD)
    eye128 = sq_r == sq_c

    eye, merge = _merge_masks(C)
    zero = jnp.zeros((C, LANES), F32)
    n = range(group)

    def prepare(g, out):
        rows = slice(g * group * C, (g + 1) * group * C)
        r_all = r_ref[rows, :].astype(F32)
        k_all = k_ref[rows, :].astype(F32)
        v_all = v_ref[rows, :].astype(F32)
        xwa = xwa_ref[rows, :]
        y = w0_ref[...] + _dot(jnp.tanh(xwa), wup_ref[...])
        lw_all = _sigmoid(y) * (-LOG2_E * math.exp(-0.5))
        a_sig = _sigmoid(a0_ref[...] + _dot(xwa, aup_ref[...]))
        kk = k_all * kk_ref[...]
        kk = kk * lax.rsqrt(jnp.maximum(_dot_exact_lhs(kk * kk, hb), 1e-24))
        k_all = k_all * (1.0 + (a_sig - 1.0) * ka_ref[...])
        bonus_ref[rows, :] = (_dot(r_all * k_all * rk_ref[...], hb) * v_all).astype(bonus_ref.dtype)
        a_all = -kk
        b_all = kk * a_sig
        yield
        for j in n:
            rw = slice(j * C, (j + 1) * C)
            cum = _dot_exact_rhs(tril, lw_all[rw])
            cum_last = cum[C - 1:C, :]
            e_neg = jnp.exp2(-cum)
            e_end = jnp.exp2(cum_last - cum)
            out.append(dict(
                rt=r_all[rw] * jnp.exp2(cum),
                at=a_all[rw] * jnp.exp2(cum - lw_all[rw]),
                bt=b_all[rw] * e_neg,
                kt=k_all[rw] * e_neg,
                ends=jnp.concatenate([b_all[rw] * e_end, k_all[rw] * e_end], axis=0),
                v=v_all[rw],
                dec=jnp.exp2(cum_last)))
            yield

    def solve(g, ops):
        lhs = [jnp.concatenate([o["at"], o["rt"]], axis=0) for o in ops]
        left = [jnp.where(tri2, _dot_nt(lhs[j], _head_stack(ops[j]["bt"])), 0.0) for j in n]
        yield
        right = [jnp.where(tri2, _dot_nt(lhs[j], _head_stack(ops[j]["kt"])), 0.0) for j in n]
        yield
        lows = [lf[:C] for lf in left]
        ts = [eye + jnp.where(merge[0], low, 0.0) for low in lows]
        for sub in merge[1:]:
            ys = [_dot(jnp.where(sub, low, 0.0), _head_stack(t)) for low, t in zip(lows, ts)]
            yield
            ts = [t + _dot(t, _head_stack(y)) for t, y in zip(ts, ys)]
            yield
        kv = [_dot(right[j], _head_stack(ops[j]["v"])) for j in n]
        yield
        wz = [_dot(ts[j], _head_stack(jnp.concatenate([ops[j]["at"], kv[j][:C]], axis=1))) for j in n]
        yield
        ro = [_dot(left[j][C:], _head_stack(wz[j])) for j in n]
        yield
        mn = [_dot_tn(ops[j]["ends"],
                      jnp.concatenate([wz[j], jnp.concatenate([zero, ops[j]["v"]], axis=1)], axis=0))
              for j in n]
        for j in n:
            c = g * group + j
            rows = slice(c * C, (c + 1) * C)
            rp_ref[rows, :] = (ops[j]["rt"] + ro[j][:, :LANES]).astype(rp_ref.dtype)
            op_ref[rows, :] = ro[j][:, LANES:] + kv[j][C:]
            m_ref[0, 0, c] = (jnp.where(eye128, ops[j]["dec"], 0.0)
                              + jnp.where(same_head, mn[j][:, :LANES], 0.0)).astype(m_ref.dtype)
            n_ref[0, 0, c] = jnp.where(same_head, mn[j][:, LANES:], 0.0).astype(n_ref.dtype)
        yield

    solve_stages = 2 + 2 * (len(merge) - 1) + 4
    groups = chunks // group
    ops = [[] for _ in range(groups + 1)]
    for _ in prepare(0, ops[0]):
        pass
    for g in range(groups):
        side = prepare(g + 1, ops[g + 1]) if g + 1 < groups else iter(())
        _run_interleaved(solve(g, ops[g]), solve_stages, side, group + 1)


def _rwkv_chunk_call(rkv, lora, w0, wup_pad, a0, aup_pad, k_k, k_a, r_k, batch, seq_len, chunks):
    n_tok = rkv.shape[0]
    tcb = chunks * RWKV_CHUNK
    steps = seq_len // tcb
    pairs = RWKV_W // LANES
    nc = seq_len // RWKV_CHUNK
    col = lambda off: (lambda b, p, c: (b * steps + c, off + p))
    par = lambda b, p, c: (0, p)
    tok = lambda b, p, c: (b * steps + c, p)
    mat = lambda b, p, c: (b, p, c, 0, 0)
    return pl.pallas_call(
        functools.partial(_rwkv_chunk_kernel, chunks=chunks, group=RWKV_GROUP),
        grid=(batch, pairs, steps),
        in_specs=[pl.BlockSpec((tcb, LANES), col(0)),
                  pl.BlockSpec((tcb, LANES), col(pairs)),
                  pl.BlockSpec((tcb, LANES), col(2 * pairs)),
                  pl.BlockSpec((tcb, RWKV_LORA), lambda b, p, c: (b * steps + c, 0)),
                  pl.BlockSpec((1, LANES), par),
                  pl.BlockSpec((RWKV_LORA, LANES), par),
                  pl.BlockSpec((1, LANES), par),
                  pl.BlockSpec((RWKV_LORA, LANES), par),
                  pl.BlockSpec((1, LANES), par),
                  pl.BlockSpec((1, LANES), par),
                  pl.BlockSpec((1, LANES), par)],
        out_specs=[pl.BlockSpec((tcb, LANES), tok),
                   pl.BlockSpec((tcb, LANES), tok),
                   pl.BlockSpec((tcb, LANES), tok),
                   pl.BlockSpec((1, 1, chunks, LANES, LANES), mat),
                   pl.BlockSpec((1, 1, chunks, LANES, LANES), mat)],
        out_shape=[jax.ShapeDtypeStruct((n_tok, RWKV_W), BF16),
                   jax.ShapeDtypeStruct((n_tok, RWKV_W), F32),
                   jax.ShapeDtypeStruct((n_tok, RWKV_W), BF16),
                   jax.ShapeDtypeStruct((batch, pairs, nc, LANES, LANES), BF16),
                   jax.ShapeDtypeStruct((batch, pairs, nc, LANES, LANES), BF16)],
        compiler_params=_cparams(("parallel", "parallel", "parallel")),
        name="l0_rwkv_chunks",
    )(rkv, rkv, rkv, lora, w0, wup_pad, a0, aup_pad, k_k, k_a, r_k)


def _rwkv_scan_kernel(rp_ref, op_ref, bonus_ref, m_ref, n_ref, lnw_ref, lnb_ref, o_ref, st_ref, *, chunks):
    c = pl.program_id(0)

    @pl.when(c == 0)
    def _():
        st_ref[...] = jnp.zeros_like(st_ref)

    C = RWKV_CHUNK
    batch = rp_ref.shape[0]
    pairs = RWKV_W // LANES
    hb = _head_block_ones()
    seqs = [(b, p) for b in range(batch) for p in range(pairs)]
    states = {bp: [st_ref[bp[0], bp[1]]] for bp in seqs}
    for j in range(chunks):
        for b, p in seqs:
            states[b, p].append(_dot(m_ref[b, p, j], states[b, p][j]) + n_ref[b, p, j])
    for b, p in seqs:
        st_ref[b, p] = states[b, p][chunks]
    cols = {bp: slice(bp[1] * LANES, (bp[1] + 1) * LANES) for bp in seqs}
    os = [jnp.concatenate([_dot(rp_ref[b, j * C:(j + 1) * C, cols[b, p]], states[b, p][j])
                           for j in range(chunks)], axis=0) + op_ref[b, :, cols[b, p]] for b, p in seqs]
    means = [_dot_exact_lhs(o, hb) * (1.0 / RWKV_HEAD) for o in os]
    ds = [o - mean for o, mean in zip(os, means)]
    variances = [_dot_exact_lhs(d * d, hb) * (1.0 / RWKV_HEAD) for d in ds]
    for (b, p), d, var in zip(seqs, ds, variances):
        c_ = cols[b, p]
        o_ref[b, :, c_] = (d * lax.rsqrt(var + RWKV_LN_EPS) * lnw_ref[:, c_] + lnb_ref[:, c_]
                           + bonus_ref[b, :, c_]).astype(o_ref.dtype)


def _rwkv_scan_call(rp, op, bonus, m, n, ln_w, ln_b, batch, seq_len, chunks):
    tcb = chunks * RWKV_CHUNK
    pairs = RWKV_W // LANES
    seq3 = lambda t: t.reshape(batch, seq_len, RWKV_W)
    tok = lambda c: (0, c, 0)
    const = lambda c: (0, 0)
    mat = lambda c: (0, 0, c, 0, 0)
    out = pl.pallas_call(
        functools.partial(_rwkv_scan_kernel, chunks=chunks),
        grid=(seq_len // tcb,),
        in_specs=[pl.BlockSpec((batch, tcb, RWKV_W), tok),
                  pl.BlockSpec((batch, tcb, RWKV_W), tok),
                  pl.BlockSpec((batch, tcb, RWKV_W), tok),
                  pl.BlockSpec((batch, pairs, chunks, LANES, LANES), mat),
                  pl.BlockSpec((batch, pairs, chunks, LANES, LANES), mat),
                  pl.BlockSpec((1, RWKV_W), const),
                  pl.BlockSpec((1, RWKV_W), const)],
        out_specs=pl.BlockSpec((batch, tcb, RWKV_W), tok),
        out_shape=jax.ShapeDtypeStruct((batch, seq_len, RWKV_W), BF16),
        scratch_shapes=[pltpu.VMEM((batch, pairs, LANES, LANES), F32)],
        compiler_params=_cparams(("arbitrary",)),
        name="l0_rwkv_scan",
    )(seq3(rp), seq3(op), seq3(bonus), m, n, ln_w, ln_b)
    return out.reshape(batch * seq_len, RWKV_W)


def _gated_out0(oa_ref, ob_ref, gate_ref, x_ref, w_ref):
    g = _silu_bf16(gate_ref[...])
    ya = oa_ref[...] * g[:, :GLA_VAL]
    yb = ob_ref[...] * g[:, GLA_VAL:]
    return (x_ref[...]
            + jnp.dot(ya, w_ref[:GLA_VAL, :], preferred_element_type=F32)
            + jnp.dot(yb, w_ref[GLA_VAL:, :], preferred_element_type=F32))


def _rope_group(x, cos, sin_lo, sin_hi):
    half = ROPE_DIMS // 2
    return x * cos + pltpu.roll(x, LANES - half, 1) * sin_lo + pltpu.roll(x, half, 1) * sin_hi


def _paired_head_order():
    return [(2 * pp + e) * SWA_GROUP + g
            for pp in range(SWA_KV_HEADS // 2) for g in range(SWA_GROUP) for e in range(2)]


def _mid_kernel(oa_ref, ob_ref, gate0_ref, x_ref, wo32_ref,
                nw_ref, w1_ref, b_ref, cos_ref, slo_ref, shi_ref,
                h_ref, q_ref, k_ref, v_ref, gate_ref, wo_ref, wq_ref, wkv_ref, wg_ref):
    @pl.when(pl.program_id(0) == 0)
    def _():
        wo_ref[...] = wo32_ref[0].astype(BF16)
        wkv_ref[...] = w1_ref[0, :, MIX1:SWA_QKV].astype(BF16)
        for new, old in enumerate(_paired_head_order()):
            dst = slice(new * SWA_HEAD, (new + 1) * SWA_HEAD)
            wq_ref[:, dst] = w1_ref[0, :, old * SWA_HEAD:(old + 1) * SWA_HEAD].astype(BF16)
            wg_ref[:, dst] = w1_ref[0, :, SWA_QKV + old * SWA_HEAD:SWA_QKV + (old + 1) * SWA_HEAD].astype(BF16)

    h = _gated_out0(oa_ref, ob_ref, gate0_ref, x_ref, wo_ref)
    h_ref[...] = h
    hn = _rmsnorm_rows(h, nw_ref[...]).astype(BF16)
    cos = cos_ref[...]
    slo = slo_ref[...]
    shi = shi_ref[...]
    scale = SWA_HEAD ** -0.5 * LOG2_E
    q = jnp.dot(hn, wq_ref[...], preferred_element_type=F32) + b_ref[:, :MIX1]
    kv = jnp.dot(hn, wkv_ref[...], preferred_element_type=F32) + b_ref[:, MIX1:]
    gate_ref[...] = jnp.dot(hn, wg_ref[...], preferred_element_type=F32).astype(gate_ref.dtype)
    for g in range(MIX1 // LANES):
        cols = slice(g * LANES, (g + 1) * LANES)
        q_ref[:, cols] = (_rope_group(q[:, cols], cos, slo, shi) * scale).astype(q_ref.dtype)
    for g in range(SWA_KV // LANES):
        cols = slice(g * LANES, (g + 1) * LANES)
        k_ref[:, cols] = _rope_group(kv[:, cols], cos, slo, shi).astype(k_ref.dtype)
    v_ref[...] = kv[:, SWA_KV:].astype(v_ref.dtype)


def _mid_call(oa, ob, gate0, x2, w_out0, norm_w, w_in1, b_in, cos, slo, shi, seq_len, tm):
    n_tok = x2.shape[0]
    tps = seq_len // tm
    row = lambda i: (i, 0)
    const = lambda i: (0, 0)
    pos = lambda i: (i % tps, 0)
    whole = lambda t: pl.BlockSpec((1,) + t.shape[1:], lambda i: (0, 0, 0), pipeline_mode=pl.Buffered(1))
    return pl.pallas_call(
        _mid_kernel,
        grid=(n_tok // tm,),
        in_specs=[pl.BlockSpec((tm, GLA_VAL), row),
                  pl.BlockSpec((tm, RWKV_W), row),
                  pl.BlockSpec((tm, MIX0), row),
                  pl.BlockSpec((tm, D_MODEL), row),
                  whole(w_out0),
                  pl.BlockSpec((1, D_MODEL), const),
                  whole(w_in1),
                  pl.BlockSpec((1, SWA_QKV), const),
                  pl.BlockSpec((tm, LANES), pos),
                  pl.BlockSpec((tm, LANES), pos),
                  pl.BlockSpec((tm, LANES), pos)],
        out_specs=[pl.BlockSpec((tm, D_MODEL), row),
                   pl.BlockSpec((tm, MIX1), row),
                   pl.BlockSpec((tm, SWA_KV), row),
                   pl.BlockSpec((tm, SWA_KV), row),
                   pl.BlockSpec((tm, MIX1), row)],
        out_shape=[jax.ShapeDtypeStruct((n_tok, D_MODEL), F32),
                   jax.ShapeDtypeStruct((n_tok, MIX1), BF16),
                   jax.ShapeDtypeStruct((n_tok, SWA_KV), BF16),
                   jax.ShapeDtypeStruct((n_tok, SWA_KV), BF16),
                   jax.ShapeDtypeStruct((n_tok, MIX1), BF16)],
        scratch_shapes=[pltpu.VMEM((MIX0, D_MODEL), BF16),
                        pltpu.VMEM((D_MODEL, MIX1), BF16),
                        pltpu.VMEM((D_MODEL, 2 * SWA_KV), BF16),
                        pltpu.VMEM((D_MODEL, MIX1), BF16)],
        compiler_params=_cparams(("arbitrary",)),
        name="l0_out_l1_proj",
    )(oa, ob, gate0, x2, w_out0, norm_w, w_in1, b_in, cos, slo, shi)


def _swa_kernel(sink_ref, q_ref, kc_ref, kp_ref, vc_ref, vp_ref, gate_ref, h_ref, w32_ref, b_ref, nw_ref,
                y_ref, o_ref, w_ref, *, q_blocks):
    n = pl.program_id(1)

    @pl.when((pl.program_id(0) == 0) & (n == 0))
    def _():
        for new, old in enumerate(_paired_head_order()):
            w_ref[new * SWA_HEAD:(new + 1) * SWA_HEAD, :] = (
                w32_ref[0, old * SWA_HEAD:(old + 1) * SWA_HEAD, :].astype(BF16))

    W = WINDOW
    from_prev = _iota((W, 2 * W), 0) > (_iota((W, 2 * W), 1) % W)
    no_prev = jnp.where(n > 0, 0.0, -jnp.inf)
    col_row = _iota((1, 2 * W), 1)
    out_row = _iota((LANES, W), 0)
    kv_groups = SWA_KV // LANES
    groups = MIX1 // LANES // kv_groups
    tasks = [(j, pp, pp * groups + g) for j in range(q_blocks) for pp in range(kv_groups) for g in range(groups)]
    kk, vt = {}, {}
    for j in range(q_blocks):
        for pp in range(kv_groups):
            cols = slice(pp * LANES, (pp + 1) * LANES)
            if j == 0:
                kk[j, pp] = jnp.concatenate([kp_ref[:, cols], kc_ref[:W, cols]], axis=0)
                vv = jnp.concatenate([vp_ref[:, cols], vc_ref[:W, cols]], axis=0)
            else:
                kk[j, pp] = kc_ref[(j - 1) * W:(j + 1) * W, cols]
                vv = vc_ref[(j - 1) * W:(j + 1) * W, cols]
            vt[j, pp] = vv.astype(F32).T.astype(BF16)

    def scores(j, pp, blk):
        q = q_ref[j * W:(j + 1) * W, blk * LANES:(blk + 1) * LANES]
        return lax.dot_general(kk[j, pp], _head_stack(q), (((1,), (1,)), ((), ())), preferred_element_type=F32)

    def projection_pieces(rows):
        width = D_MODEL // SWA_PROJ_PIECES
        gated = []

        def piece(c):
            def run():
                if not gated:
                    gated.append(o_ref[rows, :] * _silu_bf16(gate_ref[rows, :]))
                cols = slice(c * width, (c + 1) * width)
                y_ref[rows, cols] = (h_ref[rows, cols] + b_ref[:, cols]
                                     + jnp.dot(gated[0], w_ref[:, cols], preferred_element_type=F32))
            return run

        def norm():
            y_ref[rows, :] = _rmsnorm_rows(y_ref[rows, :], nw_ref[...])

        return [piece(c) for c in range(SWA_PROJ_PIECES)] + [norm]

    projections = []
    ahead = 8
    pending = [scores(*t) for t in tasks[:ahead]]
    for i, (j, pp, blk) in enumerate(tasks):
        st = pending.pop(0)
        if i + ahead < len(tasks):
            pending.append(scores(*tasks[i + ahead]))
        s_prev = st[:W] + no_prev if j == 0 else st[:W]
        s = jnp.where(from_prev, s_prev, st[W:])
        sink = jnp.where(col_row < W, sink_ref[2 * blk], sink_ref[2 * blk + 1]) * LOG2_E
        m = jnp.maximum(jnp.max(s, axis=0, keepdims=True), sink)
        p = jnp.exp2(s - m)
        denom = jnp.sum(p, axis=0, keepdims=True) + jnp.exp2(sink - m)
        pb = p.astype(BF16)
        zero = jnp.zeros_like(pb)
        p2 = jnp.concatenate([jnp.where(from_prev, pb, zero), jnp.where(from_prev, zero, pb)], axis=0)
        ot = jnp.dot(vt[j, pp], p2, preferred_element_type=F32) * (1.0 / denom)
        ot = jnp.where(out_row < HEAD, ot[:, :W], ot[:, W:])
        o_ref[j * W:(j + 1) * W, blk * LANES:(blk + 1) * LANES] = ot.T.astype(o_ref.dtype)

        last_of_block = i + 1 == len(tasks) or tasks[i + 1][0] != j
        if last_of_block and (j + 1) % SWA_PROJ_BLOCKS == 0:
            projections.extend(projection_pieces(slice((j + 1 - SWA_PROJ_BLOCKS) * W, (j + 1) * W)))
        if projections and (i % SWA_PROJ_EVERY == SWA_PROJ_EVERY - 1 or i + 1 == len(tasks)):
            projections.pop(0)()
    while projections:
        projections.pop(0)()


def _swa_call(sinks, q, k, v, gate, h1, w_out, b_out, norm_w, batch, seq_len, q_blocks):
    n_tok = q.shape[0]
    rows = q_blocks * WINDOW
    steps = seq_len // rows
    cur = lambda b, n: (b * steps + n, 0)
    prev = lambda b, n: (jnp.maximum((b * steps + n) * q_blocks - 1, 0), 0)
    const = lambda b, n: (0, 0)
    return pl.pallas_call(
        functools.partial(_swa_kernel, q_blocks=q_blocks),
        grid=(batch, steps),
        in_specs=[pl.BlockSpec(memory_space=pltpu.SMEM),
                  pl.BlockSpec((rows, MIX1), cur),
                  pl.BlockSpec((rows, SWA_KV), cur),
                  pl.BlockSpec((WINDOW, SWA_KV), prev),
                  pl.BlockSpec((rows, SWA_KV), cur),
                  pl.BlockSpec((WINDOW, SWA_KV), prev),
                  pl.BlockSpec((rows, MIX1), cur),
                  pl.BlockSpec((rows, D_MODEL), cur),
                  pl.BlockSpec((1,) + w_out.shape[1:], lambda b, n: (0, 0, 0), pipeline_mode=pl.Buffered(1)),
                  pl.BlockSpec((1, D_MODEL), const),
                  pl.BlockSpec((1, D_MODEL), const)],
        out_specs=pl.BlockSpec((rows, D_MODEL), cur),
        out_shape=jax.ShapeDtypeStruct((n_tok, D_MODEL), F32),
        scratch_shapes=[pltpu.VMEM((rows, MIX1), BF16),
                        pltpu.VMEM((MIX1, D_MODEL), BF16)],
        compiler_params=_cparams(("arbitrary", "arbitrary")),
        name="l1_swa_out",
    )(sinks, q, k, k, v, v, gate, h1, w_out, b_out, norm_w)


def _pad_rows(w, rows):
    return jnp.concatenate([w, jnp.zeros((rows - w.shape[0], w.shape[1]), w.dtype)], axis=0)


def _pair_heads(t, axis):
    shape = t.shape
    split = shape[:axis] + (SWA_KV_HEADS // 2, 2, SWA_GROUP, SWA_HEAD) + shape[axis + 1:]
    return jnp.swapaxes(t.reshape(split), axis + 1, axis + 2).reshape(shape)


def _rope_tables(seq_len):
    half = ROPE_DIMS // 2
    inv_freq = ROPE_THETA ** (-jnp.arange(half, dtype=F32) / half)
    ang = jnp.arange(seq_len).astype(F32)[:, None] * inv_freq
    trig = jnp.concatenate([jnp.cos(ang), jnp.sin(ang)], axis=1)
    d = jnp.arange(LANES) % SWA_HEAD
    src = jnp.arange(2 * half)[:, None]
    f = (d % half)[None, :]
    rot = (d < ROPE_DIMS)[None, :]
    lo = (d < half)[None, :]
    sel_cos = ((src == f) & rot).astype(F32)
    sel_lo = -((src == half + f) & lo).astype(F32)
    sel_hi = ((src == half + f) & rot & ~lo).astype(F32)
    sel = jnp.concatenate([sel_cos, sel_lo, sel_hi], axis=1)
    tab = jnp.dot(trig, sel, precision=lax.Precision.HIGHEST)
    cos = tab[:, :LANES] + (~rot).astype(F32)
    return cos, tab[:, LANES:2 * LANES], tab[:, 2 * LANES:]


def _forward(x, norm_w, w_in0, gla_gk_up, gla_gk_bias, gla_norm_w, rwkv_mu, rwkv_w0, rwkv_w_up,
             rwkv_a0, rwkv_a_up, rwkv_k_k, rwkv_k_a, rwkv_r_k, rwkv_ln_w, rwkv_ln_b, w_out0,
             w_in1, b_in1, attn_sinks, w_out1, b_out1, final_norm_w, *, tm, gla_chunks, rwkv_chunks, scan_chunks,
             swa_blocks):
    batch, seq_len, _ = x.shape
    x2 = x.reshape(batch * seq_len, D_MODEL)
    row = lambda t: t.reshape(1, -1)

    gqkv, glow, rkv, lora, gate0 = _in0_call(x2, row(norm_w[0]), jnp.swapaxes(w_in0, 1, 2), row(rwkv_mu[0]),
                                             seq_len, tm)

    up_pad = _pad_rows(gla_gk_up[0], GLOW_PAD).astype(BF16)
    o_a = _gla_call(gqkv, glow, up_pad, row(gla_gk_bias[0]), row(gla_norm_w[0]), batch, seq_len, gla_chunks)

    zeros_r = jnp.zeros((RWKV_DECAY_RANK, RWKV_W), F32)
    wup_pad = jnp.concatenate([rwkv_w_up[0], zeros_r], axis=0).astype(BF16)
    aup_pad = jnp.concatenate([zeros_r, rwkv_a_up[0]], axis=0).astype(BF16)
    rp, op, bonus, m, n = _rwkv_chunk_call(
        rkv, lora, row(rwkv_w0[0]), wup_pad, row(rwkv_a0[0]), aup_pad,
        row(rwkv_k_k[0]), row(rwkv_k_a[0]), row(rwkv_r_k[0]), batch, seq_len, rwkv_chunks)
    o_b = _rwkv_scan_call(rp, op, bonus, m, n, row(rwkv_ln_w[0]), row(rwkv_ln_b[0]),
                          batch, seq_len, scan_chunks)


    b1 = b_in1[0]
    b1p = row(jnp.concatenate([_pair_heads(b1[:MIX1], 0), b1[MIX1:]]))
    sinks_p = jnp.swapaxes(attn_sinks[0].reshape(SWA_KV_HEADS // 2, 2, SWA_GROUP), 1, 2).reshape(SWA_Q_HEADS)
    cos, slo, shi = _rope_tables(seq_len)
    h1, q, k, v, gate1 = _mid_call(o_a, o_b, gate0, x2, w_out0, row(norm_w[1]), w_in1,
                                   b1p, cos, slo, shi, seq_len, tm)
    y = _swa_call(sinks_p, q, k, v, gate1, h1, w_out1, row(b_out1[0]), row(final_norm_w),
                  batch, seq_len, swa_blocks)
    return y.reshape(batch, seq_len, D_MODEL)


def kernel(x, norm_w, w_in0, gla_gk_up, gla_gk_bias, gla_norm_w, rwkv_mu, rwkv_w0, rwkv_w_up, rwkv_a0,
           rwkv_a_up, rwkv_k_k, rwkv_k_a, rwkv_r_k, rwkv_ln_w, rwkv_ln_b, w_out0, w_in1, b_in1,
           attn_sinks, w_out1, b_out1, final_norm_w):
    return _forward(x, norm_w, w_in0, gla_gk_up, gla_gk_bias, gla_norm_w, rwkv_mu, rwkv_w0, rwkv_w_up,
                    rwkv_a0, rwkv_a_up, rwkv_k_k, rwkv_k_a, rwkv_r_k, rwkv_ln_w, rwkv_ln_b, w_out0,
                    w_in1, b_in1, attn_sinks, w_out1, b_out1, final_norm_w,
                    tm=512, gla_chunks=64, rwkv_chunks=64, scan_chunks=8, swa_blocks=8)
```

```python
import functools
import math

import jax
import jax.numpy as jnp
from jax import lax
from jax.experimental import pallas as pl
from jax.experimental.pallas import tpu as pltpu

F32 = jnp.float32
BF16 = jnp.bfloat16

D_MODEL = 1024
NORM_EPS = 1e-5

GLA_HEADS = 4
GLA_DK = 64
GLA_DV = 128
GLA_KEY = GLA_HEADS * GLA_DK
GLA_VAL = GLA_HEADS * GLA_DV
GLA_GATE_RANK = 16
GLA_GATE_NORMALIZER = 16.0
GLA_CHUNK = 64

RWKV_HEADS = 8
RWKV_HEAD = 64
RWKV_W = RWKV_HEADS * RWKV_HEAD
RWKV_DECAY_RANK = 64
RWKV_A_RANK = 64
RWKV_LN_EPS = 64e-5
RWKV_RKV = 3 * RWKV_W
RWKV_LORA = RWKV_DECAY_RANK + RWKV_A_RANK
RWKV_SHIFT = RWKV_RKV + RWKV_LORA
RWKV_CHUNK = 64
RWKV_GROUP = 16

MIX0 = GLA_VAL + RWKV_W
GLA_QKV = 2 * GLA_KEY + GLA_VAL

SWA_Q_HEADS = 16
SWA_KV_HEADS = 4
SWA_GROUP = SWA_Q_HEADS // SWA_KV_HEADS
SWA_HEAD = 64
WINDOW = 128
ROPE_DIMS = SWA_HEAD // 4
ROPE_THETA = 500000.0
MIX1 = SWA_Q_HEADS * SWA_HEAD
SWA_KV = SWA_KV_HEADS * SWA_HEAD
SWA_QKV = MIX1 + 2 * SWA_KV
SWA_PROJ_BLOCKS = 2
SWA_PROJ_PIECES = 4
SWA_PROJ_EVERY = 3

LOG2_E = 1.4426950408889634
LANES = 128
HEAD = 64
GLOW_PAD = LANES
VMEM_LIMIT = 56 * 1024 * 1024


def _cparams(sem):
    return pltpu.CompilerParams(dimension_semantics=sem, vmem_limit_bytes=VMEM_LIMIT)


def _dot(a, b):
    return jnp.dot(a.astype(BF16), b.astype(BF16), preferred_element_type=F32)


def _dot_nt(a, b):
    return lax.dot_general(a.astype(BF16), b.astype(BF16), (((1,), (1,)), ((), ())),
                           preferred_element_type=F32)


def _dot_tn(a, b):
    return lax.dot_general(a.astype(BF16), b.astype(BF16), (((0,), (0,)), ((), ())),
                           preferred_element_type=F32)


def _split2(x):
    hi = x.astype(BF16)
    lo = (x - hi.astype(F32)).astype(BF16)
    return hi, lo


def _dot_exact_rhs(a_bf16, x):
    hi, lo = _split2(x)
    return (jnp.dot(a_bf16, hi, preferred_element_type=F32)
            + jnp.dot(a_bf16, lo, preferred_element_type=F32))


def _dot_exact_lhs(x, b_bf16):
    hi, lo = _split2(x)
    return (jnp.dot(hi, b_bf16, preferred_element_type=F32)
            + jnp.dot(lo, b_bf16, preferred_element_type=F32))


def _iota(shape, dim):
    return lax.broadcasted_iota(jnp.int32, shape, dim)


def _tril_ones(n, dtype=BF16):
    return (_iota((n, n), 0) >= _iota((n, n), 1)).astype(dtype)


def _head_block_ones(n=LANES, dtype=BF16):
    return ((_iota((n, n), 0) // HEAD) == (_iota((n, n), 1) // HEAD)).astype(dtype)


def _head_stack(x):
    head = (_iota(x.shape, 1) % LANES) // HEAD
    return jnp.concatenate([jnp.where(head == 0, x, 0.0), jnp.where(head == 1, x, 0.0)], axis=0)


def _softplus(z):
    return jnp.maximum(z, 0.0) + jnp.log(1.0 + jnp.exp(-jnp.abs(z)))


def _sigmoid(z):
    return 0.5 + 0.5 * jnp.tanh(0.5 * z)


def _silu_bf16(g):
    h = g * 0.5
    return h + h * jnp.tanh(h)


def _rmsnorm_rows(x, w):
    return x * lax.rsqrt(jnp.mean(x * x, axis=-1, keepdims=True) + NORM_EPS) * w


def _in0_kernel(x_ref, nw_ref, w_ref, mu_ref,
                gqkv_ref, glow_ref, rkv_ref, lora_ref, gate_ref, carry_ref, wg_ref, wl_ref, wr_ref,
                *, tiles_per_seq):
    i = pl.program_id(0)

    @pl.when(i == 0)
    def _():
        carry_ref[...] = jnp.zeros_like(carry_ref)
        def put(dst, row0, width):
            step = 4 * LANES if width % (4 * LANES) == 0 else LANES
            for c in range(0, width, step):
                dst[:, c:c + step] = w_ref[0, row0 + c:row0 + c + step, :].T.astype(BF16)

        put(wg_ref, 0, GLA_QKV)
        put(wl_ref, GLA_QKV, GLOW_PAD)
        put(wr_ref, GLA_QKV + GLA_GATE_RANK, RWKV_SHIFT + MIX0)

    xn = _rmsnorm_rows(x_ref[...], nw_ref[...]).astype(BF16)
    gqkv_ref[...] = jnp.dot(xn, wg_ref[...], preferred_element_type=F32).astype(gqkv_ref.dtype)
    glow_ref[...] = jnp.dot(xn, wl_ref[...], preferred_element_type=F32)
    rw = jnp.dot(xn, wr_ref[:, :RWKV_SHIFT], preferred_element_type=F32)
    gate_ref[...] = jnp.dot(xn, wr_ref[:, RWKV_SHIFT:], preferred_element_type=F32).astype(gate_ref.dtype)

    tm = rw.shape[0]
    first = (i % tiles_per_seq) == 0
    prev_last = jnp.where(first, 0.0, carry_ref[7:8, :])
    rolled = pltpu.roll(rw, 1, 0)
    prev = jnp.where(_iota(rw.shape, 0) == 0, prev_last, rolled)
    mixed = rw + (prev - rw) * mu_ref[...]
    rkv_ref[...] = mixed[:, :RWKV_RKV].astype(rkv_ref.dtype)
    lora_ref[...] = mixed[:, RWKV_RKV:]
    carry_ref[...] = rw[tm - 8:tm, :]


def _in0_call(x2, norm_w, w_in, mu, seq_len, tm):
    n_tok = x2.shape[0]
    row = lambda i: (i, 0)
    const = lambda i: (0, 0)
    outs = [(GLA_QKV, BF16), (GLOW_PAD, F32), (RWKV_RKV, BF16), (RWKV_LORA, F32), (MIX0, BF16)]
    return pl.pallas_call(
        functools.partial(_in0_kernel, tiles_per_seq=seq_len // tm),
        grid=(n_tok // tm,),
        in_specs=[pl.BlockSpec((tm, D_MODEL), row),
                  pl.BlockSpec((1, D_MODEL), const),
                  pl.BlockSpec((1,) + w_in.shape[1:], lambda i: (0, 0, 0), pipeline_mode=pl.Buffered(1)),
                  pl.BlockSpec((1, RWKV_SHIFT), const)],
        out_specs=[pl.BlockSpec((tm, n), row) for n, _ in outs],
        out_shape=[jax.ShapeDtypeStruct((n_tok, n), dt) for n, dt in outs],
        scratch_shapes=[pltpu.VMEM((8, RWKV_SHIFT), F32),
                        pltpu.VMEM((D_MODEL, GLA_QKV), BF16),
                        pltpu.VMEM((D_MODEL, GLOW_PAD), BF16),
                        pltpu.VMEM((D_MODEL, RWKV_SHIFT + MIX0), BF16)],
        compiler_params=_cparams(("arbitrary",)),
        name="l0_norm_proj",
    )(x2, norm_w, w_in, mu)


def _gla_kernel(q_ref, k_ref, glow_ref, v_ref, up_ref, bias_ref, nw_ref, o_ref, st_ref, *, chunks):
    c = pl.program_id(2)

    @pl.when(c == 0)
    def _():
        st_ref[...] = jnp.zeros_like(st_ref)

    C = GLA_CHUNK
    tril = _tril_ones(C)
    causal = _iota((C, LANES), 0) >= (_iota((C, LANES), 1) % HEAD)
    sr = _iota((2 * GLA_DV, LANES), 0)
    sl = _iota((2 * GLA_DV, LANES), 1)
    st_mask = (sr // GLA_DV) == (sl // HEAD)
    vl = _iota((C, 2 * GLA_DV), 1)
    scale = GLA_DK ** -0.5
    z = _dot(glow_ref[...], up_ref[...]) + bias_ref[...]
    g_all = -_softplus(-z) * (LOG2_E / GLA_GATE_NORMALIZER)
    q_all = q_ref[...].astype(F32) * scale
    k_all = k_ref[...].astype(F32)
    rows = [slice(j * C, (j + 1) * C) for j in range(chunks)]
    bs = [_dot_exact_rhs(tril, g_all[rw]) for rw in rows]
    qe, ke, qb, kl, dec, vs = [], [], [], [], [], []
    for rw, b in zip(rows, bs):
        ref = b[C // 2:C // 2 + 1, :]
        b_last = b[C - 1:C, :]
        qe.append(q_all[rw] * jnp.exp2(b - ref))
        ke.append(k_all[rw] * jnp.exp2(ref - b))
        qb.append(qe[-1] * jnp.exp2(ref))
        kl.append(ke[-1] * jnp.exp2(b_last - ref))
        dec.append(jnp.exp2(b_last))
        vs.append(v_ref[rw, :])
    att = [jnp.where(causal, _dot_nt(qe[j], _head_stack(ke[j])), 0.0) for j in range(chunks)]
    kv = [jnp.where(st_mask, _dot_tn(vs[j], kl[j]), 0.0) for j in range(chunks)]
    v_diag = [jnp.concatenate([jnp.where(vl < GLA_DV, vs[j], jnp.zeros_like(vs[j])),
                               jnp.where(vl >= GLA_DV, vs[j], jnp.zeros_like(vs[j]))], axis=0)
              for j in range(chunks)]
    intra = [jnp.dot(att[j].astype(BF16), v_diag[j], preferred_element_type=F32) for j in range(chunks)]
    states = [st_ref[...]]
    for j in range(chunks):
        states.append(states[j] * dec[j] + kv[j])
    st_ref[...] = states[chunks]
    for j in range(chunks):
        o = intra[j] + _dot_nt(qb[j], states[j])
        for h in range(2):
            oh = o[:, h * GLA_DV:(h + 1) * GLA_DV]
            oh = oh * lax.rsqrt(jnp.mean(oh * oh, axis=-1, keepdims=True) + NORM_EPS) * nw_ref[...]
            o_ref[rows[j], h * GLA_DV:(h + 1) * GLA_DV] = oh.astype(o_ref.dtype)


def _gla_call(gqkv, glow, up_pad, bias, norm_w, batch, seq_len, chunks):
    n_tok = gqkv.shape[0]
    tcb = chunks * GLA_CHUNK
    steps = seq_len // tcb
    pairs = GLA_KEY // LANES
    return pl.pallas_call(
        functools.partial(_gla_kernel, chunks=chunks),
        grid=(batch, pairs, steps),
        in_specs=[pl.BlockSpec((tcb, LANES), lambda b, p, c: (b * steps + c, p)),
                  pl.BlockSpec((tcb, LANES), lambda b, p, c: (b * steps + c, pairs + p)),
                  pl.BlockSpec((tcb, GLOW_PAD), lambda b, p, c: (b * steps + c, 0)),
                  pl.BlockSpec((tcb, 2 * GLA_DV), lambda b, p, c: (b * steps + c, pairs + p)),
                  pl.BlockSpec((GLOW_PAD, LANES), lambda b, p, c: (0, p)),
                  pl.BlockSpec((1, LANES), lambda b, p, c: (0, p)),
                  pl.BlockSpec((1, GLA_DV), lambda b, p, c: (0, 0))],
        out_specs=pl.BlockSpec((tcb, 2 * GLA_DV), lambda b, p, c: (b * steps + c, p)),
        out_shape=jax.ShapeDtypeStruct((n_tok, GLA_VAL), BF16),
        scratch_shapes=[pltpu.VMEM((2 * GLA_DV, LANES), F32)],
        compiler_params=_cparams(("parallel", "parallel", "arbitrary")),
        name="l0_gla",
    )(gqkv, gqkv, glow, gqkv, up_pad, bias, norm_w)


def _merge_masks(n):
    r = _iota((n, LANES), 0)
    c = _iota((n, LANES), 1) % HEAD
    masks = []
    s = 1
    while s < n:
        masks.append(((r // s) % 2 == 1) & ((c // s) == (r // s) - 1))
        s *= 2
    return (r == c).astype(F32), masks


def _run_interleaved(main, main_steps, side, side_steps):
    done = 0
    spread = max(1, (3 * main_steps) // 4)
    for i, _ in enumerate(main):
        target = -(-(i + 1) * side_steps // spread)
        while done < min(target, side_steps):
            next(side, None)
            done += 1
    for _ in side:
        pass


def _rwkv_chunk_kernel(r_ref, k_ref, v_ref, xwa_ref, w0_ref, wup_ref, a0_ref, aup_ref,
                       kk_ref, ka_ref, rk_ref,
                       rp_ref, op_ref, bonus_ref, m_ref, n_ref, *, chunks, group):
    C = RWKV_CHUNK
    tril = _tril_ones(C)
    rr = _iota((2 * C, LANES), 0)
    cc = _iota((2 * C, LANES), 1) % HEAD
    tri2 = ((rr < C) & (rr > cc)) | (rr - C >= cc)
    hb = _head_block_ones()
    sq_r = _iota((LANES, LANES), 0)
    sq_c = _iota((LANES, LANES), 1)
    same_head = (sq_r // HEAD) == (sq_c // HEAD)
    eye128 = sq_r == sq_c

    eye, merge = _merge_masks(C)
    zero = jnp.zeros((C, LANES), F32)
    n = range(group)

    def prepare(g, out):
        rows = slice(g * group * C, (g + 1) * group * C)
        r_all = r_ref[rows, :].astype(F32)
        k_all = k_ref[rows, :].astype(F32)
        v_all = v_ref[rows, :].astype(F32)
        xwa = xwa_ref[rows, :]
        y = w0_ref[...] + _dot(jnp.tanh(xwa), wup_ref[...])
        lw_all = _sigmoid(y) * (-LOG2_E * math.exp(-0.5))
        a_sig = _sigmoid(a0_ref[...] + _dot(xwa, aup_ref[...]))
        kk = k_all * kk_ref[...]
        kk = kk * lax.rsqrt(jnp.maximum(_dot(kk * kk, hb), 1e-24))
        k_all = k_all * (1.0 + (a_sig - 1.0) * ka_ref[...])
        bonus_ref[rows, :] = (_dot(r_all * k_all * rk_ref[...], hb) * v_all).astype(bonus_ref.dtype)
        a_all = -kk
        b_all = kk * a_sig
        yield
        for j in n:
            rw = slice(j * C, (j + 1) * C)
            cum = _dot_exact_rhs(tril, lw_all[rw])
            cum_last = cum[C - 1:C, :]
            e_neg = jnp.exp2(-cum)
            e_end = jnp.exp2(cum_last - cum)
            out.append(dict(
                rt=r_all[rw] * jnp.exp2(cum),
                at=a_all[rw] * jnp.exp2(cum - lw_all[rw]),
                bt=b_all[rw] * e_neg,
                kt=k_all[rw] * e_neg,
                ends=jnp.concatenate([b_all[rw] * e_end, k_all[rw] * e_end], axis=0),
                v=v_all[rw],
                dec=jnp.exp2(cum_last)))
            yield

    def solve(g, ops):
        lhs = [jnp.concatenate([o["at"], o["rt"]], axis=0) for o in ops]
        left = [jnp.where(tri2, _dot_nt(lhs[j], _head_stack(ops[j]["bt"])), 0.0) for j in n]
        yield
        right = [jnp.where(tri2, _dot_nt(lhs[j], _head_stack(ops[j]["kt"])), 0.0) for j in n]
        yield
        lows = [lf[:C] for lf in left]
        ts = [eye + jnp.where(merge[0], low, 0.0) for low in lows]
        for sub in merge[1:]:
            ys = [_dot(jnp.where(sub, low, 0.0), _head_stack(t)) for low, t in zip(lows, ts)]
            yield
            ts = [t + _dot(t, _head_stack(y)) for t, y in zip(ts, ys)]
            yield
        kv = [_dot(right[j], _head_stack(ops[j]["v"])) for j in n]
        yield
        wz = [_dot(ts[j], _head_stack(jnp.concatenate([ops[j]["at"], kv[j][:C]], axis=1))) for j in n]
        yield
        ro = [_dot(left[j][C:], _head_stack(wz[j])) for j in n]
        yield
        mn = [_dot_tn(ops[j]["ends"],
                      jnp.concatenate([wz[j], jnp.concatenate([zero, ops[j]["v"]], axis=1)], axis=0))
              for j in n]
        for j in n:
            c = g * group + j
            rows = slice(c * C, (c + 1) * C)
            rp_ref[rows, :] = (ops[j]["rt"] + ro[j][:, :LANES]).astype(rp_ref.dtype)
            op_ref[rows, :] = ro[j][:, LANES:] + kv[j][C:]
            m_ref[0, 0, c] = (jnp.where(eye128, ops[j]["dec"], 0.0)
                              + jnp.where(same_head, mn[j][:, :LANES], 0.0)).astype(m_ref.dtype)
            n_ref[0, 0, c] = jnp.where(same_head, mn[j][:, LANES:], 0.0).astype(n_ref.dtype)
        yield

    solve_stages = 2 + 2 * (len(merge) - 1) + 4
    groups = chunks // group
    ops = [[] for _ in range(groups + 1)]
    for _ in prepare(0, ops[0]):
        pass
    for g in range(groups):
        side = prepare(g + 1, ops[g + 1]) if g + 1 < groups else iter(())
        _run_interleaved(solve(g, ops[g]), solve_stages, side, group + 1)


def _rwkv_chunk_call(rkv, lora, w0, wup_pad, a0, aup_pad, k_k, k_a, r_k, batch, seq_len, chunks):
    n_tok = rkv.shape[0]
    tcb = chunks * RWKV_CHUNK
    steps = seq_len // tcb
    pairs = RWKV_W // LANES
    nc = seq_len // RWKV_CHUNK
    col = lambda off: (lambda b, p, c: (b * steps + c, off + p))
    par = lambda b, p, c: (0, p)
    tok = lambda b, p, c: (b * steps + c, p)
    mat = lambda b, p, c: (b, p, c, 0, 0)
    return pl.pallas_call(
        functools.partial(_rwkv_chunk_kernel, chunks=chunks, group=RWKV_GROUP),
        grid=(batch, pairs, steps),
        in_specs=[pl.BlockSpec((tcb, LANES), col(0)),
                  pl.BlockSpec((tcb, LANES), col(pairs)),
                  pl.BlockSpec((tcb, LANES), col(2 * pairs)),
                  pl.BlockSpec((tcb, RWKV_LORA), lambda b, p, c: (b * steps + c, 0)),
                  pl.BlockSpec((1, LANES), par),
                  pl.BlockSpec((RWKV_LORA, LANES), par),
                  pl.BlockSpec((1, LANES), par),
                  pl.BlockSpec((RWKV_LORA, LANES), par),
                  pl.BlockSpec((1, LANES), par),
                  pl.BlockSpec((1, LANES), par),
                  pl.BlockSpec((1, LANES), par)],
        out_specs=[pl.BlockSpec((tcb, LANES), tok),
                   pl.BlockSpec((tcb, LANES), tok),
                   pl.BlockSpec((tcb, LANES), tok),
                   pl.BlockSpec((1, 1, chunks, LANES, LANES), mat),
                   pl.BlockSpec((1, 1, chunks, LANES, LANES), mat)],
        out_shape=[jax.ShapeDtypeStruct((n_tok, RWKV_W), BF16),
                   jax.ShapeDtypeStruct((n_tok, RWKV_W), F32),
                   jax.ShapeDtypeStruct((n_tok, RWKV_W), BF16),
                   jax.ShapeDtypeStruct((batch, pairs, nc, LANES, LANES), BF16),
                   jax.ShapeDtypeStruct((batch, pairs, nc, LANES, LANES), BF16)],
        compiler_params=_cparams(("parallel", "parallel", "parallel")),
        name="l0_rwkv_chunks",
    )(rkv, rkv, rkv, lora, w0, wup_pad, a0, aup_pad, k_k, k_a, r_k)


def _rwkv_scan_kernel(rp_ref, op_ref, bonus_ref, m_ref, n_ref, lnw_ref, lnb_ref, o_ref, st_ref, *, chunks):
    c = pl.program_id(0)

    @pl.when(c == 0)
    def _():
        st_ref[...] = jnp.zeros_like(st_ref)

    C = RWKV_CHUNK
    batch = rp_ref.shape[0]
    pairs = RWKV_W // LANES
    hb = _head_block_ones()
    seqs = [(b, p) for b in range(batch) for p in range(pairs)]
    states = {bp: [st_ref[bp[0], bp[1]]] for bp in seqs}
    for j in range(chunks):
        for b, p in seqs:
            states[b, p].append(_dot(m_ref[b, p, j], states[b, p][j]) + n_ref[b, p, j])
    for b, p in seqs:
        st_ref[b, p] = states[b, p][chunks]
    cols = {bp: slice(bp[1] * LANES, (bp[1] + 1) * LANES) for bp in seqs}
    os = [jnp.concatenate([_dot(rp_ref[b, j * C:(j + 1) * C, cols[b, p]], states[b, p][j])
                           for j in range(chunks)], axis=0) + op_ref[b, :, cols[b, p]] for b, p in seqs]
    means = [_dot_exact_lhs(o, hb) * (1.0 / RWKV_HEAD) for o in os]
    ds = [o - mean for o, mean in zip(os, means)]
    variances = [_dot(d * d, hb) * (1.0 / RWKV_HEAD) for d in ds]
    for (b, p), d, var in zip(seqs, ds, variances):
        c_ = cols[b, p]
        o_ref[b, :, c_] = (d * lax.rsqrt(var + RWKV_LN_EPS) * lnw_ref[:, c_] + lnb_ref[:, c_]
                           + bonus_ref[b, :, c_]).astype(o_ref.dtype)


def _rwkv_scan_call(rp, op, bonus, m, n, ln_w, ln_b, batch, seq_len, chunks):
    tcb = chunks * RWKV_CHUNK
    pairs = RWKV_W // LANES
    seq3 = lambda t: t.reshape(batch, seq_len, RWKV_W)
    tok = lambda c: (0, c, 0)
    const = lambda c: (0, 0)
    mat = lambda c: (0, 0, c, 0, 0)
    out = pl.pallas_call(
        functools.partial(_rwkv_scan_kernel, chunks=chunks),
        grid=(seq_len // tcb,),
        in_specs=[pl.BlockSpec((batch, tcb, RWKV_W), tok),
                  pl.BlockSpec((batch, tcb, RWKV_W), tok),
                  pl.BlockSpec((batch, tcb, RWKV_W), tok),
                  pl.BlockSpec((batch, pairs, chunks, LANES, LANES), mat),
                  pl.BlockSpec((batch, pairs, chunks, LANES, LANES), mat),
                  pl.BlockSpec((1, RWKV_W), const),
                  pl.BlockSpec((1, RWKV_W), const)],
        out_specs=pl.BlockSpec((batch, tcb, RWKV_W), tok),
        out_shape=jax.ShapeDtypeStruct((batch, seq_len, RWKV_W), BF16),
        scratch_shapes=[pltpu.VMEM((batch, pairs, LANES, LANES), F32)],
        compiler_params=_cparams(("arbitrary",)),
        name="l0_rwkv_scan",
    )(seq3(rp), seq3(op), seq3(bonus), m, n, ln_w, ln_b)
    return out.reshape(batch * seq_len, RWKV_W)


def _gated_out0(oa_ref, ob_ref, gate_ref, x_ref, w_ref):
    g = _silu_bf16(gate_ref[...])
    ya = oa_ref[...] * g[:, :GLA_VAL]
    yb = ob_ref[...] * g[:, GLA_VAL:]
    return (x_ref[...]
            + jnp.dot(ya, w_ref[:GLA_VAL, :], preferred_element_type=F32)
            + jnp.dot(yb, w_ref[GLA_VAL:, :], preferred_element_type=F32))


def _rope_group(x, cos, sin_lo, sin_hi):
    half = ROPE_DIMS // 2
    return x * cos + pltpu.roll(x, LANES - half, 1) * sin_lo + pltpu.roll(x, half, 1) * sin_hi


def _paired_head_order():
    return [(2 * pp + e) * SWA_GROUP + g
            for pp in range(SWA_KV_HEADS // 2) for g in range(SWA_GROUP) for e in range(2)]


def _mid_kernel(oa_ref, ob_ref, gate0_ref, x_ref, wo32_ref,
                nw_ref, w1_ref, b_ref, cos_ref, slo_ref, shi_ref,
                h_ref, q_ref, k_ref, v_ref, gate_ref, wo_ref, wq_ref, wkv_ref, wg_ref):
    @pl.when(pl.program_id(0) == 0)
    def _():
        wo_ref[...] = wo32_ref[0].astype(BF16)
        wkv_ref[...] = w1_ref[0, :, MIX1:SWA_QKV].astype(BF16)
        for new, old in enumerate(_paired_head_order()):
            dst = slice(new * SWA_HEAD, (new + 1) * SWA_HEAD)
            wq_ref[:, dst] = w1_ref[0, :, old * SWA_HEAD:(old + 1) * SWA_HEAD].astype(BF16)
            wg_ref[:, dst] = w1_ref[0, :, SWA_QKV + old * SWA_HEAD:SWA_QKV + (old + 1) * SWA_HEAD].astype(BF16)

    h = _gated_out0(oa_ref, ob_ref, gate0_ref, x_ref, wo_ref)
    h_ref[...] = h
    hn = _rmsnorm_rows(h, nw_ref[...]).astype(BF16)
    cos = cos_ref[...]
    slo = slo_ref[...]
    shi = shi_ref[...]
    scale = SWA_HEAD ** -0.5 * LOG2_E
    q = jnp.dot(hn, wq_ref[...], preferred_element_type=F32) + b_ref[:, :MIX1]
    kv = jnp.dot(hn, wkv_ref[...], preferred_element_type=F32) + b_ref[:, MIX1:]
    gate_ref[...] = jnp.dot(hn, wg_ref[...], preferred_element_type=F32).astype(gate_ref.dtype)
    for g in range(MIX1 // LANES):
        cols = slice(g * LANES, (g + 1) * LANES)
        q_ref[:, cols] = (_rope_group(q[:, cols], cos, slo, shi) * scale).astype(q_ref.dtype)
    for g in range(SWA_KV // LANES):
        cols = slice(g * LANES, (g + 1) * LANES)
        k_ref[:, cols] = _rope_group(kv[:, cols], cos, slo, shi).astype(k_ref.dtype)
    v_ref[...] = kv[:, SWA_KV:].astype(v_ref.dtype)


def _mid_call(oa, ob, gate0, x2, w_out0, norm_w, w_in1, b_in, cos, slo, shi, seq_len, tm):
    n_tok = x2.shape[0]
    tps = seq_len // tm
    row = lambda i: (i, 0)
    const = lambda i: (0, 0)
    pos = lambda i: (i % tps, 0)
    whole = lambda t: pl.BlockSpec((1,) + t.shape[1:], lambda i: (0, 0, 0), pipeline_mode=pl.Buffered(1))
    return pl.pallas_call(
        _mid_kernel,
        grid=(n_tok // tm,),
        in_specs=[pl.BlockSpec((tm, GLA_VAL), row),
                  pl.BlockSpec((tm, RWKV_W), row),
                  pl.BlockSpec((tm, MIX0), row),
                  pl.BlockSpec((tm, D_MODEL), row),
                  whole(w_out0),
                  pl.BlockSpec((1, D_MODEL), const),
                  whole(w_in1),
                  pl.BlockSpec((1, SWA_QKV), const),
                  pl.BlockSpec((tm, LANES), pos),
                  pl.BlockSpec((tm, LANES), pos),
                  pl.BlockSpec((tm, LANES), pos)],
        out_specs=[pl.BlockSpec((tm, D_MODEL), row),
                   pl.BlockSpec((tm, MIX1), row),
                   pl.BlockSpec((tm, SWA_KV), row),
                   pl.BlockSpec((tm, SWA_KV), row),
                   pl.BlockSpec((tm, MIX1), row)],
        out_shape=[jax.ShapeDtypeStruct((n_tok, D_MODEL), F32),
                   jax.ShapeDtypeStruct((n_tok, MIX1), BF16),
                   jax.ShapeDtypeStruct((n_tok, SWA_KV), BF16),
                   jax.ShapeDtypeStruct((n_tok, SWA_KV), BF16),
                   jax.ShapeDtypeStruct((n_tok, MIX1), BF16)],
        scratch_shapes=[pltpu.VMEM((MIX0, D_MODEL), BF16),
                        pltpu.VMEM((D_MODEL, MIX1), BF16),
                        pltpu.VMEM((D_MODEL, 2 * SWA_KV), BF16),
                        pltpu.VMEM((D_MODEL, MIX1), BF16)],
        compiler_params=_cparams(("arbitrary",)),
        name="l0_out_l1_proj",
    )(oa, ob, gate0, x2, w_out0, norm_w, w_in1, b_in, cos, slo, shi)


def _swa_kernel(sink_ref, q_ref, kc_ref, kp_ref, vc_ref, vp_ref, gate_ref, h_ref, w32_ref, b_ref, nw_ref,
                y_ref, o_ref, w_ref, *, q_blocks):
    n = pl.program_id(1)

    @pl.when((pl.program_id(0) == 0) & (n == 0))
    def _():
        for new, old in enumerate(_paired_head_order()):
            w_ref[new * SWA_HEAD:(new + 1) * SWA_HEAD, :] = (
                w32_ref[0, old * SWA_HEAD:(old + 1) * SWA_HEAD, :].astype(BF16))

    W = WINDOW
    from_prev = _iota((W, 2 * W), 0) > (_iota((W, 2 * W), 1) % W)
    no_prev = jnp.where(n > 0, 0.0, -jnp.inf)
    col_row = _iota((1, 2 * W), 1)
    out_row = _iota((LANES, W), 0)
    kv_groups = SWA_KV // LANES
    groups = MIX1 // LANES // kv_groups
    tasks = [(j, pp, pp * groups + g) for j in range(q_blocks) for pp in range(kv_groups) for g in range(groups)]
    kk, vt = {}, {}
    for j in range(q_blocks):
        for pp in range(kv_groups):
            cols = slice(pp * LANES, (pp + 1) * LANES)
            if j == 0:
                kk[j, pp] = jnp.concatenate([kp_ref[:, cols], kc_ref[:W, cols]], axis=0)
                vv = jnp.concatenate([vp_ref[:, cols], vc_ref[:W, cols]], axis=0)
            else:
                kk[j, pp] = kc_ref[(j - 1) * W:(j + 1) * W, cols]
                vv = vc_ref[(j - 1) * W:(j + 1) * W, cols]
            vt[j, pp] = vv.astype(F32).T.astype(BF16)

    def scores(j, pp, blk):
        q = q_ref[j * W:(j + 1) * W, blk * LANES:(blk + 1) * LANES]
        return lax.dot_general(kk[j, pp], _head_stack(q), (((1,), (1,)), ((), ())), preferred_element_type=F32)

    def projection_pieces(rows):
        width = D_MODEL // SWA_PROJ_PIECES
        gated = []

        def piece(c):
            def run():
                if not gated:
                    gated.append(o_ref[rows, :] * _silu_bf16(gate_ref[rows, :]))
                cols = slice(c * width, (c + 1) * width)
                y_ref[rows, cols] = (h_ref[rows, cols] + b_ref[:, cols]
                                     + jnp.dot(gated[0], w_ref[:, cols], preferred_element_type=F32))
            return run

        def norm():
            y_ref[rows, :] = _rmsnorm_rows(y_ref[rows, :], nw_ref[...])

        return [piece(c) for c in range(SWA_PROJ_PIECES)] + [norm]

    projections = []
    ahead = 8
    pending = [scores(*t) for t in tasks[:ahead]]
    for i, (j, pp, blk) in enumerate(tasks):
        st = pending.pop(0)
        if i + ahead < len(tasks):
            pending.append(scores(*tasks[i + ahead]))
        s_prev = st[:W] + no_prev if j == 0 else st[:W]
        s = jnp.where(from_prev, s_prev, st[W:])
        sink = jnp.where(col_row < W, sink_ref[2 * blk], sink_ref[2 * blk + 1]) * LOG2_E
        m = jnp.maximum(jnp.max(s, axis=0, keepdims=True), sink)
        p = jnp.exp2(s - m)
        denom = jnp.sum(p, axis=0, keepdims=True) + jnp.exp2(sink - m)
        pb = p.astype(BF16)
        zero = jnp.zeros_like(pb)
        p2 = jnp.concatenate([jnp.where(from_prev, pb, zero), jnp.where(from_prev, zero, pb)], axis=0)
        ot = jnp.dot(vt[j, pp], p2, preferred_element_type=F32) * (1.0 / denom)
        ot = jnp.where(out_row < HEAD, ot[:, :W], ot[:, W:])
        o_ref[j * W:(j + 1) * W, blk * LANES:(blk + 1) * LANES] = ot.T.astype(o_ref.dtype)

        last_of_block = i + 1 == len(tasks) or tasks[i + 1][0] != j
        if last_of_block and (j + 1) % SWA_PROJ_BLOCKS == 0:
            projections.extend(projection_pieces(slice((j + 1 - SWA_PROJ_BLOCKS) * W, (j + 1) * W)))
        if projections and (i % SWA_PROJ_EVERY == SWA_PROJ_EVERY - 1 or i + 1 == len(tasks)):
            projections.pop(0)()
    while projections:
        projections.pop(0)()


def _swa_call(sinks, q, k, v, gate, h1, w_out, b_out, norm_w, batch, seq_len, q_blocks):
    n_tok = q.shape[0]
    rows = q_blocks * WINDOW
    steps = seq_len // rows
    cur = lambda b, n: (b * steps + n, 0)
    prev = lambda b, n: (jnp.maximum((b * steps + n) * q_blocks - 1, 0), 0)
    const = lambda b, n: (0, 0)
    return pl.pallas_call(
        functools.partial(_swa_kernel, q_blocks=q_blocks),
        grid=(batch, steps),
        in_specs=[pl.BlockSpec(memory_space=pltpu.SMEM),
                  pl.BlockSpec((rows, MIX1), cur),
                  pl.BlockSpec((rows, SWA_KV), cur),
                  pl.BlockSpec((WINDOW, SWA_KV), prev),
                  pl.BlockSpec((rows, SWA_KV), cur),
                  pl.BlockSpec((WINDOW, SWA_KV), prev),
                  pl.BlockSpec((rows, MIX1), cur),
                  pl.BlockSpec((rows, D_MODEL), cur),
                  pl.BlockSpec((1,) + w_out.shape[1:], lambda b, n: (0, 0, 0), pipeline_mode=pl.Buffered(1)),
                  pl.BlockSpec((1, D_MODEL), const),
                  pl.BlockSpec((1, D_MODEL), const)],
        out_specs=pl.BlockSpec((rows, D_MODEL), cur),
        out_shape=jax.ShapeDtypeStruct((n_tok, D_MODEL), F32),
        scratch_shapes=[pltpu.VMEM((rows, MIX1), BF16),
                        pltpu.VMEM((MIX1, D_MODEL), BF16)],
        compiler_params=_cparams(("arbitrary", "arbitrary")),
        name="l1_swa_out",
    )(sinks, q, k, k, v, v, gate, h1, w_out, b_out, norm_w)


def _pad_rows(w, rows):
    return jnp.concatenate([w, jnp.zeros((rows - w.shape[0], w.shape[1]), w.dtype)], axis=0)


def _pair_heads(t, axis):
    shape = t.shape
    split = shape[:axis] + (SWA_KV_HEADS // 2, 2, SWA_GROUP, SWA_HEAD) + shape[axis + 1:]
    return jnp.swapaxes(t.reshape(split), axis + 1, axis + 2).reshape(shape)


def _rope_tables(seq_len):
    half = ROPE_DIMS // 2
    inv_freq = ROPE_THETA ** (-jnp.arange(half, dtype=F32) / half)
    ang = jnp.arange(seq_len).astype(F32)[:, None] * inv_freq
    trig = jnp.concatenate([jnp.cos(ang), jnp.sin(ang)], axis=1)
    d = jnp.arange(LANES) % SWA_HEAD
    src = jnp.arange(2 * half)[:, None]
    f = (d % half)[None, :]
    rot = (d < ROPE_DIMS)[None, :]
    lo = (d < half)[None, :]
    sel_cos = ((src == f) & rot).astype(F32)
    sel_lo = -((src == half + f) & lo).astype(F32)
    sel_hi = ((src == half + f) & rot & ~lo).astype(F32)
    sel = jnp.concatenate([sel_cos, sel_lo, sel_hi], axis=1)
    tab = jnp.dot(trig, sel, precision=lax.Precision.HIGHEST)
    cos = tab[:, :LANES] + (~rot).astype(F32)
    return cos, tab[:, LANES:2 * LANES], tab[:, 2 * LANES:]


def _forward(x, norm_w, w_in0, gla_gk_up, gla_gk_bias, gla_norm_w, rwkv_mu, rwkv_w0, rwkv_w_up,
             rwkv_a0, rwkv_a_up, rwkv_k_k, rwkv_k_a, rwkv_r_k, rwkv_ln_w, rwkv_ln_b, w_out0,
             w_in1, b_in1, attn_sinks, w_out1, b_out1, final_norm_w, *, tm, gla_chunks, rwkv_chunks, scan_chunks,
             swa_blocks):
    batch, seq_len, _ = x.shape
    x2 = x.reshape(batch * seq_len, D_MODEL)
    row = lambda t: t.reshape(1, -1)

    gqkv, glow, rkv, lora, gate0 = _in0_call(x2, row(norm_w[0]), jnp.swapaxes(w_in0, 1, 2), row(rwkv_mu[0]),
                                             seq_len, tm)

    up_pad = _pad_rows(gla_gk_up[0], GLOW_PAD).astype(BF16)
    o_a = _gla_call(gqkv, glow, up_pad, row(gla_gk_bias[0]), row(gla_norm_w[0]), batch, seq_len, gla_chunks)

    zeros_r = jnp.zeros((RWKV_DECAY_RANK, RWKV_W), F32)
    wup_pad = jnp.concatenate([rwkv_w_up[0], zeros_r], axis=0).astype(BF16)
    aup_pad = jnp.concatenate([zeros_r, rwkv_a_up[0]], axis=0).astype(BF16)
    rp, op, bonus, m, n = _rwkv_chunk_call(
        rkv, lora, row(rwkv_w0[0]), wup_pad, row(rwkv_a0[0]), aup_pad,
        row(rwkv_k_k[0]), row(rwkv_k_a[0]), row(rwkv_r_k[0]), batch, seq_len, rwkv_chunks)
    o_b = _rwkv_scan_call(rp, op, bonus, m, n, row(rwkv_ln_w[0]), row(rwkv_ln_b[0]),
                          batch, seq_len, scan_chunks)


    b1 = b_in1[0]
    b1p = row(jnp.concatenate([_pair_heads(b1[:MIX1], 0), b1[MIX1:]]))
    sinks_p = jnp.swapaxes(attn_sinks[0].reshape(SWA_KV_HEADS // 2, 2, SWA_GROUP), 1, 2).reshape(SWA_Q_HEADS)
    cos, slo, shi = _rope_tables(seq_len)
    h1, q, k, v, gate1 = _mid_call(o_a, o_b, gate0, x2, w_out0, row(norm_w[1]), w_in1,
                                   b1p, cos, slo, shi, seq_len, tm)
    y = _swa_call(sinks_p, q, k, v, gate1, h1, w_out1, row(b_out1[0]), row(final_norm_w),
                  batch, seq_len, swa_blocks)
    return y.reshape(batch, seq_len, D_MODEL)


def kernel(x, norm_w, w_in0, gla_gk_up, gla_gk_bias, gla_norm_w, rwkv_mu, rwkv_w0, rwkv_w_up, rwkv_a0,
           rwkv_a_up, rwkv_k_k, rwkv_k_a, rwkv_r_k, rwkv_ln_w, rwkv_ln_b, w_out0, w_in1, b_in1,
           attn_sinks, w_out1, b_out1, final_norm_w):
    return _forward(x, norm_w, w_in0, gla_gk_up, gla_gk_bias, gla_norm_w, rwkv_mu, rwkv_w0, rwkv_w_up,
                    rwkv_a0, rwkv_a_up, rwkv_k_k, rwkv_k_a, rwkv_r_k, rwkv_ln_w, rwkv_ln_b, w_out0,
                    w_in1, b_in1, attn_sinks, w_out1, b_out1, final_norm_w,
                    tm=512, gla_chunks=64, rwkv_chunks=64, scan_chunks=8, swa_blocks=8)
```

```python
import functools
import math

import jax
import jax.numpy as jnp
from jax import lax
from jax.experimental import pallas as pl
from jax.experimental.pallas import tpu as pltpu

F32 = jnp.float32
BF16 = jnp.bfloat16

D_MODEL = 1024
NORM_EPS = 1e-5

GLA_HEADS = 4
GLA_DK = 64
GLA_DV = 128
GLA_KEY = GLA_HEADS * GLA_DK
GLA_VAL = GLA_HEADS * GLA_DV
GLA_GATE_RANK = 16
GLA_GATE_NORMALIZER = 16.0
GLA_CHUNK = 64

RWKV_HEADS = 8
RWKV_HEAD = 64
RWKV_W = RWKV_HEADS * RWKV_HEAD
RWKV_DECAY_RANK = 64
RWKV_A_RANK = 64
RWKV_LN_EPS = 64e-5
RWKV_RKV = 3 * RWKV_W
RWKV_LORA = RWKV_DECAY_RANK + RWKV_A_RANK
RWKV_SHIFT = RWKV_RKV + RWKV_LORA
RWKV_CHUNK = 64
RWKV_GROUP = 16

MIX0 = GLA_VAL + RWKV_W
GLA_QKV = 2 * GLA_KEY + GLA_VAL

SWA_Q_HEADS = 16
SWA_KV_HEADS = 4
SWA_GROUP = SWA_Q_HEADS // SWA_KV_HEADS
SWA_HEAD = 64
WINDOW = 128
ROPE_DIMS = SWA_HEAD // 4
ROPE_THETA = 500000.0
MIX1 = SWA_Q_HEADS * SWA_HEAD
SWA_KV = SWA_KV_HEADS * SWA_HEAD
SWA_QKV = MIX1 + 2 * SWA_KV
SWA_PROJ_BLOCKS = 2
SWA_PROJ_PIECES = 4
SWA_PROJ_EVERY = 3

LOG2_E = 1.4426950408889634
LANES = 128
HEAD = 64
GLOW_PAD = LANES
VMEM_LIMIT = 56 * 1024 * 1024


def _cparams(sem):
    return pltpu.CompilerParams(dimension_semantics=sem, vmem_limit_bytes=VMEM_LIMIT)


def _dot(a, b):
    return jnp.dot(a.astype(BF16), b.astype(BF16), preferred_element_type=F32)


def _dot_nt(a, b):
    return lax.dot_general(a.astype(BF16), b.astype(BF16), (((1,), (1,)), ((), ())),
                           preferred_element_type=F32)


def _dot_tn(a, b):
    return lax.dot_general(a.astype(BF16), b.astype(BF16), (((0,), (0,)), ((), ())),
                           preferred_element_type=F32)


def _split2(x):
    hi = x.astype(BF16)
    lo = (x - hi.astype(F32)).astype(BF16)
    return hi, lo


def _dot_exact_rhs(a_bf16, x):
    hi, lo = _split2(x)
    both = jnp.dot(a_bf16, jnp.concatenate([hi, lo], axis=1), preferred_element_type=F32)
    return both[:, :x.shape[1]] + both[:, x.shape[1]:]


def _dot_exact_lhs(x, b_bf16):
    hi, lo = _split2(x)
    return (jnp.dot(hi, b_bf16, preferred_element_type=F32)
            + jnp.dot(lo, b_bf16, preferred_element_type=F32))


def _iota(shape, dim):
    return lax.broadcasted_iota(jnp.int32, shape, dim)


def _tril_ones(n, dtype=BF16):
    return (_iota((n, n), 0) >= _iota((n, n), 1)).astype(dtype)


def _head_block_ones(n=LANES, dtype=BF16):
    return ((_iota((n, n), 0) // HEAD) == (_iota((n, n), 1) // HEAD)).astype(dtype)


def _head_stack(x):
    head = (_iota(x.shape, 1) % LANES) // HEAD
    return jnp.concatenate([jnp.where(head == 0, x, 0.0), jnp.where(head == 1, x, 0.0)], axis=0)


def _softplus(z):
    return jnp.maximum(z, 0.0) + jnp.log(1.0 + jnp.exp(-jnp.abs(z)))


def _sigmoid(z):
    return 0.5 + 0.5 * jnp.tanh(0.5 * z)


def _silu_bf16(g):
    h = g * 0.5
    return h + h * jnp.tanh(h)


def _rmsnorm_rows(x, w):
    return x * lax.rsqrt(jnp.mean(x * x, axis=-1, keepdims=True) + NORM_EPS) * w


def _in0_kernel(x_ref, nw_ref, w_ref, mu_ref,
                gqkv_ref, glow_ref, rkv_ref, lora_ref, gate_ref, carry_ref, wg_ref, wl_ref, wr_ref,
                *, tiles_per_seq):
    i = pl.program_id(0)

    @pl.when(i == 0)
    def _():
        carry_ref[...] = jnp.zeros_like(carry_ref)
        def put(dst, row0, width):
            step = 4 * LANES if width % (4 * LANES) == 0 else LANES
            for c in range(0, width, step):
                dst[:, c:c + step] = w_ref[0, row0 + c:row0 + c + step, :].T.astype(BF16)

        put(wg_ref, 0, GLA_QKV)
        put(wl_ref, GLA_QKV, GLOW_PAD)
        put(wr_ref, GLA_QKV + GLA_GATE_RANK, RWKV_SHIFT + MIX0)

    xn = _rmsnorm_rows(x_ref[...], nw_ref[...]).astype(BF16)
    gqkv_ref[...] = jnp.dot(xn, wg_ref[...], preferred_element_type=F32).astype(gqkv_ref.dtype)
    glow_ref[...] = jnp.dot(xn, wl_ref[...], preferred_element_type=F32)
    rw = jnp.dot(xn, wr_ref[:, :RWKV_SHIFT], preferred_element_type=F32)
    gate_ref[...] = jnp.dot(xn, wr_ref[:, RWKV_SHIFT:], preferred_element_type=F32).astype(gate_ref.dtype)

    tm = rw.shape[0]
    first = (i % tiles_per_seq) == 0
    prev_last = jnp.where(first, 0.0, carry_ref[7:8, :])
    rolled = pltpu.roll(rw, 1, 0)
    prev = jnp.where(_iota(rw.shape, 0) == 0, prev_last, rolled)
    mixed = rw + (prev - rw) * mu_ref[...]
    rkv_ref[...] = mixed[:, :RWKV_RKV].astype(rkv_ref.dtype)
    lora_ref[...] = mixed[:, RWKV_RKV:]
    carry_ref[...] = rw[tm - 8:tm, :]


def _in0_call(x2, norm_w, w_in, mu, seq_len, tm):
    n_tok = x2.shape[0]
    row = lambda i: (i, 0)
    const = lambda i: (0, 0)
    outs = [(GLA_QKV, BF16), (GLOW_PAD, F32), (RWKV_RKV, BF16), (RWKV_LORA, F32), (MIX0, BF16)]
    return pl.pallas_call(
        functools.partial(_in0_kernel, tiles_per_seq=seq_len // tm),
        grid=(n_tok // tm,),
        in_specs=[pl.BlockSpec((tm, D_MODEL), row),
                  pl.BlockSpec((1, D_MODEL), const),
                  pl.BlockSpec((1,) + w_in.shape[1:], lambda i: (0, 0, 0), pipeline_mode=pl.Buffered(1)),
                  pl.BlockSpec((1, RWKV_SHIFT), const)],
        out_specs=[pl.BlockSpec((tm, n), row) for n, _ in outs],
        out_shape=[jax.ShapeDtypeStruct((n_tok, n), dt) for n, dt in outs],
        scratch_shapes=[pltpu.VMEM((8, RWKV_SHIFT), F32),
                        pltpu.VMEM((D_MODEL, GLA_QKV), BF16),
                        pltpu.VMEM((D_MODEL, GLOW_PAD), BF16),
                        pltpu.VMEM((D_MODEL, RWKV_SHIFT + MIX0), BF16)],
        compiler_params=_cparams(("arbitrary",)),
        name="l0_norm_proj",
    )(x2, norm_w, w_in, mu)


def _gla_kernel(q_ref, k_ref, glow_ref, v_ref, up_ref, bias_ref, nw_ref, o_ref, st_ref, *, chunks):
    c = pl.program_id(2)

    @pl.when(c == 0)
    def _():
        st_ref[...] = jnp.zeros_like(st_ref)

    C = GLA_CHUNK
    tril = _tril_ones(C)
    causal = _iota((C, LANES), 0) >= (_iota((C, LANES), 1) % HEAD)
    sr = _iota((2 * GLA_DV, LANES), 0)
    sl = _iota((2 * GLA_DV, LANES), 1)
    st_mask = (sr // GLA_DV) == (sl // HEAD)
    vl = _iota((C, 2 * GLA_DV), 1)
    scale = GLA_DK ** -0.5
    z = _dot(glow_ref[...], up_ref[...]) + bias_ref[...]
    g_all = -_softplus(-z) * (LOG2_E / GLA_GATE_NORMALIZER)
    q_all = q_ref[...].astype(F32) * scale
    k_all = k_ref[...].astype(F32)
    rows = [slice(j * C, (j + 1) * C) for j in range(chunks)]
    bs = [_dot_exact_rhs(tril, g_all[rw]) for rw in rows]
    qe, ke, qb, kl, dec, vs = [], [], [], [], [], []
    for rw, b in zip(rows, bs):
        ref = b[C // 2:C // 2 + 1, :]
        b_last = b[C - 1:C, :]
        qe.append(q_all[rw] * jnp.exp2(b - ref))
        ke.append(k_all[rw] * jnp.exp2(ref - b))
        qb.append(qe[-1] * jnp.exp2(ref))
        kl.append(ke[-1] * jnp.exp2(b_last - ref))
        dec.append(jnp.exp2(b_last))
        vs.append(v_ref[rw, :])
    att = [jnp.where(causal, _dot_nt(qe[j], _head_stack(ke[j])), 0.0) for j in range(chunks)]
    kv = [jnp.where(st_mask, _dot_tn(vs[j], kl[j]), 0.0) for j in range(chunks)]
    v_diag = [jnp.concatenate([jnp.where(vl < GLA_DV, vs[j], jnp.zeros_like(vs[j])),
                               jnp.where(vl >= GLA_DV, vs[j], jnp.zeros_like(vs[j]))], axis=0)
              for j in range(chunks)]
    intra = [jnp.dot(att[j].astype(BF16), v_diag[j], preferred_element_type=F32) for j in range(chunks)]
    states = [st_ref[...]]
    for j in range(chunks):
        states.append(states[j] * dec[j] + kv[j])
    st_ref[...] = states[chunks]
    for j in range(chunks):
        o = intra[j] + _dot_nt(qb[j], states[j])
        for h in range(2):
            oh = o[:, h * GLA_DV:(h + 1) * GLA_DV]
            oh = oh * lax.rsqrt(jnp.mean(oh * oh, axis=-1, keepdims=True) + NORM_EPS) * nw_ref[...]
            o_ref[rows[j], h * GLA_DV:(h + 1) * GLA_DV] = oh.astype(o_ref.dtype)


def _gla_call(gqkv, glow, up_pad, bias, norm_w, batch, seq_len, chunks):
    n_tok = gqkv.shape[0]
    tcb = chunks * GLA_CHUNK
    steps = seq_len // tcb
    pairs = GLA_KEY // LANES
    return pl.pallas_call(
        functools.partial(_gla_kernel, chunks=chunks),
        grid=(batch, pairs, steps),
        in_specs=[pl.BlockSpec((tcb, LANES), lambda b, p, c: (b * steps + c, p)),
                  pl.BlockSpec((tcb, LANES), lambda b, p, c: (b * steps + c, pairs + p)),
                  pl.BlockSpec((tcb, GLOW_PAD), lambda b, p, c: (b * steps + c, 0)),
                  pl.BlockSpec((tcb, 2 * GLA_DV), lambda b, p, c: (b * steps + c, pairs + p)),
                  pl.BlockSpec((GLOW_PAD, LANES), lambda b, p, c: (0, p)),
                  pl.BlockSpec((1, LANES), lambda b, p, c: (0, p)),
                  pl.BlockSpec((1, GLA_DV), lambda b, p, c: (0, 0))],
        out_specs=pl.BlockSpec((tcb, 2 * GLA_DV), lambda b, p, c: (b * steps + c, p)),
        out_shape=jax.ShapeDtypeStruct((n_tok, GLA_VAL), BF16),
        scratch_shapes=[pltpu.VMEM((2 * GLA_DV, LANES), F32)],
        compiler_params=_cparams(("parallel", "parallel", "arbitrary")),
        name="l0_gla",
    )(gqkv, gqkv, glow, gqkv, up_pad, bias, norm_w)


def _merge_masks(n):
    r = _iota((n, LANES), 0)
    c = _iota((n, LANES), 1) % HEAD
    masks = []
    s = 1
    while s < n:
        masks.append(((r // s) % 2 == 1) & ((c // s) == (r // s) - 1))
        s *= 2
    return (r == c).astype(F32), masks


def _run_interleaved(main, main_steps, side, side_steps):
    done = 0
    spread = max(1, (3 * main_steps) // 4)
    for i, _ in enumerate(main):
        target = -(-(i + 1) * side_steps // spread)
        while done < min(target, side_steps):
            next(side, None)
            done += 1
    for _ in side:
        pass


def _rwkv_chunk_kernel(r_ref, k_ref, v_ref, xwa_ref, w0_ref, wup_ref, a0_ref, aup_ref,
                       kk_ref, ka_ref, rk_ref,
                       rp_ref, op_ref, bonus_ref, m_ref, n_ref, *, chunks, group):
    C = RWKV_CHUNK
    tril = _tril_ones(C)
    rr = _iota((2 * C, LANES), 0)
    cc = _iota((2 * C, LANES), 1) % HEAD
    tri2 = ((rr < C) & (rr > cc)) | (rr - C >= cc)
    hb = _head_block_ones()
    sq_r = _iota((LANES, LANES), 0)
    sq_c = _iota((LANES, LANES), 1)
    same_head = (sq_r // HEAD) == (sq_c // HEAD)
    eye128 = sq_r == sq_c

    eye, merge = _merge_masks(C)
    zero = jnp.zeros((C, LANES), F32)
    n = range(group)

    def prepare(g, out):
        rows = slice(g * group * C, (g + 1) * group * C)
        r_all = r_ref[rows, :].astype(F32)
        k_all = k_ref[rows, :].astype(F32)
        v_all = v_ref[rows, :].astype(F32)
        xwa = xwa_ref[rows, :]
        y = w0_ref[...] + _dot(jnp.tanh(xwa), wup_ref[...])
        lw_all = _sigmoid(y) * (-LOG2_E * math.exp(-0.5))
        a_sig = _sigmoid(a0_ref[...] + _dot(xwa, aup_ref[...]))
        kk = k_all * kk_ref[...]
        kk = kk * lax.rsqrt(jnp.maximum(_dot(kk * kk, hb), 1e-24))
        k_all = k_all * (1.0 + (a_sig - 1.0) * ka_ref[...])
        bonus_ref[rows, :] = (_dot(r_all * k_all * rk_ref[...], hb) * v_all).astype(bonus_ref.dtype)
        a_all = -kk
        b_all = kk * a_sig
        yield
        for j in n:
            rw = slice(j * C, (j + 1) * C)
            cum = _dot_exact_rhs(tril, lw_all[rw])
            cum_last = cum[C - 1:C, :]
            e_neg = jnp.exp2(-cum)
            e_end = jnp.exp2(cum_last - cum)
            out.append(dict(
                rt=r_all[rw] * jnp.exp2(cum),
                at=a_all[rw] * jnp.exp2(cum - lw_all[rw]),
                bt=b_all[rw] * e_neg,
                kt=k_all[rw] * e_neg,
                ends=jnp.concatenate([b_all[rw] * e_end, k_all[rw] * e_end], axis=0),
                v=v_all[rw],
                dec=jnp.exp2(cum_last)))
            yield

    def solve(g, ops):
        lhs = [jnp.concatenate([o["at"], o["rt"]], axis=0) for o in ops]
        both = [_dot_nt(lhs[j], jnp.concatenate([_head_stack(ops[j]["bt"]), _head_stack(ops[j]["kt"])], axis=0))
                for j in n]
        yield
        left = [jnp.where(tri2, p[:, :LANES], 0.0) for p in both]
        right = [jnp.where(tri2, p[:, LANES:], 0.0) for p in both]
        yield
        lows = [lf[:C] for lf in left]
        ts = [eye + jnp.where(merge[0], low, 0.0) for low in lows]
        for sub in merge[1:]:
            ys = [_dot(jnp.where(sub, low, 0.0), _head_stack(t)) for low, t in zip(lows, ts)]
            yield
            ts = [t + _dot(t, _head_stack(y)) for t, y in zip(ts, ys)]
            yield
        kv = [_dot(right[j], _head_stack(ops[j]["v"])) for j in n]
        yield
        wz = [_dot(ts[j], _head_stack(jnp.concatenate([ops[j]["at"], kv[j][:C]], axis=1))) for j in n]
        yield
        ro = [_dot(left[j][C:], _head_stack(wz[j])) for j in n]
        yield
        mn = [_dot_tn(ops[j]["ends"],
                      jnp.concatenate([wz[j], jnp.concatenate([zero, ops[j]["v"]], axis=1)], axis=0))
              for j in n]
        for j in n:
            c = g * group + j
            rows = slice(c * C, (c + 1) * C)
            rp_ref[rows, :] = (ops[j]["rt"] + ro[j][:, :LANES]).astype(rp_ref.dtype)
            op_ref[rows, :] = ro[j][:, LANES:] + kv[j][C:]
            m_ref[0, 0, c] = (jnp.where(eye128, ops[j]["dec"], 0.0)
                              + jnp.where(same_head, mn[j][:, :LANES], 0.0)).astype(m_ref.dtype)
            n_ref[0, 0, c] = jnp.where(same_head, mn[j][:, LANES:], 0.0).astype(n_ref.dtype)
        yield

    solve_stages = 2 + 2 * (len(merge) - 1) + 4
    groups = chunks // group
    ops = [[] for _ in range(groups + 1)]
    for _ in prepare(0, ops[0]):
        pass
    for g in range(groups):
        side = prepare(g + 1, ops[g + 1]) if g + 1 < groups else iter(())
        _run_interleaved(solve(g, ops[g]), solve_stages, side, group + 1)


def _rwkv_chunk_call(rkv, lora, w0, wup_pad, a0, aup_pad, k_k, k_a, r_k, batch, seq_len, chunks):
    n_tok = rkv.shape[0]
    tcb = chunks * RWKV_CHUNK
    steps = seq_len // tcb
    pairs = RWKV_W // LANES
    nc = seq_len // RWKV_CHUNK
    col = lambda off: (lambda b, p, c: (b * steps + c, off + p))
    par = lambda b, p, c: (0, p)
    tok = lambda b, p, c: (b * steps + c, p)
    mat = lambda b, p, c: (b, p, c, 0, 0)
    return pl.pallas_call(
        functools.partial(_rwkv_chunk_kernel, chunks=chunks, group=RWKV_GROUP),
        grid=(batch, pairs, steps),
        in_specs=[pl.BlockSpec((tcb, LANES), col(0)),
                  pl.BlockSpec((tcb, LANES), col(pairs)),
                  pl.BlockSpec((tcb, LANES), col(2 * pairs)),
                  pl.BlockSpec((tcb, RWKV_LORA), lambda b, p, c: (b * steps + c, 0)),
                  pl.BlockSpec((1, LANES), par),
                  pl.BlockSpec((RWKV_LORA, LANES), par),
                  pl.BlockSpec((1, LANES), par),
                  pl.BlockSpec((RWKV_LORA, LANES), par),
                  pl.BlockSpec((1, LANES), par),
                  pl.BlockSpec((1, LANES), par),
                  pl.BlockSpec((1, LANES), par)],
        out_specs=[pl.BlockSpec((tcb, LANES), tok),
                   pl.BlockSpec((tcb, LANES), tok),
                   pl.BlockSpec((tcb, LANES), tok),
                   pl.BlockSpec((1, 1, chunks, LANES, LANES), mat),
                   pl.BlockSpec((1, 1, chunks, LANES, LANES), mat)],
        out_shape=[jax.ShapeDtypeStruct((n_tok, RWKV_W), BF16),
                   jax.ShapeDtypeStruct((n_tok, RWKV_W), F32),
                   jax.ShapeDtypeStruct((n_tok, RWKV_W), BF16),
                   jax.ShapeDtypeStruct((batch, pairs, nc, LANES, LANES), BF16),
                   jax.ShapeDtypeStruct((batch, pairs, nc, LANES, LANES), BF16)],
        compiler_params=_cparams(("parallel", "parallel", "parallel")),
        name="l0_rwkv_chunks",
    )(rkv, rkv, rkv, lora, w0, wup_pad, a0, aup_pad, k_k, k_a, r_k)


def _rwkv_scan_kernel(rp_ref, op_ref, bonus_ref, m_ref, n_ref, lnw_ref, lnb_ref, o_ref, st_ref, *, chunks):
    c = pl.program_id(0)

    @pl.when(c == 0)
    def _():
        st_ref[...] = jnp.zeros_like(st_ref)

    C = RWKV_CHUNK
    batch = rp_ref.shape[0]
    pairs = RWKV_W // LANES
    hb = _head_block_ones()
    seqs = [(b, p) for b in range(batch) for p in range(pairs)]
    states = {bp: [st_ref[bp[0], bp[1]]] for bp in seqs}
    for j in range(chunks):
        for b, p in seqs:
            states[b, p].append(_dot(m_ref[b, p, j], states[b, p][j]) + n_ref[b, p, j])
    for b, p in seqs:
        st_ref[b, p] = states[b, p][chunks]
    cols = {bp: slice(bp[1] * LANES, (bp[1] + 1) * LANES) for bp in seqs}
    os = [jnp.concatenate([_dot(rp_ref[b, j * C:(j + 1) * C, cols[b, p]], states[b, p][j])
                           for j in range(chunks)], axis=0) + op_ref[b, :, cols[b, p]] for b, p in seqs]
    means = [_dot_exact_lhs(o, hb) * (1.0 / RWKV_HEAD) for o in os]
    ds = [o - mean for o, mean in zip(os, means)]
    variances = [_dot(d * d, hb) * (1.0 / RWKV_HEAD) for d in ds]
    for (b, p), d, var in zip(seqs, ds, variances):
        c_ = cols[b, p]
        o_ref[b, :, c_] = (d * lax.rsqrt(var + RWKV_LN_EPS) * lnw_ref[:, c_] + lnb_ref[:, c_]
                           + bonus_ref[b, :, c_]).astype(o_ref.dtype)


def _rwkv_scan_call(rp, op, bonus, m, n, ln_w, ln_b, batch, seq_len, chunks):
    tcb = chunks * RWKV_CHUNK
    pairs = RWKV_W // LANES
    seq3 = lambda t: t.reshape(batch, seq_len, RWKV_W)
    tok = lambda c: (0, c, 0)
    const = lambda c: (0, 0)
    mat = lambda c: (0, 0, c, 0, 0)
    out = pl.pallas_call(
        functools.partial(_rwkv_scan_kernel, chunks=chunks),
        grid=(seq_len // tcb,),
        in_specs=[pl.BlockSpec((batch, tcb, RWKV_W), tok),
                  pl.BlockSpec((batch, tcb, RWKV_W), tok),
                  pl.BlockSpec((batch, tcb, RWKV_W), tok),
                  pl.BlockSpec((batch, pairs, chunks, LANES, LANES), mat),
                  pl.BlockSpec((batch, pairs, chunks, LANES, LANES), mat),
                  pl.BlockSpec((1, RWKV_W), const),
                  pl.BlockSpec((1, RWKV_W), const)],
        out_specs=pl.BlockSpec((batch, tcb, RWKV_W), tok),
        out_shape=jax.ShapeDtypeStruct((batch, seq_len, RWKV_W), BF16),
        scratch_shapes=[pltpu.VMEM((batch, pairs, LANES, LANES), F32)],
        compiler_params=_cparams(("arbitrary",)),
        name="l0_rwkv_scan",
    )(seq3(rp), seq3(op), seq3(bonus), m, n, ln_w, ln_b)
    return out.reshape(batch * seq_len, RWKV_W)


def _gated_out0(oa_ref, ob_ref, gate_ref, x_ref, w_ref):
    g = _silu_bf16(gate_ref[...])
    ya = oa_ref[...] * g[:, :GLA_VAL]
    yb = ob_ref[...] * g[:, GLA_VAL:]
    return (x_ref[...]
            + jnp.dot(ya, w_ref[:GLA_VAL, :], preferred_element_type=F32)
            + jnp.dot(yb, w_ref[GLA_VAL:, :], preferred_element_type=F32))


def _rope_group(x, cos, sin_lo, sin_hi):
    half = ROPE_DIMS // 2
    return x * cos + pltpu.roll(x, LANES - half, 1) * sin_lo + pltpu.roll(x, half, 1) * sin_hi


def _paired_head_order():
    return [(2 * pp + e) * SWA_GROUP + g
            for pp in range(SWA_KV_HEADS // 2) for g in range(SWA_GROUP) for e in range(2)]


def _mid_kernel(oa_ref, ob_ref, gate0_ref, x_ref, wo32_ref,
                nw_ref, w1_ref, b_ref, cos_ref, slo_ref, shi_ref,
                h_ref, q_ref, k_ref, v_ref, gate_ref, wo_ref, wq_ref, wkv_ref, wg_ref):
    @pl.when(pl.program_id(0) == 0)
    def _():
        wo_ref[...] = wo32_ref[0].astype(BF16)
        wkv_ref[...] = w1_ref[0, :, MIX1:SWA_QKV].astype(BF16)
        for new, old in enumerate(_paired_head_order()):
            dst = slice(new * SWA_HEAD, (new + 1) * SWA_HEAD)
            wq_ref[:, dst] = w1_ref[0, :, old * SWA_HEAD:(old + 1) * SWA_HEAD].astype(BF16)
            wg_ref[:, dst] = w1_ref[0, :, SWA_QKV + old * SWA_HEAD:SWA_QKV + (old + 1) * SWA_HEAD].astype(BF16)

    h = _gated_out0(oa_ref, ob_ref, gate0_ref, x_ref, wo_ref)
    h_ref[...] = h
    hn = _rmsnorm_rows(h, nw_ref[...]).astype(BF16)
    cos = cos_ref[...]
    slo = slo_ref[...]
    shi = shi_ref[...]
    scale = SWA_HEAD ** -0.5 * LOG2_E
    q = jnp.dot(hn, wq_ref[...], preferred_element_type=F32) + b_ref[:, :MIX1]
    kv = jnp.dot(hn, wkv_ref[...], preferred_element_type=F32) + b_ref[:, MIX1:]
    gate_ref[...] = jnp.dot(hn, wg_ref[...], preferred_element_type=F32).astype(gate_ref.dtype)
    for g in range(MIX1 // LANES):
        cols = slice(g * LANES, (g + 1) * LANES)
        q_ref[:, cols] = (_rope_group(q[:, cols], cos, slo, shi) * scale).astype(q_ref.dtype)
    for g in range(SWA_KV // LANES):
        cols = slice(g * LANES, (g + 1) * LANES)
        k_ref[:, cols] = _rope_group(kv[:, cols], cos, slo, shi).astype(k_ref.dtype)
    v_ref[...] = kv[:, SWA_KV:].astype(v_ref.dtype)


def _mid_call(oa, ob, gate0, x2, w_out0, norm_w, w_in1, b_in, cos, slo, shi, seq_len, tm):
    n_tok = x2.shape[0]
    tps = seq_len // tm
    row = lambda i: (i, 0)
    const = lambda i: (0, 0)
    pos = lambda i: (i % tps, 0)
    whole = lambda t: pl.BlockSpec((1,) + t.shape[1:], lambda i: (0, 0, 0), pipeline_mode=pl.Buffered(1))
    return pl.pallas_call(
        _mid_kernel,
        grid=(n_tok // tm,),
        in_specs=[pl.BlockSpec((tm, GLA_VAL), row),
                  pl.BlockSpec((tm, RWKV_W), row),
                  pl.BlockSpec((tm, MIX0), row),
                  pl.BlockSpec((tm, D_MODEL), row),
                  whole(w_out0),
                  pl.BlockSpec((1, D_MODEL), const),
                  whole(w_in1),
                  pl.BlockSpec((1, SWA_QKV), const),
                  pl.BlockSpec((tm, LANES), pos),
                  pl.BlockSpec((tm, LANES), pos),
                  pl.BlockSpec((tm, LANES), pos)],
        out_specs=[pl.BlockSpec((tm, D_MODEL), row),
                   pl.BlockSpec((tm, MIX1), row),
                   pl.BlockSpec((tm, SWA_KV), row),
                   pl.BlockSpec((tm, SWA_KV), row),
                   pl.BlockSpec((tm, MIX1), row)],
        out_shape=[jax.ShapeDtypeStruct((n_tok, D_MODEL), F32),
                   jax.ShapeDtypeStruct((n_tok, MIX1), BF16),
                   jax.ShapeDtypeStruct((n_tok, SWA_KV), BF16),
                   jax.ShapeDtypeStruct((n_tok, SWA_KV), BF16),
                   jax.ShapeDtypeStruct((n_tok, MIX1), BF16)],
        scratch_shapes=[pltpu.VMEM((MIX0, D_MODEL), BF16),
                        pltpu.VMEM((D_MODEL, MIX1), BF16),
                        pltpu.VMEM((D_MODEL, 2 * SWA_KV), BF16),
                        pltpu.VMEM((D_MODEL, MIX1), BF16)],
        compiler_params=_cparams(("arbitrary",)),
        name="l0_out_l1_proj",
    )(oa, ob, gate0, x2, w_out0, norm_w, w_in1, b_in, cos, slo, shi)


def _swa_kernel(sink_ref, q_ref, kc_ref, kp_ref, vc_ref, vp_ref, gate_ref, h_ref, w32_ref, b_ref, nw_ref,
                y_ref, o_ref, w_ref, *, q_blocks):
    n = pl.program_id(1)

    @pl.when((pl.program_id(0) == 0) & (n == 0))
    def _():
        for new, old in enumerate(_paired_head_order()):
            w_ref[new * SWA_HEAD:(new + 1) * SWA_HEAD, :] = (
                w32_ref[0, old * SWA_HEAD:(old + 1) * SWA_HEAD, :].astype(BF16))

    W = WINDOW
    from_prev = _iota((W, 2 * W), 0) > (_iota((W, 2 * W), 1) % W)
    no_prev = jnp.where(n > 0, 0.0, -jnp.inf)
    col_row = _iota((1, 2 * W), 1)
    out_row = _iota((LANES, W), 0)
    kv_groups = SWA_KV // LANES
    groups = MIX1 // LANES // kv_groups
    tasks = [(j, pp, pp * groups + g) for j in range(q_blocks) for pp in range(kv_groups) for g in range(groups)]
    kk, vt = {}, {}
    for j in range(q_blocks):
        for pp in range(kv_groups):
            cols = slice(pp * LANES, (pp + 1) * LANES)
            if j == 0:
                kk[j, pp] = jnp.concatenate([kp_ref[:, cols], kc_ref[:W, cols]], axis=0)
                vv = jnp.concatenate([vp_ref[:, cols], vc_ref[:W, cols]], axis=0)
            else:
                kk[j, pp] = kc_ref[(j - 1) * W:(j + 1) * W, cols]
                vv = vc_ref[(j - 1) * W:(j + 1) * W, cols]
            vt[j, pp] = vv.astype(F32).T.astype(BF16)

    def scores(j, pp, blk):
        q = q_ref[j * W:(j + 1) * W, blk * LANES:(blk + 1) * LANES]
        return lax.dot_general(kk[j, pp], _head_stack(q), (((1,), (1,)), ((), ())), preferred_element_type=F32)

    def projection_pieces(rows):
        width = D_MODEL // SWA_PROJ_PIECES
        gated = []

        def piece(c):
            def run():
                if not gated:
                    gated.append(o_ref[rows, :] * _silu_bf16(gate_ref[rows, :]))
                cols = slice(c * width, (c + 1) * width)
                y_ref[rows, cols] = (h_ref[rows, cols] + b_ref[:, cols]
                                     + jnp.dot(gated[0], w_ref[:, cols], preferred_element_type=F32))
            return run

        def norm():
            y_ref[rows, :] = _rmsnorm_rows(y_ref[rows, :], nw_ref[...])

        return [piece(c) for c in range(SWA_PROJ_PIECES)] + [norm]

    projections = []
    ahead = 8
    pending = [scores(*t) for t in tasks[:ahead]]
    for i, (j, pp, blk) in enumerate(tasks):
        st = pending.pop(0)
        if i + ahead < len(tasks):
            pending.append(scores(*tasks[i + ahead]))
        s_prev = st[:W] + no_prev if j == 0 else st[:W]
        s = jnp.where(from_prev, s_prev, st[W:])
        sink = jnp.where(col_row < W, sink_ref[2 * blk], sink_ref[2 * blk + 1]) * LOG2_E
        m = jnp.maximum(jnp.max(s, axis=0, keepdims=True), sink)
        p = jnp.exp2(s - m)
        denom = jnp.sum(p, axis=0, keepdims=True) + jnp.exp2(sink - m)
        pb = p.astype(BF16)
        zero = jnp.zeros_like(pb)
        p2 = jnp.concatenate([jnp.where(from_prev, pb, zero), jnp.where(from_prev, zero, pb)], axis=0)
        ot = jnp.dot(vt[j, pp], p2, preferred_element_type=F32) * (1.0 / denom)
        ot = jnp.where(out_row < HEAD, ot[:, :W], ot[:, W:])
        o_ref[j * W:(j + 1) * W, blk * LANES:(blk + 1) * LANES] = ot.T.astype(o_ref.dtype)

        last_of_block = i + 1 == len(tasks) or tasks[i + 1][0] != j
        if last_of_block and (j + 1) % SWA_PROJ_BLOCKS == 0:
            projections.extend(projection_pieces(slice((j + 1 - SWA_PROJ_BLOCKS) * W, (j + 1) * W)))
        if projections and (i % SWA_PROJ_EVERY == SWA_PROJ_EVERY - 1 or i + 1 == len(tasks)):
            projections.pop(0)()
    while projections:
        projections.pop(0)()


def _swa_call(sinks, q, k, v, gate, h1, w_out, b_out, norm_w, batch, seq_len, q_blocks):
    n_tok = q.shape[0]
    rows = q_blocks * WINDOW
    steps = seq_len // rows
    cur = lambda b, n: (b * steps + n, 0)
    prev = lambda b, n: (jnp.maximum((b * steps + n) * q_blocks - 1, 0), 0)
    const = lambda b, n: (0, 0)
    return pl.pallas_call(
        functools.partial(_swa_kernel, q_blocks=q_blocks),
        grid=(batch, steps),
        in_specs=[pl.BlockSpec(memory_space=pltpu.SMEM),
                  pl.BlockSpec((rows, MIX1), cur),
                  pl.BlockSpec((rows, SWA_KV), cur),
                  pl.BlockSpec((WINDOW, SWA_KV), prev),
                  pl.BlockSpec((rows, SWA_KV), cur),
                  pl.BlockSpec((WINDOW, SWA_KV), prev),
                  pl.BlockSpec((rows, MIX1), cur),
                  pl.BlockSpec((rows, D_MODEL), cur),
                  pl.BlockSpec((1,) + w_out.shape[1:], lambda b, n: (0, 0, 0), pipeline_mode=pl.Buffered(1)),
                  pl.BlockSpec((1, D_MODEL), const),
                  pl.BlockSpec((1, D_MODEL), const)],
        out_specs=pl.BlockSpec((rows, D_MODEL), cur),
        out_shape=jax.ShapeDtypeStruct((n_tok, D_MODEL), F32),
        scratch_shapes=[pltpu.VMEM((rows, MIX1), BF16),
                        pltpu.VMEM((MIX1, D_MODEL), BF16)],
        compiler_params=_cparams(("arbitrary", "arbitrary")),
        name="l1_swa_out",
    )(sinks, q, k, k, v, v, gate, h1, w_out, b_out, norm_w)


def _pad_rows(w, rows):
    return jnp.concatenate([w, jnp.zeros((rows - w.shape[0], w.shape[1]), w.dtype)], axis=0)


def _pair_heads(t, axis):
    shape = t.shape
    split = shape[:axis] + (SWA_KV_HEADS // 2, 2, SWA_GROUP, SWA_HEAD) + shape[axis + 1:]
    return jnp.swapaxes(t.reshape(split), axis + 1, axis + 2).reshape(shape)


def _rope_tables(seq_len):
    half = ROPE_DIMS // 2
    inv_freq = ROPE_THETA ** (-jnp.arange(half, dtype=F32) / half)
    ang = jnp.arange(seq_len).astype(F32)[:, None] * inv_freq
    trig = jnp.concatenate([jnp.cos(ang), jnp.sin(ang)], axis=1)
    d = jnp.arange(LANES) % SWA_HEAD
    src = jnp.arange(2 * half)[:, None]
    f = (d % half)[None, :]
    rot = (d < ROPE_DIMS)[None, :]
    lo = (d < half)[None, :]
    sel_cos = ((src == f) & rot).astype(F32)
    sel_lo = -((src == half + f) & lo).astype(F32)
    sel_hi = ((src == half + f) & rot & ~lo).astype(F32)
    sel = jnp.concatenate([sel_cos, sel_lo, sel_hi], axis=1)
    tab = jnp.dot(trig, sel, precision=lax.Precision.HIGHEST)
    cos = tab[:, :LANES] + (~rot).astype(F32)
    return cos, tab[:, LANES:2 * LANES], tab[:, 2 * LANES:]


def _forward(x, norm_w, w_in0, gla_gk_up, gla_gk_bias, gla_norm_w, rwkv_mu, rwkv_w0, rwkv_w_up,
             rwkv_a0, rwkv_a_up, rwkv_k_k, rwkv_k_a, rwkv_r_k, rwkv_ln_w, rwkv_ln_b, w_out0,
             w_in1, b_in1, attn_sinks, w_out1, b_out1, final_norm_w, *, tm, gla_chunks, rwkv_chunks, scan_chunks,
             swa_blocks):
    batch, seq_len, _ = x.shape
    x2 = x.reshape(batch * seq_len, D_MODEL)
    row = lambda t: t.reshape(1, -1)

    gqkv, glow, rkv, lora, gate0 = _in0_call(x2, row(norm_w[0]), jnp.swapaxes(w_in0, 1, 2), row(rwkv_mu[0]),
                                             seq_len, tm)

    up_pad = _pad_rows(gla_gk_up[0], GLOW_PAD).astype(BF16)
    o_a = _gla_call(gqkv, glow, up_pad, row(gla_gk_bias[0]), row(gla_norm_w[0]), batch, seq_len, gla_chunks)

    zeros_r = jnp.zeros((RWKV_DECAY_RANK, RWKV_W), F32)
    wup_pad = jnp.concatenate([rwkv_w_up[0], zeros_r], axis=0).astype(BF16)
    aup_pad = jnp.concatenate([zeros_r, rwkv_a_up[0]], axis=0).astype(BF16)
    rp, op, bonus, m, n = _rwkv_chunk_call(
        rkv, lora, row(rwkv_w0[0]), wup_pad, row(rwkv_a0[0]), aup_pad,
        row(rwkv_k_k[0]), row(rwkv_k_a[0]), row(rwkv_r_k[0]), batch, seq_len, rwkv_chunks)
    o_b = _rwkv_scan_call(rp, op, bonus, m, n, row(rwkv_ln_w[0]), row(rwkv_ln_b[0]),
                          batch, seq_len, scan_chunks)


    b1 = b_in1[0]
    b1p = row(jnp.concatenate([_pair_heads(b1[:MIX1], 0), b1[MIX1:]]))
    sinks_p = jnp.swapaxes(attn_sinks[0].reshape(SWA_KV_HEADS // 2, 2, SWA_GROUP), 1, 2).reshape(SWA_Q_HEADS)
    cos, slo, shi = _rope_tables(seq_len)
    h1, q, k, v, gate1 = _mid_call(o_a, o_b, gate0, x2, w_out0, row(norm_w[1]), w_in1,
                                   b1p, cos, slo, shi, seq_len, tm)
    y = _swa_call(sinks_p, q, k, v, gate1, h1, w_out1, row(b_out1[0]), row(final_norm_w),
                  batch, seq_len, swa_blocks)
    return y.reshape(batch, seq_len, D_MODEL)


def kernel(x, norm_w, w_in0, gla_gk_up, gla_gk_bias, gla_norm_w, rwkv_mu, rwkv_w0, rwkv_w_up, rwkv_a0,
           rwkv_a_up, rwkv_k_k, rwkv_k_a, rwkv_r_k, rwkv_ln_w, rwkv_ln_b, w_out0, w_in1, b_in1,
           attn_sinks, w_out1, b_out1, final_norm_w):
    return _forward(x, norm_w, w_in0, gla_gk_up, gla_gk_bias, gla_norm_w, rwkv_mu, rwkv_w0, rwkv_w_up,
                    rwkv_a0, rwkv_a_up, rwkv_k_k, rwkv_k_a, rwkv_r_k, rwkv_ln_w, rwkv_ln_b, w_out0,
                    w_in1, b_in1, attn_sinks, w_out1, b_out1, final_norm_w,
                    tm=512, gla_chunks=64, rwkv_chunks=64, scan_chunks=8, swa_blocks=8)
```

```python
import functools
import math

import jax
import jax.numpy as jnp
from jax import lax
from jax.experimental import pallas as pl
from jax.experimental.pallas import tpu as pltpu

F32 = jnp.float32
BF16 = jnp.bfloat16

D_MODEL = 1024
NORM_EPS = 1e-5

GLA_HEADS = 4
GLA_DK = 64
GLA_DV = 128
GLA_KEY = GLA_HEADS * GLA_DK
GLA_VAL = GLA_HEADS * GLA_DV
GLA_GATE_RANK = 16
GLA_GATE_NORMALIZER = 16.0
GLA_CHUNK = 64

RWKV_HEADS = 8
RWKV_HEAD = 64
RWKV_W = RWKV_HEADS * RWKV_HEAD
RWKV_DECAY_RANK = 64
RWKV_A_RANK = 64
RWKV_LN_EPS = 64e-5
RWKV_RKV = 3 * RWKV_W
RWKV_LORA = RWKV_DECAY_RANK + RWKV_A_RANK
RWKV_SHIFT = RWKV_RKV + RWKV_LORA
RWKV_CHUNK = 64
RWKV_GROUP = 16

MIX0 = GLA_VAL + RWKV_W
GLA_QKV = 2 * GLA_KEY + GLA_VAL

SWA_Q_HEADS = 16
SWA_KV_HEADS = 4
SWA_GROUP = SWA_Q_HEADS // SWA_KV_HEADS
SWA_HEAD = 64
WINDOW = 128
ROPE_DIMS = SWA_HEAD // 4
ROPE_THETA = 500000.0
MIX1 = SWA_Q_HEADS * SWA_HEAD
SWA_KV = SWA_KV_HEADS * SWA_HEAD
SWA_QKV = MIX1 + 2 * SWA_KV
SWA_PROJ_BLOCKS = 2
SWA_PROJ_PIECES = 4
SWA_PROJ_EVERY = 3

LOG2_E = 1.4426950408889634
LANES = 128
HEAD = 64
GLOW_PAD = LANES
VMEM_LIMIT = 56 * 1024 * 1024


def _cparams(sem):
    return pltpu.CompilerParams(dimension_semantics=sem, vmem_limit_bytes=VMEM_LIMIT)


def _dot(a, b):
    return jnp.dot(a.astype(BF16), b.astype(BF16), preferred_element_type=F32)


def _dot_nt(a, b):
    return lax.dot_general(a.astype(BF16), b.astype(BF16), (((1,), (1,)), ((), ())),
                           preferred_element_type=F32)


def _dot_tn(a, b):
    return lax.dot_general(a.astype(BF16), b.astype(BF16), (((0,), (0,)), ((), ())),
                           preferred_element_type=F32)


def _split2(x):
    hi = x.astype(BF16)
    lo = (x - hi.astype(F32)).astype(BF16)
    return hi, lo


def _dot_exact_rhs(a_bf16, x):
    hi, lo = _split2(x)
    both = jnp.dot(a_bf16, jnp.concatenate([hi, lo], axis=1), preferred_element_type=F32)
    return both[:, :x.shape[1]] + both[:, x.shape[1]:]


def _dot_exact_lhs(x, b_bf16):
    hi, lo = _split2(x)
    return (jnp.dot(hi, b_bf16, preferred_element_type=F32)
            + jnp.dot(lo, b_bf16, preferred_element_type=F32))


def _iota(shape, dim):
    return lax.broadcasted_iota(jnp.int32, shape, dim)


def _tril_ones(n, dtype=BF16):
    return (_iota((n, n), 0) >= _iota((n, n), 1)).astype(dtype)


def _head_block_ones(n=LANES, dtype=BF16):
    return ((_iota((n, n), 0) // HEAD) == (_iota((n, n), 1) // HEAD)).astype(dtype)


def _head_stack(x):
    head = (_iota(x.shape, 1) % LANES) // HEAD
    return jnp.concatenate([jnp.where(head == 0, x, 0.0), jnp.where(head == 1, x, 0.0)], axis=0)


def _softplus(z):
    return jnp.maximum(z, 0.0) + jnp.log(1.0 + jnp.exp(-jnp.abs(z)))


def _sigmoid(z):
    return 0.5 + 0.5 * jnp.tanh(0.5 * z)


def _silu_bf16(g):
    h = g * 0.5
    return h + h * jnp.tanh(h)


def _rmsnorm_rows(x, w):
    return x * lax.rsqrt(jnp.mean(x * x, axis=-1, keepdims=True) + NORM_EPS) * w


def _in0_kernel(x_ref, nw_ref, w_ref, mu_ref,
                gqkv_ref, glow_ref, rkv_ref, lora_ref, gate_ref, carry_ref, wg_ref, wr_ref,
                *, tiles_per_seq):
    i = pl.program_id(0)

    @pl.when(i == 0)
    def _():
        carry_ref[...] = jnp.zeros_like(carry_ref)
        def put(dst, col0, row0, width):
            step = 4 * LANES if width % (4 * LANES) == 0 else LANES
            for c in range(0, width, step):
                dst[:, col0 + c:col0 + c + step] = w_ref[0, row0 + c:row0 + c + step, :].T.astype(BF16)

        put(wg_ref, 0, 0, GLA_QKV)
        put(wr_ref, 0, GLA_QKV, GLOW_PAD)
        put(wr_ref, GLOW_PAD, GLA_QKV + GLA_GATE_RANK, RWKV_SHIFT + MIX0)

    xn = _rmsnorm_rows(x_ref[...], nw_ref[...]).astype(BF16)
    gqkv_ref[...] = jnp.dot(xn, wg_ref[...], preferred_element_type=F32).astype(gqkv_ref.dtype)
    low_rw = jnp.dot(xn, wr_ref[:, :GLOW_PAD + RWKV_SHIFT], preferred_element_type=F32)
    glow_ref[...] = low_rw[:, :GLOW_PAD]
    rw = low_rw[:, GLOW_PAD:]
    gate_ref[...] = jnp.dot(xn, wr_ref[:, GLOW_PAD + RWKV_SHIFT:], preferred_element_type=F32).astype(gate_ref.dtype)

    tm = rw.shape[0]
    first = (i % tiles_per_seq) == 0
    prev_last = jnp.where(first, 0.0, carry_ref[7:8, :])
    rolled = pltpu.roll(rw, 1, 0)
    prev = jnp.where(_iota(rw.shape, 0) == 0, prev_last, rolled)
    mixed = rw + (prev - rw) * mu_ref[...]
    rkv_ref[...] = mixed[:, :RWKV_RKV].astype(rkv_ref.dtype)
    lora_ref[...] = mixed[:, RWKV_RKV:]
    carry_ref[...] = rw[tm - 8:tm, :]


def _in0_call(x2, norm_w, w_in, mu, seq_len, tm):
    n_tok = x2.shape[0]
    row = lambda i: (i, 0)
    const = lambda i: (0, 0)
    outs = [(GLA_QKV, BF16), (GLOW_PAD, F32), (RWKV_RKV, BF16), (RWKV_LORA, F32), (MIX0, BF16)]
    return pl.pallas_call(
        functools.partial(_in0_kernel, tiles_per_seq=seq_len // tm),
        grid=(n_tok // tm,),
        in_specs=[pl.BlockSpec((tm, D_MODEL), row),
                  pl.BlockSpec((1, D_MODEL), const),
                  pl.BlockSpec((1,) + w_in.shape[1:], lambda i: (0, 0, 0), pipeline_mode=pl.Buffered(1)),
                  pl.BlockSpec((1, RWKV_SHIFT), const)],
        out_specs=[pl.BlockSpec((tm, n), row) for n, _ in outs],
        out_shape=[jax.ShapeDtypeStruct((n_tok, n), dt) for n, dt in outs],
        scratch_shapes=[pltpu.VMEM((8, RWKV_SHIFT), F32),
                        pltpu.VMEM((D_MODEL, GLA_QKV), BF16),
                        pltpu.VMEM((D_MODEL, GLOW_PAD + RWKV_SHIFT + MIX0), BF16)],
        compiler_params=_cparams(("arbitrary",)),
        name="l0_norm_proj",
    )(x2, norm_w, w_in, mu)


def _gla_kernel(q_ref, k_ref, glow_ref, v_ref, up_ref, bias_ref, nw_ref, o_ref, st_ref, *, chunks):
    c = pl.program_id(2)

    @pl.when(c == 0)
    def _():
        st_ref[...] = jnp.zeros_like(st_ref)

    C = GLA_CHUNK
    tril = _tril_ones(C)
    causal = _iota((C, LANES), 0) >= (_iota((C, LANES), 1) % HEAD)
    sr = _iota((2 * GLA_DV, LANES), 0)
    sl = _iota((2 * GLA_DV, LANES), 1)
    st_mask = (sr // GLA_DV) == (sl // HEAD)
    vl = _iota((C, 2 * GLA_DV), 1)
    scale = GLA_DK ** -0.5
    z = _dot(glow_ref[...], up_ref[...]) + bias_ref[...]
    g_all = -_softplus(-z) * (LOG2_E / GLA_GATE_NORMALIZER)
    q_all = q_ref[...].astype(F32) * scale
    k_all = k_ref[...].astype(F32)
    rows = [slice(j * C, (j + 1) * C) for j in range(chunks)]
    bs = [_dot_exact_rhs(tril, g_all[rw]) for rw in rows]
    qe, ke, qb, kl, dec, vs = [], [], [], [], [], []
    for rw, b in zip(rows, bs):
        ref = b[C // 2:C // 2 + 1, :]
        b_last = b[C - 1:C, :]
        qe.append(q_all[rw] * jnp.exp2(b - ref))
        ke.append(k_all[rw] * jnp.exp2(ref - b))
        qb.append(qe[-1] * jnp.exp2(ref))
        kl.append(ke[-1] * jnp.exp2(b_last - ref))
        dec.append(jnp.exp2(b_last))
        vs.append(v_ref[rw, :])
    att = [jnp.where(causal, _dot_nt(qe[j], _head_stack(ke[j])), 0.0) for j in range(chunks)]
    kv = [jnp.where(st_mask, _dot_tn(vs[j], kl[j]), 0.0) for j in range(chunks)]
    v_diag = [jnp.concatenate([jnp.where(vl < GLA_DV, vs[j], jnp.zeros_like(vs[j])),
                               jnp.where(vl >= GLA_DV, vs[j], jnp.zeros_like(vs[j]))], axis=0)
              for j in range(chunks)]
    intra = [jnp.dot(att[j].astype(BF16), v_diag[j], preferred_element_type=F32) for j in range(chunks)]
    states = [st_ref[...]]
    for j in range(chunks):
        states.append(states[j] * dec[j] + kv[j])
    st_ref[...] = states[chunks]
    for j in range(chunks):
        o = intra[j] + _dot_nt(qb[j], states[j])
        for h in range(2):
            oh = o[:, h * GLA_DV:(h + 1) * GLA_DV]
            oh = oh * lax.rsqrt(jnp.mean(oh * oh, axis=-1, keepdims=True) + NORM_EPS) * nw_ref[...]
            o_ref[rows[j], h * GLA_DV:(h + 1) * GLA_DV] = oh.astype(o_ref.dtype)


def _gla_call(gqkv, glow, up_pad, bias, norm_w, batch, seq_len, chunks):
    n_tok = gqkv.shape[0]
    tcb = chunks * GLA_CHUNK
    steps = seq_len // tcb
    pairs = GLA_KEY // LANES
    return pl.pallas_call(
        functools.partial(_gla_kernel, chunks=chunks),
        grid=(batch, pairs, steps),
        in_specs=[pl.BlockSpec((tcb, LANES), lambda b, p, c: (b * steps + c, p)),
                  pl.BlockSpec((tcb, LANES), lambda b, p, c: (b * steps + c, pairs + p)),
                  pl.BlockSpec((tcb, GLOW_PAD), lambda b, p, c: (b * steps + c, 0)),
                  pl.BlockSpec((tcb, 2 * GLA_DV), lambda b, p, c: (b * steps + c, pairs + p)),
                  pl.BlockSpec((GLOW_PAD, LANES), lambda b, p, c: (0, p)),
                  pl.BlockSpec((1, LANES), lambda b, p, c: (0, p)),
                  pl.BlockSpec((1, GLA_DV), lambda b, p, c: (0, 0))],
        out_specs=pl.BlockSpec((tcb, 2 * GLA_DV), lambda b, p, c: (b * steps + c, p)),
        out_shape=jax.ShapeDtypeStruct((n_tok, GLA_VAL), BF16),
        scratch_shapes=[pltpu.VMEM((2 * GLA_DV, LANES), F32)],
        compiler_params=_cparams(("parallel", "parallel", "arbitrary")),
        name="l0_gla",
    )(gqkv, gqkv, glow, gqkv, up_pad, bias, norm_w)


def _merge_masks(n):
    r = _iota((n, LANES), 0)
    c = _iota((n, LANES), 1) % HEAD
    masks = []
    s = 1
    while s < n:
        masks.append(((r // s) % 2 == 1) & ((c // s) == (r // s) - 1))
        s *= 2
    return (r == c).astype(F32), masks


def _run_interleaved(main, main_steps, side, side_steps):
    done = 0
    spread = max(1, (3 * main_steps) // 4)
    for i, _ in enumerate(main):
        target = -(-(i + 1) * side_steps // spread)
        while done < min(target, side_steps):
            next(side, None)
            done += 1
    for _ in side:
        pass


def _rwkv_chunk_kernel(r_ref, k_ref, v_ref, xwa_ref, w0_ref, wup_ref, a0_ref, aup_ref,
                       kk_ref, ka_ref, rk_ref,
                       rp_ref, op_ref, bonus_ref, m_ref, n_ref, *, chunks, group):
    C = RWKV_CHUNK
    tril = _tril_ones(C)
    rr = _iota((2 * C, LANES), 0)
    cc = _iota((2 * C, LANES), 1) % HEAD
    tri2 = ((rr < C) & (rr > cc)) | (rr - C >= cc)
    hb = _head_block_ones()
    sq_r = _iota((LANES, LANES), 0)
    sq_c = _iota((LANES, LANES), 1)
    same_head = (sq_r // HEAD) == (sq_c // HEAD)
    eye128 = sq_r == sq_c

    eye, merge = _merge_masks(C)
    zero = jnp.zeros((C, LANES), F32)
    n = range(group)

    def prepare(g, out):
        rows = slice(g * group * C, (g + 1) * group * C)
        r_all = r_ref[rows, :].astype(F32)
        k_all = k_ref[rows, :].astype(F32)
        v_all = v_ref[rows, :].astype(F32)
        xwa = xwa_ref[rows, :]
        y = w0_ref[...] + _dot(jnp.tanh(xwa), wup_ref[...])
        lw_all = _sigmoid(y) * (-LOG2_E * math.exp(-0.5))
        a_sig = _sigmoid(a0_ref[...] + _dot(xwa, aup_ref[...]))
        kk = k_all * kk_ref[...]
        kk = kk * lax.rsqrt(jnp.maximum(_dot(kk * kk, hb), 1e-24))
        k_all = k_all * (1.0 + (a_sig - 1.0) * ka_ref[...])
        bonus_ref[rows, :] = (_dot(r_all * k_all * rk_ref[...], hb) * v_all).astype(bonus_ref.dtype)
        a_all = -kk
        b_all = kk * a_sig
        yield
        for j in n:
            rw = slice(j * C, (j + 1) * C)
            cum = _dot_exact_rhs(tril, lw_all[rw])
            cum_last = cum[C - 1:C, :]
            e_neg = jnp.exp2(-cum)
            e_end = jnp.exp2(cum_last - cum)
            out.append(dict(
                rt=r_all[rw] * jnp.exp2(cum),
                at=a_all[rw] * jnp.exp2(cum - lw_all[rw]),
                bt=b_all[rw] * e_neg,
                kt=k_all[rw] * e_neg,
                ends=jnp.concatenate([b_all[rw] * e_end, k_all[rw] * e_end], axis=0),
                v=v_all[rw],
                dec=jnp.exp2(cum_last)))
            yield

    def solve(g, ops):
        lhs = [jnp.concatenate([o["at"], o["rt"]], axis=0) for o in ops]
        both = [_dot_nt(lhs[j], jnp.concatenate([_head_stack(ops[j]["bt"]), _head_stack(ops[j]["kt"])], axis=0))
                for j in n]
        yield
        left = [jnp.where(tri2, p[:, :LANES], 0.0) for p in both]
        right = [jnp.where(tri2, p[:, LANES:], 0.0) for p in both]
        yield
        lows = [lf[:C] for lf in left]
        ts = [eye + jnp.where(merge[0], low, 0.0) for low in lows]
        for sub in merge[1:]:
            ys = [_dot(jnp.where(sub, low, 0.0), _head_stack(t)) for low, t in zip(lows, ts)]
            yield
            ts = [t + _dot(t, _head_stack(y)) for t, y in zip(ts, ys)]
            yield
        kv = [_dot(right[j], _head_stack(ops[j]["v"])) for j in n]
        yield
        wz = [_dot(ts[j], _head_stack(jnp.concatenate([ops[j]["at"], kv[j][:C]], axis=1))) for j in n]
        yield
        ro = [_dot(left[j][C:], _head_stack(wz[j])) for j in n]
        yield
        mn = [_dot_tn(ops[j]["ends"],
                      jnp.concatenate([wz[j], jnp.concatenate([zero, ops[j]["v"]], axis=1)], axis=0))
              for j in n]
        for j in n:
            c = g * group + j
            rows = slice(c * C, (c + 1) * C)
            rp_ref[rows, :] = (ops[j]["rt"] + ro[j][:, :LANES]).astype(rp_ref.dtype)
            op_ref[rows, :] = ro[j][:, LANES:] + kv[j][C:]
            m_ref[0, 0, c] = (jnp.where(eye128, ops[j]["dec"], 0.0)
                              + jnp.where(same_head, mn[j][:, :LANES], 0.0)).astype(m_ref.dtype)
            n_ref[0, 0, c] = jnp.where(same_head, mn[j][:, LANES:], 0.0).astype(n_ref.dtype)
        yield

    solve_stages = 2 + 2 * (len(merge) - 1) + 4
    groups = chunks // group
    ops = [[] for _ in range(groups + 1)]
    for _ in prepare(0, ops[0]):
        pass
    for g in range(groups):
        side = prepare(g + 1, ops[g + 1]) if g + 1 < groups else iter(())
        _run_interleaved(solve(g, ops[g]), solve_stages, side, group + 1)


def _rwkv_chunk_call(rkv, lora, w0, wup_pad, a0, aup_pad, k_k, k_a, r_k, batch, seq_len, chunks):
    n_tok = rkv.shape[0]
    tcb = chunks * RWKV_CHUNK
    steps = seq_len // tcb
    pairs = RWKV_W // LANES
    nc = seq_len // RWKV_CHUNK
    col = lambda off: (lambda b, p, c: (b * steps + c, off + p))
    par = lambda b, p, c: (0, p)
    tok = lambda b, p, c: (b * steps + c, p)
    mat = lambda b, p, c: (b, p, c, 0, 0)
    return pl.pallas_call(
        functools.partial(_rwkv_chunk_kernel, chunks=chunks, group=RWKV_GROUP),
        grid=(batch, pairs, steps),
        in_specs=[pl.BlockSpec((tcb, LANES), col(0)),
                  pl.BlockSpec((tcb, LANES), col(pairs)),
                  pl.BlockSpec((tcb, LANES), col(2 * pairs)),
                  pl.BlockSpec((tcb, RWKV_LORA), lambda b, p, c: (b * steps + c, 0)),
                  pl.BlockSpec((1, LANES), par),
                  pl.BlockSpec((RWKV_LORA, LANES), par),
                  pl.BlockSpec((1, LANES), par),
                  pl.BlockSpec((RWKV_LORA, LANES), par),
                  pl.BlockSpec((1, LANES), par),
                  pl.BlockSpec((1, LANES), par),
                  pl.BlockSpec((1, LANES), par)],
        out_specs=[pl.BlockSpec((tcb, LANES), tok),
                   pl.BlockSpec((tcb, LANES), tok),
                   pl.BlockSpec((tcb, LANES), tok),
                   pl.BlockSpec((1, 1, chunks, LANES, LANES), mat),
                   pl.BlockSpec((1, 1, chunks, LANES, LANES), mat)],
        out_shape=[jax.ShapeDtypeStruct((n_tok, RWKV_W), BF16),
                   jax.ShapeDtypeStruct((n_tok, RWKV_W), F32),
                   jax.ShapeDtypeStruct((n_tok, RWKV_W), BF16),
                   jax.ShapeDtypeStruct((batch, pairs, nc, LANES, LANES), BF16),
                   jax.ShapeDtypeStruct((batch, pairs, nc, LANES, LANES), BF16)],
        compiler_params=_cparams(("parallel", "parallel", "parallel")),
        name="l0_rwkv_chunks",
    )(rkv, rkv, rkv, lora, w0, wup_pad, a0, aup_pad, k_k, k_a, r_k)


def _rwkv_scan_kernel(rp_ref, op_ref, bonus_ref, m_ref, n_ref, lnw_ref, lnb_ref, o_ref, st_ref, *, chunks):
    c = pl.program_id(0)

    @pl.when(c == 0)
    def _():
        st_ref[...] = jnp.zeros_like(st_ref)

    C = RWKV_CHUNK
    batch = rp_ref.shape[0]
    pairs = RWKV_W // LANES
    hb = _head_block_ones()
    seqs = [(b, p) for b in range(batch) for p in range(pairs)]
    states = {bp: [st_ref[bp[0], bp[1]]] for bp in seqs}
    for j in range(chunks):
        for b, p in seqs:
            states[b, p].append(_dot(m_ref[b, p, j], states[b, p][j]) + n_ref[b, p, j])
    for b, p in seqs:
        st_ref[b, p] = states[b, p][chunks]
    cols = {bp: slice(bp[1] * LANES, (bp[1] + 1) * LANES) for bp in seqs}
    os = [jnp.concatenate([_dot(rp_ref[b, j * C:(j + 1) * C, cols[b, p]], states[b, p][j])
                           for j in range(chunks)], axis=0) + op_ref[b, :, cols[b, p]] for b, p in seqs]
    means = [_dot_exact_lhs(o, hb) * (1.0 / RWKV_HEAD) for o in os]
    ds = [o - mean for o, mean in zip(os, means)]
    variances = [_dot(d * d, hb) * (1.0 / RWKV_HEAD) for d in ds]
    for (b, p), d, var in zip(seqs, ds, variances):
        c_ = cols[b, p]
        o_ref[b, :, c_] = (d * lax.rsqrt(var + RWKV_LN_EPS) * lnw_ref[:, c_] + lnb_ref[:, c_]
                           + bonus_ref[b, :, c_]).astype(o_ref.dtype)


def _rwkv_scan_call(rp, op, bonus, m, n, ln_w, ln_b, batch, seq_len, chunks):
    tcb = chunks * RWKV_CHUNK
    pairs = RWKV_W // LANES
    seq3 = lambda t: t.reshape(batch, seq_len, RWKV_W)
    tok = lambda c: (0, c, 0)
    const = lambda c: (0, 0)
    mat = lambda c: (0, 0, c, 0, 0)
    out = pl.pallas_call(
        functools.partial(_rwkv_scan_kernel, chunks=chunks),
        grid=(seq_len // tcb,),
        in_specs=[pl.BlockSpec((batch, tcb, RWKV_W), tok),
                  pl.BlockSpec((batch, tcb, RWKV_W), tok),
                  pl.BlockSpec((batch, tcb, RWKV_W), tok),
                  pl.BlockSpec((batch, pairs, chunks, LANES, LANES), mat),
                  pl.BlockSpec((batch, pairs, chunks, LANES, LANES), mat),
                  pl.BlockSpec((1, RWKV_W), const),
                  pl.BlockSpec((1, RWKV_W), const)],
        out_specs=pl.BlockSpec((batch, tcb, RWKV_W), tok),
        out_shape=jax.ShapeDtypeStruct((batch, seq_len, RWKV_W), BF16),
        scratch_shapes=[pltpu.VMEM((batch, pairs, LANES, LANES), F32)],
        compiler_params=_cparams(("arbitrary",)),
        name="l0_rwkv_scan",
    )(seq3(rp), seq3(op), seq3(bonus), m, n, ln_w, ln_b)
    return out.reshape(batch * seq_len, RWKV_W)


def _gated_out0(oa_ref, ob_ref, gate_ref, x_ref, w_ref):
    g = _silu_bf16(gate_ref[...])
    y = jnp.concatenate([oa_ref[...], ob_ref[...]], axis=1) * g
    return x_ref[...] + jnp.dot(y, w_ref[...], preferred_element_type=F32)


def _rope_group(x, cos, sin_lo, sin_hi):
    half = ROPE_DIMS // 2
    return x * cos + pltpu.roll(x, LANES - half, 1) * sin_lo + pltpu.roll(x, half, 1) * sin_hi


def _paired_head_order():
    return [(2 * pp + e) * SWA_GROUP + g
            for pp in range(SWA_KV_HEADS // 2) for g in range(SWA_GROUP) for e in range(2)]


def _mid_kernel(oa_ref, ob_ref, gate0_ref, x_ref, wo32_ref,
                nw_ref, w1_ref, b_ref, cos_ref, slo_ref, shi_ref,
                h_ref, q_ref, k_ref, v_ref, gate_ref, wo_ref, wq_ref, wkv_ref, wg_ref):
    @pl.when(pl.program_id(0) == 0)
    def _():
        wo_ref[...] = wo32_ref[0].astype(BF16)
        wkv_ref[...] = w1_ref[0, :, MIX1:SWA_QKV].astype(BF16)
        for new, old in enumerate(_paired_head_order()):
            dst = slice(new * SWA_HEAD, (new + 1) * SWA_HEAD)
            wq_ref[:, dst] = w1_ref[0, :, old * SWA_HEAD:(old + 1) * SWA_HEAD].astype(BF16)
            wg_ref[:, dst] = w1_ref[0, :, SWA_QKV + old * SWA_HEAD:SWA_QKV + (old + 1) * SWA_HEAD].astype(BF16)

    h = _gated_out0(oa_ref, ob_ref, gate0_ref, x_ref, wo_ref)
    h_ref[...] = h
    hn = _rmsnorm_rows(h, nw_ref[...]).astype(BF16)
    cos = cos_ref[...]
    slo = slo_ref[...]
    shi = shi_ref[...]
    scale = SWA_HEAD ** -0.5 * LOG2_E
    q = jnp.dot(hn, wq_ref[...], preferred_element_type=F32) + b_ref[:, :MIX1]
    kv = jnp.dot(hn, wkv_ref[...], preferred_element_type=F32) + b_ref[:, MIX1:]
    gate_ref[...] = jnp.dot(hn, wg_ref[...], preferred_element_type=F32).astype(gate_ref.dtype)
    for g in range(MIX1 // LANES):
        cols = slice(g * LANES, (g + 1) * LANES)
        q_ref[:, cols] = (_rope_group(q[:, cols], cos, slo, shi) * scale).astype(q_ref.dtype)
    for g in range(SWA_KV // LANES):
        cols = slice(g * LANES, (g + 1) * LANES)
        k_ref[:, cols] = _rope_group(kv[:, cols], cos, slo, shi).astype(k_ref.dtype)
    v_ref[...] = kv[:, SWA_KV:].astype(v_ref.dtype)


def _mid_call(oa, ob, gate0, x2, w_out0, norm_w, w_in1, b_in, cos, slo, shi, seq_len, tm):
    n_tok = x2.shape[0]
    tps = seq_len // tm
    row = lambda i: (i, 0)
    const = lambda i: (0, 0)
    pos = lambda i: (i % tps, 0)
    whole = lambda t: pl.BlockSpec((1,) + t.shape[1:], lambda i: (0, 0, 0), pipeline_mode=pl.Buffered(1))
    return pl.pallas_call(
        _mid_kernel,
        grid=(n_tok // tm,),
        in_specs=[pl.BlockSpec((tm, GLA_VAL), row),
                  pl.BlockSpec((tm, RWKV_W), row),
                  pl.BlockSpec((tm, MIX0), row),
                  pl.BlockSpec((tm, D_MODEL), row),
                  whole(w_out0),
                  pl.BlockSpec((1, D_MODEL), const),
                  whole(w_in1),
                  pl.BlockSpec((1, SWA_QKV), const),
                  pl.BlockSpec((tm, LANES), pos),
                  pl.BlockSpec((tm, LANES), pos),
                  pl.BlockSpec((tm, LANES), pos)],
        out_specs=[pl.BlockSpec((tm, D_MODEL), row),
                   pl.BlockSpec((tm, MIX1), row),
                   pl.BlockSpec((tm, SWA_KV), row),
                   pl.BlockSpec((tm, SWA_KV), row),
                   pl.BlockSpec((tm, MIX1), row)],
        out_shape=[jax.ShapeDtypeStruct((n_tok, D_MODEL), F32),
                   jax.ShapeDtypeStruct((n_tok, MIX1), BF16),
                   jax.ShapeDtypeStruct((n_tok, SWA_KV), BF16),
                   jax.ShapeDtypeStruct((n_tok, SWA_KV), BF16),
                   jax.ShapeDtypeStruct((n_tok, MIX1), BF16)],
        scratch_shapes=[pltpu.VMEM((MIX0, D_MODEL), BF16),
                        pltpu.VMEM((D_MODEL, MIX1), BF16),
                        pltpu.VMEM((D_MODEL, 2 * SWA_KV), BF16),
                        pltpu.VMEM((D_MODEL, MIX1), BF16)],
        compiler_params=_cparams(("arbitrary",)),
        name="l0_out_l1_proj",
    )(oa, ob, gate0, x2, w_out0, norm_w, w_in1, b_in, cos, slo, shi)


def _swa_kernel(sink_ref, q_ref, kc_ref, kp_ref, vc_ref, vp_ref, gate_ref, h_ref, w32_ref, b_ref, nw_ref,
                y_ref, o_ref, w_ref, *, q_blocks):
    n = pl.program_id(1)

    @pl.when((pl.program_id(0) == 0) & (n == 0))
    def _():
        for new, old in enumerate(_paired_head_order()):
            w_ref[new * SWA_HEAD:(new + 1) * SWA_HEAD, :] = (
                w32_ref[0, old * SWA_HEAD:(old + 1) * SWA_HEAD, :].astype(BF16))

    W = WINDOW
    from_prev = _iota((W, 2 * W), 0) > (_iota((W, 2 * W), 1) % W)
    no_prev = jnp.where(n > 0, 0.0, -jnp.inf)
    col_row = _iota((1, 2 * W), 1)
    out_row = _iota((LANES, W), 0)
    kv_groups = SWA_KV // LANES
    groups = MIX1 // LANES // kv_groups
    tasks = [(j, pp, pp * groups + g) for j in range(q_blocks) for pp in range(kv_groups) for g in range(groups)]
    kk, vt = {}, {}
    for j in range(q_blocks):
        for pp in range(kv_groups):
            cols = slice(pp * LANES, (pp + 1) * LANES)
            if j == 0:
                kk[j, pp] = jnp.concatenate([kp_ref[:, cols], kc_ref[:W, cols]], axis=0)
                vv = jnp.concatenate([vp_ref[:, cols], vc_ref[:W, cols]], axis=0)
            else:
                kk[j, pp] = kc_ref[(j - 1) * W:(j + 1) * W, cols]
                vv = vc_ref[(j - 1) * W:(j + 1) * W, cols]
            vt[j, pp] = vv.astype(F32).T.astype(BF16)

    def scores(j, pp, blk):
        q = q_ref[j * W:(j + 1) * W, blk * LANES:(blk + 1) * LANES]
        return lax.dot_general(kk[j, pp], _head_stack(q), (((1,), (1,)), ((), ())), preferred_element_type=F32)

    def projection_pieces(rows):
        width = D_MODEL // SWA_PROJ_PIECES
        gated = []

        def piece(c):
            def run():
                if not gated:
                    gated.append(o_ref[rows, :] * _silu_bf16(gate_ref[rows, :]))
                cols = slice(c * width, (c + 1) * width)
                y_ref[rows, cols] = (h_ref[rows, cols] + b_ref[:, cols]
                                     + jnp.dot(gated[0], w_ref[:, cols], preferred_element_type=F32))
            return run

        def norm():
            y_ref[rows, :] = _rmsnorm_rows(y_ref[rows, :], nw_ref[...])

        return [piece(c) for c in range(SWA_PROJ_PIECES)] + [norm]

    projections = []
    ahead = 8
    pending = [scores(*t) for t in tasks[:ahead]]
    for i, (j, pp, blk) in enumerate(tasks):
        st = pending.pop(0)
        if i + ahead < len(tasks):
            pending.append(scores(*tasks[i + ahead]))
        s_prev = st[:W] + no_prev if j == 0 else st[:W]
        s = jnp.where(from_prev, s_prev, st[W:])
        sink = jnp.where(col_row < W, sink_ref[2 * blk], sink_ref[2 * blk + 1]) * LOG2_E
        m = jnp.maximum(jnp.max(s, axis=0, keepdims=True), sink)
        p = jnp.exp2(s - m)
        denom = jnp.sum(p, axis=0, keepdims=True) + jnp.exp2(sink - m)
        pb = p.astype(BF16)
        zero = jnp.zeros_like(pb)
        p2 = jnp.concatenate([jnp.where(from_prev, pb, zero), jnp.where(from_prev, zero, pb)], axis=0)
        ot = jnp.dot(vt[j, pp], p2, preferred_element_type=F32) * (1.0 / denom)
        ot = jnp.where(out_row < HEAD, ot[:, :W], ot[:, W:])
        o_ref[j * W:(j + 1) * W, blk * LANES:(blk + 1) * LANES] = ot.T.astype(o_ref.dtype)

        last_of_block = i + 1 == len(tasks) or tasks[i + 1][0] != j
        if last_of_block and (j + 1) % SWA_PROJ_BLOCKS == 0:
            projections.extend(projection_pieces(slice((j + 1 - SWA_PROJ_BLOCKS) * W, (j + 1) * W)))
        if projections and (i % SWA_PROJ_EVERY == SWA_PROJ_EVERY - 1 or i + 1 == len(tasks)):
            projections.pop(0)()
    while projections:
        projections.pop(0)()


def _swa_call(sinks, q, k, v, gate, h1, w_out, b_out, norm_w, batch, seq_len, q_blocks):
    n_tok = q.shape[0]
    rows = q_blocks * WINDOW
    steps = seq_len // rows
    cur = lambda b, n: (b * steps + n, 0)
    prev = lambda b, n: (jnp.maximum((b * steps + n) * q_blocks - 1, 0), 0)
    const = lambda b, n: (0, 0)
    return pl.pallas_call(
        functools.partial(_swa_kernel, q_blocks=q_blocks),
        grid=(batch, steps),
        in_specs=[pl.BlockSpec(memory_space=pltpu.SMEM),
                  pl.BlockSpec((rows, MIX1), cur),
                  pl.BlockSpec((rows, SWA_KV), cur),
                  pl.BlockSpec((WINDOW, SWA_KV), prev),
                  pl.BlockSpec((rows, SWA_KV), cur),
                  pl.BlockSpec((WINDOW, SWA_KV), prev),
                  pl.BlockSpec((rows, MIX1), cur),
                  pl.BlockSpec((rows, D_MODEL), cur),
                  pl.BlockSpec((1,) + w_out.shape[1:], lambda b, n: (0, 0, 0), pipeline_mode=pl.Buffered(1)),
                  pl.BlockSpec((1, D_MODEL), const),
                  pl.BlockSpec((1, D_MODEL), const)],
        out_specs=pl.BlockSpec((rows, D_MODEL), cur),
        out_shape=jax.ShapeDtypeStruct((n_tok, D_MODEL), F32),
        scratch_shapes=[pltpu.VMEM((rows, MIX1), BF16),
                        pltpu.VMEM((MIX1, D_MODEL), BF16)],
        compiler_params=_cparams(("arbitrary", "arbitrary")),
        name="l1_swa_out",
    )(sinks, q, k, k, v, v, gate, h1, w_out, b_out, norm_w)


def _pad_rows(w, rows):
    return jnp.concatenate([w, jnp.zeros((rows - w.shape[0], w.shape[1]), w.dtype)], axis=0)


def _pair_heads(t, axis):
    shape = t.shape
    split = shape[:axis] + (SWA_KV_HEADS // 2, 2, SWA_GROUP, SWA_HEAD) + shape[axis + 1:]
    return jnp.swapaxes(t.reshape(split), axis + 1, axis + 2).reshape(shape)


def _rope_tables(seq_len):
    half = ROPE_DIMS // 2
    inv_freq = ROPE_THETA ** (-jnp.arange(half, dtype=F32) / half)
    ang = jnp.arange(seq_len).astype(F32)[:, None] * inv_freq
    trig = jnp.concatenate([jnp.cos(ang), jnp.sin(ang)], axis=1)
    d = jnp.arange(LANES) % SWA_HEAD
    src = jnp.arange(2 * half)[:, None]
    f = (d % half)[None, :]
    rot = (d < ROPE_DIMS)[None, :]
    lo = (d < half)[None, :]
    sel_cos = ((src == f) & rot).astype(F32)
    sel_lo = -((src == half + f) & lo).astype(F32)
    sel_hi = ((src == half + f) & rot & ~lo).astype(F32)
    sel = jnp.concatenate([sel_cos, sel_lo, sel_hi], axis=1)
    tab = jnp.dot(trig, sel, precision=lax.Precision.HIGHEST)
    cos = tab[:, :LANES] + (~rot).astype(F32)
    return cos, tab[:, LANES:2 * LANES], tab[:, 2 * LANES:]


def _forward(x, norm_w, w_in0, gla_gk_up, gla_gk_bias, gla_norm_w, rwkv_mu, rwkv_w0, rwkv_w_up,
             rwkv_a0, rwkv_a_up, rwkv_k_k, rwkv_k_a, rwkv_r_k, rwkv_ln_w, rwkv_ln_b, w_out0,
             w_in1, b_in1, attn_sinks, w_out1, b_out1, final_norm_w, *, tm, gla_chunks, rwkv_chunks, scan_chunks,
             swa_blocks):
    batch, seq_len, _ = x.shape
    x2 = x.reshape(batch * seq_len, D_MODEL)
    row = lambda t: t.reshape(1, -1)

    gqkv, glow, rkv, lora, gate0 = _in0_call(x2, row(norm_w[0]), jnp.swapaxes(w_in0, 1, 2), row(rwkv_mu[0]),
                                             seq_len, tm)

    up_pad = _pad_rows(gla_gk_up[0], GLOW_PAD).astype(BF16)
    o_a = _gla_call(gqkv, glow, up_pad, row(gla_gk_bias[0]), row(gla_norm_w[0]), batch, seq_len, gla_chunks)

    zeros_r = jnp.zeros((RWKV_DECAY_RANK, RWKV_W), F32)
    wup_pad = jnp.concatenate([rwkv_w_up[0], zeros_r], axis=0).astype(BF16)
    aup_pad = jnp.concatenate([zeros_r, rwkv_a_up[0]], axis=0).astype(BF16)
    rp, op, bonus, m, n = _rwkv_chunk_call(
        rkv, lora, row(rwkv_w0[0]), wup_pad, row(rwkv_a0[0]), aup_pad,
        row(rwkv_k_k[0]), row(rwkv_k_a[0]), row(rwkv_r_k[0]), batch, seq_len, rwkv_chunks)
    o_b = _rwkv_scan_call(rp, op, bonus, m, n, row(rwkv_ln_w[0]), row(rwkv_ln_b[0]),
                          batch, seq_len, scan_chunks)


    b1 = b_in1[0]
    b1p = row(jnp.concatenate([_pair_heads(b1[:MIX1], 0), b1[MIX1:]]))
    sinks_p = jnp.swapaxes(attn_sinks[0].reshape(SWA_KV_HEADS // 2, 2, SWA_GROUP), 1, 2).reshape(SWA_Q_HEADS)
    cos, slo, shi = _rope_tables(seq_len)
    h1, q, k, v, gate1 = _mid_call(o_a, o_b, gate0, x2, w_out0, row(norm_w[1]), w_in1,
                                   b1p, cos, slo, shi, seq_len, tm)
    y = _swa_call(sinks_p, q, k, v, gate1, h1, w_out1, row(b_out1[0]), row(final_norm_w),
                  batch, seq_len, swa_blocks)
    return y.reshape(batch, seq_len, D_MODEL)


def kernel(x, norm_w, w_in0, gla_gk_up, gla_gk_bias, gla_norm_w, rwkv_mu, rwkv_w0, rwkv_w_up, rwkv_a0,
           rwkv_a_up, rwkv_k_k, rwkv_k_a, rwkv_r_k, rwkv_ln_w, rwkv_ln_b, w_out0, w_in1, b_in1,
           attn_sinks, w_out1, b_out1, final_norm_w):
    return _forward(x, norm_w, w_in0, gla_gk_up, gla_gk_bias, gla_norm_w, rwkv_mu, rwkv_w0, rwkv_w_up,
                    rwkv_a0, rwkv_a_up, rwkv_k_k, rwkv_k_a, rwkv_r_k, rwkv_ln_w, rwkv_ln_b, w_out0,
                    w_in1, b_in1, attn_sinks, w_out1, b_out1, final_norm_w,
                    tm=512, gla_chunks=64, rwkv_chunks=64, scan_chunks=8, swa_blocks=8)
```

```python
import functools
import math

import jax
import jax.numpy as jnp
from jax import lax
from jax.experimental import pallas as pl
from jax.experimental.pallas import tpu as pltpu

F32 = jnp.float32
BF16 = jnp.bfloat16

D_MODEL = 1024
NORM_EPS = 1e-5

GLA_HEADS = 4
GLA_DK = 64
GLA_DV = 128
GLA_KEY = GLA_HEADS * GLA_DK
GLA_VAL = GLA_HEADS * GLA_DV
GLA_GATE_RANK = 16
GLA_GATE_NORMALIZER = 16.0
GLA_CHUNK = 64

RWKV_HEADS = 8
RWKV_HEAD = 64
RWKV_W = RWKV_HEADS * RWKV_HEAD
RWKV_DECAY_RANK = 64
RWKV_A_RANK = 64
RWKV_LN_EPS = 64e-5
RWKV_RKV = 3 * RWKV_W
RWKV_LORA = RWKV_DECAY_RANK + RWKV_A_RANK
RWKV_SHIFT = RWKV_RKV + RWKV_LORA
RWKV_CHUNK = 64
RWKV_GROUP = 16

MIX0 = GLA_VAL + RWKV_W
GLA_QKV = 2 * GLA_KEY + GLA_VAL

SWA_Q_HEADS = 16
SWA_KV_HEADS = 4
SWA_GROUP = SWA_Q_HEADS // SWA_KV_HEADS
SWA_HEAD = 64
WINDOW = 128
ROPE_DIMS = SWA_HEAD // 4
ROPE_THETA = 500000.0
MIX1 = SWA_Q_HEADS * SWA_HEAD
SWA_KV = SWA_KV_HEADS * SWA_HEAD
SWA_QKV = MIX1 + 2 * SWA_KV
SWA_PROJ_BLOCKS = 2
SWA_PROJ_PIECES = 4
SWA_PROJ_EVERY = 3

LOG2_E = 1.4426950408889634
LANES = 128
HEAD = 64
GLOW_PAD = LANES
VMEM_LIMIT = 56 * 1024 * 1024


def _cparams(sem):
    return pltpu.CompilerParams(dimension_semantics=sem, vmem_limit_bytes=VMEM_LIMIT)


def _dot(a, b):
    return jnp.dot(a.astype(BF16), b.astype(BF16), preferred_element_type=F32)


def _dot_nt(a, b):
    return lax.dot_general(a.astype(BF16), b.astype(BF16), (((1,), (1,)), ((), ())),
                           preferred_element_type=F32)


def _dot_tn(a, b):
    return lax.dot_general(a.astype(BF16), b.astype(BF16), (((0,), (0,)), ((), ())),
                           preferred_element_type=F32)


def _split2(x):
    hi = x.astype(BF16)
    lo = (x - hi.astype(F32)).astype(BF16)
    return hi, lo


def _dot_exact_rhs(a_bf16, x):
    hi, lo = _split2(x)
    both = jnp.dot(a_bf16, jnp.concatenate([hi, lo], axis=1), preferred_element_type=F32)
    return both[:, :x.shape[1]] + both[:, x.shape[1]:]


def _dot_exact_lhs(x, b_bf16):
    hi, lo = _split2(x)
    return (jnp.dot(hi, b_bf16, preferred_element_type=F32)
            + jnp.dot(lo, b_bf16, preferred_element_type=F32))


def _iota(shape, dim):
    return lax.broadcasted_iota(jnp.int32, shape, dim)


def _tril_ones(n, dtype=BF16):
    return (_iota((n, n), 0) >= _iota((n, n), 1)).astype(dtype)


def _head_block_ones(n=LANES, dtype=BF16):
    return ((_iota((n, n), 0) // HEAD) == (_iota((n, n), 1) // HEAD)).astype(dtype)


def _head_stack(x):
    head = (_iota(x.shape, 1) % LANES) // HEAD
    return jnp.concatenate([jnp.where(head == 0, x, 0.0), jnp.where(head == 1, x, 0.0)], axis=0)


def _softplus(z):
    return jnp.maximum(z, 0.0) + jnp.log(1.0 + jnp.exp(-jnp.abs(z)))


def _sigmoid(z):
    return 0.5 + 0.5 * jnp.tanh(0.5 * z)


def _silu_bf16(g):
    h = g * 0.5
    return h + h * jnp.tanh(h)


def _rmsnorm_rows(x, w):
    return x * lax.rsqrt(jnp.mean(x * x, axis=-1, keepdims=True) + NORM_EPS) * w


def _in0_kernel(x_ref, nw_ref, w_ref, mu_ref,
                gqkv_ref, glow_ref, rkv_ref, lora_ref, gate_ref, carry_ref, wg_ref, wr_ref,
                *, tiles_per_seq):
    i = pl.program_id(0)

    @pl.when(i == 0)
    def _():
        carry_ref[...] = jnp.zeros_like(carry_ref)
        def put(dst, col0, row0, width):
            step = 4 * LANES if width % (4 * LANES) == 0 else LANES
            for c in range(0, width, step):
                dst[:, col0 + c:col0 + c + step] = w_ref[0, row0 + c:row0 + c + step, :].T.astype(BF16)

        put(wg_ref, 0, 0, GLA_QKV)
        put(wr_ref, 0, GLA_QKV, GLOW_PAD)
        put(wr_ref, GLOW_PAD, GLA_QKV + GLA_GATE_RANK, RWKV_SHIFT + MIX0)

    xn = _rmsnorm_rows(x_ref[...], nw_ref[...]).astype(BF16)
    gqkv_ref[...] = jnp.dot(xn, wg_ref[...], preferred_element_type=F32).astype(gqkv_ref.dtype)
    low_rw = jnp.dot(xn, wr_ref[:, :GLOW_PAD + RWKV_SHIFT], preferred_element_type=F32)
    glow_ref[...] = low_rw[:, :GLOW_PAD]
    rw = low_rw[:, GLOW_PAD:]
    gate_ref[...] = jnp.dot(xn, wr_ref[:, GLOW_PAD + RWKV_SHIFT:], preferred_element_type=F32).astype(gate_ref.dtype)

    tm = rw.shape[0]
    first = (i % tiles_per_seq) == 0
    prev_last = jnp.where(first, 0.0, carry_ref[7:8, :])
    rolled = pltpu.roll(rw, 1, 0)
    prev = jnp.where(_iota(rw.shape, 0) == 0, prev_last, rolled)
    mixed = rw + (prev - rw) * mu_ref[...]
    rkv_ref[...] = mixed[:, :RWKV_RKV].astype(rkv_ref.dtype)
    lora_ref[...] = mixed[:, RWKV_RKV:]
    carry_ref[...] = rw[tm - 8:tm, :]


def _in0_call(x2, norm_w, w_in, mu, seq_len, tm):
    n_tok = x2.shape[0]
    row = lambda i: (i, 0)
    const = lambda i: (0, 0)
    outs = [(GLA_QKV, BF16), (GLOW_PAD, F32), (RWKV_RKV, BF16), (RWKV_LORA, F32), (MIX0, BF16)]
    return pl.pallas_call(
        functools.partial(_in0_kernel, tiles_per_seq=seq_len // tm),
        grid=(n_tok // tm,),
        in_specs=[pl.BlockSpec((tm, D_MODEL), row),
                  pl.BlockSpec((1, D_MODEL), const),
                  pl.BlockSpec((1,) + w_in.shape[1:], lambda i: (0, 0, 0), pipeline_mode=pl.Buffered(1)),
                  pl.BlockSpec((1, RWKV_SHIFT), const)],
        out_specs=[pl.BlockSpec((tm, n), row) for n, _ in outs],
        out_shape=[jax.ShapeDtypeStruct((n_tok, n), dt) for n, dt in outs],
        scratch_shapes=[pltpu.VMEM((8, RWKV_SHIFT), F32),
                        pltpu.VMEM((D_MODEL, GLA_QKV), BF16),
                        pltpu.VMEM((D_MODEL, GLOW_PAD + RWKV_SHIFT + MIX0), BF16)],
        compiler_params=_cparams(("arbitrary",)),
        name="l0_norm_proj",
    )(x2, norm_w, w_in, mu)


def _gla_kernel(q_ref, k_ref, glow_ref, v_ref, up_ref, bias_ref, nw_ref, o_ref, st_ref, *, chunks):
    c = pl.program_id(2)

    @pl.when(c == 0)
    def _():
        st_ref[...] = jnp.zeros_like(st_ref)

    C = GLA_CHUNK
    tril = _tril_ones(C)
    causal = _iota((C, LANES), 0) >= (_iota((C, LANES), 1) % HEAD)
    sr = _iota((2 * GLA_DV, LANES), 0)
    sl = _iota((2 * GLA_DV, LANES), 1)
    st_mask = (sr // GLA_DV) == (sl // HEAD)
    vl = _iota((C, 2 * GLA_DV), 1)
    scale = GLA_DK ** -0.5
    z = _dot(glow_ref[...], up_ref[...]) + bias_ref[...]
    g_all = -_softplus(-z) * (LOG2_E / GLA_GATE_NORMALIZER)
    q_all = q_ref[...].astype(F32) * scale
    k_all = k_ref[...].astype(F32)
    rows = [slice(j * C, (j + 1) * C) for j in range(chunks)]
    bs = [_dot_exact_rhs(tril, g_all[rw]) for rw in rows]
    qe, ke, qb, kl, dec, vs = [], [], [], [], [], []
    for rw, b in zip(rows, bs):
        ref = b[C // 2:C // 2 + 1, :]
        b_last = b[C - 1:C, :]
        qe.append(q_all[rw] * jnp.exp2(b - ref))
        ke.append(k_all[rw] * jnp.exp2(ref - b))
        qb.append(qe[-1] * jnp.exp2(ref))
        kl.append(ke[-1] * jnp.exp2(b_last - ref))
        dec.append(jnp.exp2(b_last))
        vs.append(v_ref[rw, :])
    att = [jnp.where(causal, _dot_nt(qe[j], _head_stack(ke[j])), 0.0) for j in range(chunks)]
    kv = [jnp.where(st_mask, _dot_tn(vs[j], kl[j]), 0.0) for j in range(chunks)]
    v_diag = [jnp.concatenate([jnp.where(vl < GLA_DV, vs[j], jnp.zeros_like(vs[j])),
                               jnp.where(vl >= GLA_DV, vs[j], jnp.zeros_like(vs[j]))], axis=0)
              for j in range(chunks)]
    intra = [jnp.dot(att[j].astype(BF16), v_diag[j], preferred_element_type=F32) for j in range(chunks)]
    states = [st_ref[...]]
    for j in range(chunks):
        states.append(states[j] * dec[j] + kv[j])
    st_ref[...] = states[chunks]
    for j in range(chunks):
        o = intra[j] + _dot_nt(qb[j], states[j])
        for h in range(2):
            oh = o[:, h * GLA_DV:(h + 1) * GLA_DV]
            oh = oh * lax.rsqrt(jnp.mean(oh * oh, axis=-1, keepdims=True) + NORM_EPS) * nw_ref[...]
            o_ref[rows[j], h * GLA_DV:(h + 1) * GLA_DV] = oh.astype(o_ref.dtype)


def _gla_call(gqkv, glow, up_pad, bias, norm_w, batch, seq_len, chunks):
    n_tok = gqkv.shape[0]
    tcb = chunks * GLA_CHUNK
    steps = seq_len // tcb
    pairs = GLA_KEY // LANES
    return pl.pallas_call(
        functools.partial(_gla_kernel, chunks=chunks),
        grid=(batch, pairs, steps),
        in_specs=[pl.BlockSpec((tcb, LANES), lambda b, p, c: (b * steps + c, p)),
                  pl.BlockSpec((tcb, LANES), lambda b, p, c: (b * steps + c, pairs + p)),
                  pl.BlockSpec((tcb, GLOW_PAD), lambda b, p, c: (b * steps + c, 0)),
                  pl.BlockSpec((tcb, 2 * GLA_DV), lambda b, p, c: (b * steps + c, pairs + p)),
                  pl.BlockSpec((GLOW_PAD, LANES), lambda b, p, c: (0, p)),
                  pl.BlockSpec((1, LANES), lambda b, p, c: (0, p)),
                  pl.BlockSpec((1, GLA_DV), lambda b, p, c: (0, 0))],
        out_specs=pl.BlockSpec((tcb, 2 * GLA_DV), lambda b, p, c: (b * steps + c, p)),
        out_shape=jax.ShapeDtypeStruct((n_tok, GLA_VAL), BF16),
        scratch_shapes=[pltpu.VMEM((2 * GLA_DV, LANES), F32)],
        compiler_params=_cparams(("parallel", "parallel", "arbitrary")),
        name="l0_gla",
    )(gqkv, gqkv, glow, gqkv, up_pad, bias, norm_w)


def _merge_masks(n):
    r = _iota((n, LANES), 0)
    c = _iota((n, LANES), 1) % HEAD
    masks = []
    s = 1
    while s < n:
        masks.append(((r // s) % 2 == 1) & ((c // s) == (r // s) - 1))
        s *= 2
    return (r == c).astype(F32), masks


def _run_interleaved(main, main_steps, side, side_steps):
    done = 0
    spread = max(1, (3 * main_steps) // 4)
    for i, _ in enumerate(main):
        target = -(-(i + 1) * side_steps // spread)
        while done < min(target, side_steps):
            next(side, None)
            done += 1
    for _ in side:
        pass


def _rwkv_chunk_kernel(r_ref, k_ref, v_ref, xwa_ref, w0_ref, wup_ref, a0_ref, aup_ref,
                       kk_ref, ka_ref, rk_ref,
                       rp_ref, op_ref, bonus_ref, m_ref, n_ref, *, chunks, group):
    C = RWKV_CHUNK
    tril = _tril_ones(C)
    rr = _iota((2 * C, LANES), 0)
    cc = _iota((2 * C, LANES), 1) % HEAD
    tri2 = ((rr < C) & (rr > cc)) | (rr - C >= cc)
    hb = _head_block_ones()
    sq_r = _iota((LANES, LANES), 0)
    sq_c = _iota((LANES, LANES), 1)
    same_head = (sq_r // HEAD) == (sq_c // HEAD)
    eye128 = sq_r == sq_c

    eye, merge = _merge_masks(C)
    zero = jnp.zeros((C, LANES), F32)
    n = range(group)

    def prepare(g, out):
        rows = slice(g * group * C, (g + 1) * group * C)
        r_all = r_ref[rows, :].astype(F32)
        k_all = k_ref[rows, :].astype(F32)
        v_all = v_ref[rows, :].astype(F32)
        xwa = xwa_ref[rows, :]
        y = w0_ref[...] + _dot(jnp.tanh(xwa), wup_ref[...])
        lw_all = _sigmoid(y) * (-LOG2_E * math.exp(-0.5))
        a_sig = _sigmoid(a0_ref[...] + _dot(xwa, aup_ref[...]))
        kk = k_all * kk_ref[...]
        kk = kk * lax.rsqrt(jnp.maximum(_dot(kk * kk, hb), 1e-24))
        k_all = k_all * (1.0 + (a_sig - 1.0) * ka_ref[...])
        bonus_ref[rows, :] = (_dot(r_all * k_all * rk_ref[...], hb) * v_all).astype(bonus_ref.dtype)
        a_all = -kk
        b_all = kk * a_sig
        yield
        for j in n:
            rw = slice(j * C, (j + 1) * C)
            cum = _dot_exact_rhs(tril, lw_all[rw])
            cum_last = cum[C - 1:C, :]
            e_neg = jnp.exp2(-cum)
            e_end = jnp.exp2(cum_last - cum)
            out.append(dict(
                rt=r_all[rw] * jnp.exp2(cum),
                at=a_all[rw] * jnp.exp2(cum - lw_all[rw]),
                bt=b_all[rw] * e_neg,
                kt=k_all[rw] * e_neg,
                ends=jnp.concatenate([b_all[rw] * e_end, k_all[rw] * e_end], axis=0),
                v=v_all[rw],
                dec=jnp.exp2(cum_last)))
            yield

    def solve(g, ops):
        lhs = [jnp.concatenate([o["at"], o["rt"]], axis=0) for o in ops]
        both = [_dot_nt(lhs[j], jnp.concatenate([_head_stack(ops[j]["bt"]), _head_stack(ops[j]["kt"])], axis=0))
                for j in n]
        yield
        left = [jnp.where(tri2, p[:, :LANES], 0.0) for p in both]
        right = [jnp.where(tri2, p[:, LANES:], 0.0) for p in both]
        yield
        lows = [lf[:C] for lf in left]
        ts = [eye + jnp.where(merge[0], low, 0.0) for low in lows]
        for sub in merge[1:]:
            ys = [_dot(jnp.where(sub, low, 0.0), _head_stack(t)) for low, t in zip(lows, ts)]
            yield
            ts = [t + _dot(t, _head_stack(y)) for t, y in zip(ts, ys)]
            yield
        kv = [_dot(right[j], _head_stack(ops[j]["v"])) for j in n]
        yield
        wz = [_dot(ts[j], _head_stack(jnp.concatenate([ops[j]["at"], kv[j][:C]], axis=1))) for j in n]
        yield
        ro = [_dot(left[j][C:], _head_stack(wz[j])) for j in n]
        yield
        mn = [_dot_tn(ops[j]["ends"],
                      jnp.concatenate([wz[j], jnp.concatenate([zero, ops[j]["v"]], axis=1)], axis=0))
              for j in n]
        for j in n:
            c = g * group + j
            rows = slice(c * C, (c + 1) * C)
            rp_ref[rows, :] = (ops[j]["rt"] + ro[j][:, :LANES]).astype(rp_ref.dtype)
            op_ref[rows, :] = ro[j][:, LANES:] + kv[j][C:]
            m_ref[0, 0, c] = (jnp.where(eye128, ops[j]["dec"], 0.0)
                              + jnp.where(same_head, mn[j][:, :LANES], 0.0)).astype(m_ref.dtype)
            n_ref[0, 0, c] = jnp.where(same_head, mn[j][:, LANES:], 0.0).astype(n_ref.dtype)
        yield

    solve_stages = 2 + 2 * (len(merge) - 1) + 4
    groups = chunks // group
    ops = [[] for _ in range(groups + 1)]
    for _ in prepare(0, ops[0]):
        pass
    for g in range(groups):
        side = prepare(g + 1, ops[g + 1]) if g + 1 < groups else iter(())
        _run_interleaved(solve(g, ops[g]), solve_stages, side, group + 1)


def _rwkv_chunk_call(rkv, lora, w0, wup_pad, a0, aup_pad, k_k, k_a, r_k, batch, seq_len, chunks):
    n_tok = rkv.shape[0]
    tcb = chunks * RWKV_CHUNK
    steps = seq_len // tcb
    pairs = RWKV_W // LANES
    nc = seq_len // RWKV_CHUNK
    col = lambda off: (lambda b, p, c: (b * steps + c, off + p))
    par = lambda b, p, c: (0, p)
    tok = lambda b, p, c: (b * steps + c, p)
    mat = lambda b, p, c: (b, p, c, 0, 0)
    return pl.pallas_call(
        functools.partial(_rwkv_chunk_kernel, chunks=chunks, group=RWKV_GROUP),
        grid=(batch, pairs, steps),
        in_specs=[pl.BlockSpec((tcb, LANES), col(0)),
                  pl.BlockSpec((tcb, LANES), col(pairs)),
                  pl.BlockSpec((tcb, LANES), col(2 * pairs)),
                  pl.BlockSpec((tcb, RWKV_LORA), lambda b, p, c: (b * steps + c, 0)),
                  pl.BlockSpec((1, LANES), par),
                  pl.BlockSpec((RWKV_LORA, LANES), par),
                  pl.BlockSpec((1, LANES), par),
                  pl.BlockSpec((RWKV_LORA, LANES), par),
                  pl.BlockSpec((1, LANES), par),
                  pl.BlockSpec((1, LANES), par),
                  pl.BlockSpec((1, LANES), par)],
        out_specs=[pl.BlockSpec((tcb, LANES), tok),
                   pl.BlockSpec((tcb, LANES), tok),
                   pl.BlockSpec((tcb, LANES), tok),
                   pl.BlockSpec((1, 1, chunks, LANES, LANES), mat),
                   pl.BlockSpec((1, 1, chunks, LANES, LANES), mat)],
        out_shape=[jax.ShapeDtypeStruct((n_tok, RWKV_W), BF16),
                   jax.ShapeDtypeStruct((n_tok, RWKV_W), F32),
                   jax.ShapeDtypeStruct((n_tok, RWKV_W), BF16),
                   jax.ShapeDtypeStruct((batch, pairs, nc, LANES, LANES), BF16),
                   jax.ShapeDtypeStruct((batch, pairs, nc, LANES, LANES), BF16)],
        compiler_params=_cparams(("parallel", "parallel", "parallel")),
        name="l0_rwkv_chunks",
    )(rkv, rkv, rkv, lora, w0, wup_pad, a0, aup_pad, k_k, k_a, r_k)


def _rwkv_scan_kernel(rp_ref, op_ref, bonus_ref, m_ref, n_ref, lnw_ref, lnb_ref, o_ref, st_ref, *, chunks):
    c = pl.program_id(0)

    @pl.when(c == 0)
    def _():
        st_ref[...] = jnp.zeros_like(st_ref)

    C = RWKV_CHUNK
    batch = rp_ref.shape[0]
    pairs = RWKV_W // LANES
    hb = _head_block_ones()
    seqs = [(b, p) for b in range(batch) for p in range(pairs)]
    cols = {bp: slice(bp[1] * LANES, (bp[1] + 1) * LANES) for bp in seqs}
    states = {bp: st_ref[bp[0], bp[1]] for bp in seqs}
    outs = {bp: [] for bp in seqs}
    for j in range(chunks):
        for b, p in seqs:
            both = _dot(jnp.concatenate([m_ref[b, p, j], rp_ref[b, j * C:(j + 1) * C, cols[b, p]]], axis=0),
                        states[b, p])
            outs[b, p].append(both[LANES:])
            states[b, p] = both[:LANES] + n_ref[b, p, j]
    for b, p in seqs:
        st_ref[b, p] = states[b, p]
    os = [jnp.concatenate(outs[b, p], axis=0) + op_ref[b, :, cols[b, p]] for b, p in seqs]
    means = [_dot_exact_lhs(o, hb) * (1.0 / RWKV_HEAD) for o in os]
    ds = [o - mean for o, mean in zip(os, means)]
    variances = [_dot(d * d, hb) * (1.0 / RWKV_HEAD) for d in ds]
    for (b, p), d, var in zip(seqs, ds, variances):
        c_ = cols[b, p]
        o_ref[b, :, c_] = (d * lax.rsqrt(var + RWKV_LN_EPS) * lnw_ref[:, c_] + lnb_ref[:, c_]
                           + bonus_ref[b, :, c_]).astype(o_ref.dtype)


def _rwkv_scan_call(rp, op, bonus, m, n, ln_w, ln_b, batch, seq_len, chunks):
    tcb = chunks * RWKV_CHUNK
    pairs = RWKV_W // LANES
    seq3 = lambda t: t.reshape(batch, seq_len, RWKV_W)
    tok = lambda c: (0, c, 0)
    const = lambda c: (0, 0)
    mat = lambda c: (0, 0, c, 0, 0)
    out = pl.pallas_call(
        functools.partial(_rwkv_scan_kernel, chunks=chunks),
        grid=(seq_len // tcb,),
        in_specs=[pl.BlockSpec((batch, tcb, RWKV_W), tok),
                  pl.BlockSpec((batch, tcb, RWKV_W), tok),
                  pl.BlockSpec((batch, tcb, RWKV_W), tok),
                  pl.BlockSpec((batch, pairs, chunks, LANES, LANES), mat),
                  pl.BlockSpec((batch, pairs, chunks, LANES, LANES), mat),
                  pl.BlockSpec((1, RWKV_W), const),
                  pl.BlockSpec((1, RWKV_W), const)],
        out_specs=pl.BlockSpec((batch, tcb, RWKV_W), tok),
        out_shape=jax.ShapeDtypeStruct((batch, seq_len, RWKV_W), BF16),
        scratch_shapes=[pltpu.VMEM((batch, pairs, LANES, LANES), F32)],
        compiler_params=_cparams(("arbitrary",)),
        name="l0_rwkv_scan",
    )(seq3(rp), seq3(op), seq3(bonus), m, n, ln_w, ln_b)
    return out.reshape(batch * seq_len, RWKV_W)


def _gated_out0(oa_ref, ob_ref, gate_ref, x_ref, w_ref):
    g = _silu_bf16(gate_ref[...])
    y = jnp.concatenate([oa_ref[...], ob_ref[...]], axis=1) * g
    return x_ref[...] + jnp.dot(y, w_ref[...], preferred_element_type=F32)


def _rope_group(x, cos, sin_lo, sin_hi):
    half = ROPE_DIMS // 2
    return x * cos + pltpu.roll(x, LANES - half, 1) * sin_lo + pltpu.roll(x, half, 1) * sin_hi


def _paired_head_order():
    return [(2 * pp + e) * SWA_GROUP + g
            for pp in range(SWA_KV_HEADS // 2) for g in range(SWA_GROUP) for e in range(2)]


def _mid_kernel(oa_ref, ob_ref, gate0_ref, x_ref, wo32_ref,
                nw_ref, w1_ref, b_ref, cos_ref, slo_ref, shi_ref,
                h_ref, q_ref, k_ref, v_ref, gate_ref, wo_ref, wq_ref, wkv_ref, wg_ref):
    @pl.when(pl.program_id(0) == 0)
    def _():
        wo_ref[...] = wo32_ref[0].astype(BF16)
        wkv_ref[...] = w1_ref[0, :, MIX1:SWA_QKV].astype(BF16)
        for new, old in enumerate(_paired_head_order()):
            dst = slice(new * SWA_HEAD, (new + 1) * SWA_HEAD)
            wq_ref[:, dst] = w1_ref[0, :, old * SWA_HEAD:(old + 1) * SWA_HEAD].astype(BF16)
            wg_ref[:, dst] = w1_ref[0, :, SWA_QKV + old * SWA_HEAD:SWA_QKV + (old + 1) * SWA_HEAD].astype(BF16)

    h = _gated_out0(oa_ref, ob_ref, gate0_ref, x_ref, wo_ref)
    h_ref[...] = h
    hn = _rmsnorm_rows(h, nw_ref[...]).astype(BF16)
    cos = cos_ref[...]
    slo = slo_ref[...]
    shi = shi_ref[...]
    scale = SWA_HEAD ** -0.5 * LOG2_E
    q = jnp.dot(hn, wq_ref[...], preferred_element_type=F32) + b_ref[:, :MIX1]
    kv = jnp.dot(hn, wkv_ref[...], preferred_element_type=F32) + b_ref[:, MIX1:]
    gate_ref[...] = jnp.dot(hn, wg_ref[...], preferred_element_type=F32).astype(gate_ref.dtype)
    for g in range(MIX1 // LANES):
        cols = slice(g * LANES, (g + 1) * LANES)
        q_ref[:, cols] = (_rope_group(q[:, cols], cos, slo, shi) * scale).astype(q_ref.dtype)
    for g in range(SWA_KV // LANES):
        cols = slice(g * LANES, (g + 1) * LANES)
        k_ref[:, cols] = _rope_group(kv[:, cols], cos, slo, shi).astype(k_ref.dtype)
    v_ref[...] = kv[:, SWA_KV:].astype(v_ref.dtype)


def _mid_call(oa, ob, gate0, x2, w_out0, norm_w, w_in1, b_in, cos, slo, shi, seq_len, tm):
    n_tok = x2.shape[0]
    tps = seq_len // tm
    row = lambda i: (i, 0)
    const = lambda i: (0, 0)
    pos = lambda i: (i % tps, 0)
    whole = lambda t: pl.BlockSpec((1,) + t.shape[1:], lambda i: (0, 0, 0), pipeline_mode=pl.Buffered(1))
    return pl.pallas_call(
        _mid_kernel,
        grid=(n_tok // tm,),
        in_specs=[pl.BlockSpec((tm, GLA_VAL), row),
                  pl.BlockSpec((tm, RWKV_W), row),
                  pl.BlockSpec((tm, MIX0), row),
                  pl.BlockSpec((tm, D_MODEL), row),
                  whole(w_out0),
                  pl.BlockSpec((1, D_MODEL), const),
                  whole(w_in1),
                  pl.BlockSpec((1, SWA_QKV), const),
                  pl.BlockSpec((tm, LANES), pos),
                  pl.BlockSpec((tm, LANES), pos),
                  pl.BlockSpec((tm, LANES), pos)],
        out_specs=[pl.BlockSpec((tm, D_MODEL), row),
                   pl.BlockSpec((tm, MIX1), row),
                   pl.BlockSpec((tm, SWA_KV), row),
                   pl.BlockSpec((tm, SWA_KV), row),
                   pl.BlockSpec((tm, MIX1), row)],
        out_shape=[jax.ShapeDtypeStruct((n_tok, D_MODEL), F32),
                   jax.ShapeDtypeStruct((n_tok, MIX1), BF16),
                   jax.ShapeDtypeStruct((n_tok, SWA_KV), BF16),
                   jax.ShapeDtypeStruct((n_tok, SWA_KV), BF16),
                   jax.ShapeDtypeStruct((n_tok, MIX1), BF16)],
        scratch_shapes=[pltpu.VMEM((MIX0, D_MODEL), BF16),
                        pltpu.VMEM((D_MODEL, MIX1), BF16),
                        pltpu.VMEM((D_MODEL, 2 * SWA_KV), BF16),
                        pltpu.VMEM((D_MODEL, MIX1), BF16)],
        compiler_params=_cparams(("arbitrary",)),
        name="l0_out_l1_proj",
    )(oa, ob, gate0, x2, w_out0, norm_w, w_in1, b_in, cos, slo, shi)


def _swa_kernel(sink_ref, q_ref, kc_ref, kp_ref, vc_ref, vp_ref, gate_ref, h_ref, w32_ref, b_ref, nw_ref,
                y_ref, o_ref, w_ref, *, q_blocks):
    n = pl.program_id(1)

    @pl.when((pl.program_id(0) == 0) & (n == 0))
    def _():
        for new, old in enumerate(_paired_head_order()):
            w_ref[new * SWA_HEAD:(new + 1) * SWA_HEAD, :] = (
                w32_ref[0, old * SWA_HEAD:(old + 1) * SWA_HEAD, :].astype(BF16))

    W = WINDOW
    from_prev = _iota((W, 2 * W), 0) > (_iota((W, 2 * W), 1) % W)
    no_prev = jnp.where(n > 0, 0.0, -jnp.inf)
    col_row = _iota((1, 2 * W), 1)
    out_row = _iota((LANES, W), 0)
    kv_groups = SWA_KV // LANES
    groups = MIX1 // LANES // kv_groups
    tasks = [(j, pp, pp * groups + g) for j in range(q_blocks) for pp in range(kv_groups) for g in range(groups)]
    kk, vt = {}, {}
    for j in range(q_blocks):
        for pp in range(kv_groups):
            cols = slice(pp * LANES, (pp + 1) * LANES)
            if j == 0:
                kk[j, pp] = jnp.concatenate([kp_ref[:, cols], kc_ref[:W, cols]], axis=0)
                vv = jnp.concatenate([vp_ref[:, cols], vc_ref[:W, cols]], axis=0)
            else:
                kk[j, pp] = kc_ref[(j - 1) * W:(j + 1) * W, cols]
                vv = vc_ref[(j - 1) * W:(j + 1) * W, cols]
            vt[j, pp] = vv.astype(F32).T.astype(BF16)

    def scores(j, pp, blk):
        q = q_ref[j * W:(j + 1) * W, blk * LANES:(blk + 1) * LANES]
        return lax.dot_general(kk[j, pp], _head_stack(q), (((1,), (1,)), ((), ())), preferred_element_type=F32)

    def projection_pieces(rows):
        width = D_MODEL // SWA_PROJ_PIECES
        gated = []

        def piece(c):
            def run():
                if not gated:
                    gated.append(o_ref[rows, :] * _silu_bf16(gate_ref[rows, :]))
                cols = slice(c * width, (c + 1) * width)
                y_ref[rows, cols] = (h_ref[rows, cols] + b_ref[:, cols]
                                     + jnp.dot(gated[0], w_ref[:, cols], preferred_element_type=F32))
            return run

        def norm():
            y_ref[rows, :] = _rmsnorm_rows(y_ref[rows, :], nw_ref[...])

        return [piece(c) for c in range(SWA_PROJ_PIECES)] + [norm]

    projections = []
    ahead = 8
    pending = [scores(*t) for t in tasks[:ahead]]
    for i, (j, pp, blk) in enumerate(tasks):
        st = pending.pop(0)
        if i + ahead < len(tasks):
            pending.append(scores(*tasks[i + ahead]))
        s_prev = st[:W] + no_prev if j == 0 else st[:W]
        s = jnp.where(from_prev, s_prev, st[W:])
        sink = jnp.where(col_row < W, sink_ref[2 * blk], sink_ref[2 * blk + 1]) * LOG2_E
        m = jnp.maximum(jnp.max(s, axis=0, keepdims=True), sink)
        p = jnp.exp2(s - m)
        denom = jnp.sum(p, axis=0, keepdims=True) + jnp.exp2(sink - m)
        pb = p.astype(BF16)
        zero = jnp.zeros_like(pb)
        p2 = jnp.concatenate([jnp.where(from_prev, pb, zero), jnp.where(from_prev, zero, pb)], axis=0)
        ot = jnp.dot(vt[j, pp], p2, preferred_element_type=F32) * (1.0 / denom)
        ot = jnp.where(out_row < HEAD, ot[:, :W], ot[:, W:])
        o_ref[j * W:(j + 1) * W, blk * LANES:(blk + 1) * LANES] = ot.T.astype(o_ref.dtype)

        last_of_block = i + 1 == len(tasks) or tasks[i + 1][0] != j
        if last_of_block and (j + 1) % SWA_PROJ_BLOCKS == 0:
            projections.extend(projection_pieces(slice((j + 1 - SWA_PROJ_BLOCKS) * W, (j + 1) * W)))
        if projections and (i % SWA_PROJ_EVERY == SWA_PROJ_EVERY - 1 or i + 1 == len(tasks)):
            projections.pop(0)()
    while projections:
        projections.pop(0)()


def _swa_call(sinks, q, k, v, gate, h1, w_out, b_out, norm_w, batch, seq_len, q_blocks):
    n_tok = q.shape[0]
    rows = q_blocks * WINDOW
    steps = seq_len // rows
    cur = lambda b, n: (b * steps + n, 0)
    prev = lambda b, n: (jnp.maximum((b * steps + n) * q_blocks - 1, 0), 0)
    const = lambda b, n: (0, 0)
    return pl.pallas_call(
        functools.partial(_swa_kernel, q_blocks=q_blocks),
        grid=(batch, steps),
        in_specs=[pl.BlockSpec(memory_space=pltpu.SMEM),
                  pl.BlockSpec((rows, MIX1), cur),
                  pl.BlockSpec((rows, SWA_KV), cur),
                  pl.BlockSpec((WINDOW, SWA_KV), prev),
                  pl.BlockSpec((rows, SWA_KV), cur),
                  pl.BlockSpec((WINDOW, SWA_KV), prev),
                  pl.BlockSpec((rows, MIX1), cur),
                  pl.BlockSpec((rows, D_MODEL), cur),
                  pl.BlockSpec((1,) + w_out.shape[1:], lambda b, n: (0, 0, 0), pipeline_mode=pl.Buffered(1)),
                  pl.BlockSpec((1, D_MODEL), const),
                  pl.BlockSpec((1, D_MODEL), const)],
        out_specs=pl.BlockSpec((rows, D_MODEL), cur),
        out_shape=jax.ShapeDtypeStruct((n_tok, D_MODEL), F32),
        scratch_shapes=[pltpu.VMEM((rows, MIX1), BF16),
                        pltpu.VMEM((MIX1, D_MODEL), BF16)],
        compiler_params=_cparams(("arbitrary", "arbitrary")),
        name="l1_swa_out",
    )(sinks, q, k, k, v, v, gate, h1, w_out, b_out, norm_w)


def _pad_rows(w, rows):
    return jnp.concatenate([w, jnp.zeros((rows - w.shape[0], w.shape[1]), w.dtype)], axis=0)


def _pair_heads(t, axis):
    shape = t.shape
    split = shape[:axis] + (SWA_KV_HEADS // 2, 2, SWA_GROUP, SWA_HEAD) + shape[axis + 1:]
    return jnp.swapaxes(t.reshape(split), axis + 1, axis + 2).reshape(shape)


def _rope_tables(seq_len):
    half = ROPE_DIMS // 2
    inv_freq = ROPE_THETA ** (-jnp.arange(half, dtype=F32) / half)
    ang = jnp.arange(seq_len).astype(F32)[:, None] * inv_freq
    trig = jnp.concatenate([jnp.cos(ang), jnp.sin(ang)], axis=1)
    d = jnp.arange(LANES) % SWA_HEAD
    src = jnp.arange(2 * half)[:, None]
    f = (d % half)[None, :]
    rot = (d < ROPE_DIMS)[None, :]
    lo = (d < half)[None, :]
    sel_cos = ((src == f) & rot).astype(F32)
    sel_lo = -((src == half + f) & lo).astype(F32)
    sel_hi = ((src == half + f) & rot & ~lo).astype(F32)
    sel = jnp.concatenate([sel_cos, sel_lo, sel_hi], axis=1)
    tab = jnp.dot(trig, sel, precision=lax.Precision.HIGHEST)
    cos = tab[:, :LANES] + (~rot).astype(F32)
    return cos, tab[:, LANES:2 * LANES], tab[:, 2 * LANES:]


def _forward(x, norm_w, w_in0, gla_gk_up, gla_gk_bias, gla_norm_w, rwkv_mu, rwkv_w0, rwkv_w_up,
             rwkv_a0, rwkv_a_up, rwkv_k_k, rwkv_k_a, rwkv_r_k, rwkv_ln_w, rwkv_ln_b, w_out0,
             w_in1, b_in1, attn_sinks, w_out1, b_out1, final_norm_w, *, tm, gla_chunks, rwkv_chunks, scan_chunks,
             swa_blocks):
    batch, seq_len, _ = x.shape
    x2 = x.reshape(batch * seq_len, D_MODEL)
    row = lambda t: t.reshape(1, -1)

    gqkv, glow, rkv, lora, gate0 = _in0_call(x2, row(norm_w[0]), jnp.swapaxes(w_in0, 1, 2), row(rwkv_mu[0]),
                                             seq_len, tm)

    up_pad = _pad_rows(gla_gk_up[0], GLOW_PAD).astype(BF16)
    o_a = _gla_call(gqkv, glow, up_pad, row(gla_gk_bias[0]), row(gla_norm_w[0]), batch, seq_len, gla_chunks)

    zeros_r = jnp.zeros((RWKV_DECAY_RANK, RWKV_W), F32)
    wup_pad = jnp.concatenate([rwkv_w_up[0], zeros_r], axis=0).astype(BF16)
    aup_pad = jnp.concatenate([zeros_r, rwkv_a_up[0]], axis=0).astype(BF16)
    rp, op, bonus, m, n = _rwkv_chunk_call(
        rkv, lora, row(rwkv_w0[0]), wup_pad, row(rwkv_a0[0]), aup_pad,
        row(rwkv_k_k[0]), row(rwkv_k_a[0]), row(rwkv_r_k[0]), batch, seq_len, rwkv_chunks)
    o_b = _rwkv_scan_call(rp, op, bonus, m, n, row(rwkv_ln_w[0]), row(rwkv_ln_b[0]),
                          batch, seq_len, scan_chunks)


    b1 = b_in1[0]
    b1p = row(jnp.concatenate([_pair_heads(b1[:MIX1], 0), b1[MIX1:]]))
    sinks_p = jnp.swapaxes(attn_sinks[0].reshape(SWA_KV_HEADS // 2, 2, SWA_GROUP), 1, 2).reshape(SWA_Q_HEADS)
    cos, slo, shi = _rope_tables(seq_len)
    h1, q, k, v, gate1 = _mid_call(o_a, o_b, gate0, x2, w_out0, row(norm_w[1]), w_in1,
                                   b1p, cos, slo, shi, seq_len, tm)
    y = _swa_call(sinks_p, q, k, v, gate1, h1, w_out1, row(b_out1[0]), row(final_norm_w),
                  batch, seq_len, swa_blocks)
    return y.reshape(batch, seq_len, D_MODEL)


def kernel(x, norm_w, w_in0, gla_gk_up, gla_gk_bias, gla_norm_w, rwkv_mu, rwkv_w0, rwkv_w_up, rwkv_a0,
           rwkv_a_up, rwkv_k_k, rwkv_k_a, rwkv_r_k, rwkv_ln_w, rwkv_ln_b, w_out0, w_in1, b_in1,
           attn_sinks, w_out1, b_out1, final_norm_w):
    return _forward(x, norm_w, w_in0, gla_gk_up, gla_gk_bias, gla_norm_w, rwkv_mu, rwkv_w0, rwkv_w_up,
                    rwkv_a0, rwkv_a_up, rwkv_k_k, rwkv_k_a, rwkv_r_k, rwkv_ln_w, rwkv_ln_b, w_out0,
                    w_in1, b_in1, attn_sinks, w_out1, b_out1, final_norm_w,
                    tm=512, gla_chunks=64, rwkv_chunks=64, scan_chunks=8, swa_blocks=8)
```

```python
import functools
import math

import jax
import jax.numpy as jnp
from jax import lax
from jax.experimental import pallas as pl
from jax.experimental.pallas import tpu as pltpu

F32 = jnp.float32
BF16 = jnp.bfloat16

D_MODEL = 1024
NORM_EPS = 1e-5

GLA_HEADS = 4
GLA_DK = 64
GLA_DV = 128
GLA_KEY = GLA_HEADS * GLA_DK
GLA_VAL = GLA_HEADS * GLA_DV
GLA_GATE_RANK = 16
GLA_GATE_NORMALIZER = 16.0
GLA_CHUNK = 64

RWKV_HEADS = 8
RWKV_HEAD = 64
RWKV_W = RWKV_HEADS * RWKV_HEAD
RWKV_DECAY_RANK = 64
RWKV_A_RANK = 64
RWKV_LN_EPS = 64e-5
RWKV_RKV = 3 * RWKV_W
RWKV_LORA = RWKV_DECAY_RANK + RWKV_A_RANK
RWKV_SHIFT = RWKV_RKV + RWKV_LORA
RWKV_CHUNK = 64
RWKV_GROUP = 16

MIX0 = GLA_VAL + RWKV_W
GLA_QKV = 2 * GLA_KEY + GLA_VAL

SWA_Q_HEADS = 16
SWA_KV_HEADS = 4
SWA_GROUP = SWA_Q_HEADS // SWA_KV_HEADS
SWA_HEAD = 64
WINDOW = 128
ROPE_DIMS = SWA_HEAD // 4
ROPE_THETA = 500000.0
MIX1 = SWA_Q_HEADS * SWA_HEAD
SWA_KV = SWA_KV_HEADS * SWA_HEAD
SWA_QKV = MIX1 + 2 * SWA_KV
SWA_PROJ_BLOCKS = 2
SWA_PROJ_PIECES = 4
SWA_PROJ_EVERY = 3

LOG2_E = 1.4426950408889634
LANES = 128
HEAD = 64
GLOW_PAD = LANES
VMEM_LIMIT = 56 * 1024 * 1024


def _cparams(sem):
    return pltpu.CompilerParams(dimension_semantics=sem, vmem_limit_bytes=VMEM_LIMIT)


def _dot(a, b):
    return jnp.dot(a.astype(BF16), b.astype(BF16), preferred_element_type=F32)


def _dot_nt(a, b):
    return lax.dot_general(a.astype(BF16), b.astype(BF16), (((1,), (1,)), ((), ())),
                           preferred_element_type=F32)


def _dot_tn(a, b):
    return lax.dot_general(a.astype(BF16), b.astype(BF16), (((0,), (0,)), ((), ())),
                           preferred_element_type=F32)


def _split2(x):
    hi = x.astype(BF16)
    lo = (x - hi.astype(F32)).astype(BF16)
    return hi, lo


def _dot_exact_rhs(a_bf16, x):
    hi, lo = _split2(x)
    both = jnp.dot(a_bf16, jnp.concatenate([hi, lo], axis=1), preferred_element_type=F32)
    return both[:, :x.shape[1]] + both[:, x.shape[1]:]


def _dot_exact_lhs(x, b_bf16):
    hi, lo = _split2(x)
    return (jnp.dot(hi, b_bf16, preferred_element_type=F32)
            + jnp.dot(lo, b_bf16, preferred_element_type=F32))


def _iota(shape, dim):
    return lax.broadcasted_iota(jnp.int32, shape, dim)


def _tril_ones(n, dtype=BF16):
    return (_iota((n, n), 0) >= _iota((n, n), 1)).astype(dtype)


def _head_block_ones(n=LANES, dtype=BF16):
    return ((_iota((n, n), 0) // HEAD) == (_iota((n, n), 1) // HEAD)).astype(dtype)


def _head_stack(x):
    head = (_iota(x.shape, 1) % LANES) // HEAD
    return jnp.concatenate([jnp.where(head == 0, x, 0.0), jnp.where(head == 1, x, 0.0)], axis=0)


def _softplus(z):
    return jnp.maximum(z, 0.0) + jnp.log(1.0 + jnp.exp(-jnp.abs(z)))


def _sigmoid(z):
    return 0.5 + 0.5 * jnp.tanh(0.5 * z)


def _silu_bf16(g):
    h = g * 0.5
    return h + h * jnp.tanh(h)


def _rmsnorm_rows(x, w):
    return x * lax.rsqrt(jnp.mean(x * x, axis=-1, keepdims=True) + NORM_EPS) * w


def _in0_kernel(x_ref, nw_ref, w_ref, mu_ref,
                gqkv_ref, glow_ref, rkv_ref, lora_ref, gate_ref, carry_ref, wg_ref, wr_ref,
                *, tiles_per_seq):
    i = pl.program_id(0)

    @pl.when(i == 0)
    def _():
        carry_ref[...] = jnp.zeros_like(carry_ref)
        def put(dst, col0, row0, width):
            step = 4 * LANES if width % (4 * LANES) == 0 else LANES
            for c in range(0, width, step):
                dst[:, col0 + c:col0 + c + step] = w_ref[0, row0 + c:row0 + c + step, :].T.astype(BF16)

        put(wg_ref, 0, 0, GLA_QKV)
        put(wr_ref, 0, GLA_QKV, GLOW_PAD)
        put(wr_ref, GLOW_PAD, GLA_QKV + GLA_GATE_RANK, RWKV_SHIFT + MIX0)

    xn = _rmsnorm_rows(x_ref[...], nw_ref[...]).astype(BF16)
    gqkv_ref[...] = jnp.dot(xn, wg_ref[...], preferred_element_type=F32).astype(gqkv_ref.dtype)
    low_rw = jnp.dot(xn, wr_ref[:, :GLOW_PAD + RWKV_SHIFT], preferred_element_type=F32)
    glow_ref[...] = low_rw[:, :GLOW_PAD]
    rw = low_rw[:, GLOW_PAD:]
    gate_ref[...] = jnp.dot(xn, wr_ref[:, GLOW_PAD + RWKV_SHIFT:], preferred_element_type=F32).astype(gate_ref.dtype)

    tm = rw.shape[0]
    first = (i % tiles_per_seq) == 0
    prev_last = jnp.where(first, 0.0, carry_ref[7:8, :])
    rolled = pltpu.roll(rw, 1, 0)
    prev = jnp.where(_iota(rw.shape, 0) == 0, prev_last, rolled)
    mixed = rw + (prev - rw) * mu_ref[...]
    rkv_ref[...] = mixed[:, :RWKV_RKV].astype(rkv_ref.dtype)
    lora_ref[...] = mixed[:, RWKV_RKV:]
    carry_ref[...] = rw[tm - 8:tm, :]


def _in0_call(x2, norm_w, w_in, mu, seq_len, tm):
    n_tok = x2.shape[0]
    row = lambda i: (i, 0)
    const = lambda i: (0, 0)
    outs = [(GLA_QKV, BF16), (GLOW_PAD, F32), (RWKV_RKV, BF16), (RWKV_LORA, F32), (MIX0, BF16)]
    return pl.pallas_call(
        functools.partial(_in0_kernel, tiles_per_seq=seq_len // tm),
        grid=(n_tok // tm,),
        in_specs=[pl.BlockSpec((tm, D_MODEL), row),
                  pl.BlockSpec((1, D_MODEL), const),
                  pl.BlockSpec((1,) + w_in.shape[1:], lambda i: (0, 0, 0), pipeline_mode=pl.Buffered(1)),
                  pl.BlockSpec((1, RWKV_SHIFT), const)],
        out_specs=[pl.BlockSpec((tm, n), row) for n, _ in outs],
        out_shape=[jax.ShapeDtypeStruct((n_tok, n), dt) for n, dt in outs],
        scratch_shapes=[pltpu.VMEM((8, RWKV_SHIFT), F32),
                        pltpu.VMEM((D_MODEL, GLA_QKV), BF16),
                        pltpu.VMEM((D_MODEL, GLOW_PAD + RWKV_SHIFT + MIX0), BF16)],
        compiler_params=_cparams(("arbitrary",)),
        name="l0_norm_proj",
    )(x2, norm_w, w_in, mu)


def _gla_kernel(q_ref, k_ref, glow_ref, v_ref, up_ref, bias_ref, nw_ref, o_ref, st_ref, *, chunks):
    c = pl.program_id(2)

    @pl.when(c == 0)
    def _():
        st_ref[...] = jnp.zeros_like(st_ref)

    C = GLA_CHUNK
    tril = _tril_ones(C)
    causal = _iota((C, LANES), 0) >= (_iota((C, LANES), 1) % HEAD)
    sr = _iota((2 * GLA_DV, LANES), 0)
    sl = _iota((2 * GLA_DV, LANES), 1)
    st_mask = (sr // GLA_DV) == (sl // HEAD)
    vl = _iota((C, 2 * GLA_DV), 1)
    scale = GLA_DK ** -0.5
    z = _dot(glow_ref[...], up_ref[...]) + bias_ref[...]
    g_all = -_softplus(-z) * (LOG2_E / GLA_GATE_NORMALIZER)
    q_all = q_ref[...].astype(F32) * scale
    k_all = k_ref[...].astype(F32)
    rows = [slice(j * C, (j + 1) * C) for j in range(chunks)]
    bs = [_dot_exact_rhs(tril, g_all[rw]) for rw in rows]
    qe, ke, qb, kl, dec, vs = [], [], [], [], [], []
    for rw, b in zip(rows, bs):
        ref = b[C // 2:C // 2 + 1, :]
        b_last = b[C - 1:C, :]
        qe.append(q_all[rw] * jnp.exp2(b - ref))
        ke.append(k_all[rw] * jnp.exp2(ref - b))
        qb.append(qe[-1] * jnp.exp2(ref))
        kl.append(ke[-1] * jnp.exp2(b_last - ref))
        dec.append(jnp.exp2(b_last))
        vs.append(v_ref[rw, :])
    att = [jnp.where(causal, _dot_nt(qe[j], _head_stack(ke[j])), 0.0) for j in range(chunks)]
    kv = [jnp.where(st_mask, _dot_tn(vs[j], kl[j]), 0.0) for j in range(chunks)]
    v_diag = [jnp.concatenate([jnp.where(vl < GLA_DV, vs[j], jnp.zeros_like(vs[j])),
                               jnp.where(vl >= GLA_DV, vs[j], jnp.zeros_like(vs[j]))], axis=0)
              for j in range(chunks)]
    intra = [jnp.dot(att[j].astype(BF16), v_diag[j], preferred_element_type=F32) for j in range(chunks)]
    states = [st_ref[...]]
    for j in range(chunks):
        states.append(states[j] * dec[j] + kv[j])
    st_ref[...] = states[chunks]
    for j in range(chunks):
        o = intra[j] + _dot_nt(qb[j], states[j])
        for h in range(2):
            oh = o[:, h * GLA_DV:(h + 1) * GLA_DV]
            oh = oh * lax.rsqrt(jnp.mean(oh * oh, axis=-1, keepdims=True) + NORM_EPS) * nw_ref[...]
            o_ref[rows[j], h * GLA_DV:(h + 1) * GLA_DV] = oh.astype(o_ref.dtype)


def _gla_call(gqkv, glow, up_pad, bias, norm_w, batch, seq_len, chunks):
    n_tok = gqkv.shape[0]
    tcb = chunks * GLA_CHUNK
    steps = seq_len // tcb
    pairs = GLA_KEY // LANES
    return pl.pallas_call(
        functools.partial(_gla_kernel, chunks=chunks),
        grid=(batch, pairs, steps),
        in_specs=[pl.BlockSpec((tcb, LANES), lambda b, p, c: (b * steps + c, p)),
                  pl.BlockSpec((tcb, LANES), lambda b, p, c: (b * steps + c, pairs + p)),
                  pl.BlockSpec((tcb, GLOW_PAD), lambda b, p, c: (b * steps + c, 0)),
                  pl.BlockSpec((tcb, 2 * GLA_DV), lambda b, p, c: (b * steps + c, pairs + p)),
                  pl.BlockSpec((GLOW_PAD, LANES), lambda b, p, c: (0, p)),
                  pl.BlockSpec((1, LANES), lambda b, p, c: (0, p)),
                  pl.BlockSpec((1, GLA_DV), lambda b, p, c: (0, 0))],
        out_specs=pl.BlockSpec((tcb, 2 * GLA_DV), lambda b, p, c: (b * steps + c, p)),
        out_shape=jax.ShapeDtypeStruct((n_tok, GLA_VAL), BF16),
        scratch_shapes=[pltpu.VMEM((2 * GLA_DV, LANES), F32)],
        compiler_params=_cparams(("parallel", "parallel", "arbitrary")),
        name="l0_gla",
    )(gqkv, gqkv, glow, gqkv, up_pad, bias, norm_w)


def _merge_masks(n):
    r = _iota((n, LANES), 0)
    c = _iota((n, LANES), 1) % HEAD
    masks = []
    s = 1
    while s < n:
        masks.append(((r // s) % 2 == 1) & ((c // s) == (r // s) - 1))
        s *= 2
    return (r == c).astype(F32), masks


def _run_interleaved(main, main_steps, side, side_steps):
    done = 0
    spread = max(1, (3 * main_steps) // 4)
    for i, _ in enumerate(main):
        target = -(-(i + 1) * side_steps // spread)
        while done < min(target, side_steps):
            next(side, None)
            done += 1
    for _ in side:
        pass


def _rwkv_chunk_kernel(r_ref, k_ref, v_ref, xwa_ref, w0_ref, wup_ref, a0_ref, aup_ref,
                       kk_ref, ka_ref, rk_ref,
                       rp_ref, op_ref, bonus_ref, m_ref, n_ref, *, chunks, group):
    C = RWKV_CHUNK
    tril = _tril_ones(C)
    rr = _iota((2 * C, LANES), 0)
    cc = _iota((2 * C, LANES), 1) % HEAD
    tri2 = ((rr < C) & (rr > cc)) | (rr - C >= cc)
    hb = _head_block_ones()
    sq_r = _iota((LANES, LANES), 0)
    sq_c = _iota((LANES, LANES), 1)
    same_head = (sq_r // HEAD) == (sq_c // HEAD)
    eye128 = sq_r == sq_c

    eye, merge = _merge_masks(C)
    zero = jnp.zeros((C, LANES), F32)
    n = range(group)

    def prepare(g, out):
        rows = slice(g * group * C, (g + 1) * group * C)
        r_all = r_ref[rows, :].astype(F32)
        k_all = k_ref[rows, :].astype(F32)
        v_all = v_ref[rows, :].astype(F32)
        xwa = xwa_ref[rows, :]
        y = w0_ref[...] + _dot(jnp.tanh(xwa), wup_ref[...])
        lw_all = _sigmoid(y) * (-LOG2_E * math.exp(-0.5))
        a_sig = _sigmoid(a0_ref[...] + _dot(xwa, aup_ref[...]))
        kk = k_all * kk_ref[...]
        kk = kk * lax.rsqrt(jnp.maximum(_dot(kk * kk, hb), 1e-24))
        k_all = k_all * (1.0 + (a_sig - 1.0) * ka_ref[...])
        bonus_ref[rows, :] = (_dot(r_all * k_all * rk_ref[...], hb) * v_all).astype(bonus_ref.dtype)
        a_all = -kk
        b_all = kk * a_sig
        yield
        for j in n:
            rw = slice(j * C, (j + 1) * C)
            cum = _dot_exact_rhs(tril, lw_all[rw])
            cum_last = cum[C - 1:C, :]
            e_neg = jnp.exp2(-cum)
            e_end = jnp.exp2(cum_last - cum)
            out.append(dict(
                rt=r_all[rw] * jnp.exp2(cum),
                at=a_all[rw] * jnp.exp2(cum - lw_all[rw]),
                bt=b_all[rw] * e_neg,
                kt=k_all[rw] * e_neg,
                ends=jnp.concatenate([b_all[rw] * e_end, k_all[rw] * e_end], axis=0),
                v=v_all[rw],
                dec=jnp.exp2(cum_last)))
            yield

    def solve(g, ops):
        lhs = [jnp.concatenate([o["at"], o["rt"]], axis=0) for o in ops]
        both = [_dot_nt(lhs[j], jnp.concatenate([_head_stack(ops[j]["bt"]), _head_stack(ops[j]["kt"])], axis=0))
                for j in n]
        yield
        left = [jnp.where(tri2, p[:, :LANES], 0.0) for p in both]
        right = [jnp.where(tri2, p[:, LANES:], 0.0) for p in both]
        yield
        lows = [lf[:C] for lf in left]
        ts = [eye + jnp.where(merge[0], low, 0.0) for low in lows]
        for sub in merge[1:]:
            ys = [_dot(jnp.where(sub, low, 0.0), _head_stack(t)) for low, t in zip(lows, ts)]
            yield
            ts = [t + _dot(t, _head_stack(y)) for t, y in zip(ts, ys)]
            yield
        kv = [_dot(right[j], _head_stack(ops[j]["v"])) for j in n]
        yield
        wz = [_dot(ts[j], _head_stack(jnp.concatenate([ops[j]["at"], kv[j][:C]], axis=1))) for j in n]
        yield
        ro = [_dot(left[j][C:], _head_stack(wz[j])) for j in n]
        yield
        mn = [_dot_tn(ops[j]["ends"],
                      jnp.concatenate([wz[j], jnp.concatenate([zero, ops[j]["v"]], axis=1)], axis=0))
              for j in n]
        for j in n:
            c = g * group + j
            rows = slice(c * C, (c + 1) * C)
            rp_ref[rows, :] = (ops[j]["rt"] + ro[j][:, :LANES]).astype(rp_ref.dtype)
            op_ref[rows, :] = ro[j][:, LANES:] + kv[j][C:]
            m_ref[0, 0, c] = (jnp.where(eye128, ops[j]["dec"], 0.0)
                              + jnp.where(same_head, mn[j][:, :LANES], 0.0)).astype(m_ref.dtype)
            n_ref[0, 0, c] = jnp.where(same_head, mn[j][:, LANES:], 0.0).astype(n_ref.dtype)
        yield

    solve_stages = 2 + 2 * (len(merge) - 1) + 4
    groups = chunks // group
    ops = [[] for _ in range(groups + 1)]
    for _ in prepare(0, ops[0]):
        pass
    for g in range(groups):
        side = prepare(g + 1, ops[g + 1]) if g + 1 < groups else iter(())
        _run_interleaved(solve(g, ops[g]), solve_stages, side, group + 1)


def _rwkv_chunk_call(rkv, lora, w0, wup_pad, a0, aup_pad, k_k, k_a, r_k, batch, seq_len, chunks):
    n_tok = rkv.shape[0]
    tcb = chunks * RWKV_CHUNK
    steps = seq_len // tcb
    pairs = RWKV_W // LANES
    nc = seq_len // RWKV_CHUNK
    col = lambda off: (lambda b, p, c: (b * steps + c, off + p))
    par = lambda b, p, c: (0, p)
    tok = lambda b, p, c: (b * steps + c, p)
    mat = lambda b, p, c: (b, p, c, 0, 0)
    return pl.pallas_call(
        functools.partial(_rwkv_chunk_kernel, chunks=chunks, group=RWKV_GROUP),
        grid=(batch, pairs, steps),
        in_specs=[pl.BlockSpec((tcb, LANES), col(0)),
                  pl.BlockSpec((tcb, LANES), col(pairs)),
                  pl.BlockSpec((tcb, LANES), col(2 * pairs)),
                  pl.BlockSpec((tcb, RWKV_LORA), lambda b, p, c: (b * steps + c, 0)),
                  pl.BlockSpec((1, LANES), par),
                  pl.BlockSpec((RWKV_LORA, LANES), par),
                  pl.BlockSpec((1, LANES), par),
                  pl.BlockSpec((RWKV_LORA, LANES), par),
                  pl.BlockSpec((1, LANES), par),
                  pl.BlockSpec((1, LANES), par),
                  pl.BlockSpec((1, LANES), par)],
        out_specs=[pl.BlockSpec((tcb, LANES), tok),
                   pl.BlockSpec((tcb, LANES), tok),
                   pl.BlockSpec((tcb, LANES), tok),
                   pl.BlockSpec((1, 1, chunks, LANES, LANES), mat),
                   pl.BlockSpec((1, 1, chunks, LANES, LANES), mat)],
        out_shape=[jax.ShapeDtypeStruct((n_tok, RWKV_W), BF16),
                   jax.ShapeDtypeStruct((n_tok, RWKV_W), F32),
                   jax.ShapeDtypeStruct((n_tok, RWKV_W), BF16),
                   jax.ShapeDtypeStruct((batch, pairs, nc, LANES, LANES), BF16),
                   jax.ShapeDtypeStruct((batch, pairs, nc, LANES, LANES), BF16)],
        compiler_params=_cparams(("parallel", "parallel", "parallel")),
        name="l0_rwkv_chunks",
    )(rkv, rkv, rkv, lora, w0, wup_pad, a0, aup_pad, k_k, k_a, r_k)


def _rwkv_scan_kernel(rp_ref, op_ref, bonus_ref, m_ref, n_ref, lnw_ref, lnb_ref, o_ref, st_ref, *, chunks):
    c = pl.program_id(0)

    @pl.when(c == 0)
    def _():
        st_ref[...] = jnp.zeros_like(st_ref)

    C = RWKV_CHUNK
    batch = rp_ref.shape[0]
    pairs = RWKV_W // LANES
    hb = _head_block_ones()
    seqs = [(b, p) for b in range(batch) for p in range(pairs)]
    cols = {bp: slice(bp[1] * LANES, (bp[1] + 1) * LANES) for bp in seqs}
    states = {bp: st_ref[bp[0], bp[1]] for bp in seqs}
    outs = {bp: [] for bp in seqs}
    for j in range(chunks):
        for b, p in seqs:
            both = _dot(jnp.concatenate([m_ref[b, p, j], rp_ref[b, j * C:(j + 1) * C, cols[b, p]]], axis=0),
                        states[b, p])
            outs[b, p].append(both[LANES:])
            states[b, p] = both[:LANES] + n_ref[b, p, j]
    for b, p in seqs:
        st_ref[b, p] = states[b, p]
    os = [jnp.concatenate(outs[b, p], axis=0) + op_ref[b, :, cols[b, p]] for b, p in seqs]
    means = [_dot_exact_lhs(o, hb) * (1.0 / RWKV_HEAD) for o in os]
    ds = [o - mean for o, mean in zip(os, means)]
    variances = [_dot(d * d, hb) * (1.0 / RWKV_HEAD) for d in ds]
    for (b, p), d, var in zip(seqs, ds, variances):
        c_ = cols[b, p]
        o_ref[b, :, c_] = (d * lax.rsqrt(var + RWKV_LN_EPS) * lnw_ref[:, c_] + lnb_ref[:, c_]
                           + bonus_ref[b, :, c_]).astype(o_ref.dtype)


def _rwkv_scan_call(rp, op, bonus, m, n, ln_w, ln_b, batch, seq_len, chunks):
    tcb = chunks * RWKV_CHUNK
    pairs = RWKV_W // LANES
    seq3 = lambda t: t.reshape(batch, seq_len, RWKV_W)
    tok = lambda c: (0, c, 0)
    const = lambda c: (0, 0)
    mat = lambda c: (0, 0, c, 0, 0)
    out = pl.pallas_call(
        functools.partial(_rwkv_scan_kernel, chunks=chunks),
        grid=(seq_len // tcb,),
        in_specs=[pl.BlockSpec((batch, tcb, RWKV_W), tok),
                  pl.BlockSpec((batch, tcb, RWKV_W), tok),
                  pl.BlockSpec((batch, tcb, RWKV_W), tok),
                  pl.BlockSpec((batch, pairs, chunks, LANES, LANES), mat),
                  pl.BlockSpec((batch, pairs, chunks, LANES, LANES), mat),
                  pl.BlockSpec((1, RWKV_W), const),
                  pl.BlockSpec((1, RWKV_W), const)],
        out_specs=pl.BlockSpec((batch, tcb, RWKV_W), tok),
        out_shape=jax.ShapeDtypeStruct((batch, seq_len, RWKV_W), BF16),
        scratch_shapes=[pltpu.VMEM((batch, pairs, LANES, LANES), F32)],
        compiler_params=_cparams(("arbitrary",)),
        name="l0_rwkv_scan",
    )(seq3(rp), seq3(op), seq3(bonus), m, n, ln_w, ln_b)
    return out.reshape(batch * seq_len, RWKV_W)


def _gated_out0(oa_ref, ob_ref, gate_ref, x_ref, w_ref):
    g = _silu_bf16(gate_ref[...])
    y = jnp.concatenate([oa_ref[...], ob_ref[...]], axis=1) * g
    return x_ref[...] + jnp.dot(y, w_ref[...], preferred_element_type=F32)


def _rope_group(x, cos, sin_lo, sin_hi):
    half = ROPE_DIMS // 2
    return x * cos + pltpu.roll(x, LANES - half, 1) * sin_lo + pltpu.roll(x, half, 1) * sin_hi


def _paired_head_order():
    return [(2 * pp + e) * SWA_GROUP + g
            for pp in range(SWA_KV_HEADS // 2) for g in range(SWA_GROUP) for e in range(2)]


def _mid_kernel(oa_ref, ob_ref, gate0_ref, x_ref, wo32_ref,
                nw_ref, w1_ref, b_ref, cos_ref, slo_ref, shi_ref,
                h_ref, q_ref, k_ref, v_ref, gate_ref, wo_ref, wq_ref, wkv_ref, wg_ref):
    @pl.when(pl.program_id(0) == 0)
    def _():
        wo_ref[...] = wo32_ref[0].astype(BF16)
        wkv_ref[...] = w1_ref[0, :, MIX1:SWA_QKV].astype(BF16)
        for new, old in enumerate(_paired_head_order()):
            dst = slice(new * SWA_HEAD, (new + 1) * SWA_HEAD)
            wq_ref[:, dst] = w1_ref[0, :, old * SWA_HEAD:(old + 1) * SWA_HEAD].astype(BF16)
            wg_ref[:, dst] = w1_ref[0, :, SWA_QKV + old * SWA_HEAD:SWA_QKV + (old + 1) * SWA_HEAD].astype(BF16)

    h = _gated_out0(oa_ref, ob_ref, gate0_ref, x_ref, wo_ref)
    h_ref[...] = h
    hn = _rmsnorm_rows(h, nw_ref[...]).astype(BF16)
    cos = cos_ref[...]
    slo = slo_ref[...]
    shi = shi_ref[...]
    scale = SWA_HEAD ** -0.5 * LOG2_E
    q = jnp.dot(hn, wq_ref[...], preferred_element_type=F32) + b_ref[:, :MIX1]
    kv = jnp.dot(hn, wkv_ref[...], preferred_element_type=F32) + b_ref[:, MIX1:]
    gate_ref[...] = jnp.dot(hn, wg_ref[...], preferred_element_type=F32).astype(gate_ref.dtype)
    for g in range(MIX1 // LANES):
        cols = slice(g * LANES, (g + 1) * LANES)
        q_ref[:, cols] = (_rope_group(q[:, cols], cos, slo, shi) * scale).astype(q_ref.dtype)
    for g in range(SWA_KV // LANES):
        cols = slice(g * LANES, (g + 1) * LANES)
        k_ref[:, cols] = _rope_group(kv[:, cols], cos, slo, shi).astype(k_ref.dtype)
    v_ref[...] = kv[:, SWA_KV:].astype(v_ref.dtype)


def _mid_call(oa, ob, gate0, x2, w_out0, norm_w, w_in1, b_in, cos, slo, shi, seq_len, tm):
    n_tok = x2.shape[0]
    tps = seq_len // tm
    row = lambda i: (i, 0)
    const = lambda i: (0, 0)
    pos = lambda i: (i % tps, 0)
    whole = lambda t: pl.BlockSpec((1,) + t.shape[1:], lambda i: (0, 0, 0), pipeline_mode=pl.Buffered(1))
    return pl.pallas_call(
        _mid_kernel,
        grid=(n_tok // tm,),
        in_specs=[pl.BlockSpec((tm, GLA_VAL), row),
                  pl.BlockSpec((tm, RWKV_W), row),
                  pl.BlockSpec((tm, MIX0), row),
                  pl.BlockSpec((tm, D_MODEL), row),
                  whole(w_out0),
                  pl.BlockSpec((1, D_MODEL), const),
                  whole(w_in1),
                  pl.BlockSpec((1, SWA_QKV), const),
                  pl.BlockSpec((tm, LANES), pos),
                  pl.BlockSpec((tm, LANES), pos),
                  pl.BlockSpec((tm, LANES), pos)],
        out_specs=[pl.BlockSpec((tm, D_MODEL), row),
                   pl.BlockSpec((tm, MIX1), row),
                   pl.BlockSpec((tm, SWA_KV), row),
                   pl.BlockSpec((tm, SWA_KV), row),
                   pl.BlockSpec((tm, MIX1), row)],
        out_shape=[jax.ShapeDtypeStruct((n_tok, D_MODEL), F32),
                   jax.ShapeDtypeStruct((n_tok, MIX1), BF16),
                   jax.ShapeDtypeStruct((n_tok, SWA_KV), BF16),
                   jax.ShapeDtypeStruct((n_tok, SWA_KV), BF16),
                   jax.ShapeDtypeStruct((n_tok, MIX1), BF16)],
        scratch_shapes=[pltpu.VMEM((MIX0, D_MODEL), BF16),
                        pltpu.VMEM((D_MODEL, MIX1), BF16),
                        pltpu.VMEM((D_MODEL, 2 * SWA_KV), BF16),
                        pltpu.VMEM((D_MODEL, MIX1), BF16)],
        compiler_params=_cparams(("arbitrary",)),
        name="l0_out_l1_proj",
    )(oa, ob, gate0, x2, w_out0, norm_w, w_in1, b_in, cos, slo, shi)


def _swa_kernel(sink_ref, q_ref, kc_ref, kp_ref, vc_ref, vp_ref, gate_ref, h_ref, w32_ref, b_ref, nw_ref,
                y_ref, o_ref, w_ref, *, q_blocks):
    n = pl.program_id(1)

    @pl.when((pl.program_id(0) == 0) & (n == 0))
    def _():
        for new, old in enumerate(_paired_head_order()):
            w_ref[new * SWA_HEAD:(new + 1) * SWA_HEAD, :] = (
                w32_ref[0, old * SWA_HEAD:(old + 1) * SWA_HEAD, :].astype(BF16))

    W = WINDOW
    from_prev = _iota((W, W), 0) > _iota((W, W), 1)
    no_prev = jnp.where(n > 0, 0.0, -jnp.inf)
    out_row = _iota((LANES, W), 0)
    kv_groups = SWA_KV // LANES
    groups = MIX1 // LANES // kv_groups
    tasks = [(j, pp, pp * groups + g) for j in range(q_blocks) for pp in range(kv_groups) for g in range(groups)]
    kk, vt = {}, {}
    for j in range(q_blocks):
        for pp in range(kv_groups):
            cols = slice(pp * LANES, (pp + 1) * LANES)
            if j == 0:
                kk[j, pp] = jnp.concatenate([kp_ref[:, cols], kc_ref[:W, cols]], axis=0)
                vv = jnp.concatenate([vp_ref[:, cols], vc_ref[:W, cols]], axis=0)
            else:
                kk[j, pp] = kc_ref[(j - 1) * W:(j + 1) * W, cols]
                vv = vc_ref[(j - 1) * W:(j + 1) * W, cols]
            vt[j, pp] = vv.astype(F32).T.astype(BF16)

    def scores(j, pp, blk):
        q = q_ref[j * W:(j + 1) * W, blk * LANES:(blk + 1) * LANES]
        return lax.dot_general(kk[j, pp], _head_stack(q), (((1,), (1,)), ((), ())), preferred_element_type=F32)

    def projection_pieces(rows):
        width = D_MODEL // SWA_PROJ_PIECES
        gated = []

        def piece(c):
            def run():
                if not gated:
                    gated.append(o_ref[rows, :] * _silu_bf16(gate_ref[rows, :]))
                cols = slice(c * width, (c + 1) * width)
                y_ref[rows, cols] = (h_ref[rows, cols] + b_ref[:, cols]
                                     + jnp.dot(gated[0], w_ref[:, cols], preferred_element_type=F32))
            return run

        def norm():
            y_ref[rows, :] = _rmsnorm_rows(y_ref[rows, :], nw_ref[...])

        return [piece(c) for c in range(SWA_PROJ_PIECES)] + [norm]

    projections = []
    ahead = 8
    pending = [scores(*t) for t in tasks[:ahead]]
    for i, (j, pp, blk) in enumerate(tasks):
        st = pending.pop(0)
        if i + ahead < len(tasks):
            pending.append(scores(*tasks[i + ahead]))
        halves, scales = [], []
        for hh in range(2):
            cols = slice(hh * W, (hh + 1) * W)
            s_prev = st[:W, cols] + no_prev if j == 0 else st[:W, cols]
            s = jnp.where(from_prev, s_prev, st[W:, cols])
            sink = sink_ref[2 * blk + hh] * LOG2_E
            m = jnp.maximum(jnp.max(s, axis=0, keepdims=True), sink)
            p = jnp.exp2(s - m)
            scales.append(1.0 / (jnp.sum(p, axis=0, keepdims=True) + jnp.exp2(sink - m)))
            pb = p.astype(BF16)
            zero = jnp.zeros_like(pb)
            halves.append(jnp.concatenate([jnp.where(from_prev, pb, zero), jnp.where(from_prev, zero, pb)], axis=0))
        p2 = jnp.concatenate(halves, axis=1)
        ot = jnp.dot(vt[j, pp], p2, preferred_element_type=F32) * jnp.concatenate(scales, axis=1)
        ot = jnp.where(out_row < HEAD, ot[:, :W], ot[:, W:])
        o_ref[j * W:(j + 1) * W, blk * LANES:(blk + 1) * LANES] = ot.T.astype(o_ref.dtype)

        last_of_block = i + 1 == len(tasks) or tasks[i + 1][0] != j
        if last_of_block and (j + 1) % SWA_PROJ_BLOCKS == 0:
            projections.extend(projection_pieces(slice((j + 1 - SWA_PROJ_BLOCKS) * W, (j + 1) * W)))
        if projections and (i % SWA_PROJ_EVERY == SWA_PROJ_EVERY - 1 or i + 1 == len(tasks)):
            projections.pop(0)()
    while projections:
        projections.pop(0)()


def _swa_call(sinks, q, k, v, gate, h1, w_out, b_out, norm_w, batch, seq_len, q_blocks):
    n_tok = q.shape[0]
    rows = q_blocks * WINDOW
    steps = seq_len // rows
    cur = lambda b, n: (b * steps + n, 0)
    prev = lambda b, n: (jnp.maximum((b * steps + n) * q_blocks - 1, 0), 0)
    const = lambda b, n: (0, 0)
    return pl.pallas_call(
        functools.partial(_swa_kernel, q_blocks=q_blocks),
        grid=(batch, steps),
        in_specs=[pl.BlockSpec(memory_space=pltpu.SMEM),
                  pl.BlockSpec((rows, MIX1), cur),
                  pl.BlockSpec((rows, SWA_KV), cur),
                  pl.BlockSpec((WINDOW, SWA_KV), prev),
                  pl.BlockSpec((rows, SWA_KV), cur),
                  pl.BlockSpec((WINDOW, SWA_KV), prev),
                  pl.BlockSpec((rows, MIX1), cur),
                  pl.BlockSpec((rows, D_MODEL), cur),
                  pl.BlockSpec((1,) + w_out.shape[1:], lambda b, n: (0, 0, 0), pipeline_mode=pl.Buffered(1)),
                  pl.BlockSpec((1, D_MODEL), const),
                  pl.BlockSpec((1, D_MODEL), const)],
        out_specs=pl.BlockSpec((rows, D_MODEL), cur),
        out_shape=jax.ShapeDtypeStruct((n_tok, D_MODEL), F32),
        scratch_shapes=[pltpu.VMEM((rows, MIX1), BF16),
                        pltpu.VMEM((MIX1, D_MODEL), BF16)],
        compiler_params=_cparams(("arbitrary", "arbitrary")),
        name="l1_swa_out",
    )(sinks, q, k, k, v, v, gate, h1, w_out, b_out, norm_w)


def _pad_rows(w, rows):
    return jnp.concatenate([w, jnp.zeros((rows - w.shape[0], w.shape[1]), w.dtype)], axis=0)


def _pair_heads(t, axis):
    shape = t.shape
    split = shape[:axis] + (SWA_KV_HEADS // 2, 2, SWA_GROUP, SWA_HEAD) + shape[axis + 1:]
    return jnp.swapaxes(t.reshape(split), axis + 1, axis + 2).reshape(shape)


def _rope_tables(seq_len):
    half = ROPE_DIMS // 2
    inv_freq = ROPE_THETA ** (-jnp.arange(half, dtype=F32) / half)
    ang = jnp.arange(seq_len).astype(F32)[:, None] * inv_freq
    trig = jnp.concatenate([jnp.cos(ang), jnp.sin(ang)], axis=1)
    d = jnp.arange(LANES) % SWA_HEAD
    src = jnp.arange(2 * half)[:, None]
    f = (d % half)[None, :]
    rot = (d < ROPE_DIMS)[None, :]
    lo = (d < half)[None, :]
    sel_cos = ((src == f) & rot).astype(F32)
    sel_lo = -((src == half + f) & lo).astype(F32)
    sel_hi = ((src == half + f) & rot & ~lo).astype(F32)
    sel = jnp.concatenate([sel_cos, sel_lo, sel_hi], axis=1)
    tab = jnp.dot(trig, sel, precision=lax.Precision.HIGHEST)
    cos = tab[:, :LANES] + (~rot).astype(F32)
    return cos, tab[:, LANES:2 * LANES], tab[:, 2 * LANES:]


def _forward(x, norm_w, w_in0, gla_gk_up, gla_gk_bias, gla_norm_w, rwkv_mu, rwkv_w0, rwkv_w_up,
             rwkv_a0, rwkv_a_up, rwkv_k_k, rwkv_k_a, rwkv_r_k, rwkv_ln_w, rwkv_ln_b, w_out0,
             w_in1, b_in1, attn_sinks, w_out1, b_out1, final_norm_w, *, tm, gla_chunks, rwkv_chunks, scan_chunks,
             swa_blocks):
    batch, seq_len, _ = x.shape
    x2 = x.reshape(batch * seq_len, D_MODEL)
    row = lambda t: t.reshape(1, -1)

    gqkv, glow, rkv, lora, gate0 = _in0_call(x2, row(norm_w[0]), jnp.swapaxes(w_in0, 1, 2), row(rwkv_mu[0]),
                                             seq_len, tm)

    up_pad = _pad_rows(gla_gk_up[0], GLOW_PAD).astype(BF16)
    o_a = _gla_call(gqkv, glow, up_pad, row(gla_gk_bias[0]), row(gla_norm_w[0]), batch, seq_len, gla_chunks)

    zeros_r = jnp.zeros((RWKV_DECAY_RANK, RWKV_W), F32)
    wup_pad = jnp.concatenate([rwkv_w_up[0], zeros_r], axis=0).astype(BF16)
    aup_pad = jnp.concatenate([zeros_r, rwkv_a_up[0]], axis=0).astype(BF16)
    rp, op, bonus, m, n = _rwkv_chunk_call(
        rkv, lora, row(rwkv_w0[0]), wup_pad, row(rwkv_a0[0]), aup_pad,
        row(rwkv_k_k[0]), row(rwkv_k_a[0]), row(rwkv_r_k[0]), batch, seq_len, rwkv_chunks)
    o_b = _rwkv_scan_call(rp, op, bonus, m, n, row(rwkv_ln_w[0]), row(rwkv_ln_b[0]),
                          batch, seq_len, scan_chunks)


    b1 = b_in1[0]
    b1p = row(jnp.concatenate([_pair_heads(b1[:MIX1], 0), b1[MIX1:]]))
    sinks_p = jnp.swapaxes(attn_sinks[0].reshape(SWA_KV_HEADS // 2, 2, SWA_GROUP), 1, 2).reshape(SWA_Q_HEADS)
    cos, slo, shi = _rope_tables(seq_len)
    h1, q, k, v, gate1 = _mid_call(o_a, o_b, gate0, x2, w_out0, row(norm_w[1]), w_in1,
                                   b1p, cos, slo, shi, seq_len, tm)
    y = _swa_call(sinks_p, q, k, v, gate1, h1, w_out1, row(b_out1[0]), row(final_norm_w),
                  batch, seq_len, swa_blocks)
    return y.reshape(batch, seq_len, D_MODEL)


def kernel(x, norm_w, w_in0, gla_gk_up, gla_gk_bias, gla_norm_w, rwkv_mu, rwkv_w0, rwkv_w_up, rwkv_a0,
           rwkv_a_up, rwkv_k_k, rwkv_k_a, rwkv_r_k, rwkv_ln_w, rwkv_ln_b, w_out0, w_in1, b_in1,
           attn_sinks, w_out1, b_out1, final_norm_w):
    return _forward(x, norm_w, w_in0, gla_gk_up, gla_gk_bias, gla_norm_w, rwkv_mu, rwkv_w0, rwkv_w_up,
                    rwkv_a0, rwkv_a_up, rwkv_k_k, rwkv_k_a, rwkv_r_k, rwkv_ln_w, rwkv_ln_b, w_out0,
                    w_in1, b_in1, attn_sinks, w_out1, b_out1, final_norm_w,
                    tm=512, gla_chunks=64, rwkv_chunks=64, scan_chunks=8, swa_blocks=8)
```
